```python
import math
import jax, jax.numpy as jnp
from jax import lax
import numpy as np

D_MODEL = 1024
BATCH = 4
SEQ = 4096
DEPTH = 4

MLA_HEADS = 4
MLA_NOPE = 128
MLA_ROPE = 64
MLA_V = 128
MLA_Q_RANK = 384
MLA_KV_RANK = 256
ROPE_THETA = 10000.0
DIFF_HEADS = 4
DIFF_QK = 64
DIFF_V = 2 * DIFF_QK
MIX_WIDTH = MLA_HEADS * MLA_V + DIFF_HEADS * DIFF_V
IN_SIZES = (MLA_Q_RANK, MLA_KV_RANK, MLA_ROPE,
            DIFF_HEADS * 2 * DIFF_QK, DIFF_HEADS * 2 * DIFF_QK, DIFF_HEADS * DIFF_V)
IN_COLS = MLA_Q_RANK + MLA_KV_RANK + MLA_ROPE + 3 * DIFF_HEADS * 2 * DIFF_QK
NUM_BUCKETS = 32
MAX_DISTANCE = 128
D_FF_DENSE = 2816
N_EXPERTS = 8
TOP_K = 2
D_FF_EXPERT = 3584
N_DENSE = (DEPTH + 1) // 2
N_MOE = DEPTH // 2
DN_ALPHA = (2 * DEPTH) ** 0.25
DN_BETA = (8 * DEPTH) ** -0.25
BLOCK_Q = 128

kernel_name = "hybrid_mla_diffattn_deepnorm_moe"


def layer_norm(x, g, b, eps=1e-5):
    xf = x.astype(jnp.float32)
    mu = jnp.mean(xf, -1, keepdims=True)
    var = jnp.mean(jnp.square(xf - mu), -1, keepdims=True)
    return ((xf - mu) * lax.rsqrt(var + eps) * g.astype(jnp.float32) + b.astype(jnp.float32)).astype(x.dtype)


def rms_norm(x, g, eps):
    xf = x.astype(jnp.float32)
    return (xf * lax.rsqrt(jnp.mean(jnp.square(xf), -1, keepdims=True) + eps) * g.astype(jnp.float32)).astype(x.dtype)


def rope_tables(seq):
    pos = jnp.arange(seq, dtype=jnp.float32)
    inv = 1.0 / (ROPE_THETA ** (jnp.arange(0, MLA_ROPE, 2, dtype=jnp.float32) / MLA_ROPE))
    ang = pos[:, None] * inv[None, :]
    return jnp.cos(ang), jnp.sin(ang)


def apply_rope(x, cos, sin):
    xf = x.astype(jnp.float32)
    x1, x2 = jnp.split(xf, 2, axis=-1)
    return jnp.concatenate([x1 * cos - x2 * sin, x2 * cos + x1 * sin], -1).astype(x.dtype)


def t5_causal_bucket(n):
    max_exact = NUM_BUCKETS // 2
    nf = jnp.maximum(n, 1).astype(jnp.float32)
    large = max_exact + (jnp.log(nf / max_exact) / math.log(MAX_DISTANCE / max_exact)
                         * (NUM_BUCKETS - max_exact)).astype(jnp.int32)
    large = jnp.minimum(large, NUM_BUCKETS - 1)
    return jnp.where(n < max_exact, n, large)


def _block_positions(q0, seq):
    qpos = q0 + jnp.arange(BLOCK_Q)
    kpos = jnp.arange(seq)
    return qpos, kpos, kpos[None, :] <= qpos[:, None]


def mla_attention(q_nope, q_rope, k_nope, k_rope, v):
    b, s, h, dv = v.shape
    scale = (MLA_NOPE + MLA_ROPE) ** -0.5

    def block(q0):
        qn = lax.dynamic_slice_in_dim(q_nope, q0, BLOCK_Q, 1)
        qr = lax.dynamic_slice_in_dim(q_rope, q0, BLOCK_Q, 1)
        logits = (jnp.einsum('bqhd,bkhd->bhqk', qn, k_nope).astype(jnp.float32)
                  + jnp.einsum('bqhd,bkd->bhqk', qr, k_rope).astype(jnp.float32)) * scale
        _, _, mask = _block_positions(q0, s)
        p = jax.nn.softmax(jnp.where(mask, logits, -jnp.inf), axis=-1).astype(v.dtype)
        return jnp.einsum('bhqk,bkhd->bqhd', p, v)

    out = lax.map(block, jnp.arange(s // BLOCK_Q) * BLOCK_Q)
    return jnp.moveaxis(out, 0, 1).reshape(b, s, h, dv)


def diff_attention(q, k, v, bias_dist, lam):
    b, s, h, dv = v.shape
    scale = DIFF_QK ** -0.5

    def block(q0):
        qb = lax.dynamic_slice_in_dim(q, q0, BLOCK_Q, 1)
        logits = jnp.einsum('bqhcd,bkhcd->bhcqk', qb, k).astype(jnp.float32) * scale
        qpos, kpos, mask = _block_positions(q0, s)
        rel = jnp.maximum(qpos[:, None] - kpos[None, :], 0)
        bias = jnp.transpose(bias_dist[rel], (2, 0, 1))
        logits = jnp.where(mask, logits + bias[None, :, None], -jnp.inf)
        p = jax.nn.softmax(logits, axis=-1)
        a = (p[:, :, 0] - lam * p[:, :, 1]).astype(v.dtype)
        return jnp.einsum('bhqk,bkhd->bqhd', a, v)

    out = lax.map(block, jnp.arange(s // BLOCK_Q) * BLOCK_Q)
    return jnp.moveaxis(out, 0, 1).reshape(b, s, h, dv)


def hybrid_mixer(x, w_in, q_norm, kv_norm, w_uq, w_uk, w_uv, lam_params, diff_norm, w_o,
                 cos, sin, bias_dist, lambda_init):
    b, s, _ = x.shape
    h = x @ w_in
    splits, acc = [], 0
    for size in IN_SIZES[:-1]:
        acc += size
        splits.append(acc)
    c_q, c_kv, k_r, dq, dk, dv = jnp.split(h, splits, axis=-1)

    q = (rms_norm(c_q, q_norm, 1e-6) @ w_uq).reshape(b, s, MLA_HEADS, MLA_NOPE + MLA_ROPE)
    q_nope = q[..., :MLA_NOPE]
    q_rope = apply_rope(q[..., MLA_NOPE:], cos[None, :, None], sin[None, :, None])
    c_kv = rms_norm(c_kv, kv_norm, 1e-6)
    k_nope = (c_kv @ w_uk).reshape(b, s, MLA_HEADS, MLA_NOPE)
    v_mla = (c_kv @ w_uv).reshape(b, s, MLA_HEADS, MLA_V)
    k_rope = apply_rope(k_r, cos[None], sin[None])
    mla_out = mla_attention(q_nope, q_rope, k_nope, k_rope, v_mla).reshape(b, s, MLA_HEADS * MLA_V)

    lp = lam_params.astype(jnp.float32)
    lam = jnp.exp(jnp.sum(lp[0] * lp[1])) - jnp.exp(jnp.sum(lp[2] * lp[3])) + lambda_init
    dq = dq.reshape(b, s, DIFF_HEADS, 2, DIFF_QK)
    dk = dk.reshape(b, s, DIFF_HEADS, 2, DIFF_QK)
    dv = dv.reshape(b, s, DIFF_HEADS, DIFF_V)
    d_out = diff_attention(dq, dk, dv, bias_dist, lam)
    d_out = rms_norm(d_out, diff_norm, 1e-5) * (1.0 - lambda_init)

    merged = jnp.concatenate([mla_out, d_out.reshape(b, s, DIFF_HEADS * DIFF_V)], axis=-1)
    return merged @ w_o


def swiglu(x, wg, wu, wd):
    return (jax.nn.silu(x @ wg) * (x @ wu)) @ wd


def moe_swiglu(x, router, wg, wu, wd):
    b, s, d = x.shape
    xf = x.reshape(b * s, d)
    logits = (xf @ router).astype(jnp.float32)
    top_v, top_i = lax.top_k(logits, TOP_K)
    gates = jax.nn.softmax(top_v, axis=-1)
    dense_g = jnp.sum(jax.nn.one_hot(top_i, N_EXPERTS, dtype=jnp.float32) * gates[..., None], axis=1).astype(x.dtype)
    y = jnp.zeros_like(xf)
    for e in range(N_EXPERTS):
        y = y + dense_g[:, e:e + 1] * swiglu(xf, wg[e], wu[e], wd[e])
    return y.reshape(b, s, d)


def setup_inputs(seed: int = 0) -> dict:
    key = jax.random.key(seed)
    ks = jax.random.split(key, 24)

    def nrm(k, shape, scale):
        return jax.random.normal(k, shape, jnp.float32) * scale

    col_scale = jnp.concatenate([jnp.ones((IN_COLS - DIFF_HEADS * DIFF_V,), jnp.float32),
                                 jnp.full((DIFF_HEADS * DIFF_V,), DN_BETA, jnp.float32)])
    return {
        "x": nrm(ks[0], (BATCH, SEQ, D_MODEL), 1.0),
        "rel_bias": nrm(ks[1], (NUM_BUCKETS, DIFF_HEADS), 0.5),
        "w_in": nrm(ks[2], (DEPTH, D_MODEL, IN_COLS), D_MODEL ** -0.5) * col_scale,
        "mla_q_norm": 1.0 + nrm(ks[3], (DEPTH, MLA_Q_RANK), 0.02),
        "mla_kv_norm": 1.0 + nrm(ks[4], (DEPTH, MLA_KV_RANK), 0.02),
        "mla_w_uq": nrm(ks[5], (DEPTH, MLA_Q_RANK, MLA_HEADS * (MLA_NOPE + MLA_ROPE)), MLA_Q_RANK ** -0.5),
        "mla_w_uk": nrm(ks[6], (DEPTH, MLA_KV_RANK, MLA_HEADS * MLA_NOPE), MLA_KV_RANK ** -0.5),
        "mla_w_uv": nrm(ks[7], (DEPTH, MLA_KV_RANK, MLA_HEADS * MLA_V), MLA_KV_RANK ** -0.5 * DN_BETA),
        "diff_lambda": nrm(ks[8], (DEPTH, 4, DIFF_QK), 0.1),
        "diff_norm": 1.0 + nrm(ks[9], (DEPTH, DIFF_V), 0.02),
        "w_o": nrm(ks[10], (DEPTH, MIX_WIDTH, D_MODEL), MIX_WIDTH ** -0.5 * DN_BETA),
        "ln1_g": 1.0 + nrm(ks[11], (DEPTH, D_MODEL), 0.02),
        "ln1_b": nrm(ks[12], (DEPTH, D_MODEL), 0.02),
        "ln2_g": 1.0 + nrm(ks[13], (DEPTH, D_MODEL), 0.02),
        "ln2_b": nrm(ks[14], (DEPTH, D_MODEL), 0.02),
        "ffn_w_gate": nrm(ks[15], (N_DENSE, D_MODEL, D_FF_DENSE), D_MODEL ** -0.5 * DN_BETA),
        "ffn_w_up": nrm(ks[16], (N_DENSE, D_MODEL, D_FF_DENSE), D_MODEL ** -0.5 * DN_BETA),
        "ffn_w_down": nrm(ks[17], (N_DENSE, D_FF_DENSE, D_MODEL), D_FF_DENSE ** -0.5 * DN_BETA),
        "moe_router": nrm(ks[18], (N_MOE, D_MODEL, N_EXPERTS), D_MODEL ** -0.5),
        "moe_w_gate": nrm(ks[19], (N_MOE, N_EXPERTS, D_MODEL, D_FF_EXPERT), D_MODEL ** -0.5 * DN_BETA),
        "moe_w_up": nrm(ks[20], (N_MOE, N_EXPERTS, D_MODEL, D_FF_EXPERT), D_MODEL ** -0.5 * DN_BETA),
        "moe_w_down": nrm(ks[21], (N_MOE, N_EXPERTS, D_FF_EXPERT, D_MODEL), D_FF_EXPERT ** -0.5 * DN_BETA),
    }


def reference(x, rel_bias, w_in, mla_q_norm, mla_kv_norm, mla_w_uq, mla_w_uk, mla_w_uv,
              diff_lambda, diff_norm, w_o, ln1_g, ln1_b, ln2_g, ln2_b,
              ffn_w_gate, ffn_w_up, ffn_w_down, moe_router, moe_w_gate, moe_w_up, moe_w_down):
    seq = x.shape[1]
    cos, sin = rope_tables(seq)
    bias_dist = rel_bias.astype(jnp.float32)[t5_causal_bucket(jnp.arange(seq))]
    for l in range(DEPTH):
        lambda_init = 0.8 - 0.6 * math.exp(-0.3 * l)
        mix = hybrid_mixer(x, w_in[l], mla_q_norm[l], mla_kv_norm[l], mla_w_uq[l], mla_w_uk[l],
                           mla_w_uv[l], diff_lambda[l], diff_norm[l], w_o[l],
                           cos, sin, bias_dist, lambda_init)
        x = layer_norm(DN_ALPHA * x + mix, ln1_g[l], ln1_b[l])
        if l % 2 == 0:
            f = swiglu(x, ffn_w_gate[l // 2], ffn_w_up[l // 2], ffn_w_down[l // 2])
        else:
            f = moe_swiglu(x, moe_router[l // 2], moe_w_gate[l // 2], moe_w_up[l // 2], moe_w_down[l // 2])
        x = layer_norm(DN_ALPHA * x + f, ln2_g[l], ln2_b[l])
    return x
```

```python
import functools
import math

import jax
import jax.numpy as jnp
from jax import lax
from jax.experimental import pallas as pl
from jax.experimental.pallas import tpu as pltpu

D_MODEL = 1024
DEPTH = 4
MLA_HEADS = 4
MLA_NOPE = 128
MLA_ROPE = 64
MLA_V = 128
MLA_Q_RANK = 384
MLA_KV_RANK = 256
ROPE_THETA = 10000.0
DIFF_HEADS = 4
DIFF_QK = 64
DIFF_V = 2 * DIFF_QK
NUM_BUCKETS = 32
MAX_DISTANCE = 128
D_FF_DENSE = 2816
N_EXPERTS = 8
D_FF_EXPERT = 3584
DN_ALPHA = (2 * DEPTH) ** 0.25

MLA_QK = MLA_NOPE + MLA_ROPE
MLA_W = MLA_HEADS * MLA_V
DIFF_W = DIFF_HEADS * DIFF_V
_CQ0, _CKV0 = 0, MLA_Q_RANK
_KR0 = _CKV0 + MLA_KV_RANK
_DQ0 = _KR0 + 2 * MLA_ROPE
_DK0 = _DQ0 + DIFF_W
_DV0 = _DK0 + DIFF_W
IN_COLS_W = _DV0 + DIFF_W

LANES = 128
VMEM_LIMIT = 56 * 1024 * 1024
NEG_BIG = -1e30

BF16 = jnp.bfloat16
F32 = jnp.float32


def _dot(a, b):
    return jnp.dot(a, b, preferred_element_type=F32)


def _layer_norm_rows(y, g, b):
    mu = jnp.mean(y, axis=-1, keepdims=True)
    d = y - mu
    var = jnp.mean(d * d, axis=-1, keepdims=True)
    return d * lax.rsqrt(var + 1e-5) * g + b


def _rms_rows(y, g, eps):
    return y * lax.rsqrt(jnp.mean(y * y, axis=-1, keepdims=True) + eps) * g


def _proj_kernel(x_ref, win_ref, qn_ref, kvn_ref, wuq_ref, wukv_ref, cos_ref, sin_ref,
                 qT_ref, k_ref, vT_ref, dqT_ref, dk_ref, dvT_ref):
    x = x_ref[0].astype(BF16)
    h = _dot(x, win_ref[...])
    cqn = _rms_rows(h[:, _CQ0:_CQ0 + MLA_Q_RANK], qn_ref[...], 1e-6)
    ckvn = _rms_rows(h[:, _CKV0:_CKV0 + MLA_KV_RANK], kvn_ref[...], 1e-6)
    q = _dot(cqn.astype(BF16), wuq_ref[...]) * (MLA_QK ** -0.5)
    kv = _dot(ckvn.astype(BF16), wukv_ref[...])
    cos4, sin4 = cos_ref[...], sin_ref[...]
    nr = MLA_HEADS * MLA_ROPE
    qr = q[:, MLA_W:MLA_W + nr] * cos4 + q[:, MLA_W + nr:] * sin4
    kr = (h[:, _KR0:_KR0 + MLA_ROPE] * cos4[:, :MLA_ROPE]
          + h[:, _KR0 + MLA_ROPE:_KR0 + 2 * MLA_ROPE] * sin4[:, :MLA_ROPE])
    qrT = qr.T
    for hh in range(MLA_HEADS):
        c0 = hh * LANES
        qnT = q[:, c0:c0 + MLA_NOPE].T
        qT_ref[0, hh] = jnp.concatenate(
            [qnT, qrT[hh * MLA_ROPE:(hh + 1) * MLA_ROPE]], axis=0).astype(BF16)
        k_ref[0, hh] = jnp.concatenate([kv[:, c0:c0 + MLA_NOPE], kr], axis=1).astype(BF16)
        vT_ref[0, hh] = kv[:, MLA_W + c0:MLA_W + c0 + MLA_V].T.astype(BF16)
        dqT_ref[0, hh] = (h[:, _DQ0 + c0:_DQ0 + c0 + LANES] * (DIFF_QK ** -0.5)).T.astype(BF16)
        dk_ref[0, hh] = h[:, _DK0 + c0:_DK0 + c0 + LANES].astype(BF16)
        dvT_ref[0, hh] = h[:, _DV0 + c0:_DV0 + c0 + DIFF_V].T.astype(BF16)


def _proj(x, win, qn, kvn, wuq, wukv, cos4, sin4, tm):
    b, s, d = x.shape
    hd = MLA_HEADS
    full = lambda shape: pl.BlockSpec(shape, lambda bi, i: (0,) * len(shape))
    tmaj = lambda w: pl.BlockSpec((1, hd, tm, w), lambda bi, i: (bi, 0, i, 0))
    fmaj = lambda w: pl.BlockSpec((1, hd, w, tm), lambda bi, i: (bi, 0, 0, i))
    sds = jax.ShapeDtypeStruct
    return pl.pallas_call(
        _proj_kernel,
        grid=(b, s // tm),
        in_specs=[
            pl.BlockSpec((1, tm, d), lambda bi, i: (bi, i, 0)),
            full(win.shape), full(qn.shape), full(kvn.shape), full(wuq.shape), full(wukv.shape),
            pl.BlockSpec((tm, cos4.shape[1]), lambda bi, i: (i, 0)),
            pl.BlockSpec((tm, sin4.shape[1]), lambda bi, i: (i, 0)),
        ],
        out_specs=[fmaj(MLA_QK), tmaj(MLA_QK), fmaj(MLA_V), fmaj(LANES), tmaj(LANES), fmaj(DIFF_V)],
        out_shape=[
            sds((b, hd, MLA_QK, s), BF16), sds((b, hd, s, MLA_QK), BF16), sds((b, hd, MLA_V, s), BF16),
            sds((b, hd, LANES, s), BF16), sds((b, hd, s, LANES), BF16), sds((b, hd, DIFF_V, s), BF16),
        ],
        compiler_params=pltpu.CompilerParams(
            dimension_semantics=("arbitrary", "arbitrary"), vmem_limit_bytes=VMEM_LIMIT),
        name="proj",
    )(x, win, qn, kvn, wuq, wukv, cos4, sin4)


def _softmax_step(s, vj, m_ref, l_ref, acc_ref):
    m_prev = m_ref[...]
    m_new = jnp.maximum(m_prev, jnp.max(s, axis=0, keepdims=True))
    a = jnp.exp(m_prev - m_new)
    p = jnp.exp(s - m_new)
    l_ref[...] = a * l_ref[...] + jnp.sum(p, axis=0, keepdims=True)
    acc_ref[...] = a * acc_ref[...] + _dot(vj, p.astype(BF16))
    m_ref[...] = m_new


def _mla_attn_kernel(qT_ref, k_ref, vT_ref, o_ref, m_ref, l_ref, acc_ref, *, ta):
    qi = pl.program_id(2)
    qT = qT_ref[0, 0]
    m_ref[...] = jnp.full(m_ref.shape, NEG_BIG, F32)
    l_ref[...] = jnp.zeros(l_ref.shape, F32)
    acc_ref[...] = jnp.zeros(acc_ref.shape, F32)

    def step(j, masked):
        off = pl.multiple_of(j * ta, ta)
        s = _dot(k_ref[0, 0, pl.ds(off, ta), :], qT)
        if masked:
            kpos = lax.broadcasted_iota(jnp.int32, (ta, ta), 0)
            qpos = lax.broadcasted_iota(jnp.int32, (ta, ta), 1)
            s = jnp.where(kpos <= qpos, s, NEG_BIG)
        _softmax_step(s, vT_ref[0, 0, :, pl.ds(off, ta)], m_ref, l_ref, acc_ref)

    def body(j, carry):
        step(j, False)
        return carry

    lax.fori_loop(0, qi, body, 0)
    step(qi, True)
    out = acc_ref[...] * (1.0 / l_ref[...])
    o_ref[0] = out.T.astype(BF16)


def _mla_attn(qT, k, vT, ta):
    b, hd, _, s = qT.shape
    return pl.pallas_call(
        functools.partial(_mla_attn_kernel, ta=ta),
        grid=(b, hd, s // ta),
        in_specs=[
            pl.BlockSpec((1, 1, MLA_QK, ta), lambda bi, h, i: (bi, h, 0, i)),
            pl.BlockSpec((1, 1, s, MLA_QK), lambda bi, h, i: (bi, h, 0, 0)),
            pl.BlockSpec((1, 1, MLA_V, s), lambda bi, h, i: (bi, h, 0, 0)),
        ],
        out_specs=pl.BlockSpec((1, ta, MLA_V), lambda bi, h, i: (bi, i, h)),
        out_shape=jax.ShapeDtypeStruct((b, s, hd * MLA_V), BF16),
        scratch_shapes=[pltpu.VMEM((1, ta), F32), pltpu.VMEM((1, ta), F32),
                        pltpu.VMEM((MLA_V, ta), F32)],
        compiler_params=pltpu.CompilerParams(
            dimension_semantics=("arbitrary",) * 3, vmem_limit_bytes=VMEM_LIMIT),
        name="mla_attn",
    )(qT, k, vT)


def _diff_attn_kernel(cfar_ref, qT_ref, k_ref, vT_ref, bias_ref, lamp_ref, g_ref, o_ref,
                      m_ref, l_ref, acc_ref, *, ta, lambda_init):
    h = pl.program_id(1)
    qi = pl.program_id(2)
    qT = qT_ref[0, 0]
    zeros = jnp.zeros((DIFF_QK, ta), BF16)
    qs = (jnp.concatenate([qT[:DIFF_QK], zeros], axis=0),
          jnp.concatenate([zeros, qT[DIFF_QK:]], axis=0))
    m_ref[...] = jnp.full(m_ref.shape, NEG_BIG, F32)
    l_ref[...] = jnp.zeros(l_ref.shape, F32)
    acc_ref[...] = jnp.zeros(acc_ref.shape, F32)
    cfar = cfar_ref[h]

    def step(j, bias_slot):
        off = pl.multiple_of(j * ta, ta)
        kj = k_ref[0, 0, pl.ds(off, ta), :]
        vj = vT_ref[0, 0, :, pl.ds(off, ta)]
        for c in range(2):
            s = _dot(kj, qs[c])
            s = s + (cfar if bias_slot is None else bias_ref[0, bias_slot])
            _softmax_step(s, vj, m_ref.at[c], l_ref.at[c], acc_ref.at[c])

    def body(j, carry):
        step(j, None)
        return carry

    lax.fori_loop(0, jnp.maximum(qi - 1, 0), body, 0)

    @pl.when(qi >= 1)
    def _():
        step(qi - 1, 1)

    step(qi, 0)

    lp = lamp_ref[...]
    lam = (jnp.exp(jnp.sum(lp[0:1] * lp[1:2], axis=-1, keepdims=True))
           - jnp.exp(jnp.sum(lp[2:3] * lp[3:4], axis=-1, keepdims=True)) + lambda_init)
    out = acc_ref[0] * (1.0 / l_ref[0]) - lam * (acc_ref[1] * (1.0 / l_ref[1]))
    ms = jnp.mean(out * out, axis=0, keepdims=True)
    out = out * lax.rsqrt(ms + 1e-5) * g_ref[...] * (1.0 - lambda_init)
    o_ref[0] = out.T.astype(BF16)


def _diff_attn(cfar, qT, k, vT, bias_tiles, lam_params, g_col, ta, lambda_init):
    b, hd, _, s = qT.shape
    grid_spec = pltpu.PrefetchScalarGridSpec(
        num_scalar_prefetch=1,
        grid=(b, hd, s // ta),
        in_specs=[
            pl.BlockSpec((1, 1, LANES, ta), lambda bi, h, i, cf: (bi, h, 0, i)),
            pl.BlockSpec((1, 1, s, LANES), lambda bi, h, i, cf: (bi, h, 0, 0)),
            pl.BlockSpec((1, 1, DIFF_V, s), lambda bi, h, i, cf: (bi, h, 0, 0)),
            pl.BlockSpec((1, 2, ta, ta), lambda bi, h, i, cf: (h, 0, 0, 0)),
            pl.BlockSpec(lam_params.shape, lambda bi, h, i, cf: (0, 0)),
            pl.BlockSpec(g_col.shape, lambda bi, h, i, cf: (0, 0)),
        ],
        out_specs=pl.BlockSpec((1, ta, DIFF_V), lambda bi, h, i, cf: (bi, i, h)),
        scratch_shapes=[pltpu.VMEM((2, 1, ta), F32), pltpu.VMEM((2, 1, ta), F32),
                        pltpu.VMEM((2, DIFF_V, ta), F32)],
    )
    return pl.pallas_call(
        functools.partial(_diff_attn_kernel, ta=ta, lambda_init=lambda_init),
        grid_spec=grid_spec,
        out_shape=jax.ShapeDtypeStruct((b, s, hd * DIFF_V), BF16),
        compiler_params=pltpu.CompilerParams(
            dimension_semantics=("arbitrary",) * 3, vmem_limit_bytes=VMEM_LIMIT),
        name="diff_attn",
    )(cfar, qT, k, vT, bias_tiles, lam_params, g_col)


def _split_bf16(v):
    hi = v.astype(BF16)
    return hi, (v - hi.astype(F32)).astype(BF16)


def _oproj_kernel(am_ref, ad_ref, x_ref, wo_ref, g_ref, b_ref, *rest, route):
    mix = _dot(am_ref[...], wo_ref[:MLA_W, :]) + _dot(ad_ref[...], wo_ref[MLA_W:, :])
    y = _layer_norm_rows(DN_ALPHA * x_ref[...] + mix, g_ref[...], b_ref[...])
    if not route:
        (o_ref,) = rest
        o_ref[...] = y
        return
    rhi_ref, rlo_ref, o_ref, r_ref = rest
    yhi, ylo = _split_bf16(y)
    logits = _dot(yhi, rhi_ref[...]) + (_dot(yhi, rlo_ref[...]) + _dot(ylo, rhi_ref[...]))
    lane = lax.broadcasted_iota(jnp.int32, logits.shape, 1)
    logits = jnp.where(lane < N_EXPERTS, logits, NEG_BIG)
    m1 = jnp.max(logits, axis=-1, keepdims=True)
    i1 = jnp.min(jnp.where(logits == m1, lane, LANES), axis=-1, keepdims=True)
    rest_l = jnp.where(lane == i1, NEG_BIG, logits)
    m2 = jnp.max(rest_l, axis=-1, keepdims=True)
    i2 = jnp.min(jnp.where(rest_l == m2, lane, LANES), axis=-1, keepdims=True)
    e2 = jnp.exp(m2 - m1)
    g1 = 1.0 / (1.0 + e2)
    g2 = e2 * g1
    rec = jnp.where(lane == 0, i1.astype(F32),
                    jnp.where(lane == 1, i2.astype(F32),
                              jnp.where(lane == 2, g1, jnp.where(lane == 3, g2, 0.0))))
    o_ref[...] = y
    r_ref[...] = rec


def _oproj(a_mla, a_diff, x2d, wo, g, b, tm, router=None):
    t, d = x2d.shape
    row = lambda w: pl.BlockSpec((tm, w), lambda i: (i, 0))
    full = lambda shape: pl.BlockSpec(shape, lambda i: (0,) * len(shape))
    in_specs = [row(MLA_W), row(DIFF_W), row(d), full(wo.shape), full(g.shape), full(b.shape)]
    args = [a_mla, a_diff, x2d, wo, g, b]
    out_specs = [row(d)]
    out_shape = [jax.ShapeDtypeStruct((t, d), F32)]
    if router is not None:
        in_specs += [full(router[0].shape), full(router[1].shape)]
        args += list(router)
        out_specs.append(row(LANES))
        out_shape.append(jax.ShapeDtypeStruct((t, LANES), F32))
    res = pl.pallas_call(
        functools.partial(_oproj_kernel, route=router is not None),
        grid=(t // tm,),
        in_specs=in_specs, out_specs=out_specs, out_shape=out_shape,
        compiler_params=pltpu.CompilerParams(
            dimension_semantics=("arbitrary",), vmem_limit_bytes=VMEM_LIMIT),
        name="oproj_route" if router is not None else "oproj",
    )(*args)
    return res if router is not None else res[0]


def _ffn_kernel(x_ref, wg_ref, wu_ref, wd_ref, g_ref, b_ref, o_ref):
    x = x_ref[...]
    xb = x.astype(BF16)
    gate = _dot(xb, wg_ref[...])
    up = _dot(xb, wu_ref[...])
    hid = (gate * jax.nn.sigmoid(gate) * up).astype(BF16)
    f = _dot(hid, wd_ref[...])
    o_ref[...] = _layer_norm_rows(DN_ALPHA * x + f, g_ref[...], b_ref[...])


def _ffn(x2d, wg, wu, wd, g, b, tm):
    t, d = x2d.shape
    row = pl.BlockSpec((tm, d), lambda i: (i, 0))
    const = lambda shape: pl.BlockSpec(shape, lambda i: (0,) * len(shape),
                                       pipeline_mode=pl.Buffered(1))
    return pl.pallas_call(
        _ffn_kernel,
        grid=(t // tm,),
        in_specs=[row, const(wg.shape), const(wu.shape), const(wd.shape),
                  const(g.shape), const(b.shape)],
        out_specs=row,
        out_shape=jax.ShapeDtypeStruct((t, d), F32),
        compiler_params=pltpu.CompilerParams(
            dimension_semantics=("arbitrary",), vmem_limit_bytes=VMEM_LIMIT),
        name="ffn",
    )(x2d, wg, wu, wd, g, b)


def _row_gather_kernel(idx_ref, src_ref, dst_ref, sem, *, chunk):
    base = pl.program_id(0) * chunk

    def issue(r, carry):
        pltpu.make_async_copy(src_ref.at[pl.ds(idx_ref[r], 1)],
                              dst_ref.at[pl.ds(base + r, 1)], sem).start()
        return carry

    lax.fori_loop(0, chunk, issue, 0)
    pltpu.make_async_copy(src_ref.at[pl.ds(0, chunk)], dst_ref.at[pl.ds(base, chunk)], sem).wait()


def _row_gather(src, idx, chunk):
    n = idx.shape[0]
    return pl.pallas_call(
        functools.partial(_row_gather_kernel, chunk=chunk),
        grid=(n // chunk,),
        in_specs=[pl.BlockSpec((chunk,), lambda i: (i,), memory_space=pltpu.SMEM),
                  pl.BlockSpec(memory_space=pl.ANY)],
        out_specs=pl.BlockSpec(memory_space=pl.ANY),
        out_shape=jax.ShapeDtypeStruct((n, src.shape[1]), src.dtype),
        scratch_shapes=[pltpu.SemaphoreType.DMA(())],
        compiler_params=pltpu.CompilerParams(
            dimension_semantics=("arbitrary",), has_side_effects=True),
        name="row_gather",
    )(idx, src)


def _gffn_kernel(te_ref, tv_ref, x_ref, wg_ref, wu_ref, wd_ref, o_ref, acc_ref, *, nf):
    i = pl.program_id(0)
    f = pl.program_id(1)
    valid = tv_ref[i] == 1

    @pl.when(valid)
    def _():
        xb = x_ref[...].astype(BF16)
        gate = _dot(xb, wg_ref[0])
        up = _dot(xb, wu_ref[0])
        hid = (gate * jax.nn.sigmoid(gate) * up).astype(BF16)
        part = _dot(hid, wd_ref[0])

        @pl.when(f == 0)
        def _():
            acc_ref[...] = part

        @pl.when(f > 0)
        def _():
            acc_ref[...] += part

    @pl.when(f == nf - 1)
    def _():
        o_ref[...] = jnp.where(valid, acc_ref[...], 0.0)


def _gffn(tile_e, tile_v, xs, wg, wu, wd, tm, fc):
    p, d = xs.shape
    nt = p // tm
    nf = wg.shape[2] // fc
    fidx = lambda i, f, te, tv: jnp.where(tv[i] == 1, f, nf - 1)
    grid_spec = pltpu.PrefetchScalarGridSpec(
        num_scalar_prefetch=2,
        grid=(nt, nf),
        in_specs=[
            pl.BlockSpec((tm, d), lambda i, f, te, tv: (i, 0)),
            pl.BlockSpec((1, d, fc), lambda i, f, te, tv: (te[i], 0, fidx(i, f, te, tv))),
            pl.BlockSpec((1, d, fc), lambda i, f, te, tv: (te[i], 0, fidx(i, f, te, tv))),
            pl.BlockSpec((1, fc, d), lambda i, f, te, tv: (te[i], fidx(i, f, te, tv), 0)),
        ],
        out_specs=pl.BlockSpec((tm, d), lambda i, f, te, tv: (i, 0)),
        scratch_shapes=[pltpu.VMEM((tm, d), F32)],
    )
    return pl.pallas_call(
        functools.partial(_gffn_kernel, nf=nf),
        grid_spec=grid_spec,
        out_shape=jax.ShapeDtypeStruct((p, d), F32),
        compiler_params=pltpu.CompilerParams(
            dimension_semantics=("arbitrary", "arbitrary"), vmem_limit_bytes=VMEM_LIMIT),
        name="grouped_ffn",
    )(tile_e, tile_v, xs, wg, wu, wd)


def _combine_kernel(x_ref, y_ref, r_ref, g_ref, b_ref, o_ref):
    x = x_ref[...]
    rec = r_ref[...]
    f = rec[:, 2:3] * y_ref[0] + rec[:, 3:4] * y_ref[1]
    o_ref[...] = _layer_norm_rows(DN_ALPHA * x + f, g_ref[...], b_ref[...])


def _combine(x2d, y2, rec, g, b, tm):
    t, d = x2d.shape
    row = lambda w: pl.BlockSpec((tm, w), lambda i: (i, 0))
    full = lambda shape: pl.BlockSpec(shape, lambda i: (0,) * len(shape))
    return pl.pallas_call(
        _combine_kernel,
        grid=(t // tm,),
        in_specs=[row(d), pl.BlockSpec((2, tm, d), lambda i: (0, i, 0)), row(LANES),
                  full(g.shape), full(b.shape)],
        out_specs=row(d),
        out_shape=jax.ShapeDtypeStruct((t, d), F32),
        compiler_params=pltpu.CompilerParams(
            dimension_semantics=("arbitrary",), vmem_limit_bytes=VMEM_LIMIT),
        name="combine",
    )(x2d, y2, rec, g, b)


def _route_plan(rec, tm, nt):
    t = rec.shape[0]
    e = jnp.concatenate([rec[:, 0], rec[:, 1]]).astype(jnp.int32)
    onehot = (e[:, None] == jnp.arange(N_EXPERTS, dtype=jnp.int32)[None, :]).astype(jnp.int32)
    csum = jnp.cumsum(onehot, axis=0)
    rank = jnp.take_along_axis(csum, e[:, None], axis=1)[:, 0] - 1
    counts = csum[-1]
    padded = ((counts + tm - 1) // tm) * tm
    ends = jnp.cumsum(padded)
    pos = (ends - padded)[e] + rank
    tok = jnp.tile(jnp.arange(t, dtype=jnp.int32), 2)
    src = jnp.zeros((nt * tm,), jnp.int32).at[pos].set(tok)
    tile_start = jnp.arange(nt, dtype=jnp.int32) * tm
    tile_e = jnp.sum((tile_start[:, None] >= ends[None, :]).astype(jnp.int32), axis=1)
    tile_v = (tile_e < N_EXPERTS).astype(jnp.int32)
    last_e = jnp.max(jnp.where(tile_v == 1, tile_e, 0))
    tile_e = jnp.where(tile_v == 1, tile_e, last_e)
    return pos.astype(jnp.int32), src, tile_e.astype(jnp.int32), tile_v


def _moe(x2d, rec, wg, wu, wd, g, b, tm_tok, tm_e, fc, chunk):
    t, d = x2d.shape
    nt = (2 * t) // tm_e + N_EXPERTS
    pos, src, tile_e, tile_v = _route_plan(rec, tm_e, nt)
    xs = _row_gather(x2d, src, chunk)
    ys = _gffn(tile_e, tile_v, xs, wg, wu, wd, tm_e, fc)
    y2 = _row_gather(ys, pos, chunk).reshape(2, t, d)
    return _combine(x2d, y2, rec, g, b, tm_tok)


def _rotate_half_cols(w):
    half = w.shape[-1] // 2
    return jnp.concatenate([-w[..., half:], w[..., :half]], axis=-1)


def _t5_bucket(n):
    max_exact = NUM_BUCKETS // 2
    nf = jnp.maximum(n, 1).astype(F32)
    large = max_exact + (jnp.log(nf / max_exact) / math.log(MAX_DISTANCE / max_exact)
                         * (NUM_BUCKETS - max_exact)).astype(jnp.int32)
    large = jnp.minimum(large, NUM_BUCKETS - 1)
    return jnp.where(n < max_exact, n, large)


def _pick(n, prefs):
    for p in prefs:
        if n % p == 0:
            return p
    raise ValueError(f"no tile for extent {n}")


def kernel(x, rel_bias, w_in, mla_q_norm, mla_kv_norm, mla_w_uq, mla_w_uk, mla_w_uv, diff_lambda, diff_norm, w_o, ln1_g, ln1_b, ln2_g, ln2_b, ffn_w_gate, ffn_w_up, ffn_w_down, moe_router, moe_w_gate, moe_w_up, moe_w_down):
    b, s, d = x.shape
    t = b * s
    assert d == D_MODEL
    ta = _pick(s, (512, 256, 128))
    assert s >= 2 * ta or s == ta
    tm_proj = _pick(s, (256, 128))
    tm_tok = _pick(t, (256, 128))
    tm_e = _pick(2 * t, (512, 256, 128))
    chunk = 1024
    assert (2 * t) % chunk == 0 and ((2 * t) // tm_e + N_EXPERTS) * tm_e % chunk == 0
    fc = D_FF_EXPERT // 2

    pos = jnp.arange(s, dtype=F32)
    inv = 1.0 / (ROPE_THETA ** (jnp.arange(0, MLA_ROPE, 2, dtype=F32) / MLA_ROPE))
    ang = pos[:, None] * inv[None, :]
    cos4 = jnp.tile(jnp.cos(ang), (1, 2 * MLA_HEADS))
    sin4 = jnp.tile(jnp.sin(ang), (1, 2 * MLA_HEADS))
    bias_dist = rel_bias.astype(F32)[_t5_bucket(jnp.arange(s))]
    kk = jnp.arange(ta)[:, None]
    qq = jnp.arange(ta)[None, :]
    tiles = []
    for off in (0, ta):
        dist = qq + off - kk
        bt = jnp.transpose(bias_dist[jnp.clip(dist, 0, s - 1)], (2, 0, 1))
        tiles.append(jnp.where((dist >= 0)[None], bt, NEG_BIG))
    bias_tiles = jnp.stack(tiles, axis=1)
    assert ta + 1 >= MAX_DISTANCE
    cfar = rel_bias.astype(F32)[NUM_BUCKETS - 1]

    xc = x
    for l in range(DEPTH):
        lambda_init = 0.8 - 0.6 * math.exp(-0.3 * l)
        wl = w_in[l]
        kr_w = wl[:, _KR0:_KR0 + MLA_ROPE]
        win = jnp.concatenate([wl[:, :_KR0 + MLA_ROPE], _rotate_half_cols(kr_w),
                               wl[:, _KR0 + MLA_ROPE:]], axis=1).astype(BF16)
        uq = mla_w_uq[l].reshape(MLA_Q_RANK, MLA_HEADS, MLA_QK)
        uq_n = uq[:, :, :MLA_NOPE].reshape(MLA_Q_RANK, -1)
        uq_r = uq[:, :, MLA_NOPE:]
        wuq = jnp.concatenate([uq_n, uq_r.reshape(MLA_Q_RANK, -1),
                               _rotate_half_cols(uq_r).reshape(MLA_Q_RANK, -1)], axis=1).astype(BF16)
        wukv = jnp.concatenate([mla_w_uk[l], mla_w_uv[l]], axis=1).astype(BF16)

        qT, k, vT, dqT, dk, dvT = _proj(
            xc, win, mla_q_norm[l][None, :], mla_kv_norm[l][None, :], wuq, wukv, cos4, sin4, tm_proj)
        a_mla = _mla_attn(qT, k, vT, ta).reshape(t, MLA_W)
        a_diff = _diff_attn(cfar, dqT, dk, dvT, bias_tiles, diff_lambda[l].astype(F32),
                            diff_norm[l].astype(F32)[:, None], ta, lambda_init).reshape(t, DIFF_W)
        x2d = xc.reshape(t, d)
        wo = w_o[l].astype(BF16)
        g1, b1 = ln1_g[l][None, :], ln1_b[l][None, :]
        g2, b2 = ln2_g[l][None, :], ln2_b[l][None, :]
        if l % 2 == 0:
            x1 = _oproj(a_mla, a_diff, x2d, wo, g1, b1, tm_tok)
            i = l // 2
            x2 = _ffn(x1, ffn_w_gate[i].astype(BF16), ffn_w_up[i].astype(BF16),
                      ffn_w_down[i].astype(BF16), g2, b2, tm_tok)
        else:
            i = l // 2
            rpad = jnp.pad(moe_router[i].astype(F32), ((0, 0), (0, LANES - N_EXPERTS)))
            rhi = rpad.astype(BF16)
            rlo = (rpad - rhi.astype(F32)).astype(BF16)
            x1, rec = _oproj(a_mla, a_diff, x2d, wo, g1, b1, tm_tok, router=(rhi, rlo))
            x2 = _moe(x1, rec, moe_w_gate[i].astype(BF16), moe_w_up[i].astype(BF16),
                      moe_w_down[i].astype(BF16), g2, b2, tm_tok, tm_e, fc, chunk)
        xc = x2.reshape(b, s, d)
    return xc
```

```python
import functools
import math

import jax
import jax.numpy as jnp
from jax import lax
from jax.experimental import pallas as pl
from jax.experimental.pallas import tpu as pltpu

D_MODEL = 1024
DEPTH = 4
MLA_HEADS = 4
MLA_NOPE = 128
MLA_ROPE = 64
MLA_V = 128
MLA_Q_RANK = 384
MLA_KV_RANK = 256
ROPE_THETA = 10000.0
DIFF_HEADS = 4
DIFF_QK = 64
DIFF_V = 2 * DIFF_QK
NUM_BUCKETS = 32
MAX_DISTANCE = 128
D_FF_DENSE = 2816
N_EXPERTS = 8
D_FF_EXPERT = 3584
DN_ALPHA = (2 * DEPTH) ** 0.25

MLA_QK = MLA_NOPE + MLA_ROPE
MLA_W = MLA_HEADS * MLA_V
DIFF_W = DIFF_HEADS * DIFF_V
_CQ0, _CKV0 = 0, MLA_Q_RANK
_KR0 = _CKV0 + MLA_KV_RANK
_DQ0 = _KR0 + 2 * MLA_ROPE
_DK0 = _DQ0 + DIFF_W
_DV0 = _DK0 + DIFF_W
IN_COLS_W = _DV0 + DIFF_W

LANES = 128
VMEM_LIMIT = 56 * 1024 * 1024
NEG_BIG = -1e30

BF16 = jnp.bfloat16
F32 = jnp.float32


def _dot(a, b):
    return jnp.dot(a, b, preferred_element_type=F32)


def _layer_norm_rows(y, g, b):
    mu = jnp.mean(y, axis=-1, keepdims=True)
    d = y - mu
    var = jnp.mean(d * d, axis=-1, keepdims=True)
    return d * lax.rsqrt(var + 1e-5) * g + b


def _rms_rows(y, g, eps):
    return y * lax.rsqrt(jnp.mean(y * y, axis=-1, keepdims=True) + eps) * g


def _proj_kernel(x_ref, win_ref, qn_ref, kvn_ref, wuq_ref, wukv_ref, cos_ref, sin_ref,
                 qT_ref, k_ref, vT_ref, dqT_ref, dk_ref, dvT_ref):
    x = x_ref[0].astype(BF16)
    h = _dot(x, win_ref[...])
    cqn = _rms_rows(h[:, _CQ0:_CQ0 + MLA_Q_RANK], qn_ref[...], 1e-6)
    ckvn = _rms_rows(h[:, _CKV0:_CKV0 + MLA_KV_RANK], kvn_ref[...], 1e-6)
    q = _dot(cqn.astype(BF16), wuq_ref[...]) * (MLA_QK ** -0.5)
    kv = _dot(ckvn.astype(BF16), wukv_ref[...])
    cos4, sin4 = cos_ref[...], sin_ref[...]
    nr = MLA_HEADS * MLA_ROPE
    qr = q[:, MLA_W:MLA_W + nr] * cos4 + q[:, MLA_W + nr:] * sin4
    kr = (h[:, _KR0:_KR0 + MLA_ROPE] * cos4[:, :MLA_ROPE]
          + h[:, _KR0 + MLA_ROPE:_KR0 + 2 * MLA_ROPE] * sin4[:, :MLA_ROPE])
    qrT = qr.T
    for hh in range(MLA_HEADS):
        c0 = hh * LANES
        qnT = q[:, c0:c0 + MLA_NOPE].T
        qT_ref[0, hh] = jnp.concatenate(
            [qnT, qrT[hh * MLA_ROPE:(hh + 1) * MLA_ROPE]], axis=0).astype(BF16)
        k_ref[0, hh] = jnp.concatenate([kv[:, c0:c0 + MLA_NOPE], kr], axis=1).astype(BF16)
        vT_ref[0, hh] = kv[:, MLA_W + c0:MLA_W + c0 + MLA_V].T.astype(BF16)
        dqT_ref[0, hh] = (h[:, _DQ0 + c0:_DQ0 + c0 + LANES] * (DIFF_QK ** -0.5)).T.astype(BF16)
        dk_ref[0, hh] = h[:, _DK0 + c0:_DK0 + c0 + LANES].astype(BF16)
        dvT_ref[0, hh] = h[:, _DV0 + c0:_DV0 + c0 + DIFF_V].T.astype(BF16)


def _proj(x, win, qn, kvn, wuq, wukv, cos4, sin4, tm):
    b, s, d = x.shape
    hd = MLA_HEADS
    full = lambda shape: pl.BlockSpec(shape, lambda bi, i: (0,) * len(shape))
    tmaj = lambda w: pl.BlockSpec((1, hd, tm, w), lambda bi, i: (bi, 0, i, 0))
    fmaj = lambda w: pl.BlockSpec((1, hd, w, tm), lambda bi, i: (bi, 0, 0, i))
    sds = jax.ShapeDtypeStruct
    return pl.pallas_call(
        _proj_kernel,
        grid=(b, s // tm),
        in_specs=[
            pl.BlockSpec((1, tm, d), lambda bi, i: (bi, i, 0)),
            full(win.shape), full(qn.shape), full(kvn.shape), full(wuq.shape), full(wukv.shape),
            pl.BlockSpec((tm, cos4.shape[1]), lambda bi, i: (i, 0)),
            pl.BlockSpec((tm, sin4.shape[1]), lambda bi, i: (i, 0)),
        ],
        out_specs=[fmaj(MLA_QK), tmaj(MLA_QK), fmaj(MLA_V), fmaj(LANES), tmaj(LANES), fmaj(DIFF_V)],
        out_shape=[
            sds((b, hd, MLA_QK, s), BF16), sds((b, hd, s, MLA_QK), BF16), sds((b, hd, MLA_V, s), BF16),
            sds((b, hd, LANES, s), BF16), sds((b, hd, s, LANES), BF16), sds((b, hd, DIFF_V, s), BF16),
        ],
        compiler_params=pltpu.CompilerParams(
            dimension_semantics=("arbitrary", "arbitrary"), vmem_limit_bytes=VMEM_LIMIT),
        name="proj",
    )(x, win, qn, kvn, wuq, wukv, cos4, sin4)


def _softmax_step(s, vj, m_ref, l_ref, acc_ref):
    m_prev = m_ref[...]
    m_new = jnp.maximum(m_prev, jnp.max(s, axis=0, keepdims=True))
    a = jnp.exp(m_prev - m_new)
    p = jnp.exp(s - m_new)
    l_ref[...] = a * l_ref[...] + jnp.sum(p, axis=0, keepdims=True)
    acc_ref[...] = a * acc_ref[...] + _dot(vj, p.astype(BF16))
    m_ref[...] = m_new


def _mla_attn_kernel(qT_ref, k_ref, vT_ref, o_ref, m_ref, l_ref, acc_ref, *, ta):
    qi = pl.program_id(2)
    qT = qT_ref[0, 0]
    m_ref[...] = jnp.full(m_ref.shape, NEG_BIG, F32)
    l_ref[...] = jnp.zeros(l_ref.shape, F32)
    acc_ref[...] = jnp.zeros(acc_ref.shape, F32)

    def step(j, masked):
        off = pl.multiple_of(j * ta, ta)
        s = _dot(k_ref[0, 0, pl.ds(off, ta), :], qT)
        if masked:
            kpos = lax.broadcasted_iota(jnp.int32, (ta, ta), 0)
            qpos = lax.broadcasted_iota(jnp.int32, (ta, ta), 1)
            s = jnp.where(kpos <= qpos, s, NEG_BIG)
        _softmax_step(s, vT_ref[0, 0, :, pl.ds(off, ta)], m_ref, l_ref, acc_ref)

    def body(j, carry):
        step(j, False)
        return carry

    lax.fori_loop(0, qi, body, 0)
    step(qi, True)
    out = acc_ref[...] * (1.0 / l_ref[...])
    o_ref[0] = out.T.astype(BF16)


def _mla_attn(qT, k, vT, ta):
    b, hd, _, s = qT.shape
    return pl.pallas_call(
        functools.partial(_mla_attn_kernel, ta=ta),
        grid=(b, hd, s // ta),
        in_specs=[
            pl.BlockSpec((1, 1, MLA_QK, ta), lambda bi, h, i: (bi, h, 0, i)),
            pl.BlockSpec((1, 1, s, MLA_QK), lambda bi, h, i: (bi, h, 0, 0)),
            pl.BlockSpec((1, 1, MLA_V, s), lambda bi, h, i: (bi, h, 0, 0)),
        ],
        out_specs=pl.BlockSpec((1, ta, MLA_V), lambda bi, h, i: (bi, i, h)),
        out_shape=jax.ShapeDtypeStruct((b, s, hd * MLA_V), BF16),
        scratch_shapes=[pltpu.VMEM((1, ta), F32), pltpu.VMEM((1, ta), F32),
                        pltpu.VMEM((MLA_V, ta), F32)],
        compiler_params=pltpu.CompilerParams(
            dimension_semantics=("arbitrary",) * 3, vmem_limit_bytes=VMEM_LIMIT),
        name="mla_attn",
    )(qT, k, vT)


def _diff_attn_kernel(cfar_ref, qT_ref, k_ref, vT_ref, bias_ref, lamp_ref, g_ref, o_ref,
                      m_ref, l_ref, acc_ref, *, ta, lambda_init):
    h = pl.program_id(1)
    qi = pl.program_id(2)
    qT = qT_ref[0, 0]
    zeros = jnp.zeros((DIFF_QK, ta), BF16)
    qs = (jnp.concatenate([qT[:DIFF_QK], zeros], axis=0),
          jnp.concatenate([zeros, qT[DIFF_QK:]], axis=0))
    m_ref[...] = jnp.full(m_ref.shape, NEG_BIG, F32)
    l_ref[...] = jnp.zeros(l_ref.shape, F32)
    acc_ref[...] = jnp.zeros(acc_ref.shape, F32)
    cfar = cfar_ref[h]

    def step(j, bias_slot):
        off = pl.multiple_of(j * ta, ta)
        kj = k_ref[0, 0, pl.ds(off, ta), :]
        vj = vT_ref[0, 0, :, pl.ds(off, ta)]
        for c in range(2):
            s = _dot(kj, qs[c])
            s = s + (cfar if bias_slot is None else bias_ref[0, bias_slot])
            _softmax_step(s, vj, m_ref.at[c], l_ref.at[c], acc_ref.at[c])

    def body(j, carry):
        step(j, None)
        return carry

    lax.fori_loop(0, jnp.maximum(qi - 1, 0), body, 0)

    @pl.when(qi >= 1)
    def _():
        step(qi - 1, 1)

    step(qi, 0)

    lp = lamp_ref[...]
    lam = (jnp.exp(jnp.sum(lp[0:1] * lp[1:2], axis=-1, keepdims=True))
           - jnp.exp(jnp.sum(lp[2:3] * lp[3:4], axis=-1, keepdims=True)) + lambda_init)
    out = acc_ref[0] * (1.0 / l_ref[0]) - lam * (acc_ref[1] * (1.0 / l_ref[1]))
    ms = jnp.mean(out * out, axis=0, keepdims=True)
    out = out * lax.rsqrt(ms + 1e-5) * g_ref[...] * (1.0 - lambda_init)
    o_ref[0] = out.T.astype(BF16)


def _diff_attn(cfar, qT, k, vT, bias_tiles, lam_params, g_col, ta, lambda_init):
    b, hd, _, s = qT.shape
    grid_spec = pltpu.PrefetchScalarGridSpec(
        num_scalar_prefetch=1,
        grid=(b, hd, s // ta),
        in_specs=[
            pl.BlockSpec((1, 1, LANES, ta), lambda bi, h, i, cf: (bi, h, 0, i)),
            pl.BlockSpec((1, 1, s, LANES), lambda bi, h, i, cf: (bi, h, 0, 0)),
            pl.BlockSpec((1, 1, DIFF_V, s), lambda bi, h, i, cf: (bi, h, 0, 0)),
            pl.BlockSpec((1, 2, ta, ta), lambda bi, h, i, cf: (h, 0, 0, 0)),
            pl.BlockSpec(lam_params.shape, lambda bi, h, i, cf: (0, 0)),
            pl.BlockSpec(g_col.shape, lambda bi, h, i, cf: (0, 0)),
        ],
        out_specs=pl.BlockSpec((1, ta, DIFF_V), lambda bi, h, i, cf: (bi, i, h)),
        scratch_shapes=[pltpu.VMEM((2, 1, ta), F32), pltpu.VMEM((2, 1, ta), F32),
                        pltpu.VMEM((2, DIFF_V, ta), F32)],
    )
    return pl.pallas_call(
        functools.partial(_diff_attn_kernel, ta=ta, lambda_init=lambda_init),
        grid_spec=grid_spec,
        out_shape=jax.ShapeDtypeStruct((b, s, hd * DIFF_V), BF16),
        compiler_params=pltpu.CompilerParams(
            dimension_semantics=("arbitrary",) * 3, vmem_limit_bytes=VMEM_LIMIT),
        name="diff_attn",
    )(cfar, qT, k, vT, bias_tiles, lam_params, g_col)


def _split_bf16(v):
    hi = v.astype(BF16)
    return hi, (v - hi.astype(F32)).astype(BF16)


def _oproj_kernel(am_ref, ad_ref, x_ref, wo_ref, g_ref, b_ref, *rest, route):
    mix = _dot(am_ref[...], wo_ref[:MLA_W, :]) + _dot(ad_ref[...], wo_ref[MLA_W:, :])
    y = _layer_norm_rows(DN_ALPHA * x_ref[...] + mix, g_ref[...], b_ref[...])
    if not route:
        (o_ref,) = rest
        o_ref[...] = y
        return
    rhi_ref, rlo_ref, o_ref, ob_ref, r_ref = rest
    yhi, ylo = _split_bf16(y)
    logits = _dot(yhi, rhi_ref[...]) + (_dot(yhi, rlo_ref[...]) + _dot(ylo, rhi_ref[...]))
    lane = lax.broadcasted_iota(jnp.int32, logits.shape, 1)
    logits = jnp.where(lane < N_EXPERTS, logits, NEG_BIG)
    m1 = jnp.max(logits, axis=-1, keepdims=True)
    i1 = jnp.min(jnp.where(logits == m1, lane, LANES), axis=-1, keepdims=True)
    rest_l = jnp.where(lane == i1, NEG_BIG, logits)
    m2 = jnp.max(rest_l, axis=-1, keepdims=True)
    i2 = jnp.min(jnp.where(rest_l == m2, lane, LANES), axis=-1, keepdims=True)
    e2 = jnp.exp(m2 - m1)
    g1 = 1.0 / (1.0 + e2)
    g2 = e2 * g1
    rec = jnp.where(lane == 0, i1.astype(F32),
                    jnp.where(lane == 1, i2.astype(F32),
                              jnp.where(lane == 2, g1, jnp.where(lane == 3, g2, 0.0))))
    o_ref[...] = y
    ob_ref[...] = yhi
    r_ref[...] = rec


def _oproj(a_mla, a_diff, x2d, wo, g, b, tm, router=None):
    t, d = x2d.shape
    row = lambda w: pl.BlockSpec((tm, w), lambda i: (i, 0))
    full = lambda shape: pl.BlockSpec(shape, lambda i: (0,) * len(shape))
    in_specs = [row(MLA_W), row(DIFF_W), row(d), full(wo.shape), full(g.shape), full(b.shape)]
    args = [a_mla, a_diff, x2d, wo, g, b]
    out_specs = [row(d)]
    out_shape = [jax.ShapeDtypeStruct((t, d), F32)]
    if router is not None:
        in_specs += [full(router[0].shape), full(router[1].shape)]
        args += list(router)
        out_specs += [row(d), row(LANES)]
        out_shape += [jax.ShapeDtypeStruct((t, d), BF16), jax.ShapeDtypeStruct((t, LANES), F32)]
    res = pl.pallas_call(
        functools.partial(_oproj_kernel, route=router is not None),
        grid=(t // tm,),
        in_specs=in_specs, out_specs=out_specs, out_shape=out_shape,
        compiler_params=pltpu.CompilerParams(
            dimension_semantics=("arbitrary",), vmem_limit_bytes=VMEM_LIMIT),
        name="oproj_route" if router is not None else "oproj",
    )(*args)
    return res if router is not None else res[0]


def _ffn_kernel(x_ref, wg_ref, wu_ref, wd_ref, g_ref, b_ref, o_ref):
    x = x_ref[...]
    xb = x.astype(BF16)
    gate = _dot(xb, wg_ref[...])
    up = _dot(xb, wu_ref[...])
    hid = (gate * jax.nn.sigmoid(gate) * up).astype(BF16)
    f = _dot(hid, wd_ref[...])
    o_ref[...] = _layer_norm_rows(DN_ALPHA * x + f, g_ref[...], b_ref[...])


def _ffn(x2d, wg, wu, wd, g, b, tm):
    t, d = x2d.shape
    row = pl.BlockSpec((tm, d), lambda i: (i, 0))
    const = lambda shape: pl.BlockSpec(shape, lambda i: (0,) * len(shape),
                                       pipeline_mode=pl.Buffered(1))
    return pl.pallas_call(
        _ffn_kernel,
        grid=(t // tm,),
        in_specs=[row, const(wg.shape), const(wu.shape), const(wd.shape),
                  const(g.shape), const(b.shape)],
        out_specs=row,
        out_shape=jax.ShapeDtypeStruct((t, d), F32),
        compiler_params=pltpu.CompilerParams(
            dimension_semantics=("arbitrary",), vmem_limit_bytes=VMEM_LIMIT),
        name="ffn",
    )(x2d, wg, wu, wd, g, b)


def _dispatch_kernel(te_ref, uoff_ref, unc_ref, x_ref, post_ref, o_ref, acc_ref, *, tm, unit, kc):
    i = pl.program_id(0)
    e = te_ref[i]
    t = x_ref.shape[0]
    upt = tm // unit
    for r in range(upt):
        u = i * upt + r
        n = unc_ref[u]
        start0 = uoff_ref[u]
        row_pos = u * unit + lax.broadcasted_iota(jnp.int32, (unit, 1), 0)

        def chunk(c):
            nominal = start0 + c * kc
            off = pl.multiple_of(jnp.minimum(nominal, t - kc), LANES)
            dest = post_ref[pl.ds(e, 1), pl.ds(off, kc)]
            tok = off + lax.broadcasted_iota(jnp.int32, (1, kc), 1)
            dest = jnp.where(tok >= nominal, dest, -1)
            sel = jnp.where(dest == row_pos, 1.0, 0.0).astype(BF16)
            return _dot(sel, x_ref[pl.ds(off, kc), :])

        @pl.when(n > 0)
        def _():
            acc_ref[...] = chunk(0)

        @pl.when(n == 0)
        def _():
            acc_ref[...] = jnp.zeros(acc_ref.shape, F32)

        def more(c, carry):
            acc_ref[...] += chunk(c)
            return carry

        lax.fori_loop(1, n, more, 0)
        o_ref[r * unit:(r + 1) * unit, :] = acc_ref[...].astype(BF16)


def _dispatch(tile_e, uoff, unc, xb, post, nt, tm, unit, kc):
    t, d = xb.shape
    const = lambda shape: pl.BlockSpec(shape, lambda i, *_: (0,) * len(shape),
                                       pipeline_mode=pl.Buffered(1))
    grid_spec = pltpu.PrefetchScalarGridSpec(
        num_scalar_prefetch=3,
        grid=(nt,),
        in_specs=[const(xb.shape), const(post.shape)],
        out_specs=pl.BlockSpec((tm, d), lambda i, *_: (i, 0)),
        scratch_shapes=[pltpu.VMEM((unit, d), F32)],
    )
    return pl.pallas_call(
        functools.partial(_dispatch_kernel, tm=tm, unit=unit, kc=kc),
        grid_spec=grid_spec,
        out_shape=jax.ShapeDtypeStruct((nt * tm, d), BF16),
        compiler_params=pltpu.CompilerParams(
            dimension_semantics=("arbitrary",), vmem_limit_bytes=VMEM_LIMIT),
        name="dispatch",
    )(tile_e, uoff, unc, xb, post)


def _gffn_kernel(te_ref, tv_ref, x_ref, wg_ref, wu_ref, wd_ref, o_ref, acc_ref, *, nf):
    i = pl.program_id(0)
    f = pl.program_id(1)
    valid = tv_ref[i] == 1

    @pl.when(valid)
    def _():
        xb = x_ref[...]
        gate = _dot(xb, wg_ref[0])
        up = _dot(xb, wu_ref[0])
        hid = (gate * jax.nn.sigmoid(gate) * up).astype(BF16)
        part = _dot(hid, wd_ref[0])

        @pl.when(f == 0)
        def _():
            acc_ref[...] = part

        @pl.when(f > 0)
        def _():
            acc_ref[...] += part

    @pl.when(f == nf - 1)
    def _():
        o_ref[...] = jnp.where(valid, acc_ref[...], 0.0).astype(BF16)


def _gffn(tile_e, tile_v, xs, wg, wu, wd, tm, fc):
    p, d = xs.shape
    nt = p // tm
    nf = wg.shape[2] // fc
    fidx = lambda i, f, te, tv: jnp.where(tv[i] == 1, f, nf - 1)
    grid_spec = pltpu.PrefetchScalarGridSpec(
        num_scalar_prefetch=2,
        grid=(nt, nf),
        in_specs=[
            pl.BlockSpec((tm, d), lambda i, f, te, tv: (i, 0)),
            pl.BlockSpec((1, d, fc), lambda i, f, te, tv: (te[i], 0, fidx(i, f, te, tv))),
            pl.BlockSpec((1, d, fc), lambda i, f, te, tv: (te[i], 0, fidx(i, f, te, tv))),
            pl.BlockSpec((1, fc, d), lambda i, f, te, tv: (te[i], fidx(i, f, te, tv), 0)),
        ],
        out_specs=pl.BlockSpec((tm, d), lambda i, f, te, tv: (i, 0)),
        scratch_shapes=[pltpu.VMEM((tm, d), F32)],
    )
    return pl.pallas_call(
        functools.partial(_gffn_kernel, nf=nf),
        grid_spec=grid_spec,
        out_shape=jax.ShapeDtypeStruct((p, d), BF16),
        compiler_params=pltpu.CompilerParams(
            dimension_semantics=("arbitrary", "arbitrary"), vmem_limit_bytes=VMEM_LIMIT),
        name="grouped_ffn",
    )(tile_e, tile_v, xs, wg, wu, wd)


def _combine_kernel(ral_ref, rlo_ref, rhi_ref, x_ref, rec_ref, g_ref, b_ref, ys_ref, o_ref,
                    win_ref, xwin_ref, acc_ref, sems, xsem, *, w):
    j = pl.program_id(0)
    nj = pl.num_programs(0)
    slot = lax.rem(j, 2)
    p_rows = ys_ref.shape[0]

    def win_copy(jj, e, sl):
        r = pl.multiple_of(ral_ref[jj * N_EXPERTS + e], 16)
        return pltpu.make_async_copy(ys_ref.at[pl.ds(r, w)], win_ref.at[sl, pl.ds(e * w, w)],
                                     sems.at[sl, e])

    @pl.when(j == 0)
    def _():
        for e in range(N_EXPERTS):
            win_copy(0, e, 0).start()

    @pl.when(j + 1 < nj)
    def _():
        for e in range(N_EXPERTS):
            win_copy(j + 1, e, 1 - slot).start()

    rec = rec_ref[...]
    p1, p2, g1, g2 = rec[:, 0:1], rec[:, 1:2], rec[:, 2:3], rec[:, 3:4]

    def sel(base, lo, hi):
        rows = base + lax.broadcasted_iota(jnp.int32, (1, w), 1)
        rp = jnp.where((rows >= lo) & (rows < hi), rows, -1).astype(F32)
        return (jnp.where(p1 == rp, g1, 0.0) + jnp.where(p2 == rp, g2, 0.0)).astype(BF16)

    pieces = []
    for e in range(N_EXPERTS):
        win_copy(j, e, slot).wait()
        k = j * N_EXPERTS + e
        pieces.append(sel(ral_ref[k], rlo_ref[k], rhi_ref[k]))
    acc_ref[...] = _dot(jnp.concatenate(pieces, axis=1), win_ref[slot])

    for e in range(N_EXPERTS):
        k = j * N_EXPERTS + e
        nwin = (rhi_ref[k] - ral_ref[k] + (w - 1)) // w

        def extra(wi, carry, k=k):
            nominal = ral_ref[k] + wi * w
            base = pl.multiple_of(jnp.minimum(nominal, p_rows - w), 16)
            cp = pltpu.make_async_copy(ys_ref.at[pl.ds(base, w)], xwin_ref, xsem)
            cp.start()
            cp.wait()
            acc_ref[...] += _dot(sel(base, nominal, rhi_ref[k]), xwin_ref[...])
            return carry

        lax.fori_loop(1, nwin, extra, 0)

    o_ref[...] = _layer_norm_rows(DN_ALPHA * x_ref[...] + acc_ref[...], g_ref[...], b_ref[...])


def _combine(ral, rlo, rhi, x2d, rec2, g, b, ys, tm, w):
    t, d = x2d.shape
    row = lambda c: pl.BlockSpec((tm, c), lambda i, *_: (i, 0))
    full = lambda shape: pl.BlockSpec(shape, lambda i, *_: (0,) * len(shape))
    grid_spec = pltpu.PrefetchScalarGridSpec(
        num_scalar_prefetch=3,
        grid=(t // tm,),
        in_specs=[row(d), row(rec2.shape[1]), full(g.shape), full(b.shape),
                  pl.BlockSpec(memory_space=pl.ANY)],
        out_specs=row(d),
        scratch_shapes=[pltpu.VMEM((2, N_EXPERTS * w, d), BF16), pltpu.VMEM((w, d), BF16),
                        pltpu.VMEM((tm, d), F32), pltpu.SemaphoreType.DMA((2, N_EXPERTS)),
                        pltpu.SemaphoreType.DMA(())],
    )
    return pl.pallas_call(
        functools.partial(_combine_kernel, w=w),
        grid_spec=grid_spec,
        out_shape=jax.ShapeDtypeStruct((t, d), F32),
        compiler_params=pltpu.CompilerParams(
            dimension_semantics=("arbitrary",), vmem_limit_bytes=VMEM_LIMIT),
        name="combine",
    )(ral, rlo, rhi, x2d, rec2, g, b, ys)


def _route_plan(rec, nt, tm_e, unit, kc, tm_c, w):
    t = rec.shape[0]
    i32 = jnp.int32
    ar = jnp.arange(N_EXPERTS, dtype=i32)
    m1 = (rec[:, 0].astype(i32)[:, None] == ar[None, :]).astype(i32)
    m2 = (rec[:, 1].astype(i32)[:, None] == ar[None, :]).astype(i32)
    m = m1 + m2
    cinc = jnp.cumsum(m, axis=0)
    counts = cinc[-1]
    padded = ((counts + tm_e - 1) // tm_e) * tm_e
    ends = jnp.cumsum(padded)
    starts = ends - padded
    posmat = starts[None, :] + (cinc - m)
    pos1 = jnp.sum(m1 * posmat, axis=1)
    pos2 = jnp.sum(m2 * posmat, axis=1)
    post = jnp.where(m.T > 0, posmat.T, -1)
    rec2 = jnp.concatenate([pos1.astype(F32)[:, None], pos2.astype(F32)[:, None], rec[:, 2:4],
                            jnp.zeros((t, 4), F32)], axis=1)
    tile_start = jnp.arange(nt, dtype=i32) * tm_e
    tile_e = jnp.sum((tile_start[:, None] >= ends[None, :]).astype(i32), axis=1)
    tile_v = tile_e < N_EXPERTS
    last_e = jnp.max(jnp.where(tile_v, tile_e, 0))
    tile_e = jnp.where(tile_v, tile_e, last_e)
    upt = tm_e // unit
    nu = nt * upt
    oh = (jnp.repeat(tile_e, upt)[:, None] == ar[None, :]).astype(i32)
    q0 = jnp.arange(nu, dtype=i32) * unit - jnp.sum(oh * starts[None, :], axis=1)
    q1 = jnp.minimum(q0 + unit, jnp.sum(oh * counts[None, :], axis=1))
    has = jnp.repeat(tile_v, upt) & (q1 > q0)
    cb = jnp.concatenate([jnp.zeros((1, N_EXPERTS), i32), cinc[LANES - 1::LANES]], axis=0)
    cbu = jnp.sum(oh[:, None, :] * cb[None, :, :], axis=2)
    jlo = jnp.sum((cbu[:, 1:] <= q0[:, None]).astype(i32), axis=1)
    jend = jnp.sum((cbu[:, :-1] < q1[:, None]).astype(i32), axis=1)
    uoff = jnp.where(has, jlo * LANES, 0)
    unc = jnp.where(has, (jend * LANES - uoff + kc - 1) // kc, 0)
    cbt = jnp.concatenate([jnp.zeros((1, N_EXPERTS), i32), cinc[tm_c - 1::tm_c]], axis=0)
    r_lo = starts[None, :] + cbt[:-1]
    r_hi = starts[None, :] + cbt[1:]
    ral = jnp.minimum((r_lo // 16) * 16, nt * tm_e - w)
    return (rec2, post, tile_e, tile_v.astype(i32), uoff, unc,
            ral.reshape(-1), r_lo.reshape(-1), r_hi.reshape(-1))


def _moe(x2d, xb, rec, wg, wu, wd, g, b, tm_c, tm_e, fc, unit, kc, w):
    t, d = x2d.shape
    nt = (2 * t) // tm_e + N_EXPERTS
    rec2, post, tile_e, tile_v, uoff, unc, ral, rlo, rhi = _route_plan(
        rec, nt, tm_e, unit, kc, tm_c, w)
    xs = _dispatch(tile_e, uoff, unc, xb, post, nt, tm_e, unit, kc)
    ys = _gffn(tile_e, tile_v, xs, wg, wu, wd, tm_e, fc)
    return _combine(ral, rlo, rhi, x2d, rec2, g, b, ys, tm_c, w)


def _rotate_half_cols(w):
    half = w.shape[-1] // 2
    return jnp.concatenate([-w[..., half:], w[..., :half]], axis=-1)


def _t5_bucket(n):
    max_exact = NUM_BUCKETS // 2
    nf = jnp.maximum(n, 1).astype(F32)
    large = max_exact + (jnp.log(nf / max_exact) / math.log(MAX_DISTANCE / max_exact)
                         * (NUM_BUCKETS - max_exact)).astype(jnp.int32)
    large = jnp.minimum(large, NUM_BUCKETS - 1)
    return jnp.where(n < max_exact, n, large)


def _pick(n, prefs):
    for p in prefs:
        if n % p == 0:
            return p
    raise ValueError(f"no tile for extent {n}")


def kernel(x, rel_bias, w_in, mla_q_norm, mla_kv_norm, mla_w_uq, mla_w_uk, mla_w_uv, diff_lambda, diff_norm, w_o, ln1_g, ln1_b, ln2_g, ln2_b, ffn_w_gate, ffn_w_up, ffn_w_down, moe_router, moe_w_gate, moe_w_up, moe_w_down):
    b, s, d = x.shape
    t = b * s
    assert d == D_MODEL
    ta = _pick(s, (512, 256, 128))
    assert s >= 2 * ta or s == ta
    tm_proj = _pick(s, (256, 128))
    tm_tok = _pick(t, (256, 128))
    tm_e = _pick(2 * t, (512, 256, 128))
    tm_c = _pick(t, (512, 256))
    unit, wrows = 256, 256
    kc = 1536
    assert tm_e % unit == 0 and t >= kc and t % LANES == 0
    fc = D_FF_EXPERT // 2

    pos = jnp.arange(s, dtype=F32)
    inv = 1.0 / (ROPE_THETA ** (jnp.arange(0, MLA_ROPE, 2, dtype=F32) / MLA_ROPE))
    ang = pos[:, None] * inv[None, :]
    cos4 = jnp.tile(jnp.cos(ang), (1, 2 * MLA_HEADS))
    sin4 = jnp.tile(jnp.sin(ang), (1, 2 * MLA_HEADS))
    bias_dist = rel_bias.astype(F32)[_t5_bucket(jnp.arange(s))]
    kk = jnp.arange(ta)[:, None]
    qq = jnp.arange(ta)[None, :]
    tiles = []
    for off in (0, ta):
        dist = qq + off - kk
        bt = jnp.transpose(bias_dist[jnp.clip(dist, 0, s - 1)], (2, 0, 1))
        tiles.append(jnp.where((dist >= 0)[None], bt, NEG_BIG))
    bias_tiles = jnp.stack(tiles, axis=1)
    assert ta + 1 >= MAX_DISTANCE
    cfar = rel_bias.astype(F32)[NUM_BUCKETS - 1]

    xc = x
    for l in range(DEPTH):
        lambda_init = 0.8 - 0.6 * math.exp(-0.3 * l)
        wl = w_in[l]
        kr_w = wl[:, _KR0:_KR0 + MLA_ROPE]
        win = jnp.concatenate([wl[:, :_KR0 + MLA_ROPE], _rotate_half_cols(kr_w),
                               wl[:, _KR0 + MLA_ROPE:]], axis=1).astype(BF16)
        uq = mla_w_uq[l].reshape(MLA_Q_RANK, MLA_HEADS, MLA_QK)
        uq_n = uq[:, :, :MLA_NOPE].reshape(MLA_Q_RANK, -1)
        uq_r = uq[:, :, MLA_NOPE:]
        wuq = jnp.concatenate([uq_n, uq_r.reshape(MLA_Q_RANK, -1),
                               _rotate_half_cols(uq_r).reshape(MLA_Q_RANK, -1)], axis=1).astype(BF16)
        wukv = jnp.concatenate([mla_w_uk[l], mla_w_uv[l]], axis=1).astype(BF16)

        qT, k, vT, dqT, dk, dvT = _proj(
            xc, win, mla_q_norm[l][None, :], mla_kv_norm[l][None, :], wuq, wukv, cos4, sin4, tm_proj)
        a_mla = _mla_attn(qT, k, vT, ta).reshape(t, MLA_W)
        a_diff = _diff_attn(cfar, dqT, dk, dvT, bias_tiles, diff_lambda[l].astype(F32),
                            diff_norm[l].astype(F32)[:, None], ta, lambda_init).reshape(t, DIFF_W)
        x2d = xc.reshape(t, d)
        wo = w_o[l].astype(BF16)
        g1, b1 = ln1_g[l][None, :], ln1_b[l][None, :]
        g2, b2 = ln2_g[l][None, :], ln2_b[l][None, :]
        if l % 2 == 0:
            x1 = _oproj(a_mla, a_diff, x2d, wo, g1, b1, tm_tok)
            i = l // 2
            x2 = _ffn(x1, ffn_w_gate[i].astype(BF16), ffn_w_up[i].astype(BF16),
                      ffn_w_down[i].astype(BF16), g2, b2, tm_tok)
        else:
            i = l // 2
            rpad = jnp.pad(moe_router[i].astype(F32), ((0, 0), (0, LANES - N_EXPERTS)))
            rhi = rpad.astype(BF16)
            rlo = (rpad - rhi.astype(F32)).astype(BF16)
            x1, x1b, rec = _oproj(a_mla, a_diff, x2d, wo, g1, b1, tm_tok, router=(rhi, rlo))
            x2 = _moe(x1, x1b, rec, moe_w_gate[i].astype(BF16), moe_w_up[i].astype(BF16),
                      moe_w_down[i].astype(BF16), g2, b2, tm_c, tm_e, fc, unit, kc, wrows)
        xc = x2.reshape(b, s, d)
    return xc
```

```python
import functools
import math

import jax
import jax.numpy as jnp
from jax import lax
from jax.experimental import pallas as pl
from jax.experimental.pallas import tpu as pltpu

D_MODEL = 1024
DEPTH = 4
MLA_HEADS = 4
MLA_NOPE = 128
MLA_ROPE = 64
MLA_V = 128
MLA_Q_RANK = 384
MLA_KV_RANK = 256
ROPE_THETA = 10000.0
DIFF_HEADS = 4
DIFF_QK = 64
DIFF_V = 2 * DIFF_QK
NUM_BUCKETS = 32
MAX_DISTANCE = 128
D_FF_DENSE = 2816
N_EXPERTS = 8
D_FF_EXPERT = 3584
DN_ALPHA = (2 * DEPTH) ** 0.25

MLA_QK = MLA_NOPE + MLA_ROPE
MLA_W = MLA_HEADS * MLA_V
DIFF_W = DIFF_HEADS * DIFF_V
_CQ0, _CKV0 = 0, MLA_Q_RANK
_KR0 = _CKV0 + MLA_KV_RANK
_DQ0 = _KR0 + 2 * MLA_ROPE
_DK0 = _DQ0 + DIFF_W
_DV0 = _DK0 + DIFF_W
IN_COLS_W = _DV0 + DIFF_W

LANES = 128
VMEM_LIMIT = 56 * 1024 * 1024
NEG_BIG = -1e30
LOG2E = math.log2(math.e)
ATT_BLK = 256
MLA_HP = 2
DIFF_HP = 1

BF16 = jnp.bfloat16
F32 = jnp.float32


def _dot(a, b):
    return jnp.dot(a, b, preferred_element_type=F32)


def _layer_norm_rows(y, g, b):
    mu = jnp.mean(y, axis=-1, keepdims=True)
    d = y - mu
    var = jnp.mean(d * d, axis=-1, keepdims=True)
    return d * lax.rsqrt(var + 1e-5) * g + b


def _rms_rows(y, g, eps):
    return y * lax.rsqrt(jnp.mean(y * y, axis=-1, keepdims=True) + eps) * g


def _proj_kernel(x_ref, win_ref, qn_ref, kvn_ref, wuq_ref, wukv_ref, cos_ref, sin_ref,
                 qT_ref, k_ref, vT_ref, dqT_ref, dk_ref, dvT_ref):
    x = x_ref[0].astype(BF16)
    h = _dot(x, win_ref[...])
    cqn = _rms_rows(h[:, _CQ0:_CQ0 + MLA_Q_RANK], qn_ref[...], 1e-6)
    ckvn = _rms_rows(h[:, _CKV0:_CKV0 + MLA_KV_RANK], kvn_ref[...], 1e-6)
    q = _dot(cqn.astype(BF16), wuq_ref[...]) * (MLA_QK ** -0.5 * LOG2E)
    kv = _dot(ckvn.astype(BF16), wukv_ref[...])
    cos4, sin4 = cos_ref[...], sin_ref[...]
    nr = MLA_HEADS * MLA_ROPE
    qr = q[:, MLA_W:MLA_W + nr] * cos4 + q[:, MLA_W + nr:] * sin4
    kr = (h[:, _KR0:_KR0 + MLA_ROPE] * cos4[:, :MLA_ROPE]
          + h[:, _KR0 + MLA_ROPE:_KR0 + 2 * MLA_ROPE] * sin4[:, :MLA_ROPE])
    qrT = qr.T
    for hh in range(MLA_HEADS):
        c0 = hh * LANES
        qnT = q[:, c0:c0 + MLA_NOPE].T
        qT_ref[0, hh] = jnp.concatenate(
            [qnT, qrT[hh * MLA_ROPE:(hh + 1) * MLA_ROPE]], axis=0).astype(BF16)
        k_ref[0, hh] = jnp.concatenate([kv[:, c0:c0 + MLA_NOPE], kr], axis=1).astype(BF16)
        vT_ref[0, hh] = kv[:, MLA_W + c0:MLA_W + c0 + MLA_V].T.astype(BF16)
        dqT_ref[0, hh] = (h[:, _DQ0 + c0:_DQ0 + c0 + LANES]
                          * (DIFF_QK ** -0.5 * LOG2E)).T.astype(BF16)
        dk_ref[0, hh] = h[:, _DK0 + c0:_DK0 + c0 + LANES].astype(BF16)
        dvT_ref[0, hh] = h[:, _DV0 + c0:_DV0 + c0 + DIFF_V].T.astype(BF16)


def _proj(x, win, qn, kvn, wuq, wukv, cos4, sin4, tm):
    b, s, d = x.shape
    hd = MLA_HEADS
    full = lambda shape: pl.BlockSpec(shape, lambda bi, i: (0,) * len(shape))
    tmaj = lambda w: pl.BlockSpec((1, hd, tm, w), lambda bi, i: (bi, 0, i, 0))
    fmaj = lambda w: pl.BlockSpec((1, hd, w, tm), lambda bi, i: (bi, 0, 0, i))
    sds = jax.ShapeDtypeStruct
    return pl.pallas_call(
        _proj_kernel,
        grid=(b, s // tm),
        in_specs=[
            pl.BlockSpec((1, tm, d), lambda bi, i: (bi, i, 0)),
            full(win.shape), full(qn.shape), full(kvn.shape), full(wuq.shape), full(wukv.shape),
            pl.BlockSpec((tm, cos4.shape[1]), lambda bi, i: (i, 0)),
            pl.BlockSpec((tm, sin4.shape[1]), lambda bi, i: (i, 0)),
        ],
        out_specs=[fmaj(MLA_QK), tmaj(MLA_QK), fmaj(MLA_V), fmaj(LANES), tmaj(LANES), fmaj(DIFF_V)],
        out_shape=[
            sds((b, hd, MLA_QK, s), BF16), sds((b, hd, s, MLA_QK), BF16), sds((b, hd, MLA_V, s), BF16),
            sds((b, hd, LANES, s), BF16), sds((b, hd, s, LANES), BF16), sds((b, hd, DIFF_V, s), BF16),
        ],
        compiler_params=pltpu.CompilerParams(
            dimension_semantics=("arbitrary", "arbitrary"), vmem_limit_bytes=VMEM_LIMIT),
        name="proj",
    )(x, win, qn, kvn, wuq, wukv, cos4, sin4)


def _softmax_step(s, vj, m_ref, l_ref, acc_ref):
    m_prev = m_ref[...]
    m_new = jnp.maximum(m_prev, jnp.max(s, axis=0, keepdims=True))
    a = jnp.exp2(m_prev - m_new)
    p = jnp.exp2(s - m_new)
    l_ref[...] = a * l_ref[...] + jnp.sum(p, axis=0, keepdims=True)
    acc_ref[...] = a * acc_ref[...] + _dot(vj, p.astype(BF16))
    m_ref[...] = m_new


def _chain_scratch(n, dv):
    return [pltpu.VMEM((1, ATT_BLK), F32)] * (2 * n) + [pltpu.VMEM((dv, ATT_BLK), F32)] * n


def _init_chains(state):
    n = len(state) // 3
    m_refs, l_refs, acc_refs = state[:n], state[n:2 * n], state[2 * n:]
    for m_ref, l_ref, acc_ref in zip(m_refs, l_refs, acc_refs):
        m_ref[...] = jnp.full(m_ref.shape, NEG_BIG, F32)
        l_ref[...] = jnp.zeros(l_ref.shape, F32)
        acc_ref[...] = jnp.zeros(acc_ref.shape, F32)
    return m_refs, l_refs, acc_refs


def _pipelined_tiles(qi, logits, softmax_pv):
    logits(0, 0)

    def body(i, carry):
        j = 2 * i
        logits(j + 1, 1)
        softmax_pv(j, 0, False)
        logits(j + 2, 0)
        softmax_pv(j + 1, 1, False)
        return carry

    lax.fori_loop(0, lax.shift_right_logical(qi, 1), body, 0)
    odd = lax.rem(qi, 2) == 1

    @pl.when(odd)
    def _():
        logits(qi, 1)
        softmax_pv(qi - 1, 0, False)
        softmax_pv(qi, 1, True)

    @pl.when(jnp.logical_not(odd))
    def _():
        softmax_pv(qi, 0, True)


def _mla_attn_kernel(qT_ref, k_ref, vT_ref, o_ref, sa_ref, sb_ref, *state, ta, hp):
    qi = pl.program_id(2)
    m_refs, l_refs, acc_refs = _init_chains(state)
    s_refs = (sa_ref, sb_ref)

    nb = ta // ATT_BLK
    blk = ATT_BLK

    def logits(j, buf):
        off = pl.multiple_of(j * ta, ta)
        for h in range(hp):
            s_refs[buf][h] = _dot(k_ref[0, h, pl.ds(off, ta), :], qT_ref[0, h])

    def softmax_pv(j, buf, diagonal):
        off = pl.multiple_of(j * ta, ta)
        for h in range(hp):
            for qs in range(nb):
                nk = (qs + 1) * blk if diagonal else ta
                s = s_refs[buf][h, :nk, qs * blk:(qs + 1) * blk]
                if diagonal:
                    kpos = lax.broadcasted_iota(jnp.int32, (nk, blk), 0)
                    qpos = lax.broadcasted_iota(jnp.int32, (nk, blk), 1) + qs * blk
                    s = jnp.where(kpos <= qpos, s, NEG_BIG)
                c = h * nb + qs
                _softmax_step(s, vT_ref[0, h, :, pl.ds(off, nk)], m_refs[c], l_refs[c], acc_refs[c])

    _pipelined_tiles(qi, logits, softmax_pv)
    for h in range(hp):
        for qs in range(nb):
            c = h * nb + qs
            out = acc_refs[c][...] * (1.0 / l_refs[c][...])
            o_ref[0, qs * blk:(qs + 1) * blk, h * MLA_V:(h + 1) * MLA_V] = out.T.astype(BF16)


def _mla_attn(qT, k, vT, ta, hp):
    b, hd, _, s = qT.shape
    return pl.pallas_call(
        functools.partial(_mla_attn_kernel, ta=ta, hp=hp),
        grid=(b, hd // hp, s // ta),
        in_specs=[
            pl.BlockSpec((1, hp, MLA_QK, ta), lambda bi, h, i: (bi, h, 0, i)),
            pl.BlockSpec((1, hp, s, MLA_QK), lambda bi, h, i: (bi, h, 0, 0)),
            pl.BlockSpec((1, hp, MLA_V, s), lambda bi, h, i: (bi, h, 0, 0)),
        ],
        out_specs=pl.BlockSpec((1, ta, hp * MLA_V), lambda bi, h, i: (bi, i, h)),
        out_shape=jax.ShapeDtypeStruct((b, s, hd * MLA_V), BF16),
        scratch_shapes=([pltpu.VMEM((hp, ta, ta), F32)] * 2
                        + _chain_scratch(hp * ta // ATT_BLK, MLA_V)),
        compiler_params=pltpu.CompilerParams(
            dimension_semantics=("arbitrary",) * 3, vmem_limit_bytes=VMEM_LIMIT),
        name="mla_attn",
    )(qT, k, vT)


def _diff_attn_kernel(qT_ref, k_ref, vT_ref, bias_ref, lamp_ref, g_ref, o_ref,
                      sa_ref, sb_ref, *state, ta, hp, lambda_init):
    qi = pl.program_id(2)
    s_refs = (sa_ref, sb_ref)
    zeros = jnp.zeros((DIFF_QK, ta), BF16)
    qs = []
    for h in range(hp):
        qT = qT_ref[0, h]
        qs.append((jnp.concatenate([qT[:DIFF_QK], zeros], axis=0),
                   jnp.concatenate([zeros, qT[DIFF_QK:]], axis=0)))
    m_refs, l_refs, acc_refs = _init_chains(state)

    nb = ta // ATT_BLK
    blk = ATT_BLK

    def logits(j, buf):
        off = pl.multiple_of(j * ta, ta)
        for h in range(hp):
            kj = k_ref[0, h, pl.ds(off, ta), :]
            for c in range(2):
                s_refs[buf][2 * h + c] = _dot(kj, qs[h][c])

    def softmax_pv(j, buf, diagonal):
        off = pl.multiple_of(j * ta, ta)
        slot = 0 if diagonal else jnp.where(j == qi - 1, 1, 2)
        for h in range(hp):
            for qb in range(nb):
                nk = (qb + 1) * blk if diagonal else ta
                vj = vT_ref[0, h, :, pl.ds(off, nk)]
                bias = bias_ref[h, slot, :nk, qb * blk:(qb + 1) * blk]
                for c in range(2):
                    s = s_refs[buf][2 * h + c, :nk, qb * blk:(qb + 1) * blk] + bias
                    i = (2 * h + c) * nb + qb
                    _softmax_step(s, vj, m_refs[i], l_refs[i], acc_refs[i])

    _pipelined_tiles(qi, logits, softmax_pv)

    lp = lamp_ref[...]
    lam = (jnp.exp(jnp.sum(lp[0:1] * lp[1:2], axis=-1, keepdims=True))
           - jnp.exp(jnp.sum(lp[2:3] * lp[3:4], axis=-1, keepdims=True)) + lambda_init)
    for h in range(hp):
        for qb in range(nb):
            i0, i1 = (2 * h) * nb + qb, (2 * h + 1) * nb + qb
            out = (acc_refs[i0][...] * (1.0 / l_refs[i0][...])
                   - lam * (acc_refs[i1][...] * (1.0 / l_refs[i1][...])))
            ms = jnp.mean(out * out, axis=0, keepdims=True)
            out = out * lax.rsqrt(ms + 1e-5) * g_ref[...] * (1.0 - lambda_init)
            o_ref[0, qb * blk:(qb + 1) * blk, h * DIFF_V:(h + 1) * DIFF_V] = out.T.astype(BF16)


def _diff_attn(qT, k, vT, bias_tiles, lam_params, g_col, ta, hp, lambda_init):
    b, hd, _, s = qT.shape
    return pl.pallas_call(
        functools.partial(_diff_attn_kernel, ta=ta, hp=hp, lambda_init=lambda_init),
        grid=(b, hd // hp, s // ta),
        in_specs=[
            pl.BlockSpec((1, hp, LANES, ta), lambda bi, h, i: (bi, h, 0, i)),
            pl.BlockSpec((1, hp, s, LANES), lambda bi, h, i: (bi, h, 0, 0)),
            pl.BlockSpec((1, hp, DIFF_V, s), lambda bi, h, i: (bi, h, 0, 0)),
            pl.BlockSpec((hp, 3, ta, ta), lambda bi, h, i: (h, 0, 0, 0)),
            pl.BlockSpec(lam_params.shape, lambda bi, h, i: (0, 0)),
            pl.BlockSpec(g_col.shape, lambda bi, h, i: (0, 0)),
        ],
        out_specs=pl.BlockSpec((1, ta, hp * DIFF_V), lambda bi, h, i: (bi, i, h)),
        out_shape=jax.ShapeDtypeStruct((b, s, hd * DIFF_V), BF16),
        scratch_shapes=([pltpu.VMEM((2 * hp, ta, ta), F32)] * 2
                        + _chain_scratch(2 * hp * ta // ATT_BLK, DIFF_V)),
        compiler_params=pltpu.CompilerParams(
            dimension_semantics=("arbitrary",) * 3, vmem_limit_bytes=VMEM_LIMIT),
        name="diff_attn",
    )(qT, k, vT, bias_tiles, lam_params, g_col)


def _split_bf16(v):
    hi = v.astype(BF16)
    return hi, (v - hi.astype(F32)).astype(BF16)


def _oproj_kernel(am_ref, ad_ref, x_ref, wo_ref, g_ref, b_ref, *rest, route):
    mix = _dot(am_ref[...], wo_ref[:MLA_W, :]) + _dot(ad_ref[...], wo_ref[MLA_W:, :])
    y = _layer_norm_rows(DN_ALPHA * x_ref[...] + mix, g_ref[...], b_ref[...])
    if not route:
        (o_ref,) = rest
        o_ref[...] = y
        return
    rhi_ref, rlo_ref, o_ref, ob_ref, r_ref = rest
    yhi, ylo = _split_bf16(y)
    logits = _dot(yhi, rhi_ref[...]) + (_dot(yhi, rlo_ref[...]) + _dot(ylo, rhi_ref[...]))
    lane = lax.broadcasted_iota(jnp.int32, logits.shape, 1)
    logits = jnp.where(lane < N_EXPERTS, logits, NEG_BIG)
    m1 = jnp.max(logits, axis=-1, keepdims=True)
    i1 = jnp.min(jnp.where(logits == m1, lane, LANES), axis=-1, keepdims=True)
    rest_l = jnp.where(lane == i1, NEG_BIG, logits)
    m2 = jnp.max(rest_l, axis=-1, keepdims=True)
    i2 = jnp.min(jnp.where(rest_l == m2, lane, LANES), axis=-1, keepdims=True)
    e2 = jnp.exp(m2 - m1)
    g1 = 1.0 / (1.0 + e2)
    g2 = e2 * g1
    rec = jnp.where(lane == 0, i1.astype(F32),
                    jnp.where(lane == 1, i2.astype(F32),
                              jnp.where(lane == 2, g1, jnp.where(lane == 3, g2, 0.0))))
    o_ref[...] = y
    ob_ref[...] = yhi
    r_ref[...] = rec


def _oproj(a_mla, a_diff, x2d, wo, g, b, tm, router=None):
    t, d = x2d.shape
    row = lambda w: pl.BlockSpec((tm, w), lambda i: (i, 0))
    full = lambda shape: pl.BlockSpec(shape, lambda i: (0,) * len(shape))
    in_specs = [row(MLA_W), row(DIFF_W), row(d), full(wo.shape), full(g.shape), full(b.shape)]
    args = [a_mla, a_diff, x2d, wo, g, b]
    out_specs = [row(d)]
    out_shape = [jax.ShapeDtypeStruct((t, d), F32)]
    if router is not None:
        in_specs += [full(router[0].shape), full(router[1].shape)]
        args += list(router)
        out_specs += [row(d), row(LANES)]
        out_shape += [jax.ShapeDtypeStruct((t, d), BF16), jax.ShapeDtypeStruct((t, LANES), F32)]
    res = pl.pallas_call(
        functools.partial(_oproj_kernel, route=router is not None),
        grid=(t // tm,),
        in_specs=in_specs, out_specs=out_specs, out_shape=out_shape,
        compiler_params=pltpu.CompilerParams(
            dimension_semantics=("arbitrary",), vmem_limit_bytes=VMEM_LIMIT),
        name="oproj_route" if router is not None else "oproj",
    )(*args)
    return res if router is not None else res[0]


def _ffn_kernel(x_ref, wg_ref, wu_ref, wd_ref, g_ref, b_ref, o_ref):
    x = x_ref[...]
    xb = x.astype(BF16)
    gate = _dot(xb, wg_ref[...])
    up = _dot(xb, wu_ref[...])
    hid = (gate * jax.nn.sigmoid(gate) * up).astype(BF16)
    f = _dot(hid, wd_ref[...])
    o_ref[...] = _layer_norm_rows(DN_ALPHA * x + f, g_ref[...], b_ref[...])


def _ffn(x2d, wg, wu, wd, g, b, tm):
    t, d = x2d.shape
    row = pl.BlockSpec((tm, d), lambda i: (i, 0))
    const = lambda shape: pl.BlockSpec(shape, lambda i: (0,) * len(shape),
                                       pipeline_mode=pl.Buffered(1))
    return pl.pallas_call(
        _ffn_kernel,
        grid=(t // tm,),
        in_specs=[row, const(wg.shape), const(wu.shape), const(wd.shape),
                  const(g.shape), const(b.shape)],
        out_specs=row,
        out_shape=jax.ShapeDtypeStruct((t, d), F32),
        compiler_params=pltpu.CompilerParams(
            dimension_semantics=("arbitrary",), vmem_limit_bytes=VMEM_LIMIT),
        name="ffn",
    )(x2d, wg, wu, wd, g, b)


def _dispatch_kernel(te_ref, uoff_ref, unc_ref, x_ref, post_ref, o_ref, acc_ref, *, tm, unit, kc):
    i = pl.program_id(0)
    e = te_ref[i]
    t = x_ref.shape[0]
    upt = tm // unit
    for r in range(upt):
        u = i * upt + r
        n = unc_ref[u]
        start0 = uoff_ref[u]
        row_pos = u * unit + lax.broadcasted_iota(jnp.int32, (unit, 1), 0)

        def chunk(c):
            nominal = start0 + c * kc
            off = pl.multiple_of(jnp.minimum(nominal, t - kc), LANES)
            dest = post_ref[pl.ds(e, 1), pl.ds(off, kc)]
            tok = off + lax.broadcasted_iota(jnp.int32, (1, kc), 1)
            dest = jnp.where(tok >= nominal, dest, -1)
            sel = jnp.where(dest == row_pos, 1.0, 0.0).astype(BF16)
            return _dot(sel, x_ref[pl.ds(off, kc), :])

        @pl.when(n > 0)
        def _():
            acc_ref[...] = chunk(0)

        @pl.when(n == 0)
        def _():
            acc_ref[...] = jnp.zeros(acc_ref.shape, F32)

        def more(c, carry):
            acc_ref[...] += chunk(c)
            return carry

        lax.fori_loop(1, n, more, 0)
        o_ref[r * unit:(r + 1) * unit, :] = acc_ref[...].astype(BF16)


def _dispatch(tile_e, uoff, unc, xb, post, nt, tm, unit, kc):
    t, d = xb.shape
    const = lambda shape: pl.BlockSpec(shape, lambda i, *_: (0,) * len(shape),
                                       pipeline_mode=pl.Buffered(1))
    grid_spec = pltpu.PrefetchScalarGridSpec(
        num_scalar_prefetch=3,
        grid=(nt,),
        in_specs=[const(xb.shape), const(post.shape)],
        out_specs=pl.BlockSpec((tm, d), lambda i, *_: (i, 0)),
        scratch_shapes=[pltpu.VMEM((unit, d), F32)],
    )
    return pl.pallas_call(
        functools.partial(_dispatch_kernel, tm=tm, unit=unit, kc=kc),
        grid_spec=grid_spec,
        out_shape=jax.ShapeDtypeStruct((nt * tm, d), BF16),
        compiler_params=pltpu.CompilerParams(
            dimension_semantics=("arbitrary",), vmem_limit_bytes=VMEM_LIMIT),
        name="dispatch",
    )(tile_e, uoff, unc, xb, post)


def _gffn_kernel(te_ref, tv_ref, x_ref, wg_ref, wu_ref, wd_ref, o_ref, acc_ref, *, nf):
    i = pl.program_id(0)
    f = pl.program_id(1)
    valid = tv_ref[i] == 1

    @pl.when(valid)
    def _():
        xb = x_ref[...]
        gate = _dot(xb, wg_ref[0, 0])
        up = _dot(xb, wu_ref[0, 0])
        hid = (gate * jax.nn.sigmoid(gate) * up).astype(BF16)
        part = _dot(hid, wd_ref[0, 0])

        @pl.when(f == 0)
        def _():
            acc_ref[...] = part

        @pl.when(f > 0)
        def _():
            acc_ref[...] += part

    @pl.when(f == nf - 1)
    def _():
        o_ref[...] = jnp.where(valid, acc_ref[...], 0.0).astype(BF16)


def _gffn(tile_e, tile_v, xs, li, wg, wu, wd, tm, fc):
    p, d = xs.shape
    nt = p // tm
    nf = wg.shape[3] // fc
    fidx = lambda i, f, te, tv: jnp.where(tv[i] == 1, f, nf - 1)
    grid_spec = pltpu.PrefetchScalarGridSpec(
        num_scalar_prefetch=2,
        grid=(nt, nf),
        in_specs=[
            pl.BlockSpec((tm, d), lambda i, f, te, tv: (i, 0)),
            pl.BlockSpec((1, 1, d, fc), lambda i, f, te, tv: (li, te[i], 0, fidx(i, f, te, tv))),
            pl.BlockSpec((1, 1, d, fc), lambda i, f, te, tv: (li, te[i], 0, fidx(i, f, te, tv))),
            pl.BlockSpec((1, 1, fc, d), lambda i, f, te, tv: (li, te[i], fidx(i, f, te, tv), 0)),
        ],
        out_specs=pl.BlockSpec((tm, d), lambda i, f, te, tv: (i, 0)),
        scratch_shapes=[pltpu.VMEM((tm, d), F32)],
    )
    return pl.pallas_call(
        functools.partial(_gffn_kernel, nf=nf),
        grid_spec=grid_spec,
        out_shape=jax.ShapeDtypeStruct((p, d), BF16),
        compiler_params=pltpu.CompilerParams(
            dimension_semantics=("arbitrary", "arbitrary"), vmem_limit_bytes=VMEM_LIMIT),
        name="grouped_ffn",
    )(tile_e, tile_v, xs, wg, wu, wd)


def _combine_kernel(ral_ref, rlo_ref, rhi_ref, x_ref, rec_ref, g_ref, b_ref, ys_ref, o_ref,
                    win_ref, xwin_ref, acc_ref, sems, xsem, *, w):
    j = pl.program_id(0)
    nj = pl.num_programs(0)
    slot = lax.rem(j, 2)
    p_rows = ys_ref.shape[0]

    def win_copy(jj, e, sl):
        r = pl.multiple_of(ral_ref[jj * N_EXPERTS + e], 16)
        return pltpu.make_async_copy(ys_ref.at[pl.ds(r, w)], win_ref.at[sl, pl.ds(e * w, w)],
                                     sems.at[sl, e])

    @pl.when(j == 0)
    def _():
        for e in range(N_EXPERTS):
            win_copy(0, e, 0).start()

    @pl.when(j + 1 < nj)
    def _():
        for e in range(N_EXPERTS):
            win_copy(j + 1, e, 1 - slot).start()

    rec = rec_ref[...]
    p1, p2, g1, g2 = rec[:, 0:1], rec[:, 1:2], rec[:, 2:3], rec[:, 3:4]

    def sel(base, lo, hi):
        rows = base + lax.broadcasted_iota(jnp.int32, (1, w), 1)
        rp = jnp.where((rows >= lo) & (rows < hi), rows, -1).astype(F32)
        return (jnp.where(p1 == rp, g1, 0.0) + jnp.where(p2 == rp, g2, 0.0)).astype(BF16)

    pieces = []
    for e in range(N_EXPERTS):
        win_copy(j, e, slot).wait()
        k = j * N_EXPERTS + e
        pieces.append(sel(ral_ref[k], rlo_ref[k], rhi_ref[k]))
    acc_ref[...] = _dot(jnp.concatenate(pieces, axis=1), win_ref[slot])

    for e in range(N_EXPERTS):
        k = j * N_EXPERTS + e
        nwin = (rhi_ref[k] - ral_ref[k] + (w - 1)) // w

        def extra(wi, carry, k=k):
            nominal = ral_ref[k] + wi * w
            base = pl.multiple_of(jnp.minimum(nominal, p_rows - w), 16)
            cp = pltpu.make_async_copy(ys_ref.at[pl.ds(base, w)], xwin_ref, xsem)
            cp.start()
            cp.wait()
            acc_ref[...] += _dot(sel(base, nominal, rhi_ref[k]), xwin_ref[...])
            return carry

        lax.fori_loop(1, nwin, extra, 0)

    o_ref[...] = _layer_norm_rows(DN_ALPHA * x_ref[...] + acc_ref[...], g_ref[...], b_ref[...])


def _combine(ral, rlo, rhi, x2d, rec2, g, b, ys, tm, w):
    t, d = x2d.shape
    row = lambda c: pl.BlockSpec((tm, c), lambda i, *_: (i, 0))
    full = lambda shape: pl.BlockSpec(shape, lambda i, *_: (0,) * len(shape))
    grid_spec = pltpu.PrefetchScalarGridSpec(
        num_scalar_prefetch=3,
        grid=(t // tm,),
        in_specs=[row(d), row(rec2.shape[1]), full(g.shape), full(b.shape),
                  pl.BlockSpec(memory_space=pl.ANY)],
        out_specs=row(d),
        scratch_shapes=[pltpu.VMEM((2, N_EXPERTS * w, d), BF16), pltpu.VMEM((w, d), BF16),
                        pltpu.VMEM((tm, d), F32), pltpu.SemaphoreType.DMA((2, N_EXPERTS)),
                        pltpu.SemaphoreType.DMA(())],
    )
    return pl.pallas_call(
        functools.partial(_combine_kernel, w=w),
        grid_spec=grid_spec,
        out_shape=jax.ShapeDtypeStruct((t, d), F32),
        compiler_params=pltpu.CompilerParams(
            dimension_semantics=("arbitrary",), vmem_limit_bytes=VMEM_LIMIT),
        name="combine",
    )(ral, rlo, rhi, x2d, rec2, g, b, ys)


def _route_plan(rec, nt, tm_e, unit, kc, tm_c, w):
    t = rec.shape[0]
    i32 = jnp.int32
    ar = jnp.arange(N_EXPERTS, dtype=i32)
    m1 = (rec[:, 0].astype(i32)[:, None] == ar[None, :]).astype(i32)
    m2 = (rec[:, 1].astype(i32)[:, None] == ar[None, :]).astype(i32)
    m = m1 + m2
    cinc = jnp.cumsum(m, axis=0)
    counts = cinc[-1]
    padded = ((counts + tm_e - 1) // tm_e) * tm_e
    ends = jnp.cumsum(padded)
    starts = ends - padded
    posmat = starts[None, :] + (cinc - m)
    pos1 = jnp.sum(m1 * posmat, axis=1)
    pos2 = jnp.sum(m2 * posmat, axis=1)
    post = jnp.where(m.T > 0, posmat.T, -1)
    rec2 = jnp.concatenate([pos1.astype(F32)[:, None], pos2.astype(F32)[:, None], rec[:, 2:4],
                            jnp.zeros((t, 4), F32)], axis=1)
    tile_start = jnp.arange(nt, dtype=i32) * tm_e
    tile_e = jnp.sum((tile_start[:, None] >= ends[None, :]).astype(i32), axis=1)
    tile_v = tile_e < N_EXPERTS
    last_e = jnp.max(jnp.where(tile_v, tile_e, 0))
    tile_e = jnp.where(tile_v, tile_e, last_e)
    upt = tm_e // unit
    nu = nt * upt
    oh = (jnp.repeat(tile_e, upt)[:, None] == ar[None, :]).astype(i32)
    q0 = jnp.arange(nu, dtype=i32) * unit - jnp.sum(oh * starts[None, :], axis=1)
    q1 = jnp.minimum(q0 + unit, jnp.sum(oh * counts[None, :], axis=1))
    has = jnp.repeat(tile_v, upt) & (q1 > q0)
    cb = jnp.concatenate([jnp.zeros((1, N_EXPERTS), i32), cinc[LANES - 1::LANES]], axis=0)
    cbu = jnp.sum(oh[:, None, :] * cb[None, :, :], axis=2)
    jlo = jnp.sum((cbu[:, 1:] <= q0[:, None]).astype(i32), axis=1)
    jend = jnp.sum((cbu[:, :-1] < q1[:, None]).astype(i32), axis=1)
    uoff = jnp.where(has, jlo * LANES, 0)
    unc = jnp.where(has, (jend * LANES - uoff + kc - 1) // kc, 0)
    cbt = jnp.concatenate([jnp.zeros((1, N_EXPERTS), i32), cinc[tm_c - 1::tm_c]], axis=0)
    r_lo = starts[None, :] + cbt[:-1]
    r_hi = starts[None, :] + cbt[1:]
    ral = jnp.minimum((r_lo // 16) * 16, nt * tm_e - w)
    return (rec2, post, tile_e, tile_v.astype(i32), uoff, unc,
            ral.reshape(-1), r_lo.reshape(-1), r_hi.reshape(-1))


def _moe(x2d, xb, rec, li, wg, wu, wd, g, b, tm_c, tm_e, fc, unit, kc, w):
    t, d = x2d.shape
    nt = (2 * t) // tm_e + N_EXPERTS
    rec2, post, tile_e, tile_v, uoff, unc, ral, rlo, rhi = _route_plan(
        rec, nt, tm_e, unit, kc, tm_c, w)
    xs = _dispatch(tile_e, uoff, unc, xb, post, nt, tm_e, unit, kc)
    ys = _gffn(tile_e, tile_v, xs, li, wg, wu, wd, tm_e, fc)
    return _combine(ral, rlo, rhi, x2d, rec2, g, b, ys, tm_c, w)


def _rotate_half_cols(w):
    half = w.shape[-1] // 2
    return jnp.concatenate([-w[..., half:], w[..., :half]], axis=-1)


def _t5_bucket(n):
    max_exact = NUM_BUCKETS // 2
    nf = jnp.maximum(n, 1).astype(F32)
    large = max_exact + (jnp.log(nf / max_exact) / math.log(MAX_DISTANCE / max_exact)
                         * (NUM_BUCKETS - max_exact)).astype(jnp.int32)
    large = jnp.minimum(large, NUM_BUCKETS - 1)
    return jnp.where(n < max_exact, n, large)


def _pick(n, prefs):
    for p in prefs:
        if n % p == 0:
            return p
    raise ValueError(f"no tile for extent {n}")


def kernel(x, rel_bias, w_in, mla_q_norm, mla_kv_norm, mla_w_uq, mla_w_uk, mla_w_uv, diff_lambda, diff_norm, w_o, ln1_g, ln1_b, ln2_g, ln2_b, ffn_w_gate, ffn_w_up, ffn_w_down, moe_router, moe_w_gate, moe_w_up, moe_w_down):
    b, s, d = x.shape
    t = b * s
    assert d == D_MODEL
    ta = _pick(s, (512, 256))
    assert ta % ATT_BLK == 0
    assert s >= 2 * ta or s == ta
    tm_proj = _pick(s, (256, 128))
    tm_tok = _pick(t, (256, 128))
    tm_e = _pick(2 * t, (512, 256, 128))
    tm_c = _pick(t, (512, 256))
    unit, wrows = 256, 256
    kc = 1536
    assert tm_e % unit == 0 and t >= kc and t % LANES == 0
    fc = D_FF_EXPERT // 2

    pos = jnp.arange(s, dtype=F32)
    inv = 1.0 / (ROPE_THETA ** (jnp.arange(0, MLA_ROPE, 2, dtype=F32) / MLA_ROPE))
    ang = pos[:, None] * inv[None, :]
    cos4 = jnp.tile(jnp.cos(ang), (1, 2 * MLA_HEADS))
    sin4 = jnp.tile(jnp.sin(ang), (1, 2 * MLA_HEADS))
    rb = rel_bias.astype(F32)
    kk = jnp.arange(ta)[:, None]
    qq = jnp.arange(ta)[None, :]
    tiles = []
    for off in (0, ta):
        dist = qq + off - kk
        bucket = _t5_bucket(jnp.maximum(dist, 0))
        bt = jnp.zeros((DIFF_HEADS, ta, ta), F32)
        for bk in range(NUM_BUCKETS):
            bt = bt + jnp.where((bucket == bk)[None], rb[bk][:, None, None], 0.0)
        tiles.append(jnp.where((dist >= 0)[None], bt * LOG2E, NEG_BIG))
    assert ta + 1 >= MAX_DISTANCE
    tiles.append(jnp.broadcast_to((rb[NUM_BUCKETS - 1] * LOG2E)[:, None, None],
                                  (DIFF_HEADS, ta, ta)))
    bias_tiles = jnp.stack(tiles, axis=1)
    moe_wg, moe_wu, moe_wd = (w.astype(BF16) for w in (moe_w_gate, moe_w_up, moe_w_down))

    xc = x
    for l in range(DEPTH):
        lambda_init = 0.8 - 0.6 * math.exp(-0.3 * l)
        wl = w_in[l]
        kr_w = wl[:, _KR0:_KR0 + MLA_ROPE]
        win = jnp.concatenate([wl[:, :_KR0 + MLA_ROPE], _rotate_half_cols(kr_w),
                               wl[:, _KR0 + MLA_ROPE:]], axis=1).astype(BF16)
        uq = mla_w_uq[l].reshape(MLA_Q_RANK, MLA_HEADS, MLA_QK)
        uq_n = uq[:, :, :MLA_NOPE].reshape(MLA_Q_RANK, -1)
        uq_r = uq[:, :, MLA_NOPE:]
        wuq = jnp.concatenate([uq_n, uq_r.reshape(MLA_Q_RANK, -1),
                               _rotate_half_cols(uq_r).reshape(MLA_Q_RANK, -1)], axis=1).astype(BF16)
        wukv = jnp.concatenate([mla_w_uk[l], mla_w_uv[l]], axis=1).astype(BF16)

        qT, k, vT, dqT, dk, dvT = _proj(
            xc, win, mla_q_norm[l][None, :], mla_kv_norm[l][None, :], wuq, wukv, cos4, sin4, tm_proj)
        a_mla = _mla_attn(qT, k, vT, ta, MLA_HP).reshape(t, MLA_W)
        a_diff = _diff_attn(dqT, dk, dvT, bias_tiles, diff_lambda[l].astype(F32),
                            diff_norm[l].astype(F32)[:, None], ta, DIFF_HP,
                            lambda_init).reshape(t, DIFF_W)
        x2d = xc.reshape(t, d)
        wo = w_o[l].astype(BF16)
        g1, b1 = ln1_g[l][None, :], ln1_b[l][None, :]
        g2, b2 = ln2_g[l][None, :], ln2_b[l][None, :]
        if l % 2 == 0:
            x1 = _oproj(a_mla, a_diff, x2d, wo, g1, b1, tm_tok)
            i = l // 2
            x2 = _ffn(x1, ffn_w_gate[i].astype(BF16), ffn_w_up[i].astype(BF16),
                      ffn_w_down[i].astype(BF16), g2, b2, tm_tok)
        else:
            i = l // 2
            rpad = jnp.pad(moe_router[i].astype(F32), ((0, 0), (0, LANES - N_EXPERTS)))
            rhi = rpad.astype(BF16)
            rlo = (rpad - rhi.astype(F32)).astype(BF16)
            x1, x1b, rec = _oproj(a_mla, a_diff, x2d, wo, g1, b1, tm_tok, router=(rhi, rlo))
            x2 = _moe(x1, x1b, rec, i, moe_wg, moe_wu, moe_wd, g2, b2,
                      tm_c, tm_e, fc, unit, kc, wrows)
        xc = x2.reshape(b, s, d)
    return xc
```

```python
import functools
import math

import jax
import jax.numpy as jnp
from jax import lax
from jax.experimental import pallas as pl
from jax.experimental.pallas import tpu as pltpu

D_MODEL = 1024
DEPTH = 4
MLA_HEADS = 4
MLA_NOPE = 128
MLA_ROPE = 64
MLA_V = 128
MLA_Q_RANK = 384
MLA_KV_RANK = 256
ROPE_THETA = 10000.0
DIFF_HEADS = 4
DIFF_QK = 64
DIFF_V = 2 * DIFF_QK
NUM_BUCKETS = 32
MAX_DISTANCE = 128
D_FF_DENSE = 2816
N_EXPERTS = 8
D_FF_EXPERT = 3584
DN_ALPHA = (2 * DEPTH) ** 0.25

MLA_QK = MLA_NOPE + MLA_ROPE
MLA_W = MLA_HEADS * MLA_V
DIFF_W = DIFF_HEADS * DIFF_V
DIFF_VA = DIFF_V + 16
_CQ0, _CKV0 = 0, MLA_Q_RANK
_KR0 = _CKV0 + MLA_KV_RANK
_DQ0 = _KR0 + 2 * MLA_ROPE
_DK0 = _DQ0 + DIFF_W
_DV0 = _DK0 + DIFF_W
IN_COLS_W = _DV0 + DIFF_W

LANES = 128
VMEM_LIMIT = 56 * 1024 * 1024
NEG_BIG = -1e30
LOG2E = math.log2(math.e)
ATT_BLK = 256
MLA_HP = 2
DIFF_HP = 1

BF16 = jnp.bfloat16
F32 = jnp.float32


def _dot(a, b):
    return jnp.dot(a, b, preferred_element_type=F32)


def _layer_norm_rows(y, g, b):
    mu = jnp.mean(y, axis=-1, keepdims=True)
    d = y - mu
    var = jnp.mean(d * d, axis=-1, keepdims=True)
    return d * lax.rsqrt(var + 1e-5) * g + b


def _rms_rows(y, g, eps):
    return y * lax.rsqrt(jnp.mean(y * y, axis=-1, keepdims=True) + eps) * g


def _proj_kernel(x_ref, win_ref, qn_ref, kvn_ref, wuq_ref, wukv_ref, cos_ref, sin_ref,
                 qT_ref, k_ref, vT_ref, dqT_ref, dk_ref, dvT_ref):
    x = x_ref[0].astype(BF16)
    h = _dot(x, win_ref[...])
    cqn = _rms_rows(h[:, _CQ0:_CQ0 + MLA_Q_RANK], qn_ref[...], 1e-6)
    ckvn = _rms_rows(h[:, _CKV0:_CKV0 + MLA_KV_RANK], kvn_ref[...], 1e-6)
    q = _dot(cqn.astype(BF16), wuq_ref[...]) * (MLA_QK ** -0.5 * LOG2E)
    kv = _dot(ckvn.astype(BF16), wukv_ref[...])
    cos4, sin4 = cos_ref[...], sin_ref[...]
    nr = MLA_HEADS * MLA_ROPE
    qr = q[:, MLA_W:MLA_W + nr] * cos4 + q[:, MLA_W + nr:] * sin4
    kr = (h[:, _KR0:_KR0 + MLA_ROPE] * cos4[:, :MLA_ROPE]
          + h[:, _KR0 + MLA_ROPE:_KR0 + 2 * MLA_ROPE] * sin4[:, :MLA_ROPE])
    qrT = qr.T
    for hh in range(MLA_HEADS):
        c0 = hh * LANES
        qnT = q[:, c0:c0 + MLA_NOPE].T
        qT_ref[0, hh] = jnp.concatenate(
            [qnT, qrT[hh * MLA_ROPE:(hh + 1) * MLA_ROPE]], axis=0).astype(BF16)
        k_ref[0, hh] = jnp.concatenate([kv[:, c0:c0 + MLA_NOPE], kr], axis=1).astype(BF16)
        vT_ref[0, hh] = kv[:, MLA_W + c0:MLA_W + c0 + MLA_V].T.astype(BF16)
        dqT_ref[0, hh] = (h[:, _DQ0 + c0:_DQ0 + c0 + LANES]
                          * (DIFF_QK ** -0.5 * LOG2E)).T.astype(BF16)
        dk_ref[0, hh] = h[:, _DK0 + c0:_DK0 + c0 + LANES].astype(BF16)
        dvT_ref[0, hh] = jnp.concatenate(
            [h[:, _DV0 + c0:_DV0 + c0 + DIFF_V].T, jnp.ones((DIFF_VA - DIFF_V, x.shape[0]), F32)],
            axis=0).astype(BF16)


def _proj(x, win, qn, kvn, wuq, wukv, cos4, sin4, tm):
    b, s, d = x.shape
    hd = MLA_HEADS
    full = lambda shape: pl.BlockSpec(shape, lambda bi, i: (0,) * len(shape))
    tmaj = lambda w: pl.BlockSpec((1, hd, tm, w), lambda bi, i: (bi, 0, i, 0))
    fmaj = lambda w: pl.BlockSpec((1, hd, w, tm), lambda bi, i: (bi, 0, 0, i))
    sds = jax.ShapeDtypeStruct
    return pl.pallas_call(
        _proj_kernel,
        grid=(b, s // tm),
        in_specs=[
            pl.BlockSpec((1, tm, d), lambda bi, i: (bi, i, 0)),
            full(win.shape), full(qn.shape), full(kvn.shape), full(wuq.shape), full(wukv.shape),
            pl.BlockSpec((tm, cos4.shape[1]), lambda bi, i: (i, 0)),
            pl.BlockSpec((tm, sin4.shape[1]), lambda bi, i: (i, 0)),
        ],
        out_specs=[fmaj(MLA_QK), tmaj(MLA_QK), fmaj(MLA_V), fmaj(LANES), tmaj(LANES), fmaj(DIFF_VA)],
        out_shape=[
            sds((b, hd, MLA_QK, s), BF16), sds((b, hd, s, MLA_QK), BF16), sds((b, hd, MLA_V, s), BF16),
            sds((b, hd, LANES, s), BF16), sds((b, hd, s, LANES), BF16), sds((b, hd, DIFF_VA, s), BF16),
        ],
        compiler_params=pltpu.CompilerParams(
            dimension_semantics=("arbitrary", "arbitrary"), vmem_limit_bytes=VMEM_LIMIT),
        name="proj",
    )(x, win, qn, kvn, wuq, wukv, cos4, sin4)


def _softmax_step(s, vj, m_ref, l_ref, acc_ref, shift=None):
    m_prev = m_ref[...]
    smax = jnp.max(s, axis=0, keepdims=True)
    if shift is not None:
        smax = smax + shift
    m_new = jnp.maximum(m_prev, smax)
    a = jnp.exp2(m_prev - m_new)
    p = jnp.exp2(s - (m_new if shift is None else m_new - shift))
    if l_ref is not None:
        l_ref[...] = a * l_ref[...] + jnp.sum(p, axis=0, keepdims=True)
    acc_ref[...] = a * acc_ref[...] + _dot(vj, p.astype(BF16))
    m_ref[...] = m_new


def _chain_scratch(n, dv):
    return [pltpu.VMEM((1, ATT_BLK), F32)] * (2 * n) + [pltpu.VMEM((dv, ATT_BLK), F32)] * n


def _init_chains(state):
    n = len(state) // 3
    m_refs, l_refs, acc_refs = state[:n], state[n:2 * n], state[2 * n:]
    for m_ref, l_ref, acc_ref in zip(m_refs, l_refs, acc_refs):
        m_ref[...] = jnp.full(m_ref.shape, NEG_BIG, F32)
        l_ref[...] = jnp.zeros(l_ref.shape, F32)
        acc_ref[...] = jnp.zeros(acc_ref.shape, F32)
    return m_refs, l_refs, acc_refs


def _pipelined_tiles(qi, logits, softmax_pv):
    logits(0, 0)

    def body(i, carry):
        j = 2 * i
        logits(j + 1, 1)
        softmax_pv(j, 0, False)
        logits(j + 2, 0)
        softmax_pv(j + 1, 1, False)
        return carry

    lax.fori_loop(0, lax.shift_right_logical(qi, 1), body, 0)
    odd = lax.rem(qi, 2) == 1

    @pl.when(odd)
    def _():
        logits(qi, 1)
        softmax_pv(qi - 1, 0, False)
        softmax_pv(qi, 1, True)

    @pl.when(jnp.logical_not(odd))
    def _():
        softmax_pv(qi, 0, True)


def _pipelined_tiles_biased(qi, logits, softmax_pv):
    logits(0, 0)
    nfar = jnp.maximum(qi - 1, 0)

    def body(i, carry):
        j = 2 * i
        logits(j + 1, 1)
        softmax_pv(j, 0, "far")
        logits(j + 2, 0)
        softmax_pv(j + 1, 1, "far")
        return carry

    lax.fori_loop(0, lax.shift_right_logical(nfar, 1), body, 0)
    odd = lax.rem(nfar, 2) == 1

    @pl.when(qi == 0)
    def _():
        softmax_pv(qi, 0, "diag")

    @pl.when(jnp.logical_and(qi >= 1, jnp.logical_not(odd)))
    def _():
        logits(qi, 1)
        softmax_pv(qi - 1, 0, "near")
        softmax_pv(qi, 1, "diag")

    @pl.when(odd)
    def _():
        logits(qi - 1, 1)
        softmax_pv(qi - 2, 0, "far")
        logits(qi, 0)
        softmax_pv(qi - 1, 1, "near")
        softmax_pv(qi, 0, "diag")


def _mla_attn_kernel(qT_ref, k_ref, vT_ref, o_ref, sa_ref, sb_ref, *state, ta, hp):
    qi = pl.program_id(2)
    m_refs, l_refs, acc_refs = _init_chains(state)
    s_refs = (sa_ref, sb_ref)

    nb = ta // ATT_BLK
    blk = ATT_BLK

    def logits(j, buf):
        off = pl.multiple_of(j * ta, ta)
        for h in range(hp):
            s_refs[buf][h] = _dot(k_ref[0, h, pl.ds(off, ta), :], qT_ref[0, h])

    def softmax_pv(j, buf, diagonal):
        off = pl.multiple_of(j * ta, ta)
        for h in range(hp):
            for qs in range(nb):
                nk = (qs + 1) * blk if diagonal else ta
                s = s_refs[buf][h, :nk, qs * blk:(qs + 1) * blk]
                if diagonal:
                    kpos = lax.broadcasted_iota(jnp.int32, (nk, blk), 0)
                    qpos = lax.broadcasted_iota(jnp.int32, (nk, blk), 1) + qs * blk
                    s = jnp.where(kpos <= qpos, s, NEG_BIG)
                c = h * nb + qs
                _softmax_step(s, vT_ref[0, h, :, pl.ds(off, nk)], m_refs[c], l_refs[c], acc_refs[c])

    _pipelined_tiles(qi, logits, softmax_pv)
    for h in range(hp):
        for qs in range(nb):
            c = h * nb + qs
            out = acc_refs[c][...] * (1.0 / l_refs[c][...])
            o_ref[0, qs * blk:(qs + 1) * blk, h * MLA_V:(h + 1) * MLA_V] = out.T.astype(BF16)


def _mla_attn(qT, k, vT, ta, hp):
    b, hd, _, s = qT.shape
    return pl.pallas_call(
        functools.partial(_mla_attn_kernel, ta=ta, hp=hp),
        grid=(b, hd // hp, s // ta),
        in_specs=[
            pl.BlockSpec((1, hp, MLA_QK, ta), lambda bi, h, i: (bi, h, 0, i)),
            pl.BlockSpec((1, hp, s, MLA_QK), lambda bi, h, i: (bi, h, 0, 0)),
            pl.BlockSpec((1, hp, MLA_V, s), lambda bi, h, i: (bi, h, 0, 0)),
        ],
        out_specs=pl.BlockSpec((1, ta, hp * MLA_V), lambda bi, h, i: (bi, i, h)),
        out_shape=jax.ShapeDtypeStruct((b, s, hd * MLA_V), BF16),
        scratch_shapes=([pltpu.VMEM((hp, ta, ta), F32)] * 2
                        + _chain_scratch(hp * ta // ATT_BLK, MLA_V)),
        compiler_params=pltpu.CompilerParams(
            dimension_semantics=("arbitrary",) * 3, vmem_limit_bytes=VMEM_LIMIT),
        name="mla_attn",
    )(qT, k, vT)


def _diff_attn_kernel(cfar_ref, qT_ref, k_ref, vT_ref, bias_ref, lamp_ref, g_ref, o_ref,
                      sa_ref, sb_ref, *state, ta, hp, lambda_init):
    hg = pl.program_id(1)
    qi = pl.program_id(2)
    s_refs = (sa_ref, sb_ref)
    zeros = jnp.zeros((DIFF_QK, ta), BF16)
    qs = []
    for h in range(hp):
        qT = qT_ref[0, h]
        qs.append((jnp.concatenate([qT[:DIFF_QK], zeros], axis=0),
                   jnp.concatenate([zeros, qT[DIFF_QK:]], axis=0)))
    m_refs, l_refs, acc_refs = _init_chains(state)

    nb = ta // ATT_BLK
    blk = ATT_BLK

    def logits(j, buf):
        off = pl.multiple_of(j * ta, ta)
        for h in range(hp):
            kj = k_ref[0, h, pl.ds(off, ta), :]
            for c in range(2):
                s_refs[buf][2 * h + c] = _dot(kj, qs[h][c])

    def softmax_pv(j, buf, kind):
        off = pl.multiple_of(j * ta, ta)
        diagonal = kind == "diag"
        for h in range(hp):
            for qb in range(nb):
                nk = (qb + 1) * blk if diagonal else ta
                vj = vT_ref[0, h, :, pl.ds(off, nk)]
                for c in range(2):
                    s = s_refs[buf][2 * h + c, :nk, qb * blk:(qb + 1) * blk]
                    i = (2 * h + c) * nb + qb
                    if kind == "far" or (kind == "near" and qb * blk + 1 >= MAX_DISTANCE):
                        _softmax_step(s, vj, m_refs[i], None, acc_refs[i],
                                      shift=cfar_ref[hg * hp + h])
                    else:
                        bias = bias_ref[h, 0 if diagonal else 1, :nk, qb * blk:(qb + 1) * blk]
                        _softmax_step(s + bias, vj, m_refs[i], None, acc_refs[i])

    _pipelined_tiles_biased(qi, logits, softmax_pv)

    lp = lamp_ref[...]
    lam = (jnp.exp(jnp.sum(lp[0:1] * lp[1:2], axis=-1, keepdims=True))
           - jnp.exp(jnp.sum(lp[2:3] * lp[3:4], axis=-1, keepdims=True)) + lambda_init)
    for h in range(hp):
        for qb in range(nb):
            a0 = acc_refs[(2 * h) * nb + qb][...]
            a1 = acc_refs[(2 * h + 1) * nb + qb][...]
            out = (a0[:DIFF_V] * (1.0 / a0[DIFF_V:DIFF_V + 1])
                   - lam * (a1[:DIFF_V] * (1.0 / a1[DIFF_V:DIFF_V + 1])))
            ms = jnp.mean(out * out, axis=0, keepdims=True)
            out = out * lax.rsqrt(ms + 1e-5) * g_ref[...] * (1.0 - lambda_init)
            o_ref[0, qb * blk:(qb + 1) * blk, h * DIFF_V:(h + 1) * DIFF_V] = out.T.astype(BF16)


def _diff_attn(cfar, qT, k, vT, bias_tiles, lam_params, g_col, ta, hp, lambda_init):
    b, hd, _, s = qT.shape
    grid_spec = pltpu.PrefetchScalarGridSpec(
        num_scalar_prefetch=1,
        grid=(b, hd // hp, s // ta),
        in_specs=[
            pl.BlockSpec((1, hp, LANES, ta), lambda bi, h, i, cf: (bi, h, 0, i)),
            pl.BlockSpec((1, hp, s, LANES), lambda bi, h, i, cf: (bi, h, 0, 0)),
            pl.BlockSpec((1, hp, DIFF_VA, s), lambda bi, h, i, cf: (bi, h, 0, 0)),
            pl.BlockSpec((hp, 2, ta, ta), lambda bi, h, i, cf: (h, 0, 0, 0)),
            pl.BlockSpec(lam_params.shape, lambda bi, h, i, cf: (0, 0)),
            pl.BlockSpec(g_col.shape, lambda bi, h, i, cf: (0, 0)),
        ],
        out_specs=pl.BlockSpec((1, ta, hp * DIFF_V), lambda bi, h, i, cf: (bi, i, h)),
        scratch_shapes=([pltpu.VMEM((2 * hp, ta, ta), F32)] * 2
                        + _chain_scratch(2 * hp * ta // ATT_BLK, DIFF_VA)),
    )
    return pl.pallas_call(
        functools.partial(_diff_attn_kernel, ta=ta, hp=hp, lambda_init=lambda_init),
        grid_spec=grid_spec,
        out_shape=jax.ShapeDtypeStruct((b, s, hd * DIFF_V), BF16),
        compiler_params=pltpu.CompilerParams(
            dimension_semantics=("arbitrary",) * 3, vmem_limit_bytes=VMEM_LIMIT),
        name="diff_attn",
    )(cfar, qT, k, vT, bias_tiles, lam_params, g_col)


def _split_bf16(v):
    hi = v.astype(BF16)
    return hi, (v - hi.astype(F32)).astype(BF16)


def _oproj_kernel(am_ref, ad_ref, x_ref, wo_ref, g_ref, b_ref, *rest, route):
    mix = _dot(am_ref[...], wo_ref[:MLA_W, :]) + _dot(ad_ref[...], wo_ref[MLA_W:, :])
    y = _layer_norm_rows(DN_ALPHA * x_ref[...] + mix, g_ref[...], b_ref[...])
    if not route:
        (o_ref,) = rest
        o_ref[...] = y
        return
    rhi_ref, rlo_ref, o_ref, ob_ref, r_ref = rest
    yhi, ylo = _split_bf16(y)
    logits = _dot(yhi, rhi_ref[...]) + (_dot(yhi, rlo_ref[...]) + _dot(ylo, rhi_ref[...]))
    lane = lax.broadcasted_iota(jnp.int32, logits.shape, 1)
    logits = jnp.where(lane < N_EXPERTS, logits, NEG_BIG)
    m1 = jnp.max(logits, axis=-1, keepdims=True)
    i1 = jnp.min(jnp.where(logits == m1, lane, LANES), axis=-1, keepdims=True)
    rest_l = jnp.where(lane == i1, NEG_BIG, logits)
    m2 = jnp.max(rest_l, axis=-1, keepdims=True)
    i2 = jnp.min(jnp.where(rest_l == m2, lane, LANES), axis=-1, keepdims=True)
    e2 = jnp.exp(m2 - m1)
    g1 = 1.0 / (1.0 + e2)
    g2 = e2 * g1
    rec = jnp.where(lane == 0, i1.astype(F32),
                    jnp.where(lane == 1, i2.astype(F32),
                              jnp.where(lane == 2, g1, jnp.where(lane == 3, g2, 0.0))))
    o_ref[...] = y
    ob_ref[...] = yhi
    r_ref[...] = rec


def _oproj(a_mla, a_diff, x2d, wo, g, b, tm, router=None):
    t, d = x2d.shape
    row = lambda w: pl.BlockSpec((tm, w), lambda i: (i, 0))
    full = lambda shape: pl.BlockSpec(shape, lambda i: (0,) * len(shape))
    in_specs = [row(MLA_W), row(DIFF_W), row(d), full(wo.shape), full(g.shape), full(b.shape)]
    args = [a_mla, a_diff, x2d, wo, g, b]
    out_specs = [row(d)]
    out_shape = [jax.ShapeDtypeStruct((t, d), F32)]
    if router is not None:
        in_specs += [full(router[0].shape), full(router[1].shape)]
        args += list(router)
        out_specs += [row(d), row(LANES)]
        out_shape += [jax.ShapeDtypeStruct((t, d), BF16), jax.ShapeDtypeStruct((t, LANES), F32)]
    res = pl.pallas_call(
        functools.partial(_oproj_kernel, route=router is not None),
        grid=(t // tm,),
        in_specs=in_specs, out_specs=out_specs, out_shape=out_shape,
        compiler_params=pltpu.CompilerParams(
            dimension_semantics=("arbitrary",), vmem_limit_bytes=VMEM_LIMIT),
        name="oproj_route" if router is not None else "oproj",
    )(*args)
    return res if router is not None else res[0]


def _ffn_kernel(x_ref, wg_ref, wu_ref, wd_ref, g_ref, b_ref, o_ref):
    x = x_ref[...]
    xb = x.astype(BF16)
    gate = _dot(xb, wg_ref[...])
    up = _dot(xb, wu_ref[...])
    hid = (gate * jax.nn.sigmoid(gate) * up).astype(BF16)
    f = _dot(hid, wd_ref[...])
    o_ref[...] = _layer_norm_rows(DN_ALPHA * x + f, g_ref[...], b_ref[...])


def _ffn(x2d, wg, wu, wd, g, b, tm):
    t, d = x2d.shape
    row = pl.BlockSpec((tm, d), lambda i: (i, 0))
    const = lambda shape: pl.BlockSpec(shape, lambda i: (0,) * len(shape),
                                       pipeline_mode=pl.Buffered(1))
    return pl.pallas_call(
        _ffn_kernel,
        grid=(t // tm,),
        in_specs=[row, const(wg.shape), const(wu.shape), const(wd.shape),
                  const(g.shape), const(b.shape)],
        out_specs=row,
        out_shape=jax.ShapeDtypeStruct((t, d), F32),
        compiler_params=pltpu.CompilerParams(
            dimension_semantics=("arbitrary",), vmem_limit_bytes=VMEM_LIMIT),
        name="ffn",
    )(x2d, wg, wu, wd, g, b)


def _dispatch_kernel(te_ref, uoff_ref, unc_ref, x_ref, post_ref, o_ref, acc_ref, *, tm, unit, kc):
    i = pl.program_id(0)
    e = te_ref[i]
    t = x_ref.shape[0]
    upt = tm // unit
    for r in range(upt):
        u = i * upt + r
        n = unc_ref[u]
        start0 = uoff_ref[u]
        row_pos = u * unit + lax.broadcasted_iota(jnp.int32, (unit, 1), 0)

        def chunk(c):
            nominal = start0 + c * kc
            off = pl.multiple_of(jnp.minimum(nominal, t - kc), LANES)
            dest = post_ref[pl.ds(e, 1), pl.ds(off, kc)]
            tok = off + lax.broadcasted_iota(jnp.int32, (1, kc), 1)
            dest = jnp.where(tok >= nominal, dest, -1)
            sel = jnp.where(dest == row_pos, 1.0, 0.0).astype(BF16)
            return _dot(sel, x_ref[pl.ds(off, kc), :])

        @pl.when(n > 0)
        def _():
            acc_ref[...] = chunk(0)

        @pl.when(n == 0)
        def _():
            acc_ref[...] = jnp.zeros(acc_ref.shape, F32)

        def more(c, carry):
            acc_ref[...] += chunk(c)
            return carry

        lax.fori_loop(1, n, more, 0)
        o_ref[r * unit:(r + 1) * unit, :] = acc_ref[...].astype(BF16)


def _dispatch(tile_e, uoff, unc, xb, post, nt, tm, unit, kc):
    t, d = xb.shape
    const = lambda shape: pl.BlockSpec(shape, lambda i, *_: (0,) * len(shape),
                                       pipeline_mode=pl.Buffered(1))
    grid_spec = pltpu.PrefetchScalarGridSpec(
        num_scalar_prefetch=3,
        grid=(nt,),
        in_specs=[const(xb.shape), const(post.shape)],
        out_specs=pl.BlockSpec((tm, d), lambda i, *_: (i, 0)),
        scratch_shapes=[pltpu.VMEM((unit, d), F32)],
    )
    return pl.pallas_call(
        functools.partial(_dispatch_kernel, tm=tm, unit=unit, kc=kc),
        grid_spec=grid_spec,
        out_shape=jax.ShapeDtypeStruct((nt * tm, d), BF16),
        compiler_params=pltpu.CompilerParams(
            dimension_semantics=("arbitrary",), vmem_limit_bytes=VMEM_LIMIT),
        name="dispatch",
    )(tile_e, uoff, unc, xb, post)


def _gffn_kernel(te_ref, tv_ref, x_ref, wg_ref, wu_ref, wd_ref, o_ref, acc_ref, *, nf):
    i = pl.program_id(0)
    f = pl.program_id(1)
    valid = tv_ref[i] == 1

    @pl.when(valid)
    def _():
        xb = x_ref[...]
        gate = _dot(xb, wg_ref[0, 0])
        up = _dot(xb, wu_ref[0, 0])
        hid = (gate * jax.nn.sigmoid(gate) * up).astype(BF16)
        part = _dot(hid, wd_ref[0, 0])

        @pl.when(f == 0)
        def _():
            acc_ref[...] = part

        @pl.when(f > 0)
        def _():
            acc_ref[...] += part

    @pl.when(f == nf - 1)
    def _():
        o_ref[...] = jnp.where(valid, acc_ref[...], 0.0).astype(BF16)


def _gffn(tile_e, tile_v, xs, li, wg, wu, wd, tm, fc):
    p, d = xs.shape
    nt = p // tm
    nf = wg.shape[3] // fc
    fidx = lambda i, f, te, tv: jnp.where(tv[i] == 1, f, nf - 1)
    grid_spec = pltpu.PrefetchScalarGridSpec(
        num_scalar_prefetch=2,
        grid=(nt, nf),
        in_specs=[
            pl.BlockSpec((tm, d), lambda i, f, te, tv: (i, 0)),
            pl.BlockSpec((1, 1, d, fc), lambda i, f, te, tv: (li, te[i], 0, fidx(i, f, te, tv))),
            pl.BlockSpec((1, 1, d, fc), lambda i, f, te, tv: (li, te[i], 0, fidx(i, f, te, tv))),
            pl.BlockSpec((1, 1, fc, d), lambda i, f, te, tv: (li, te[i], fidx(i, f, te, tv), 0)),
        ],
        out_specs=pl.BlockSpec((tm, d), lambda i, f, te, tv: (i, 0)),
        scratch_shapes=[pltpu.VMEM((tm, d), F32)],
    )
    return pl.pallas_call(
        functools.partial(_gffn_kernel, nf=nf),
        grid_spec=grid_spec,
        out_shape=jax.ShapeDtypeStruct((p, d), BF16),
        compiler_params=pltpu.CompilerParams(
            dimension_semantics=("arbitrary", "arbitrary"), vmem_limit_bytes=VMEM_LIMIT),
        name="grouped_ffn",
    )(tile_e, tile_v, xs, wg, wu, wd)


def _combine_kernel(ral_ref, rlo_ref, rhi_ref, x_ref, rec_ref, g_ref, b_ref, ys_ref, o_ref,
                    win_ref, xwin_ref, acc_ref, sems, xsem, *, w):
    j = pl.program_id(0)
    nj = pl.num_programs(0)
    slot = lax.rem(j, 2)
    p_rows = ys_ref.shape[0]

    def win_copy(jj, e, sl):
        r = pl.multiple_of(ral_ref[jj * N_EXPERTS + e], 16)
        return pltpu.make_async_copy(ys_ref.at[pl.ds(r, w)], win_ref.at[sl, pl.ds(e * w, w)],
                                     sems.at[sl, e])

    @pl.when(j == 0)
    def _():
        for e in range(N_EXPERTS):
            win_copy(0, e, 0).start()

    @pl.when(j + 1 < nj)
    def _():
        for e in range(N_EXPERTS):
            win_copy(j + 1, e, 1 - slot).start()

    rec = rec_ref[...]
    p1, p2, g1, g2 = rec[:, 0:1], rec[:, 1:2], rec[:, 2:3], rec[:, 3:4]

    def sel(base, lo, hi):
        rows = base + lax.broadcasted_iota(jnp.int32, (1, w), 1)
        rp = jnp.where((rows >= lo) & (rows < hi), rows, -1).astype(F32)
        return (jnp.where(p1 == rp, g1, 0.0) + jnp.where(p2 == rp, g2, 0.0)).astype(BF16)

    pieces = []
    for e in range(N_EXPERTS):
        win_copy(j, e, slot).wait()
        k = j * N_EXPERTS + e
        pieces.append(sel(ral_ref[k], rlo_ref[k], rhi_ref[k]))
    acc_ref[...] = _dot(jnp.concatenate(pieces, axis=1), win_ref[slot])

    for e in range(N_EXPERTS):
        k = j * N_EXPERTS + e
        nwin = (rhi_ref[k] - ral_ref[k] + (w - 1)) // w

        def extra(wi, carry, k=k):
            nominal = ral_ref[k] + wi * w
            base = pl.multiple_of(jnp.minimum(nominal, p_rows - w), 16)
            cp = pltpu.make_async_copy(ys_ref.at[pl.ds(base, w)], xwin_ref, xsem)
            cp.start()
            cp.wait()
            acc_ref[...] += _dot(sel(base, nominal, rhi_ref[k]), xwin_ref[...])
            return carry

        lax.fori_loop(1, nwin, extra, 0)

    o_ref[...] = _layer_norm_rows(DN_ALPHA * x_ref[...] + acc_ref[...], g_ref[...], b_ref[...])


def _combine(ral, rlo, rhi, x2d, rec2, g, b, ys, tm, w):
    t, d = x2d.shape
    row = lambda c: pl.BlockSpec((tm, c), lambda i, *_: (i, 0))
    full = lambda shape: pl.BlockSpec(shape, lambda i, *_: (0,) * len(shape))
    grid_spec = pltpu.PrefetchScalarGridSpec(
        num_scalar_prefetch=3,
        grid=(t // tm,),
        in_specs=[row(d), row(rec2.shape[1]), full(g.shape), full(b.shape),
                  pl.BlockSpec(memory_space=pl.ANY)],
        out_specs=row(d),
        scratch_shapes=[pltpu.VMEM((2, N_EXPERTS * w, d), BF16), pltpu.VMEM((w, d), BF16),
                        pltpu.VMEM((tm, d), F32), pltpu.SemaphoreType.DMA((2, N_EXPERTS)),
                        pltpu.SemaphoreType.DMA(())],
    )
    return pl.pallas_call(
        functools.partial(_combine_kernel, w=w),
        grid_spec=grid_spec,
        out_shape=jax.ShapeDtypeStruct((t, d), F32),
        compiler_params=pltpu.CompilerParams(
            dimension_semantics=("arbitrary",), vmem_limit_bytes=VMEM_LIMIT),
        name="combine",
    )(ral, rlo, rhi, x2d, rec2, g, b, ys)


def _route_plan(rec, nt, tm_e, unit, kc, tm_c, w):
    t = rec.shape[0]
    i32 = jnp.int32
    ar = jnp.arange(N_EXPERTS, dtype=i32)
    m1 = (rec[:, 0].astype(i32)[:, None] == ar[None, :]).astype(i32)
    m2 = (rec[:, 1].astype(i32)[:, None] == ar[None, :]).astype(i32)
    m = m1 + m2
    cinc = jnp.cumsum(m, axis=0)
    counts = cinc[-1]
    padded = ((counts + tm_e - 1) // tm_e) * tm_e
    ends = jnp.cumsum(padded)
    starts = ends - padded
    posmat = starts[None, :] + (cinc - m)
    pos1 = jnp.sum(m1 * posmat, axis=1)
    pos2 = jnp.sum(m2 * posmat, axis=1)
    post = jnp.where(m.T > 0, posmat.T, -1)
    rec2 = jnp.concatenate([pos1.astype(F32)[:, None], pos2.astype(F32)[:, None], rec[:, 2:4],
                            jnp.zeros((t, 4), F32)], axis=1)
    tile_start = jnp.arange(nt, dtype=i32) * tm_e
    tile_e = jnp.sum((tile_start[:, None] >= ends[None, :]).astype(i32), axis=1)
    tile_v = tile_e < N_EXPERTS
    last_e = jnp.max(jnp.where(tile_v, tile_e, 0))
    tile_e = jnp.where(tile_v, tile_e, last_e)
    upt = tm_e // unit
    nu = nt * upt
    oh = (jnp.repeat(tile_e, upt)[:, None] == ar[None, :]).astype(i32)
    q0 = jnp.arange(nu, dtype=i32) * unit - jnp.sum(oh * starts[None, :], axis=1)
    q1 = jnp.minimum(q0 + unit, jnp.sum(oh * counts[None, :], axis=1))
    has = jnp.repeat(tile_v, upt) & (q1 > q0)
    cb = jnp.concatenate([jnp.zeros((1, N_EXPERTS), i32), cinc[LANES - 1::LANES]], axis=0)
    cbu = jnp.sum(oh[:, None, :] * cb[None, :, :], axis=2)
    jlo = jnp.sum((cbu[:, 1:] <= q0[:, None]).astype(i32), axis=1)
    jend = jnp.sum((cbu[:, :-1] < q1[:, None]).astype(i32), axis=1)
    uoff = jnp.where(has, jlo * LANES, 0)
    unc = jnp.where(has, (jend * LANES - uoff + kc - 1) // kc, 0)
    cbt = jnp.concatenate([jnp.zeros((1, N_EXPERTS), i32), cinc[tm_c - 1::tm_c]], axis=0)
    r_lo = starts[None, :] + cbt[:-1]
    r_hi = starts[None, :] + cbt[1:]
    ral = jnp.minimum((r_lo // 16) * 16, nt * tm_e - w)
    return (rec2, post, tile_e, tile_v.astype(i32), uoff, unc,
            ral.reshape(-1), r_lo.reshape(-1), r_hi.reshape(-1))


def _moe(x2d, xb, rec, li, wg, wu, wd, g, b, tm_c, tm_e, fc, unit, kc, w):
    t, d = x2d.shape
    nt = (2 * t) // tm_e + N_EXPERTS
    rec2, post, tile_e, tile_v, uoff, unc, ral, rlo, rhi = _route_plan(
        rec, nt, tm_e, unit, kc, tm_c, w)
    xs = _dispatch(tile_e, uoff, unc, xb, post, nt, tm_e, unit, kc)
    ys = _gffn(tile_e, tile_v, xs, li, wg, wu, wd, tm_e, fc)
    return _combine(ral, rlo, rhi, x2d, rec2, g, b, ys, tm_c, w)


def _rotate_half_cols(w):
    half = w.shape[-1] // 2
    return jnp.concatenate([-w[..., half:], w[..., :half]], axis=-1)


def _t5_bucket(n):
    max_exact = NUM_BUCKETS // 2
    nf = jnp.maximum(n, 1).astype(F32)
    large = max_exact + (jnp.log(nf / max_exact) / math.log(MAX_DISTANCE / max_exact)
                         * (NUM_BUCKETS - max_exact)).astype(jnp.int32)
    large = jnp.minimum(large, NUM_BUCKETS - 1)
    return jnp.where(n < max_exact, n, large)


def _pick(n, prefs):
    for p in prefs:
        if n % p == 0:
            return p
    raise ValueError(f"no tile for extent {n}")


def kernel(x, rel_bias, w_in, mla_q_norm, mla_kv_norm, mla_w_uq, mla_w_uk, mla_w_uv, diff_lambda, diff_norm, w_o, ln1_g, ln1_b, ln2_g, ln2_b, ffn_w_gate, ffn_w_up, ffn_w_down, moe_router, moe_w_gate, moe_w_up, moe_w_down):
    b, s, d = x.shape
    t = b * s
    assert d == D_MODEL
    ta = _pick(s, (512, 256))
    assert ta % ATT_BLK == 0
    assert s >= 2 * ta or s == ta
    tm_proj = _pick(s, (512, 256, 128))
    tm_tok = _pick(t, (512, 256, 128))
    tm_e = _pick(2 * t, (512, 256, 128))
    tm_c = _pick(t, (512, 256))
    unit, wrows = 256, 256
    kc = 1536
    assert tm_e % unit == 0 and t >= kc and t % LANES == 0
    fc = D_FF_EXPERT // 2

    pos = jnp.arange(s, dtype=F32)
    inv = 1.0 / (ROPE_THETA ** (jnp.arange(0, MLA_ROPE, 2, dtype=F32) / MLA_ROPE))
    ang = pos[:, None] * inv[None, :]
    cos4 = jnp.tile(jnp.cos(ang), (1, 2 * MLA_HEADS))
    sin4 = jnp.tile(jnp.sin(ang), (1, 2 * MLA_HEADS))
    rb = rel_bias.astype(F32)
    kk = jnp.arange(ta)[:, None]
    qq = jnp.arange(ta)[None, :]
    tiles = []
    for off in (0, ta):
        dist = qq + off - kk
        bucket = _t5_bucket(jnp.maximum(dist, 0))
        bt = jnp.zeros((DIFF_HEADS, ta, ta), F32)
        for bk in range(NUM_BUCKETS):
            bt = bt + jnp.where((bucket == bk)[None], rb[bk][:, None, None], 0.0)
        tiles.append(jnp.where((dist >= 0)[None], bt * LOG2E, NEG_BIG))
    bias_tiles = jnp.stack(tiles, axis=1)
    assert ta + 1 >= MAX_DISTANCE
    cfar = rb[NUM_BUCKETS - 1] * LOG2E
    moe_wg, moe_wu, moe_wd = (w.astype(BF16) for w in (moe_w_gate, moe_w_up, moe_w_down))

    xc = x
    for l in range(DEPTH):
        lambda_init = 0.8 - 0.6 * math.exp(-0.3 * l)
        wl = w_in[l]
        kr_w = wl[:, _KR0:_KR0 + MLA_ROPE]
        win = jnp.concatenate([wl[:, :_KR0 + MLA_ROPE], _rotate_half_cols(kr_w),
                               wl[:, _KR0 + MLA_ROPE:]], axis=1).astype(BF16)
        uq = mla_w_uq[l].reshape(MLA_Q_RANK, MLA_HEADS, MLA_QK)
        uq_n = uq[:, :, :MLA_NOPE].reshape(MLA_Q_RANK, -1)
        uq_r = uq[:, :, MLA_NOPE:]
        wuq = jnp.concatenate([uq_n, uq_r.reshape(MLA_Q_RANK, -1),
                               _rotate_half_cols(uq_r).reshape(MLA_Q_RANK, -1)], axis=1).astype(BF16)
        wukv = jnp.concatenate([mla_w_uk[l], mla_w_uv[l]], axis=1).astype(BF16)

        qT, k, vT, dqT, dk, dvT = _proj(
            xc, win, mla_q_norm[l][None, :], mla_kv_norm[l][None, :], wuq, wukv, cos4, sin4, tm_proj)
        a_mla = _mla_attn(qT, k, vT, ta, MLA_HP).reshape(t, MLA_W)
        a_diff = _diff_attn(cfar, dqT, dk, dvT, bias_tiles, diff_lambda[l].astype(F32),
                            diff_norm[l].astype(F32)[:, None], ta, DIFF_HP,
                            lambda_init).reshape(t, DIFF_W)
        x2d = xc.reshape(t, d)
        wo = w_o[l].astype(BF16)
        g1, b1 = ln1_g[l][None, :], ln1_b[l][None, :]
        g2, b2 = ln2_g[l][None, :], ln2_b[l][None, :]
        if l % 2 == 0:
            x1 = _oproj(a_mla, a_diff, x2d, wo, g1, b1, tm_tok)
            i = l // 2
            x2 = _ffn(x1, ffn_w_gate[i].astype(BF16), ffn_w_up[i].astype(BF16),
                      ffn_w_down[i].astype(BF16), g2, b2, tm_tok)
        else:
            i = l // 2
            rpad = jnp.pad(moe_router[i].astype(F32), ((0, 0), (0, LANES - N_EXPERTS)))
            rhi = rpad.astype(BF16)
            rlo = (rpad - rhi.astype(F32)).astype(BF16)
            x1, x1b, rec = _oproj(a_mla, a_diff, x2d, wo, g1, b1, tm_tok, router=(rhi, rlo))
            x2 = _moe(x1, x1b, rec, i, moe_wg, moe_wu, moe_wd, g2, b2,
                      tm_c, tm_e, fc, unit, kc, wrows)
        xc = x2.reshape(b, s, d)
    return xc
```

```python
import functools
import math

import jax
import jax.numpy as jnp
from jax import lax
from jax.experimental import pallas as pl
from jax.experimental.pallas import tpu as pltpu

D_MODEL = 1024
DEPTH = 4
MLA_HEADS = 4
MLA_NOPE = 128
MLA_ROPE = 64
MLA_V = 128
MLA_Q_RANK = 384
MLA_KV_RANK = 256
ROPE_THETA = 10000.0
DIFF_HEADS = 4
DIFF_QK = 64
DIFF_V = 2 * DIFF_QK
NUM_BUCKETS = 32
MAX_DISTANCE = 128
D_FF_DENSE = 2816
N_EXPERTS = 8
D_FF_EXPERT = 3584
DN_ALPHA = (2 * DEPTH) ** 0.25

MLA_QK = MLA_NOPE + MLA_ROPE
MLA_W = MLA_HEADS * MLA_V
DIFF_W = DIFF_HEADS * DIFF_V
DIFF_VA = DIFF_V + 16
_CQ0, _CKV0 = 0, MLA_Q_RANK
_KR0 = _CKV0 + MLA_KV_RANK
_DQ0 = _KR0 + 2 * MLA_ROPE
_DK0 = _DQ0 + DIFF_W
_DV0 = _DK0 + DIFF_W
IN_COLS_W = _DV0 + DIFF_W

LANES = 128
VMEM_LIMIT = 56 * 1024 * 1024
NEG_BIG = -1e30
ROUTER_ROWS = 16
REC_ROWS = 8
LOG2E = math.log2(math.e)
ATT_BLK = 256
MLA_HP = 4
DIFF_HP = 2

BF16 = jnp.bfloat16
F32 = jnp.float32


def _dot(a, b):
    return jnp.dot(a, b, preferred_element_type=F32)


def _layer_norm_rows(y, g, b):
    mu = jnp.mean(y, axis=-1, keepdims=True)
    d = y - mu
    var = jnp.mean(d * d, axis=-1, keepdims=True)
    return d * lax.rsqrt(var + 1e-5) * g + b


def _rms_rows(y, g, eps):
    return y * lax.rsqrt(jnp.mean(y * y, axis=-1, keepdims=True) + eps) * g


def _proj_kernel(x_ref, win_ref, qn_ref, kvn_ref, wuq_ref, wukv_ref, cos_ref, sin_ref,
                 qT_ref, k_ref, vT_ref, dqT_ref, dk_ref, dvT_ref):
    x = x_ref[0].astype(BF16)
    h = _dot(x, win_ref[...])
    cqn = _rms_rows(h[:, _CQ0:_CQ0 + MLA_Q_RANK], qn_ref[...], 1e-6)
    ckvn = _rms_rows(h[:, _CKV0:_CKV0 + MLA_KV_RANK], kvn_ref[...], 1e-6)
    q = _dot(cqn.astype(BF16), wuq_ref[...]) * (MLA_QK ** -0.5 * LOG2E)
    kv = _dot(ckvn.astype(BF16), wukv_ref[...])
    cos4, sin4 = cos_ref[...], sin_ref[...]
    nr = MLA_HEADS * MLA_ROPE
    qr = q[:, MLA_W:MLA_W + nr] * cos4 + q[:, MLA_W + nr:] * sin4
    kr = (h[:, _KR0:_KR0 + MLA_ROPE] * cos4[:, :MLA_ROPE]
          + h[:, _KR0 + MLA_ROPE:_KR0 + 2 * MLA_ROPE] * sin4[:, :MLA_ROPE])
    qrT = qr.T
    for hh in range(MLA_HEADS):
        c0 = hh * LANES
        qnT = q[:, c0:c0 + MLA_NOPE].T
        qT_ref[0, hh] = jnp.concatenate(
            [qnT, qrT[hh * MLA_ROPE:(hh + 1) * MLA_ROPE]], axis=0).astype(BF16)
        k_ref[0, hh] = jnp.concatenate([kv[:, c0:c0 + MLA_NOPE], kr], axis=1).astype(BF16)
        vT_ref[0, hh] = kv[:, MLA_W + c0:MLA_W + c0 + MLA_V].T.astype(BF16)
        dqT_ref[0, hh] = (h[:, _DQ0 + c0:_DQ0 + c0 + LANES]
                          * (DIFF_QK ** -0.5 * LOG2E)).T.astype(BF16)
        dk_ref[0, hh] = h[:, _DK0 + c0:_DK0 + c0 + LANES].astype(BF16)
        dvT_ref[0, hh] = jnp.concatenate(
            [h[:, _DV0 + c0:_DV0 + c0 + DIFF_V].T, jnp.ones((DIFF_VA - DIFF_V, x.shape[0]), F32)],
            axis=0).astype(BF16)


def _proj(x, win, qn, kvn, wuq, wukv, cos4, sin4, tm):
    b, s, d = x.shape
    hd = MLA_HEADS
    full = lambda shape: pl.BlockSpec(shape, lambda bi, i: (0,) * len(shape))
    tmaj = lambda w: pl.BlockSpec((1, hd, tm, w), lambda bi, i: (bi, 0, i, 0))
    fmaj = lambda w: pl.BlockSpec((1, hd, w, tm), lambda bi, i: (bi, 0, 0, i))
    sds = jax.ShapeDtypeStruct
    return pl.pallas_call(
        _proj_kernel,
        grid=(b, s // tm),
        in_specs=[
            pl.BlockSpec((1, tm, d), lambda bi, i: (bi, i, 0)),
            full(win.shape), full(qn.shape), full(kvn.shape), full(wuq.shape), full(wukv.shape),
            pl.BlockSpec((tm, cos4.shape[1]), lambda bi, i: (i, 0)),
            pl.BlockSpec((tm, sin4.shape[1]), lambda bi, i: (i, 0)),
        ],
        out_specs=[fmaj(MLA_QK), tmaj(MLA_QK), fmaj(MLA_V), fmaj(LANES), tmaj(LANES), fmaj(DIFF_VA)],
        out_shape=[
            sds((b, hd, MLA_QK, s), BF16), sds((b, hd, s, MLA_QK), BF16), sds((b, hd, MLA_V, s), BF16),
            sds((b, hd, LANES, s), BF16), sds((b, hd, s, LANES), BF16), sds((b, hd, DIFF_VA, s), BF16),
        ],
        compiler_params=pltpu.CompilerParams(
            dimension_semantics=("arbitrary", "arbitrary"), vmem_limit_bytes=VMEM_LIMIT),
        name="proj",
    )(x, win, qn, kvn, wuq, wukv, cos4, sin4)


def _softmax_step(s, vj, m_ref, l_ref, acc_ref, shift=None):
    m_prev = m_ref[...]
    smax = jnp.max(s, axis=0, keepdims=True)
    if shift is not None:
        smax = smax + shift
    m_new = jnp.maximum(m_prev, smax)
    a = jnp.exp2(m_prev - m_new)
    p = jnp.exp2(s - (m_new if shift is None else m_new - shift))
    if l_ref is not None:
        l_ref[...] = a * l_ref[...] + jnp.sum(p, axis=0, keepdims=True)
    acc_ref[...] = a * acc_ref[...] + _dot(vj, p.astype(BF16))
    m_ref[...] = m_new


def _chain_scratch(n, dv):
    return [pltpu.VMEM((1, ATT_BLK), F32)] * (2 * n) + [pltpu.VMEM((dv, ATT_BLK), F32)] * n


def _init_chains(state):
    n = len(state) // 3
    m_refs, l_refs, acc_refs = state[:n], state[n:2 * n], state[2 * n:]
    for m_ref, l_ref, acc_ref in zip(m_refs, l_refs, acc_refs):
        m_ref[...] = jnp.full(m_ref.shape, NEG_BIG, F32)
        l_ref[...] = jnp.zeros(l_ref.shape, F32)
        acc_ref[...] = jnp.zeros(acc_ref.shape, F32)
    return m_refs, l_refs, acc_refs


def _pipelined_tiles(qi, logits, softmax_pv):
    logits(0, 0)

    def body(i, carry):
        j = 2 * i
        logits(j + 1, 1)
        softmax_pv(j, 0, False)
        logits(j + 2, 0)
        softmax_pv(j + 1, 1, False)
        return carry

    lax.fori_loop(0, lax.shift_right_logical(qi, 1), body, 0)
    odd = lax.rem(qi, 2) == 1

    @pl.when(odd)
    def _():
        logits(qi, 1)
        softmax_pv(qi - 1, 0, False)
        softmax_pv(qi, 1, True)

    @pl.when(jnp.logical_not(odd))
    def _():
        softmax_pv(qi, 0, True)


def _pipelined_tiles_biased(qi, logits, softmax_pv):
    logits(0, 0)
    nfar = jnp.maximum(qi - 1, 0)

    def body(i, carry):
        j = 2 * i
        logits(j + 1, 1)
        softmax_pv(j, 0, "far")
        logits(j + 2, 0)
        softmax_pv(j + 1, 1, "far")
        return carry

    lax.fori_loop(0, lax.shift_right_logical(nfar, 1), body, 0)
    odd = lax.rem(nfar, 2) == 1

    @pl.when(qi == 0)
    def _():
        softmax_pv(qi, 0, "diag")

    @pl.when(jnp.logical_and(qi >= 1, jnp.logical_not(odd)))
    def _():
        logits(qi, 1)
        softmax_pv(qi - 1, 0, "near")
        softmax_pv(qi, 1, "diag")

    @pl.when(odd)
    def _():
        logits(qi - 1, 1)
        softmax_pv(qi - 2, 0, "far")
        logits(qi, 0)
        softmax_pv(qi - 1, 1, "near")
        softmax_pv(qi, 0, "diag")


def _mla_attn_kernel(qT_ref, k_ref, vT_ref, o_ref, sa_ref, sb_ref, *state, ta, hp):
    qi = pl.program_id(2)
    m_refs, l_refs, acc_refs = _init_chains(state)
    s_refs = (sa_ref, sb_ref)

    nb = ta // ATT_BLK
    blk = ATT_BLK

    def logits(j, buf):
        off = pl.multiple_of(j * ta, ta)
        for h in range(hp):
            s_refs[buf][h] = _dot(k_ref[0, h, pl.ds(off, ta), :], qT_ref[0, h])

    def softmax_pv(j, buf, diagonal):
        off = pl.multiple_of(j * ta, ta)
        for h in range(hp):
            for qs in range(nb):
                nk = (qs + 1) * blk if diagonal else ta
                s = s_refs[buf][h, :nk, qs * blk:(qs + 1) * blk]
                if diagonal:
                    kpos = lax.broadcasted_iota(jnp.int32, (nk, blk), 0)
                    qpos = lax.broadcasted_iota(jnp.int32, (nk, blk), 1) + qs * blk
                    s = jnp.where(kpos <= qpos, s, NEG_BIG)
                c = h * nb + qs
                _softmax_step(s, vT_ref[0, h, :, pl.ds(off, nk)], m_refs[c], l_refs[c], acc_refs[c])

    _pipelined_tiles(qi, logits, softmax_pv)
    for h in range(hp):
        for qs in range(nb):
            c = h * nb + qs
            out = acc_refs[c][...] * (1.0 / l_refs[c][...])
            o_ref[0, qs * blk:(qs + 1) * blk, h * MLA_V:(h + 1) * MLA_V] = out.T.astype(BF16)


def _mla_attn(qT, k, vT, ta, hp):
    b, hd, _, s = qT.shape
    return pl.pallas_call(
        functools.partial(_mla_attn_kernel, ta=ta, hp=hp),
        grid=(b, hd // hp, s // ta),
        in_specs=[
            pl.BlockSpec((1, hp, MLA_QK, ta), lambda bi, h, i: (bi, h, 0, i)),
            pl.BlockSpec((1, hp, s, MLA_QK), lambda bi, h, i: (bi, h, 0, 0)),
            pl.BlockSpec((1, hp, MLA_V, s), lambda bi, h, i: (bi, h, 0, 0)),
        ],
        out_specs=pl.BlockSpec((1, ta, hp * MLA_V), lambda bi, h, i: (bi, i, h)),
        out_shape=jax.ShapeDtypeStruct((b, s, hd * MLA_V), BF16),
        scratch_shapes=([pltpu.VMEM((hp, ta, ta), F32)] * 2
                        + _chain_scratch(hp * ta // ATT_BLK, MLA_V)),
        compiler_params=pltpu.CompilerParams(
            dimension_semantics=("arbitrary",) * 3, vmem_limit_bytes=VMEM_LIMIT),
        name="mla_attn",
    )(qT, k, vT)


def _diff_attn_kernel(cfar_ref, qT_ref, k_ref, vT_ref, bias_ref, lamp_ref, g_ref, o_ref,
                      sa_ref, sb_ref, *state, ta, hp, lambda_init):
    hg = pl.program_id(1)
    qi = pl.program_id(2)
    s_refs = (sa_ref, sb_ref)
    zeros = jnp.zeros((DIFF_QK, ta), BF16)
    qs = []
    for h in range(hp):
        qT = qT_ref[0, h]
        qs.append((jnp.concatenate([qT[:DIFF_QK], zeros], axis=0),
                   jnp.concatenate([zeros, qT[DIFF_QK:]], axis=0)))
    m_refs, l_refs, acc_refs = _init_chains(state)

    nb = ta // ATT_BLK
    blk = ATT_BLK

    def logits(j, buf):
        off = pl.multiple_of(j * ta, ta)
        for h in range(hp):
            kj = k_ref[0, h, pl.ds(off, ta), :]
            for c in range(2):
                s_refs[buf][2 * h + c] = _dot(kj, qs[h][c])

    def softmax_pv(j, buf, kind):
        off = pl.multiple_of(j * ta, ta)
        diagonal = kind == "diag"
        for h in range(hp):
            for qb in range(nb):
                nk = (qb + 1) * blk if diagonal else ta
                vj = vT_ref[0, h, :, pl.ds(off, nk)]
                for c in range(2):
                    s = s_refs[buf][2 * h + c, :nk, qb * blk:(qb + 1) * blk]
                    i = (2 * h + c) * nb + qb
                    if kind == "far" or (kind == "near" and qb * blk + 1 >= MAX_DISTANCE):
                        _softmax_step(s, vj, m_refs[i], None, acc_refs[i],
                                      shift=cfar_ref[hg * hp + h])
                    else:
                        bias = bias_ref[h, 0 if diagonal else 1, :nk, qb * blk:(qb + 1) * blk]
                        _softmax_step(s + bias, vj, m_refs[i], None, acc_refs[i])

    _pipelined_tiles_biased(qi, logits, softmax_pv)

    lp = lamp_ref[...]
    lam = (jnp.exp(jnp.sum(lp[0:1] * lp[1:2], axis=-1, keepdims=True))
           - jnp.exp(jnp.sum(lp[2:3] * lp[3:4], axis=-1, keepdims=True)) + lambda_init)
    for h in range(hp):
        for qb in range(nb):
            a0 = acc_refs[(2 * h) * nb + qb][...]
            a1 = acc_refs[(2 * h + 1) * nb + qb][...]
            out = (a0[:DIFF_V] * (1.0 / a0[DIFF_V:DIFF_V + 1])
                   - lam * (a1[:DIFF_V] * (1.0 / a1[DIFF_V:DIFF_V + 1])))
            ms = jnp.mean(out * out, axis=0, keepdims=True)
            out = out * lax.rsqrt(ms + 1e-5) * g_ref[...] * (1.0 - lambda_init)
            o_ref[0, qb * blk:(qb + 1) * blk, h * DIFF_V:(h + 1) * DIFF_V] = out.T.astype(BF16)


def _diff_attn(cfar, qT, k, vT, bias_tiles, lam_params, g_col, ta, hp, lambda_init):
    b, hd, _, s = qT.shape
    grid_spec = pltpu.PrefetchScalarGridSpec(
        num_scalar_prefetch=1,
        grid=(b, hd // hp, s // ta),
        in_specs=[
            pl.BlockSpec((1, hp, LANES, ta), lambda bi, h, i, cf: (bi, h, 0, i)),
            pl.BlockSpec((1, hp, s, LANES), lambda bi, h, i, cf: (bi, h, 0, 0)),
            pl.BlockSpec((1, hp, DIFF_VA, s), lambda bi, h, i, cf: (bi, h, 0, 0)),
            pl.BlockSpec((hp, 2, ta, ta), lambda bi, h, i, cf: (h, 0, 0, 0)),
            pl.BlockSpec(lam_params.shape, lambda bi, h, i, cf: (0, 0)),
            pl.BlockSpec(g_col.shape, lambda bi, h, i, cf: (0, 0)),
        ],
        out_specs=pl.BlockSpec((1, ta, hp * DIFF_V), lambda bi, h, i, cf: (bi, i, h)),
        scratch_shapes=([pltpu.VMEM((2 * hp, ta, ta), F32)] * 2
                        + _chain_scratch(2 * hp * ta // ATT_BLK, DIFF_VA)),
    )
    return pl.pallas_call(
        functools.partial(_diff_attn_kernel, ta=ta, hp=hp, lambda_init=lambda_init),
        grid_spec=grid_spec,
        out_shape=jax.ShapeDtypeStruct((b, s, hd * DIFF_V), BF16),
        compiler_params=pltpu.CompilerParams(
            dimension_semantics=("arbitrary",) * 3, vmem_limit_bytes=VMEM_LIMIT),
        name="diff_attn",
    )(cfar, qT, k, vT, bias_tiles, lam_params, g_col)


def _split_bf16(v):
    hi = v.astype(BF16)
    return hi, (v - hi.astype(F32)).astype(BF16)


def _oproj_kernel(am_ref, ad_ref, x_ref, wo_ref, g_ref, b_ref, *rest, route):
    mix = _dot(am_ref[...], wo_ref[:MLA_W, :]) + _dot(ad_ref[...], wo_ref[MLA_W:, :])
    y = _layer_norm_rows(DN_ALPHA * x_ref[...] + mix, g_ref[...], b_ref[...])
    if not route:
        (o_ref,) = rest
        o_ref[...] = y
        return
    rhl_ref, rh_ref, o_ref, ob_ref, r_ref = rest
    yhi, ylo = _split_bf16(y)
    nt = (((1,), (1,)), ((), ()))
    both = lax.dot_general(rhl_ref[...], yhi, nt, preferred_element_type=F32)
    logits = (both[:ROUTER_ROWS] + both[ROUTER_ROWS:]
              + lax.dot_general(rh_ref[...], ylo, nt, preferred_element_type=F32))
    row = lax.broadcasted_iota(jnp.int32, logits.shape, 0)
    logits = jnp.where(row < N_EXPERTS, logits, NEG_BIG)
    m1 = jnp.max(logits, axis=0, keepdims=True)
    i1 = jnp.min(jnp.where(logits == m1, row, ROUTER_ROWS), axis=0, keepdims=True)
    rest_l = jnp.where(row == i1, NEG_BIG, logits)
    m2 = jnp.max(rest_l, axis=0, keepdims=True)
    i2 = jnp.min(jnp.where(rest_l == m2, row, ROUTER_ROWS), axis=0, keepdims=True)
    e2 = jnp.exp(m2 - m1)
    g1 = 1.0 / (1.0 + e2)
    g2 = e2 * g1
    row8 = lax.broadcasted_iota(jnp.int32, r_ref.shape, 0)
    rec = jnp.where(row8 == 0, i1.astype(F32),
                    jnp.where(row8 == 1, i2.astype(F32),
                              jnp.where(row8 == 2, g1, jnp.where(row8 == 3, g2, 0.0))))
    o_ref[...] = y
    ob_ref[...] = yhi
    r_ref[...] = rec


def _oproj(a_mla, a_diff, x2d, wo, g, b, tm, router=None):
    t, d = x2d.shape
    row = lambda w: pl.BlockSpec((tm, w), lambda i: (i, 0))
    full = lambda shape: pl.BlockSpec(shape, lambda i: (0,) * len(shape))
    in_specs = [row(MLA_W), row(DIFF_W), row(d), full(wo.shape), full(g.shape), full(b.shape)]
    args = [a_mla, a_diff, x2d, wo, g, b]
    out_specs = [row(d)]
    out_shape = [jax.ShapeDtypeStruct((t, d), F32)]
    if router is not None:
        in_specs += [full(router[0].shape), full(router[1].shape)]
        args += list(router)
        out_specs += [row(d), pl.BlockSpec((REC_ROWS, tm), lambda i: (0, i))]
        out_shape += [jax.ShapeDtypeStruct((t, d), BF16), jax.ShapeDtypeStruct((REC_ROWS, t), F32)]
    res = pl.pallas_call(
        functools.partial(_oproj_kernel, route=router is not None),
        grid=(t // tm,),
        in_specs=in_specs, out_specs=out_specs, out_shape=out_shape,
        compiler_params=pltpu.CompilerParams(
            dimension_semantics=("arbitrary",), vmem_limit_bytes=VMEM_LIMIT),
        name="oproj_route" if router is not None else "oproj",
    )(*args)
    return res if router is not None else res[0]


def _ffn_kernel(x_ref, wg_ref, wu_ref, wd_ref, g_ref, b_ref, o_ref):
    x = x_ref[...]
    xb = x.astype(BF16)
    gate = _dot(xb, wg_ref[...])
    up = _dot(xb, wu_ref[...])
    hid = (gate * jax.nn.sigmoid(gate) * up).astype(BF16)
    f = _dot(hid, wd_ref[...])
    o_ref[...] = _layer_norm_rows(DN_ALPHA * x + f, g_ref[...], b_ref[...])


def _ffn(x2d, wg, wu, wd, g, b, tm):
    t, d = x2d.shape
    row = pl.BlockSpec((tm, d), lambda i: (i, 0))
    const = lambda shape: pl.BlockSpec(shape, lambda i: (0,) * len(shape),
                                       pipeline_mode=pl.Buffered(1))
    return pl.pallas_call(
        _ffn_kernel,
        grid=(t // tm,),
        in_specs=[row, const(wg.shape), const(wu.shape), const(wd.shape),
                  const(g.shape), const(b.shape)],
        out_specs=row,
        out_shape=jax.ShapeDtypeStruct((t, d), F32),
        compiler_params=pltpu.CompilerParams(
            dimension_semantics=("arbitrary",), vmem_limit_bytes=VMEM_LIMIT),
        name="ffn",
    )(x2d, wg, wu, wd, g, b)


def _dispatch_kernel(te_ref, uoff_ref, unc_ref, x_ref, post_ref, o_ref, acc_ref, *, tm, unit, kc):
    i = pl.program_id(0)
    e = te_ref[i]
    t = x_ref.shape[0]
    upt = tm // unit
    for r in range(upt):
        u = i * upt + r
        n = unc_ref[u]
        start0 = uoff_ref[u]
        row_pos = u * unit + lax.broadcasted_iota(jnp.int32, (unit, 1), 0)

        def chunk(c):
            nominal = start0 + c * kc
            off = pl.multiple_of(jnp.minimum(nominal, t - kc), LANES)
            dest = post_ref[pl.ds(e, 1), pl.ds(off, kc)]
            tok = off + lax.broadcasted_iota(jnp.int32, (1, kc), 1)
            dest = jnp.where(tok >= nominal, dest, -1)
            sel = jnp.where(dest == row_pos, 1.0, 0.0).astype(BF16)
            return _dot(sel, x_ref[pl.ds(off, kc), :])

        @pl.when(n > 0)
        def _():
            acc_ref[...] = chunk(0)

        @pl.when(n == 0)
        def _():
            acc_ref[...] = jnp.zeros(acc_ref.shape, F32)

        def more(c, carry):
            acc_ref[...] += chunk(c)
            return carry

        lax.fori_loop(1, n, more, 0)
        o_ref[r * unit:(r + 1) * unit, :] = acc_ref[...].astype(BF16)


def _dispatch(tile_e, uoff, unc, xb, post, nt, tm, unit, kc):
    t, d = xb.shape
    const = lambda shape: pl.BlockSpec(shape, lambda i, *_: (0,) * len(shape),
                                       pipeline_mode=pl.Buffered(1))
    grid_spec = pltpu.PrefetchScalarGridSpec(
        num_scalar_prefetch=3,
        grid=(nt,),
        in_specs=[const(xb.shape), const(post.shape)],
        out_specs=pl.BlockSpec((tm, d), lambda i, *_: (i, 0)),
        scratch_shapes=[pltpu.VMEM((unit, d), F32)],
    )
    return pl.pallas_call(
        functools.partial(_dispatch_kernel, tm=tm, unit=unit, kc=kc),
        grid_spec=grid_spec,
        out_shape=jax.ShapeDtypeStruct((nt * tm, d), BF16),
        compiler_params=pltpu.CompilerParams(
            dimension_semantics=("arbitrary",), vmem_limit_bytes=VMEM_LIMIT),
        name="dispatch",
    )(tile_e, uoff, unc, xb, post)


def _gffn_kernel(te_ref, tv_ref, x_ref, wg_ref, wu_ref, wd_ref, o_ref, acc_ref, *, nf):
    i = pl.program_id(0)
    f = pl.program_id(1)
    valid = tv_ref[i] == 1

    @pl.when(valid)
    def _():
        xb = x_ref[...]
        gate = _dot(xb, wg_ref[0, 0])
        up = _dot(xb, wu_ref[0, 0])
        hid = (gate * jax.nn.sigmoid(gate) * up).astype(BF16)
        part = _dot(hid, wd_ref[0, 0])

        @pl.when(f == 0)
        def _():
            acc_ref[...] = part

        @pl.when(f > 0)
        def _():
            acc_ref[...] += part

    @pl.when(f == nf - 1)
    def _():
        o_ref[...] = jnp.where(valid, acc_ref[...], 0.0).astype(BF16)


def _gffn(tile_e, tile_v, xs, li, wg, wu, wd, tm, fc):
    p, d = xs.shape
    nt = p // tm
    nf = wg.shape[3] // fc
    fidx = lambda i, f, te, tv: jnp.where(tv[i] == 1, f, nf - 1)
    grid_spec = pltpu.PrefetchScalarGridSpec(
        num_scalar_prefetch=2,
        grid=(nt, nf),
        in_specs=[
            pl.BlockSpec((tm, d), lambda i, f, te, tv: (i, 0)),
            pl.BlockSpec((1, 1, d, fc), lambda i, f, te, tv: (li, te[i], 0, fidx(i, f, te, tv))),
            pl.BlockSpec((1, 1, d, fc), lambda i, f, te, tv: (li, te[i], 0, fidx(i, f, te, tv))),
            pl.BlockSpec((1, 1, fc, d), lambda i, f, te, tv: (li, te[i], fidx(i, f, te, tv), 0)),
        ],
        out_specs=pl.BlockSpec((tm, d), lambda i, f, te, tv: (i, 0)),
        scratch_shapes=[pltpu.VMEM((tm, d), F32)],
    )
    return pl.pallas_call(
        functools.partial(_gffn_kernel, nf=nf),
        grid_spec=grid_spec,
        out_shape=jax.ShapeDtypeStruct((p, d), BF16),
        compiler_params=pltpu.CompilerParams(
            dimension_semantics=("arbitrary", "arbitrary"), vmem_limit_bytes=VMEM_LIMIT),
        name="grouped_ffn",
    )(tile_e, tile_v, xs, wg, wu, wd)


def _combine_kernel(ral_ref, rlo_ref, rhi_ref, x_ref, rec_ref, g_ref, b_ref, ys_ref, o_ref,
                    win_ref, xwin_ref, acc_ref, sems, xsem, *, w):
    j = pl.program_id(0)
    nj = pl.num_programs(0)
    slot = lax.rem(j, 2)
    p_rows = ys_ref.shape[0]

    def win_copy(jj, e, sl):
        r = pl.multiple_of(ral_ref[jj * N_EXPERTS + e], 16)
        return pltpu.make_async_copy(ys_ref.at[pl.ds(r, w)], win_ref.at[sl, pl.ds(e * w, w)],
                                     sems.at[sl, e])

    @pl.when(j == 0)
    def _():
        for e in range(N_EXPERTS):
            win_copy(0, e, 0).start()

    @pl.when(j + 1 < nj)
    def _():
        for e in range(N_EXPERTS):
            win_copy(j + 1, e, 1 - slot).start()

    rec = rec_ref[...]
    p1, p2, g1, g2 = rec[:, 0:1], rec[:, 1:2], rec[:, 2:3], rec[:, 3:4]

    def sel(base, lo, hi):
        rows = base + lax.broadcasted_iota(jnp.int32, (1, w), 1)
        rp = jnp.where((rows >= lo) & (rows < hi), rows, -1).astype(F32)
        return (jnp.where(p1 == rp, g1, 0.0) + jnp.where(p2 == rp, g2, 0.0)).astype(BF16)

    lane = lax.broadcasted_iota(jnp.int32, (1, N_EXPERTS * w), 1)
    rp = jnp.full((1, N_EXPERTS * w), -1, jnp.int32)
    for e in range(N_EXPERTS):
        win_copy(j, e, slot).wait()
        k = j * N_EXPERTS + e
        rows = ral_ref[k] + lane - e * w
        mine = ((lane >= e * w) & (lane < (e + 1) * w)
                & (rows >= rlo_ref[k]) & (rows < rhi_ref[k]))
        rp = jnp.where(mine, rows, rp)
    rp = rp.astype(F32)
    sel_all = (jnp.where(p1 == rp, g1, 0.0) + jnp.where(p2 == rp, g2, 0.0)).astype(BF16)
    acc_ref[...] = _dot(sel_all, win_ref[slot])

    for e in range(N_EXPERTS):
        k = j * N_EXPERTS + e
        nwin = (rhi_ref[k] - ral_ref[k] + (w - 1)) // w

        def extra(wi, carry, k=k):
            nominal = ral_ref[k] + wi * w
            base = pl.multiple_of(jnp.minimum(nominal, p_rows - w), 16)
            cp = pltpu.make_async_copy(ys_ref.at[pl.ds(base, w)], xwin_ref, xsem)
            cp.start()
            cp.wait()
            acc_ref[...] += _dot(sel(base, nominal, rhi_ref[k]), xwin_ref[...])
            return carry

        lax.fori_loop(1, nwin, extra, 0)

    o_ref[...] = _layer_norm_rows(DN_ALPHA * x_ref[...] + acc_ref[...], g_ref[...], b_ref[...])


def _combine(ral, rlo, rhi, x2d, rec2, g, b, ys, tm, w):
    t, d = x2d.shape
    row = lambda c: pl.BlockSpec((tm, c), lambda i, *_: (i, 0))
    full = lambda shape: pl.BlockSpec(shape, lambda i, *_: (0,) * len(shape))
    grid_spec = pltpu.PrefetchScalarGridSpec(
        num_scalar_prefetch=3,
        grid=(t // tm,),
        in_specs=[row(d), row(rec2.shape[1]), full(g.shape), full(b.shape),
                  pl.BlockSpec(memory_space=pl.ANY)],
        out_specs=row(d),
        scratch_shapes=[pltpu.VMEM((2, N_EXPERTS * w, d), BF16), pltpu.VMEM((w, d), BF16),
                        pltpu.VMEM((tm, d), F32), pltpu.SemaphoreType.DMA((2, N_EXPERTS)),
                        pltpu.SemaphoreType.DMA(())],
    )
    return pl.pallas_call(
        functools.partial(_combine_kernel, w=w),
        grid_spec=grid_spec,
        out_shape=jax.ShapeDtypeStruct((t, d), F32),
        compiler_params=pltpu.CompilerParams(
            dimension_semantics=("arbitrary",), vmem_limit_bytes=VMEM_LIMIT),
        name="combine",
    )(ral, rlo, rhi, x2d, rec2, g, b, ys)


def _route_plan(rec, nt, tm_e, unit, kc, tm_c, w):
    t = rec.shape[1]
    i32 = jnp.int32
    ar = jnp.arange(N_EXPERTS, dtype=i32)
    m1 = (rec[0].astype(i32)[:, None] == ar[None, :]).astype(i32)
    m2 = (rec[1].astype(i32)[:, None] == ar[None, :]).astype(i32)
    m = m1 + m2
    cinc = jnp.cumsum(m, axis=0)
    counts = cinc[-1]
    padded = ((counts + tm_e - 1) // tm_e) * tm_e
    ends = jnp.cumsum(padded)
    starts = ends - padded
    posmat = starts[None, :] + (cinc - m)
    pos1 = jnp.sum(m1 * posmat, axis=1)
    pos2 = jnp.sum(m2 * posmat, axis=1)
    post = jnp.where(m.T > 0, posmat.T, -1)
    rec2 = jnp.stack([pos1.astype(F32), pos2.astype(F32), rec[2], rec[3]]
                     + [jnp.zeros((t,), F32)] * 4, axis=1)
    tile_start = jnp.arange(nt, dtype=i32) * tm_e
    tile_e = jnp.sum((tile_start[:, None] >= ends[None, :]).astype(i32), axis=1)
    tile_v = tile_e < N_EXPERTS
    last_e = jnp.max(jnp.where(tile_v, tile_e, 0))
    tile_e = jnp.where(tile_v, tile_e, last_e)
    upt = tm_e // unit
    nu = nt * upt
    oh = (jnp.repeat(tile_e, upt)[:, None] == ar[None, :]).astype(i32)
    q0 = jnp.arange(nu, dtype=i32) * unit - jnp.sum(oh * starts[None, :], axis=1)
    q1 = jnp.minimum(q0 + unit, jnp.sum(oh * counts[None, :], axis=1))
    has = jnp.repeat(tile_v, upt) & (q1 > q0)
    cb = jnp.concatenate([jnp.zeros((1, N_EXPERTS), i32), cinc[LANES - 1::LANES]], axis=0)
    cbu = jnp.sum(oh[:, None, :] * cb[None, :, :], axis=2)
    jlo = jnp.sum((cbu[:, 1:] <= q0[:, None]).astype(i32), axis=1)
    jend = jnp.sum((cbu[:, :-1] < q1[:, None]).astype(i32), axis=1)
    uoff = jnp.where(has, jlo * LANES, 0)
    unc = jnp.where(has, (jend * LANES - uoff + kc - 1) // kc, 0)
    cbt = jnp.concatenate([jnp.zeros((1, N_EXPERTS), i32), cinc[tm_c - 1::tm_c]], axis=0)
    r_lo = starts[None, :] + cbt[:-1]
    r_hi = starts[None, :] + cbt[1:]
    ral = jnp.minimum((r_lo // 16) * 16, nt * tm_e - w)
    return (rec2, post, tile_e, tile_v.astype(i32), uoff, unc,
            ral.reshape(-1), r_lo.reshape(-1), r_hi.reshape(-1))


def _moe(x2d, xb, rec, li, wg, wu, wd, g, b, tm_c, tm_e, fc, unit, kc, w):
    t, d = x2d.shape
    nt = (2 * t) // tm_e + N_EXPERTS
    rec2, post, tile_e, tile_v, uoff, unc, ral, rlo, rhi = _route_plan(
        rec, nt, tm_e, unit, kc, tm_c, w)
    xs = _dispatch(tile_e, uoff, unc, xb, post, nt, tm_e, unit, kc)
    ys = _gffn(tile_e, tile_v, xs, li, wg, wu, wd, tm_e, fc)
    return _combine(ral, rlo, rhi, x2d, rec2, g, b, ys, tm_c, w)


def _rotate_half_cols(w):
    half = w.shape[-1] // 2
    return jnp.concatenate([-w[..., half:], w[..., :half]], axis=-1)


def _t5_bucket(n):
    max_exact = NUM_BUCKETS // 2
    nf = jnp.maximum(n, 1).astype(F32)
    large = max_exact + (jnp.log(nf / max_exact) / math.log(MAX_DISTANCE / max_exact)
                         * (NUM_BUCKETS - max_exact)).astype(jnp.int32)
    large = jnp.minimum(large, NUM_BUCKETS - 1)
    return jnp.where(n < max_exact, n, large)


def _pick(n, prefs):
    for p in prefs:
        if n % p == 0:
            return p
    raise ValueError(f"no tile for extent {n}")


def kernel(x, rel_bias, w_in, mla_q_norm, mla_kv_norm, mla_w_uq, mla_w_uk, mla_w_uv, diff_lambda, diff_norm, w_o, ln1_g, ln1_b, ln2_g, ln2_b, ffn_w_gate, ffn_w_up, ffn_w_down, moe_router, moe_w_gate, moe_w_up, moe_w_down):
    b, s, d = x.shape
    t = b * s
    assert d == D_MODEL
    ta = _pick(s, (512, 256))
    assert ta % ATT_BLK == 0
    assert s >= 2 * ta or s == ta
    tm_proj = _pick(s, (512, 256, 128))
    tm_tok = _pick(t, (512, 256, 128))
    tm_e = _pick(2 * t, (1024, 512, 256))
    tm_c = _pick(t, (512, 256))
    unit, wrows = 256, 192
    kc = 1280
    assert tm_e % unit == 0 and t >= kc and t % LANES == 0
    fc = D_FF_EXPERT // 4

    pos = jnp.arange(s, dtype=F32)
    inv = 1.0 / (ROPE_THETA ** (jnp.arange(0, MLA_ROPE, 2, dtype=F32) / MLA_ROPE))
    ang = pos[:, None] * inv[None, :]
    cos4 = jnp.tile(jnp.cos(ang), (1, 2 * MLA_HEADS))
    sin4 = jnp.tile(jnp.sin(ang), (1, 2 * MLA_HEADS))
    rb = rel_bias.astype(F32)
    kk = jnp.arange(ta)[:, None]
    qq = jnp.arange(ta)[None, :]
    tiles = []
    for off in (0, ta):
        dist = qq + off - kk
        bucket = _t5_bucket(jnp.maximum(dist, 0))
        bt = jnp.zeros((DIFF_HEADS, ta, ta), F32)
        for bk in range(NUM_BUCKETS):
            bt = bt + jnp.where((bucket == bk)[None], rb[bk][:, None, None], 0.0)
        tiles.append(jnp.where((dist >= 0)[None], bt * LOG2E, NEG_BIG))
    bias_tiles = jnp.stack(tiles, axis=1)
    assert ta + 1 >= MAX_DISTANCE
    cfar = rb[NUM_BUCKETS - 1] * LOG2E
    moe_wg, moe_wu, moe_wd = (w.astype(BF16) for w in (moe_w_gate, moe_w_up, moe_w_down))

    xc = x
    for l in range(DEPTH):
        lambda_init = 0.8 - 0.6 * math.exp(-0.3 * l)
        wl = w_in[l]
        kr_w = wl[:, _KR0:_KR0 + MLA_ROPE]
        win = jnp.concatenate([wl[:, :_KR0 + MLA_ROPE], _rotate_half_cols(kr_w),
                               wl[:, _KR0 + MLA_ROPE:]], axis=1).astype(BF16)
        uq = mla_w_uq[l].reshape(MLA_Q_RANK, MLA_HEADS, MLA_QK)
        uq_n = uq[:, :, :MLA_NOPE].reshape(MLA_Q_RANK, -1)
        uq_r = uq[:, :, MLA_NOPE:]
        wuq = jnp.concatenate([uq_n, uq_r.reshape(MLA_Q_RANK, -1),
                               _rotate_half_cols(uq_r).reshape(MLA_Q_RANK, -1)], axis=1).astype(BF16)
        wukv = jnp.concatenate([mla_w_uk[l], mla_w_uv[l]], axis=1).astype(BF16)

        qT, k, vT, dqT, dk, dvT = _proj(
            xc, win, mla_q_norm[l][None, :], mla_kv_norm[l][None, :], wuq, wukv, cos4, sin4, tm_proj)
        a_mla = _mla_attn(qT, k, vT, ta, MLA_HP).reshape(t, MLA_W)
        a_diff = _diff_attn(cfar, dqT, dk, dvT, bias_tiles, diff_lambda[l].astype(F32),
                            diff_norm[l].astype(F32)[:, None], ta, DIFF_HP,
                            lambda_init).reshape(t, DIFF_W)
        x2d = xc.reshape(t, d)
        wo = w_o[l].astype(BF16)
        g1, b1 = ln1_g[l][None, :], ln1_b[l][None, :]
        g2, b2 = ln2_g[l][None, :], ln2_b[l][None, :]
        if l % 2 == 0:
            x1 = _oproj(a_mla, a_diff, x2d, wo, g1, b1, tm_tok)
            i = l // 2
            x2 = _ffn(x1, ffn_w_gate[i].astype(BF16), ffn_w_up[i].astype(BF16),
                      ffn_w_down[i].astype(BF16), g2, b2, tm_tok)
        else:
            i = l // 2
            rpad = jnp.pad(moe_router[i].astype(F32).T, ((0, ROUTER_ROWS - N_EXPERTS), (0, 0)))
            rhi = rpad.astype(BF16)
            rlo = (rpad - rhi.astype(F32)).astype(BF16)
            x1, x1b, rec = _oproj(a_mla, a_diff, x2d, wo, g1, b1, tm_tok,
                                  router=(jnp.concatenate([rhi, rlo], axis=0), rhi))
            x2 = _moe(x1, x1b, rec, i, moe_wg, moe_wu, moe_wd, g2, b2,
                      tm_c, tm_e, fc, unit, kc, wrows)
        xc = x2.reshape(b, s, d)
    return xc
```

```python
import functools
import math

import jax
import jax.numpy as jnp
from jax import lax
from jax.experimental import pallas as pl
from jax.experimental.pallas import tpu as pltpu

D_MODEL = 1024
DEPTH = 4
MLA_HEADS = 4
MLA_NOPE = 128
MLA_ROPE = 64
MLA_V = 128
MLA_Q_RANK = 384
MLA_KV_RANK = 256
ROPE_THETA = 10000.0
DIFF_HEADS = 4
DIFF_QK = 64
DIFF_V = 2 * DIFF_QK
NUM_BUCKETS = 32
MAX_DISTANCE = 128
D_FF_DENSE = 2816
N_EXPERTS = 8
D_FF_EXPERT = 3584
DN_ALPHA = (2 * DEPTH) ** 0.25

MLA_QK = MLA_NOPE + MLA_ROPE
MLA_W = MLA_HEADS * MLA_V
DIFF_W = DIFF_HEADS * DIFF_V
DIFF_VA = DIFF_V + 16
_CQ0, _CKV0 = 0, MLA_Q_RANK
_KR0 = _CKV0 + MLA_KV_RANK
_DQ0 = _KR0 + 2 * MLA_ROPE
_DK0 = _DQ0 + DIFF_W
_DV0 = _DK0 + DIFF_W
IN_COLS_W = _DV0 + DIFF_W

LANES = 128
VMEM_LIMIT = 56 * 1024 * 1024
NEG_BIG = -1e30
ROUTER_ROWS = 16
REC_ROWS = 8
LOG2E = math.log2(math.e)
ATT_BLK = 256
MLA_HP = 4
DIFF_HP = 2

BF16 = jnp.bfloat16
F32 = jnp.float32


def _dot(a, b):
    return jnp.dot(a, b, preferred_element_type=F32)


def _layer_norm_rows(y, g, b):
    mu = jnp.mean(y, axis=-1, keepdims=True)
    d = y - mu
    var = jnp.mean(d * d, axis=-1, keepdims=True)
    return d * lax.rsqrt(var + 1e-5) * g + b


def _rms_rows(y, g, eps):
    return y * lax.rsqrt(jnp.mean(y * y, axis=-1, keepdims=True) + eps) * g


def _proj_kernel(x_ref, win_ref, qn_ref, kvn_ref, wuq_ref, wukv_ref, cos_ref, sin_ref,
                 qT_ref, k_ref, vT_ref, dqT_ref, dk_ref, dvT_ref):
    x = x_ref[0].astype(BF16)
    h = _dot(x, win_ref[...])
    cqn = _rms_rows(h[:, _CQ0:_CQ0 + MLA_Q_RANK], qn_ref[...], 1e-6)
    ckvn = _rms_rows(h[:, _CKV0:_CKV0 + MLA_KV_RANK], kvn_ref[...], 1e-6)
    q = _dot(cqn.astype(BF16), wuq_ref[...]) * (MLA_QK ** -0.5 * LOG2E)
    kv = _dot(ckvn.astype(BF16), wukv_ref[...])
    cos4, sin4 = cos_ref[...], sin_ref[...]
    nr = MLA_HEADS * MLA_ROPE
    qr = q[:, MLA_W:MLA_W + nr] * cos4 + q[:, MLA_W + nr:] * sin4
    kr = (h[:, _KR0:_KR0 + MLA_ROPE] * cos4[:, :MLA_ROPE]
          + h[:, _KR0 + MLA_ROPE:_KR0 + 2 * MLA_ROPE] * sin4[:, :MLA_ROPE])
    qrT = qr.T
    for hh in range(MLA_HEADS):
        c0 = hh * LANES
        qnT = q[:, c0:c0 + MLA_NOPE].T
        qT_ref[0, hh] = jnp.concatenate(
            [qnT, qrT[hh * MLA_ROPE:(hh + 1) * MLA_ROPE]], axis=0).astype(BF16)
        k_ref[0, hh] = jnp.concatenate([kv[:, c0:c0 + MLA_NOPE], kr], axis=1).astype(BF16)
        vT_ref[0, hh] = kv[:, MLA_W + c0:MLA_W + c0 + MLA_V].T.astype(BF16)
        dqT_ref[0, hh] = (h[:, _DQ0 + c0:_DQ0 + c0 + LANES]
                          * (DIFF_QK ** -0.5 * LOG2E)).T.astype(BF16)
        dk_ref[0, hh] = h[:, _DK0 + c0:_DK0 + c0 + LANES].astype(BF16)
        dvT_ref[0, hh] = jnp.concatenate(
            [h[:, _DV0 + c0:_DV0 + c0 + DIFF_V].T, jnp.ones((DIFF_VA - DIFF_V, x.shape[0]), F32)],
            axis=0).astype(BF16)


def _proj(x, win, qn, kvn, wuq, wukv, cos4, sin4, tm):
    b, s, d = x.shape
    hd = MLA_HEADS
    full = lambda shape: pl.BlockSpec(shape, lambda bi, i: (0,) * len(shape))
    tmaj = lambda w: pl.BlockSpec((1, hd, tm, w), lambda bi, i: (bi, 0, i, 0))
    fmaj = lambda w: pl.BlockSpec((1, hd, w, tm), lambda bi, i: (bi, 0, 0, i))
    sds = jax.ShapeDtypeStruct
    return pl.pallas_call(
        _proj_kernel,
        grid=(b, s // tm),
        in_specs=[
            pl.BlockSpec((1, tm, d), lambda bi, i: (bi, i, 0)),
            full(win.shape), full(qn.shape), full(kvn.shape), full(wuq.shape), full(wukv.shape),
            pl.BlockSpec((tm, cos4.shape[1]), lambda bi, i: (i, 0)),
            pl.BlockSpec((tm, sin4.shape[1]), lambda bi, i: (i, 0)),
        ],
        out_specs=[fmaj(MLA_QK), tmaj(MLA_QK), fmaj(MLA_V), fmaj(LANES), tmaj(LANES), fmaj(DIFF_VA)],
        out_shape=[
            sds((b, hd, MLA_QK, s), BF16), sds((b, hd, s, MLA_QK), BF16), sds((b, hd, MLA_V, s), BF16),
            sds((b, hd, LANES, s), BF16), sds((b, hd, s, LANES), BF16), sds((b, hd, DIFF_VA, s), BF16),
        ],
        compiler_params=pltpu.CompilerParams(
            dimension_semantics=("arbitrary", "arbitrary"), vmem_limit_bytes=VMEM_LIMIT),
        name="proj",
    )(x, win, qn, kvn, wuq, wukv, cos4, sin4)


def _softmax_step(s, vj, m_ref, l_ref, acc_ref, shift=None):
    m_prev = m_ref[...]
    smax = jnp.max(s, axis=0, keepdims=True)
    if shift is not None:
        smax = smax + shift
    m_new = jnp.maximum(m_prev, smax)
    a = jnp.exp2(m_prev - m_new)
    p = jnp.exp2(s - (m_new if shift is None else m_new - shift))
    if l_ref is not None:
        l_ref[...] = a * l_ref[...] + jnp.sum(p, axis=0, keepdims=True)
    acc_ref[...] = a * acc_ref[...] + _dot(vj, p.astype(BF16))
    m_ref[...] = m_new


def _chain_scratch(n, dv):
    return [pltpu.VMEM((1, ATT_BLK), F32)] * (2 * n) + [pltpu.VMEM((dv, ATT_BLK), F32)] * n


def _init_chains(state):
    n = len(state) // 3
    m_refs, l_refs, acc_refs = state[:n], state[n:2 * n], state[2 * n:]
    for m_ref, l_ref, acc_ref in zip(m_refs, l_refs, acc_refs):
        m_ref[...] = jnp.full(m_ref.shape, NEG_BIG, F32)
        l_ref[...] = jnp.zeros(l_ref.shape, F32)
        acc_ref[...] = jnp.zeros(acc_ref.shape, F32)
    return m_refs, l_refs, acc_refs


def _pipelined_tiles(qi, logits, softmax_pv):
    logits(0, 0)

    def body(i, carry):
        j = 2 * i
        logits(j + 1, 1)
        softmax_pv(j, 0, False)
        logits(j + 2, 0)
        softmax_pv(j + 1, 1, False)
        return carry

    lax.fori_loop(0, lax.shift_right_logical(qi, 1), body, 0)
    odd = lax.rem(qi, 2) == 1

    @pl.when(odd)
    def _():
        logits(qi, 1)
        softmax_pv(qi - 1, 0, False)
        softmax_pv(qi, 1, True)

    @pl.when(jnp.logical_not(odd))
    def _():
        softmax_pv(qi, 0, True)


def _pipelined_tiles_biased(qi, logits, softmax_pv):
    logits(0, 0)
    nfar = jnp.maximum(qi - 1, 0)

    def body(i, carry):
        j = 2 * i
        logits(j + 1, 1)
        softmax_pv(j, 0, "far")
        logits(j + 2, 0)
        softmax_pv(j + 1, 1, "far")
        return carry

    lax.fori_loop(0, lax.shift_right_logical(nfar, 1), body, 0)
    odd = lax.rem(nfar, 2) == 1

    @pl.when(qi == 0)
    def _():
        softmax_pv(qi, 0, "diag")

    @pl.when(jnp.logical_and(qi >= 1, jnp.logical_not(odd)))
    def _():
        logits(qi, 1)
        softmax_pv(qi - 1, 0, "near")
        softmax_pv(qi, 1, "diag")

    @pl.when(odd)
    def _():
        logits(qi - 1, 1)
        softmax_pv(qi - 2, 0, "far")
        logits(qi, 0)
        softmax_pv(qi - 1, 1, "near")
        softmax_pv(qi, 0, "diag")


def _mla_attn_kernel(qT_ref, k_ref, vT_ref, o_ref, sa_ref, sb_ref, *state, ta, hp):
    qi = pl.program_id(2)
    m_refs, l_refs, acc_refs = _init_chains(state)
    s_refs = (sa_ref, sb_ref)

    nb = ta // ATT_BLK
    blk = ATT_BLK

    def logits(j, buf):
        off = pl.multiple_of(j * ta, ta)
        for h in range(hp):
            s_refs[buf][h] = _dot(k_ref[0, h, pl.ds(off, ta), :], qT_ref[0, h])

    def softmax_pv(j, buf, diagonal):
        off = pl.multiple_of(j * ta, ta)
        for h in range(hp):
            for qs in range(nb):
                nk = (qs + 1) * blk if diagonal else ta
                s = s_refs[buf][h, :nk, qs * blk:(qs + 1) * blk]
                if diagonal:
                    kpos = lax.broadcasted_iota(jnp.int32, (nk, blk), 0)
                    qpos = lax.broadcasted_iota(jnp.int32, (nk, blk), 1) + qs * blk
                    s = jnp.where(kpos <= qpos, s, NEG_BIG)
                c = h * nb + qs
                _softmax_step(s, vT_ref[0, h, :, pl.ds(off, nk)], m_refs[c], l_refs[c], acc_refs[c])

    _pipelined_tiles(qi, logits, softmax_pv)
    for h in range(hp):
        for qs in range(nb):
            c = h * nb + qs
            out = acc_refs[c][...] * (1.0 / l_refs[c][...])
            o_ref[0, qs * blk:(qs + 1) * blk, h * MLA_V:(h + 1) * MLA_V] = out.T.astype(BF16)


def _mla_attn(qT, k, vT, ta, hp):
    b, hd, _, s = qT.shape
    return pl.pallas_call(
        functools.partial(_mla_attn_kernel, ta=ta, hp=hp),
        grid=(b, hd // hp, s // ta),
        in_specs=[
            pl.BlockSpec((1, hp, MLA_QK, ta), lambda bi, h, i: (bi, h, 0, i)),
            pl.BlockSpec((1, hp, s, MLA_QK), lambda bi, h, i: (bi, h, 0, 0)),
            pl.BlockSpec((1, hp, MLA_V, s), lambda bi, h, i: (bi, h, 0, 0)),
        ],
        out_specs=pl.BlockSpec((1, ta, hp * MLA_V), lambda bi, h, i: (bi, i, h)),
        out_shape=jax.ShapeDtypeStruct((b, s, hd * MLA_V), BF16),
        scratch_shapes=([pltpu.VMEM((hp, ta, ta), F32)] * 2
                        + _chain_scratch(hp * ta // ATT_BLK, MLA_V)),
        compiler_params=pltpu.CompilerParams(
            dimension_semantics=("arbitrary",) * 3, vmem_limit_bytes=VMEM_LIMIT),
        name="mla_attn",
    )(qT, k, vT)


def _diff_attn_kernel(cfar_ref, qT_ref, k_ref, vT_ref, bias_ref, lamp_ref, g_ref, o_ref,
                      sa_ref, sb_ref, *state, ta, hp, lambda_init):
    hg = pl.program_id(1)
    qi = pl.program_id(2)
    s_refs = (sa_ref, sb_ref)
    zeros = jnp.zeros((DIFF_QK, ta), BF16)
    qs = []
    for h in range(hp):
        qT = qT_ref[0, h]
        qs.append((jnp.concatenate([qT[:DIFF_QK], zeros], axis=0),
                   jnp.concatenate([zeros, qT[DIFF_QK:]], axis=0)))
    m_refs, l_refs, acc_refs = _init_chains(state)

    nb = ta // ATT_BLK
    blk = ATT_BLK

    def logits(j, buf):
        off = pl.multiple_of(j * ta, ta)
        for h in range(hp):
            kj = k_ref[0, h, pl.ds(off, ta), :]
            for c in range(2):
                s_refs[buf][2 * h + c] = _dot(kj, qs[h][c])

    def softmax_pv(j, buf, kind):
        off = pl.multiple_of(j * ta, ta)
        diagonal = kind == "diag"
        for h in range(hp):
            for qb in range(nb):
                nk = (qb + 1) * blk if diagonal else ta
                vj = vT_ref[0, h, :, pl.ds(off, nk)]
                for c in range(2):
                    s = s_refs[buf][2 * h + c, :nk, qb * blk:(qb + 1) * blk]
                    i = (2 * h + c) * nb + qb
                    if kind == "far" or (kind == "near" and qb * blk + 1 >= MAX_DISTANCE):
                        _softmax_step(s, vj, m_refs[i], None, acc_refs[i],
                                      shift=cfar_ref[hg * hp + h])
                    else:
                        bias = bias_ref[h, 0 if diagonal else 1, :nk, qb * blk:(qb + 1) * blk]
                        _softmax_step(s + bias, vj, m_refs[i], None, acc_refs[i])

    _pipelined_tiles_biased(qi, logits, softmax_pv)

    lp = lamp_ref[...]
    lam = (jnp.exp(jnp.sum(lp[0:1] * lp[1:2], axis=-1, keepdims=True))
           - jnp.exp(jnp.sum(lp[2:3] * lp[3:4], axis=-1, keepdims=True)) + lambda_init)
    for h in range(hp):
        for qb in range(nb):
            a0 = acc_refs[(2 * h) * nb + qb][...]
            a1 = acc_refs[(2 * h + 1) * nb + qb][...]
            out = (a0[:DIFF_V] * (1.0 / a0[DIFF_V:DIFF_V + 1])
                   - lam * (a1[:DIFF_V] * (1.0 / a1[DIFF_V:DIFF_V + 1])))
            ms = jnp.mean(out * out, axis=0, keepdims=True)
            out = out * lax.rsqrt(ms + 1e-5) * g_ref[...] * (1.0 - lambda_init)
            o_ref[0, qb * blk:(qb + 1) * blk, h * DIFF_V:(h + 1) * DIFF_V] = out.T.astype(BF16)


def _diff_attn(cfar, qT, k, vT, bias_tiles, lam_params, g_col, ta, hp, lambda_init):
    b, hd, _, s = qT.shape
    grid_spec = pltpu.PrefetchScalarGridSpec(
        num_scalar_prefetch=1,
        grid=(b, hd // hp, s // ta),
        in_specs=[
            pl.BlockSpec((1, hp, LANES, ta), lambda bi, h, i, cf: (bi, h, 0, i)),
            pl.BlockSpec((1, hp, s, LANES), lambda bi, h, i, cf: (bi, h, 0, 0)),
            pl.BlockSpec((1, hp, DIFF_VA, s), lambda bi, h, i, cf: (bi, h, 0, 0)),
            pl.BlockSpec((hp, 2, ta, ta), lambda bi, h, i, cf: (h, 0, 0, 0)),
            pl.BlockSpec(lam_params.shape, lambda bi, h, i, cf: (0, 0)),
            pl.BlockSpec(g_col.shape, lambda bi, h, i, cf: (0, 0)),
        ],
        out_specs=pl.BlockSpec((1, ta, hp * DIFF_V), lambda bi, h, i, cf: (bi, i, h)),
        scratch_shapes=([pltpu.VMEM((2 * hp, ta, ta), F32)] * 2
                        + _chain_scratch(2 * hp * ta // ATT_BLK, DIFF_VA)),
    )
    return pl.pallas_call(
        functools.partial(_diff_attn_kernel, ta=ta, hp=hp, lambda_init=lambda_init),
        grid_spec=grid_spec,
        out_shape=jax.ShapeDtypeStruct((b, s, hd * DIFF_V), BF16),
        compiler_params=pltpu.CompilerParams(
            dimension_semantics=("arbitrary",) * 3, vmem_limit_bytes=VMEM_LIMIT),
        name="diff_attn",
    )(cfar, qT, k, vT, bias_tiles, lam_params, g_col)


def _split_bf16(v):
    hi = v.astype(BF16)
    return hi, (v - hi.astype(F32)).astype(BF16)


def _oproj_kernel(am_ref, ad_ref, x_ref, wo_ref, g_ref, b_ref, *rest, route):
    mix = _dot(am_ref[...], wo_ref[:MLA_W, :]) + _dot(ad_ref[...], wo_ref[MLA_W:, :])
    y = _layer_norm_rows(DN_ALPHA * x_ref[...] + mix, g_ref[...], b_ref[...])
    if not route:
        (o_ref,) = rest
        o_ref[...] = y
        return
    rhl_ref, rh_ref, o_ref, ob_ref, r_ref = rest
    yhi, ylo = _split_bf16(y)
    nt = (((1,), (1,)), ((), ()))
    both = lax.dot_general(rhl_ref[...], yhi, nt, preferred_element_type=F32)
    logits = (both[:ROUTER_ROWS] + both[ROUTER_ROWS:]
              + lax.dot_general(rh_ref[...], ylo, nt, preferred_element_type=F32))
    row = lax.broadcasted_iota(jnp.int32, logits.shape, 0)
    logits = jnp.where(row < N_EXPERTS, logits, NEG_BIG)
    m1 = jnp.max(logits, axis=0, keepdims=True)
    i1 = jnp.min(jnp.where(logits == m1, row, ROUTER_ROWS), axis=0, keepdims=True)
    rest_l = jnp.where(row == i1, NEG_BIG, logits)
    m2 = jnp.max(rest_l, axis=0, keepdims=True)
    i2 = jnp.min(jnp.where(rest_l == m2, row, ROUTER_ROWS), axis=0, keepdims=True)
    e2 = jnp.exp(m2 - m1)
    g1 = 1.0 / (1.0 + e2)
    g2 = e2 * g1
    row8 = lax.broadcasted_iota(jnp.int32, r_ref.shape, 0)
    rec = jnp.where(row8 == 0, i1.astype(F32),
                    jnp.where(row8 == 1, i2.astype(F32),
                              jnp.where(row8 == 2, g1, jnp.where(row8 == 3, g2, 0.0))))
    o_ref[...] = y
    ob_ref[...] = yhi
    r_ref[...] = rec


def _oproj(a_mla, a_diff, x2d, wo, g, b, tm, router=None):
    t, d = x2d.shape
    row = lambda w: pl.BlockSpec((tm, w), lambda i: (i, 0))
    full = lambda shape: pl.BlockSpec(shape, lambda i: (0,) * len(shape))
    in_specs = [row(MLA_W), row(DIFF_W), row(d), full(wo.shape), full(g.shape), full(b.shape)]
    args = [a_mla, a_diff, x2d, wo, g, b]
    out_specs = [row(d)]
    out_shape = [jax.ShapeDtypeStruct((t, d), F32)]
    if router is not None:
        in_specs += [full(router[0].shape), full(router[1].shape)]
        args += list(router)
        out_specs += [row(d), pl.BlockSpec((REC_ROWS, tm), lambda i: (0, i))]
        out_shape += [jax.ShapeDtypeStruct((t, d), BF16), jax.ShapeDtypeStruct((REC_ROWS, t), F32)]
    res = pl.pallas_call(
        functools.partial(_oproj_kernel, route=router is not None),
        grid=(t // tm,),
        in_specs=in_specs, out_specs=out_specs, out_shape=out_shape,
        compiler_params=pltpu.CompilerParams(
            dimension_semantics=("arbitrary",), vmem_limit_bytes=VMEM_LIMIT),
        name="oproj_route" if router is not None else "oproj",
    )(*args)
    return res if router is not None else res[0]


def _ffn_kernel(x_ref, wg_ref, wu_ref, wd_ref, g_ref, b_ref, o_ref):
    x = x_ref[...]
    xb = x.astype(BF16)
    gate = _dot(xb, wg_ref[...])
    up = _dot(xb, wu_ref[...])
    hid = (gate * jax.nn.sigmoid(gate) * up).astype(BF16)
    f = _dot(hid, wd_ref[...])
    o_ref[...] = _layer_norm_rows(DN_ALPHA * x + f, g_ref[...], b_ref[...])


def _ffn(x2d, wg, wu, wd, g, b, tm):
    t, d = x2d.shape
    row = pl.BlockSpec((tm, d), lambda i: (i, 0))
    const = lambda shape: pl.BlockSpec(shape, lambda i: (0,) * len(shape),
                                       pipeline_mode=pl.Buffered(1))
    return pl.pallas_call(
        _ffn_kernel,
        grid=(t // tm,),
        in_specs=[row, const(wg.shape), const(wu.shape), const(wd.shape),
                  const(g.shape), const(b.shape)],
        out_specs=row,
        out_shape=jax.ShapeDtypeStruct((t, d), F32),
        compiler_params=pltpu.CompilerParams(
            dimension_semantics=("arbitrary",), vmem_limit_bytes=VMEM_LIMIT),
        name="ffn",
    )(x2d, wg, wu, wd, g, b)


def _dispatch_kernel(te_ref, uoff_ref, unc_ref, x_ref, post_ref, o_ref, acc_ref, *, tm, unit, kc):
    i = pl.program_id(0)
    e = te_ref[i]
    t = x_ref.shape[0]
    upt = tm // unit
    for r in range(upt):
        u = i * upt + r
        n = unc_ref[u]
        start0 = uoff_ref[u]
        row_pos = u * unit + lax.broadcasted_iota(jnp.int32, (unit, 1), 0)

        def chunk(c):
            nominal = start0 + c * kc
            off = pl.multiple_of(jnp.minimum(nominal, t - kc), LANES)
            dest = post_ref[pl.ds(e, 1), pl.ds(off, kc)]
            tok = off + lax.broadcasted_iota(jnp.int32, (1, kc), 1)
            dest = jnp.where(tok >= nominal, dest, -1)
            sel = jnp.where(dest == row_pos, 1.0, 0.0).astype(BF16)
            return _dot(sel, x_ref[pl.ds(off, kc), :])

        @pl.when(n > 0)
        def _():
            acc_ref[...] = chunk(0)

        @pl.when(n == 0)
        def _():
            acc_ref[...] = jnp.zeros(acc_ref.shape, F32)

        def more(c, carry):
            acc_ref[...] += chunk(c)
            return carry

        lax.fori_loop(1, n, more, 0)
        o_ref[r * unit:(r + 1) * unit, :] = acc_ref[...].astype(BF16)


def _dispatch(tile_e, uoff, unc, xb, post, nt, tm, unit, kc):
    t, d = xb.shape
    const = lambda shape: pl.BlockSpec(shape, lambda i, *_: (0,) * len(shape),
                                       pipeline_mode=pl.Buffered(1))
    grid_spec = pltpu.PrefetchScalarGridSpec(
        num_scalar_prefetch=3,
        grid=(nt,),
        in_specs=[const(xb.shape), const(post.shape)],
        out_specs=pl.BlockSpec((tm, d), lambda i, *_: (i, 0)),
        scratch_shapes=[pltpu.VMEM((unit, d), F32)],
    )
    return pl.pallas_call(
        functools.partial(_dispatch_kernel, tm=tm, unit=unit, kc=kc),
        grid_spec=grid_spec,
        out_shape=jax.ShapeDtypeStruct((nt * tm, d), BF16),
        compiler_params=pltpu.CompilerParams(
            dimension_semantics=("arbitrary",), vmem_limit_bytes=VMEM_LIMIT),
        name="dispatch",
    )(tile_e, uoff, unc, xb, post)


def _gffn_kernel(te_ref, tv_ref, x_ref, wg_ref, wu_ref, wd_ref, o_ref, acc_ref, *, nf):
    i = pl.program_id(0)
    f = pl.program_id(1)
    valid = tv_ref[i] == 1

    @pl.when(valid)
    def _():
        xb = x_ref[...]
        gate = _dot(xb, wg_ref[0, 0])
        up = _dot(xb, wu_ref[0, 0])
        hid = (gate * jax.nn.sigmoid(gate) * up).astype(BF16)
        part = _dot(hid, wd_ref[0, 0])

        @pl.when(f == 0)
        def _():
            acc_ref[...] = part

        @pl.when(f > 0)
        def _():
            acc_ref[...] += part

    @pl.when(f == nf - 1)
    def _():
        o_ref[...] = jnp.where(valid, acc_ref[...], 0.0).astype(BF16)


def _gffn(tile_e, tile_v, xs, li, wg, wu, wd, tm, fc):
    p, d = xs.shape
    nt = p // tm
    nf = wg.shape[3] // fc
    fidx = lambda i, f, te, tv: jnp.where(tv[i] == 1, f, nf - 1)
    grid_spec = pltpu.PrefetchScalarGridSpec(
        num_scalar_prefetch=2,
        grid=(nt, nf),
        in_specs=[
            pl.BlockSpec((tm, d), lambda i, f, te, tv: (i, 0)),
            pl.BlockSpec((1, 1, d, fc), lambda i, f, te, tv: (li, te[i], 0, fidx(i, f, te, tv))),
            pl.BlockSpec((1, 1, d, fc), lambda i, f, te, tv: (li, te[i], 0, fidx(i, f, te, tv))),
            pl.BlockSpec((1, 1, fc, d), lambda i, f, te, tv: (li, te[i], fidx(i, f, te, tv), 0)),
        ],
        out_specs=pl.BlockSpec((tm, d), lambda i, f, te, tv: (i, 0)),
        scratch_shapes=[pltpu.VMEM((tm, d), F32)],
    )
    return pl.pallas_call(
        functools.partial(_gffn_kernel, nf=nf),
        grid_spec=grid_spec,
        out_shape=jax.ShapeDtypeStruct((p, d), BF16),
        compiler_params=pltpu.CompilerParams(
            dimension_semantics=("arbitrary", "arbitrary"), vmem_limit_bytes=VMEM_LIMIT),
        name="grouped_ffn",
    )(tile_e, tile_v, xs, wg, wu, wd)


def _combine_kernel(ral_ref, rlo_ref, rhi_ref, x_ref, rec_ref, g_ref, b_ref, ys_ref, o_ref,
                    win_ref, xwin_ref, acc_ref, sems, xsem, *, w):
    j = pl.program_id(0)
    nj = pl.num_programs(0)
    slot = lax.rem(j, 2)
    p_rows = ys_ref.shape[0]

    def win_copy(jj, e, sl):
        r = pl.multiple_of(ral_ref[jj * N_EXPERTS + e], 16)
        return pltpu.make_async_copy(ys_ref.at[pl.ds(r, w)], win_ref.at[sl, pl.ds(e * w, w)],
                                     sems.at[sl, e])

    @pl.when(j == 0)
    def _():
        for e in range(N_EXPERTS):
            win_copy(0, e, 0).start()

    @pl.when(j + 1 < nj)
    def _():
        for e in range(N_EXPERTS):
            win_copy(j + 1, e, 1 - slot).start()

    rec = rec_ref[...]
    p1, p2, g1, g2 = rec[:, 0:1], rec[:, 1:2], rec[:, 2:3], rec[:, 3:4]

    def sel(base, lo, hi):
        rows = base + lax.broadcasted_iota(jnp.int32, (1, w), 1)
        rp = jnp.where((rows >= lo) & (rows < hi), rows, -1).astype(F32)
        return (jnp.where(p1 == rp, g1, 0.0) + jnp.where(p2 == rp, g2, 0.0)).astype(BF16)

    lane = lax.broadcasted_iota(jnp.int32, (1, N_EXPERTS * w), 1)
    rp = jnp.full((1, N_EXPERTS * w), -1, jnp.int32)
    for e in range(N_EXPERTS):
        win_copy(j, e, slot).wait()
        k = j * N_EXPERTS + e
        rows = ral_ref[k] + lane - e * w
        mine = ((lane >= e * w) & (lane < (e + 1) * w)
                & (rows >= rlo_ref[k]) & (rows < rhi_ref[k]))
        rp = jnp.where(mine, rows, rp)
    rp = rp.astype(F32)
    sel_all = (jnp.where(p1 == rp, g1, 0.0) + jnp.where(p2 == rp, g2, 0.0)).astype(BF16)
    acc_ref[...] = _dot(sel_all, win_ref[slot])

    for e in range(N_EXPERTS):
        k = j * N_EXPERTS + e
        nwin = (rhi_ref[k] - ral_ref[k] + (w - 1)) // w

        def extra(wi, carry, k=k):
            nominal = ral_ref[k] + wi * w
            base = pl.multiple_of(jnp.minimum(nominal, p_rows - w), 16)
            cp = pltpu.make_async_copy(ys_ref.at[pl.ds(base, w)], xwin_ref, xsem)
            cp.start()
            cp.wait()
            acc_ref[...] += _dot(sel(base, nominal, rhi_ref[k]), xwin_ref[...])
            return carry

        lax.fori_loop(1, nwin, extra, 0)

    o_ref[...] = _layer_norm_rows(DN_ALPHA * x_ref[...] + acc_ref[...], g_ref[...], b_ref[...])


def _combine(ral, rlo, rhi, x2d, rec2, g, b, ys, tm, w):
    t, d = x2d.shape
    row = lambda c: pl.BlockSpec((tm, c), lambda i, *_: (i, 0))
    full = lambda shape: pl.BlockSpec(shape, lambda i, *_: (0,) * len(shape))
    grid_spec = pltpu.PrefetchScalarGridSpec(
        num_scalar_prefetch=3,
        grid=(t // tm,),
        in_specs=[row(d), row(rec2.shape[1]), full(g.shape), full(b.shape),
                  pl.BlockSpec(memory_space=pl.ANY)],
        out_specs=row(d),
        scratch_shapes=[pltpu.VMEM((2, N_EXPERTS * w, d), BF16), pltpu.VMEM((w, d), BF16),
                        pltpu.VMEM((tm, d), F32), pltpu.SemaphoreType.DMA((2, N_EXPERTS)),
                        pltpu.SemaphoreType.DMA(())],
    )
    return pl.pallas_call(
        functools.partial(_combine_kernel, w=w),
        grid_spec=grid_spec,
        out_shape=jax.ShapeDtypeStruct((t, d), F32),
        compiler_params=pltpu.CompilerParams(
            dimension_semantics=("arbitrary",), vmem_limit_bytes=VMEM_LIMIT),
        name="combine",
    )(ral, rlo, rhi, x2d, rec2, g, b, ys)


def _route_plan(rec, nt, tm_e, unit, kc, tm_c, w):
    t = rec.shape[1]
    i32 = jnp.int32
    ar = jnp.arange(N_EXPERTS, dtype=i32)
    m1 = (rec[0].astype(i32)[:, None] == ar[None, :]).astype(i32)
    m2 = (rec[1].astype(i32)[:, None] == ar[None, :]).astype(i32)
    m = m1 + m2
    cinc = jnp.cumsum(m, axis=0)
    counts = cinc[-1]
    padded = ((counts + tm_e - 1) // tm_e) * tm_e
    ends = jnp.cumsum(padded)
    starts = ends - padded
    posmat = starts[None, :] + (cinc - m)
    pos1 = jnp.sum(m1 * posmat, axis=1)
    pos2 = jnp.sum(m2 * posmat, axis=1)
    post = jnp.where(m.T > 0, posmat.T, -1)
    rec2 = jnp.stack([pos1.astype(F32), pos2.astype(F32), rec[2], rec[3]]
                     + [jnp.zeros((t,), F32)] * 4, axis=1)
    tile_start = jnp.arange(nt, dtype=i32) * tm_e
    tile_e = jnp.sum((tile_start[:, None] >= ends[None, :]).astype(i32), axis=1)
    tile_v = tile_e < N_EXPERTS
    last_e = jnp.max(jnp.where(tile_v, tile_e, 0))
    tile_e = jnp.where(tile_v, tile_e, last_e)
    upt = tm_e // unit
    nu = nt * upt
    oh = (jnp.repeat(tile_e, upt)[:, None] == ar[None, :]).astype(i32)
    q0 = jnp.arange(nu, dtype=i32) * unit - jnp.sum(oh * starts[None, :], axis=1)
    q1 = jnp.minimum(q0 + unit, jnp.sum(oh * counts[None, :], axis=1))
    has = jnp.repeat(tile_v, upt) & (q1 > q0)
    cb = jnp.concatenate([jnp.zeros((1, N_EXPERTS), i32), cinc[LANES - 1::LANES]], axis=0)
    cbu = jnp.sum(oh[:, None, :] * cb[None, :, :], axis=2)
    jlo = jnp.sum((cbu[:, 1:] <= q0[:, None]).astype(i32), axis=1)
    jend = jnp.sum((cbu[:, :-1] < q1[:, None]).astype(i32), axis=1)
    uoff = jnp.where(has, jlo * LANES, 0)
    unc = jnp.where(has, (jend * LANES - uoff + kc - 1) // kc, 0)
    cbt = jnp.concatenate([jnp.zeros((1, N_EXPERTS), i32), cinc[tm_c - 1::tm_c]], axis=0)
    r_lo = starts[None, :] + cbt[:-1]
    r_hi = starts[None, :] + cbt[1:]
    ral = jnp.minimum((r_lo // 16) * 16, nt * tm_e - w)
    return (rec2, post, tile_e, tile_v.astype(i32), uoff, unc,
            ral.reshape(-1), r_lo.reshape(-1), r_hi.reshape(-1))


def _moe(x2d, xb, rec, li, wg, wu, wd, g, b, tm_c, tm_e, fc, unit, kc, w):
    t, d = x2d.shape
    nt = (2 * t) // tm_e + N_EXPERTS
    rec2, post, tile_e, tile_v, uoff, unc, ral, rlo, rhi = _route_plan(
        rec, nt, tm_e, unit, kc, tm_c, w)
    xs = _dispatch(tile_e, uoff, unc, xb, post, nt, tm_e, unit, kc)
    ys = _gffn(tile_e, tile_v, xs, li, wg, wu, wd, tm_e, fc)
    return _combine(ral, rlo, rhi, x2d, rec2, g, b, ys, tm_c, w)


def _rotate_half_cols(w):
    half = w.shape[-1] // 2
    return jnp.concatenate([-w[..., half:], w[..., :half]], axis=-1)


def _t5_bucket(n):
    max_exact = NUM_BUCKETS // 2
    nf = jnp.maximum(n, 1).astype(F32)
    large = max_exact + (jnp.log(nf / max_exact) / math.log(MAX_DISTANCE / max_exact)
                         * (NUM_BUCKETS - max_exact)).astype(jnp.int32)
    large = jnp.minimum(large, NUM_BUCKETS - 1)
    return jnp.where(n < max_exact, n, large)


def _pick(n, prefs):
    for p in prefs:
        if n % p == 0:
            return p
    raise ValueError(f"no tile for extent {n}")


def kernel(x, rel_bias, w_in, mla_q_norm, mla_kv_norm, mla_w_uq, mla_w_uk, mla_w_uv, diff_lambda, diff_norm, w_o, ln1_g, ln1_b, ln2_g, ln2_b, ffn_w_gate, ffn_w_up, ffn_w_down, moe_router, moe_w_gate, moe_w_up, moe_w_down):
    b, s, d = x.shape
    t = b * s
    assert d == D_MODEL
    ta = _pick(s, (512, 256))
    assert ta % ATT_BLK == 0
    assert s >= 2 * ta or s == ta
    tm_proj = _pick(s, (512, 256, 128))
    tm_tok = _pick(t, (512, 256, 128))
    tm_e = _pick(2 * t, (512, 256))
    tm_c = _pick(t, (512, 256))
    unit, wrows = 128, 192
    kc = 768
    assert tm_e % unit == 0 and t >= kc and t % LANES == 0
    fc = D_FF_EXPERT // 2

    pos = jnp.arange(s, dtype=F32)
    inv = 1.0 / (ROPE_THETA ** (jnp.arange(0, MLA_ROPE, 2, dtype=F32) / MLA_ROPE))
    ang = pos[:, None] * inv[None, :]
    cos4 = jnp.tile(jnp.cos(ang), (1, 2 * MLA_HEADS))
    sin4 = jnp.tile(jnp.sin(ang), (1, 2 * MLA_HEADS))
    rb = rel_bias.astype(F32)
    kk = jnp.arange(ta)[:, None]
    qq = jnp.arange(ta)[None, :]
    tiles = []
    for off in (0, ta):
        dist = qq + off - kk
        bucket = _t5_bucket(jnp.maximum(dist, 0))
        bt = jnp.zeros((DIFF_HEADS, ta, ta), F32)
        for bk in range(NUM_BUCKETS):
            bt = bt + jnp.where((bucket == bk)[None], rb[bk][:, None, None], 0.0)
        tiles.append(jnp.where((dist >= 0)[None], bt * LOG2E, NEG_BIG))
    bias_tiles = jnp.stack(tiles, axis=1)
    assert ta + 1 >= MAX_DISTANCE
    cfar = rb[NUM_BUCKETS - 1] * LOG2E
    moe_wg, moe_wu, moe_wd = (w.astype(BF16) for w in (moe_w_gate, moe_w_up, moe_w_down))

    xc = x
    for l in range(DEPTH):
        lambda_init = 0.8 - 0.6 * math.exp(-0.3 * l)
        wl = w_in[l]
        kr_w = wl[:, _KR0:_KR0 + MLA_ROPE]
        win = jnp.concatenate([wl[:, :_KR0 + MLA_ROPE], _rotate_half_cols(kr_w),
                               wl[:, _KR0 + MLA_ROPE:]], axis=1).astype(BF16)
        uq = mla_w_uq[l].reshape(MLA_Q_RANK, MLA_HEADS, MLA_QK)
        uq_n = uq[:, :, :MLA_NOPE].reshape(MLA_Q_RANK, -1)
        uq_r = uq[:, :, MLA_NOPE:]
        wuq = jnp.concatenate([uq_n, uq_r.reshape(MLA_Q_RANK, -1),
                               _rotate_half_cols(uq_r).reshape(MLA_Q_RANK, -1)], axis=1).astype(BF16)
        wukv = jnp.concatenate([mla_w_uk[l], mla_w_uv[l]], axis=1).astype(BF16)

        qT, k, vT, dqT, dk, dvT = _proj(
            xc, win, mla_q_norm[l][None, :], mla_kv_norm[l][None, :], wuq, wukv, cos4, sin4, tm_proj)
        a_mla = _mla_attn(qT, k, vT, ta, MLA_HP).reshape(t, MLA_W)
        a_diff = _diff_attn(cfar, dqT, dk, dvT, bias_tiles, diff_lambda[l].astype(F32),
                            diff_norm[l].astype(F32)[:, None], ta, DIFF_HP,
                            lambda_init).reshape(t, DIFF_W)
        x2d = xc.reshape(t, d)
        wo = w_o[l].astype(BF16)
        g1, b1 = ln1_g[l][None, :], ln1_b[l][None, :]
        g2, b2 = ln2_g[l][None, :], ln2_b[l][None, :]
        if l % 2 == 0:
            x1 = _oproj(a_mla, a_diff, x2d, wo, g1, b1, tm_tok)
            i = l // 2
            x2 = _ffn(x1, ffn_w_gate[i].astype(BF16), ffn_w_up[i].astype(BF16),
                      ffn_w_down[i].astype(BF16), g2, b2, tm_tok)
        else:
            i = l // 2
            rpad = jnp.pad(moe_router[i].astype(F32).T, ((0, ROUTER_ROWS - N_EXPERTS), (0, 0)))
            rhi = rpad.astype(BF16)
            rlo = (rpad - rhi.astype(F32)).astype(BF16)
            x1, x1b, rec = _oproj(a_mla, a_diff, x2d, wo, g1, b1, tm_tok,
                                  router=(jnp.concatenate([rhi, rlo], axis=0), rhi))
            x2 = _moe(x1, x1b, rec, i, moe_wg, moe_wu, moe_wd, g2, b2,
                      tm_c, tm_e, fc, unit, kc, wrows)
        xc = x2.reshape(b, s, d)
    return xc
```

```python
import functools
import math

import jax
import jax.numpy as jnp
from jax import lax
from jax.experimental import pallas as pl
from jax.experimental.pallas import tpu as pltpu

D_MODEL = 1024
DEPTH = 4
MLA_HEADS = 4
MLA_NOPE = 128
MLA_ROPE = 64
MLA_V = 128
MLA_Q_RANK = 384
MLA_KV_RANK = 256
ROPE_THETA = 10000.0
DIFF_HEADS = 4
DIFF_QK = 64
DIFF_V = 2 * DIFF_QK
NUM_BUCKETS = 32
MAX_DISTANCE = 128
D_FF_DENSE = 2816
N_EXPERTS = 8
D_FF_EXPERT = 3584
DN_ALPHA = (2 * DEPTH) ** 0.25

MLA_QK = MLA_NOPE + MLA_ROPE
MLA_W = MLA_HEADS * MLA_V
DIFF_W = DIFF_HEADS * DIFF_V
DIFF_VA = DIFF_V + 16
_CQ0, _CKV0 = 0, MLA_Q_RANK
_KR0 = _CKV0 + MLA_KV_RANK
_DQ0 = _KR0 + 2 * MLA_ROPE
_DK0 = _DQ0 + DIFF_W
_DV0 = _DK0 + DIFF_W
IN_COLS_W = _DV0 + DIFF_W

LANES = 128
VMEM_LIMIT = 56 * 1024 * 1024
NEG_BIG = -1e30
ROUTER_ROWS = 16
REC_ROWS = 8
LOG2E = math.log2(math.e)
ATT_BLK = 256
MLA_HP = 4
DIFF_HP = 2

BF16 = jnp.bfloat16
F32 = jnp.float32


def _dot(a, b):
    return jnp.dot(a, b, preferred_element_type=F32)


def _layer_norm_rows(y, g, b):
    mu = jnp.mean(y, axis=-1, keepdims=True)
    d = y - mu
    var = jnp.mean(d * d, axis=-1, keepdims=True)
    return d * lax.rsqrt(var + 1e-5) * g + b


def _rms_rows(y, g, eps):
    return y * lax.rsqrt(jnp.mean(y * y, axis=-1, keepdims=True) + eps) * g


def _proj_kernel(x_ref, win_ref, qn_ref, kvn_ref, wuq_ref, wukv_ref, cos_ref, sin_ref,
                 qT_ref, k_ref, vT_ref, dqT_ref, dk_ref, dvT_ref):
    x = x_ref[0].astype(BF16)
    h = _dot(x, win_ref[...])
    cqn = _rms_rows(h[:, _CQ0:_CQ0 + MLA_Q_RANK], qn_ref[...], 1e-6)
    ckvn = _rms_rows(h[:, _CKV0:_CKV0 + MLA_KV_RANK], kvn_ref[...], 1e-6)
    q = _dot(cqn.astype(BF16), wuq_ref[...]) * (MLA_QK ** -0.5 * LOG2E)
    kv = _dot(ckvn.astype(BF16), wukv_ref[...])
    cos4, sin4 = cos_ref[...], sin_ref[...]
    nr = MLA_HEADS * MLA_ROPE
    qr = q[:, MLA_W:MLA_W + nr] * cos4 + q[:, MLA_W + nr:] * sin4
    kr = (h[:, _KR0:_KR0 + MLA_ROPE] * cos4[:, :MLA_ROPE]
          + h[:, _KR0 + MLA_ROPE:_KR0 + 2 * MLA_ROPE] * sin4[:, :MLA_ROPE])
    qrT = qr.T
    for hh in range(MLA_HEADS):
        c0 = hh * LANES
        qnT = q[:, c0:c0 + MLA_NOPE].T
        qT_ref[0, hh] = jnp.concatenate(
            [qnT, qrT[hh * MLA_ROPE:(hh + 1) * MLA_ROPE]], axis=0).astype(BF16)
        k_ref[0, hh] = jnp.concatenate([kv[:, c0:c0 + MLA_NOPE], kr], axis=1).astype(BF16)
        vT_ref[0, hh] = kv[:, MLA_W + c0:MLA_W + c0 + MLA_V].T.astype(BF16)
        dqT = (h[:, _DQ0 + c0:_DQ0 + c0 + LANES] * (DIFF_QK ** -0.5 * LOG2E)).T.astype(BF16)
        zeros = jnp.zeros((DIFF_QK, x.shape[0]), BF16)
        dqT_ref[0, hh, 0] = jnp.concatenate([dqT[:DIFF_QK], zeros], axis=0)
        dqT_ref[0, hh, 1] = jnp.concatenate([zeros, dqT[DIFF_QK:]], axis=0)
        dk_ref[0, hh] = h[:, _DK0 + c0:_DK0 + c0 + LANES].astype(BF16)
        dvT_ref[0, hh] = jnp.concatenate(
            [h[:, _DV0 + c0:_DV0 + c0 + DIFF_V].T, jnp.ones((DIFF_VA - DIFF_V, x.shape[0]), F32)],
            axis=0).astype(BF16)


def _proj(x, win, qn, kvn, wuq, wukv, cos4, sin4, tm):
    b, s, d = x.shape
    hd = MLA_HEADS
    full = lambda shape: pl.BlockSpec(shape, lambda bi, i: (0,) * len(shape))
    tmaj = lambda w: pl.BlockSpec((1, hd, tm, w), lambda bi, i: (bi, 0, i, 0))
    fmaj = lambda w: pl.BlockSpec((1, hd, w, tm), lambda bi, i: (bi, 0, 0, i))
    sds = jax.ShapeDtypeStruct
    return pl.pallas_call(
        _proj_kernel,
        grid=(b, s // tm),
        in_specs=[
            pl.BlockSpec((1, tm, d), lambda bi, i: (bi, i, 0)),
            full(win.shape), full(qn.shape), full(kvn.shape), full(wuq.shape), full(wukv.shape),
            pl.BlockSpec((tm, cos4.shape[1]), lambda bi, i: (i, 0)),
            pl.BlockSpec((tm, sin4.shape[1]), lambda bi, i: (i, 0)),
        ],
        out_specs=[fmaj(MLA_QK), tmaj(MLA_QK), fmaj(MLA_V),
                   pl.BlockSpec((1, hd, 2, LANES, tm), lambda bi, i: (bi, 0, 0, 0, i)),
                   tmaj(LANES), fmaj(DIFF_VA)],
        out_shape=[
            sds((b, hd, MLA_QK, s), BF16), sds((b, hd, s, MLA_QK), BF16), sds((b, hd, MLA_V, s), BF16),
            sds((b, hd, 2, LANES, s), BF16), sds((b, hd, s, LANES), BF16), sds((b, hd, DIFF_VA, s), BF16),
        ],
        compiler_params=pltpu.CompilerParams(
            dimension_semantics=("arbitrary", "arbitrary"), vmem_limit_bytes=VMEM_LIMIT),
        name="proj",
    )(x, win, qn, kvn, wuq, wukv, cos4, sin4)


def _softmax_step(s, vj, m_ref, l_ref, acc_ref, shift=None):
    m_prev = m_ref[...]
    smax = jnp.max(s, axis=0, keepdims=True)
    if shift is not None:
        smax = smax + shift
    m_new = jnp.maximum(m_prev, smax)
    a = jnp.exp2(m_prev - m_new)
    p = jnp.exp2(s - (m_new if shift is None else m_new - shift))
    if l_ref is not None:
        l_ref[...] = a * l_ref[...] + jnp.sum(p, axis=0, keepdims=True)
    acc_ref[...] = a * acc_ref[...] + _dot(vj, p.astype(BF16))
    m_ref[...] = m_new


def _chain_scratch(n, dv):
    return [pltpu.VMEM((1, ATT_BLK), F32)] * (2 * n) + [pltpu.VMEM((dv, ATT_BLK), F32)] * n


def _init_chains(state):
    n = len(state) // 3
    m_refs, l_refs, acc_refs = state[:n], state[n:2 * n], state[2 * n:]
    for m_ref, l_ref, acc_ref in zip(m_refs, l_refs, acc_refs):
        m_ref[...] = jnp.full(m_ref.shape, NEG_BIG, F32)
        l_ref[...] = jnp.zeros(l_ref.shape, F32)
        acc_ref[...] = jnp.zeros(acc_ref.shape, F32)
    return m_refs, l_refs, acc_refs


def _pipelined_tiles(qi, logits, next_logits, softmax_pv):
    a, b, c = 0, 1, 2

    @pl.when(qi == 0)
    def _():
        logits(0, a, True)
        next_logits(c)
        softmax_pv(0, a, "diag")

    @pl.when(qi == 1)
    def _():
        logits(1, a, True)
        softmax_pv(0, c, "near")
        next_logits(c)
        softmax_pv(1, a, "diag")

    @pl.when(qi >= 2)
    def _():
        logits(1, a)
        softmax_pv(0, c, "far")
        nfar = qi - 2

        def body(i, carry):
            j = 2 * i + 1
            logits(j + 1, b)
            softmax_pv(j, a, "far")
            logits(j + 2, a)
            softmax_pv(j + 1, b, "far")
            return carry

        lax.fori_loop(0, lax.shift_right_logical(nfar, 1), body, 0)
        odd = lax.rem(nfar, 2) == 1

        @pl.when(jnp.logical_not(odd))
        def _():
            logits(qi, b, True)
            softmax_pv(qi - 1, a, "near")
            next_logits(c)
            softmax_pv(qi, b, "diag")

        @pl.when(odd)
        def _():
            logits(qi - 1, b)
            softmax_pv(qi - 2, a, "far")
            logits(qi, a, True)
            softmax_pv(qi - 1, b, "near")
            next_logits(c)
            softmax_pv(qi, a, "diag")


def _mla_attn_kernel(qT_ref, qTn_ref, k_ref, vT_ref, o_ref, sa_ref, sb_ref, sc_ref, *state,
                     ta, hp):
    qi = pl.program_id(2)
    m_refs, l_refs, acc_refs = _init_chains(state)
    s_refs = (sa_ref, sb_ref, sc_ref)

    nb = ta // ATT_BLK
    blk = ATT_BLK

    def logits(j, buf, diagonal=False):
        off = pl.multiple_of(j * ta, ta)
        for h in range(hp):
            if not diagonal:
                s_refs[buf][h] = _dot(k_ref[0, h, pl.ds(off, ta), :], qT_ref[0, h])
                continue
            for qs in range(nb):
                nk, q0 = (qs + 1) * blk, qs * blk
                s_refs[buf][h, :nk, q0:q0 + blk] = _dot(k_ref[0, h, pl.ds(off, nk), :],
                                                        qT_ref[0, h, :, q0:q0 + blk])

    def next_logits(buf):
        for h in range(hp):
            s_refs[buf][h] = _dot(k_ref[0, h, :ta, :], qTn_ref[0, h])

    def softmax_pv(j, buf, kind):
        off = pl.multiple_of(j * ta, ta)
        diagonal = kind == "diag"
        for h in range(hp):
            for qs in range(nb):
                nk = (qs + 1) * blk if diagonal else ta
                s = s_refs[buf][h, :nk, qs * blk:(qs + 1) * blk]
                if diagonal:
                    kpos = lax.broadcasted_iota(jnp.int32, (nk, blk), 0)
                    qpos = lax.broadcasted_iota(jnp.int32, (nk, blk), 1) + qs * blk
                    s = jnp.where(kpos <= qpos, s, NEG_BIG)
                c = h * nb + qs
                _softmax_step(s, vT_ref[0, h, :, pl.ds(off, nk)], m_refs[c], l_refs[c], acc_refs[c])

    _pipelined_tiles(qi, logits, next_logits, softmax_pv)
    for h in range(hp):
        for qs in range(nb):
            c = h * nb + qs
            out = acc_refs[c][...] * (1.0 / l_refs[c][...])
            o_ref[0, qs * blk:(qs + 1) * blk, h * MLA_V:(h + 1) * MLA_V] = out.T.astype(BF16)


def _mla_attn(qT, k, vT, ta, hp):
    b, hd, _, s = qT.shape
    nq = s // ta
    return pl.pallas_call(
        functools.partial(_mla_attn_kernel, ta=ta, hp=hp),
        grid=(b, hd // hp, nq),
        in_specs=[
            pl.BlockSpec((1, hp, MLA_QK, ta), lambda bi, h, i: (bi, h, 0, i)),
            pl.BlockSpec((1, hp, MLA_QK, ta), lambda bi, h, i: (bi, h, 0, jnp.minimum(i + 1, nq - 1))),
            pl.BlockSpec((1, hp, s, MLA_QK), lambda bi, h, i: (bi, h, 0, 0)),
            pl.BlockSpec((1, hp, MLA_V, s), lambda bi, h, i: (bi, h, 0, 0)),
        ],
        out_specs=pl.BlockSpec((1, ta, hp * MLA_V), lambda bi, h, i: (bi, i, h)),
        out_shape=jax.ShapeDtypeStruct((b, s, hd * MLA_V), BF16),
        scratch_shapes=([pltpu.VMEM((hp, ta, ta), F32)] * 3
                        + _chain_scratch(hp * ta // ATT_BLK, MLA_V)),
        compiler_params=pltpu.CompilerParams(
            dimension_semantics=("arbitrary",) * 3, vmem_limit_bytes=VMEM_LIMIT),
        name="mla_attn",
    )(qT, qT, k, vT)


def _diff_attn_kernel(cfar_ref, qT_ref, qTn_ref, k_ref, vT_ref, bias_ref, lamp_ref, g_ref, o_ref,
                      sa_ref, sb_ref, sc_ref, *state, ta, hp, lambda_init):
    hg = pl.program_id(1)
    qi = pl.program_id(2)
    s_refs = (sa_ref, sb_ref, sc_ref)
    m_refs, l_refs, acc_refs = _init_chains(state)

    nb = ta // ATT_BLK
    blk = ATT_BLK

    def logits(j, buf, diagonal=False):
        off = pl.multiple_of(j * ta, ta)
        for h in range(hp):
            for c in range(2):
                if not diagonal:
                    s_refs[buf][2 * h + c] = _dot(k_ref[0, h, pl.ds(off, ta), :],
                                                  qT_ref[0, h, c])
                    continue
                for qb in range(nb):
                    nk, q0 = (qb + 1) * blk, qb * blk
                    s_refs[buf][2 * h + c, :nk, q0:q0 + blk] = _dot(
                        k_ref[0, h, pl.ds(off, nk), :], qT_ref[0, h, c, :, q0:q0 + blk])

    def next_logits(buf):
        for h in range(hp):
            for c in range(2):
                s_refs[buf][2 * h + c] = _dot(k_ref[0, h, :ta, :], qTn_ref[0, h, c])

    def softmax_pv(j, buf, kind):
        off = pl.multiple_of(j * ta, ta)
        diagonal = kind == "diag"
        for h in range(hp):
            for qb in range(nb):
                nk = (qb + 1) * blk if diagonal else ta
                vj = vT_ref[0, h, :, pl.ds(off, nk)]
                for c in range(2):
                    s = s_refs[buf][2 * h + c, :nk, qb * blk:(qb + 1) * blk]
                    i = (2 * h + c) * nb + qb
                    if kind == "far" or (kind == "near" and qb * blk + 1 >= MAX_DISTANCE):
                        _softmax_step(s, vj, m_refs[i], None, acc_refs[i],
                                      shift=cfar_ref[hg * hp + h])
                    else:
                        bias = bias_ref[h, 0 if diagonal else 1, :nk, qb * blk:(qb + 1) * blk]
                        _softmax_step(s + bias, vj, m_refs[i], None, acc_refs[i])

    _pipelined_tiles(qi, logits, next_logits, softmax_pv)

    lp = lamp_ref[...]
    lam = (jnp.exp(jnp.sum(lp[0:1] * lp[1:2], axis=-1, keepdims=True))
           - jnp.exp(jnp.sum(lp[2:3] * lp[3:4], axis=-1, keepdims=True)) + lambda_init)
    for h in range(hp):
        for qb in range(nb):
            a0 = acc_refs[(2 * h) * nb + qb][...]
            a1 = acc_refs[(2 * h + 1) * nb + qb][...]
            out = (a0[:DIFF_V] * (1.0 / a0[DIFF_V:DIFF_V + 1])
                   - lam * (a1[:DIFF_V] * (1.0 / a1[DIFF_V:DIFF_V + 1])))
            ms = jnp.mean(out * out, axis=0, keepdims=True)
            out = out * lax.rsqrt(ms + 1e-5) * g_ref[...] * (1.0 - lambda_init)
            o_ref[0, qb * blk:(qb + 1) * blk, h * DIFF_V:(h + 1) * DIFF_V] = out.T.astype(BF16)


def _diff_attn(cfar, qT, k, vT, bias_tiles, lam_params, g_col, ta, hp, lambda_init):
    b, hd, _, _, s = qT.shape
    nq = s // ta
    grid_spec = pltpu.PrefetchScalarGridSpec(
        num_scalar_prefetch=1,
        grid=(b, hd // hp, nq),
        in_specs=[
            pl.BlockSpec((1, hp, 2, LANES, ta), lambda bi, h, i, cf: (bi, h, 0, 0, i)),
            pl.BlockSpec((1, hp, 2, LANES, ta),
                         lambda bi, h, i, cf: (bi, h, 0, 0, jnp.minimum(i + 1, nq - 1))),
            pl.BlockSpec((1, hp, s, LANES), lambda bi, h, i, cf: (bi, h, 0, 0)),
            pl.BlockSpec((1, hp, DIFF_VA, s), lambda bi, h, i, cf: (bi, h, 0, 0)),
            pl.BlockSpec((hp, 2, ta, ta), lambda bi, h, i, cf: (h, 0, 0, 0)),
            pl.BlockSpec(lam_params.shape, lambda bi, h, i, cf: (0, 0)),
            pl.BlockSpec(g_col.shape, lambda bi, h, i, cf: (0, 0)),
        ],
        out_specs=pl.BlockSpec((1, ta, hp * DIFF_V), lambda bi, h, i, cf: (bi, i, h)),
        scratch_shapes=([pltpu.VMEM((2 * hp, ta, ta), F32)] * 3
                        + _chain_scratch(2 * hp * ta // ATT_BLK, DIFF_VA)),
    )
    return pl.pallas_call(
        functools.partial(_diff_attn_kernel, ta=ta, hp=hp, lambda_init=lambda_init),
        grid_spec=grid_spec,
        out_shape=jax.ShapeDtypeStruct((b, s, hd * DIFF_V), BF16),
        compiler_params=pltpu.CompilerParams(
            dimension_semantics=("arbitrary",) * 3, vmem_limit_bytes=VMEM_LIMIT),
        name="diff_attn",
    )(cfar, qT, qT, k, vT, bias_tiles, lam_params, g_col)


def _split_bf16(v):
    hi = v.astype(BF16)
    return hi, (v - hi.astype(F32)).astype(BF16)


def _oproj_kernel(am_ref, ad_ref, x_ref, wo_ref, g_ref, b_ref, *rest, route):
    mix = _dot(am_ref[...], wo_ref[:MLA_W, :]) + _dot(ad_ref[...], wo_ref[MLA_W:, :])
    y = _layer_norm_rows(DN_ALPHA * x_ref[...] + mix, g_ref[...], b_ref[...])
    if not route:
        (o_ref,) = rest
        o_ref[...] = y
        return
    rhl_ref, rh_ref, o_ref, ob_ref, r_ref = rest
    yhi, ylo = _split_bf16(y)
    nt = (((1,), (1,)), ((), ()))
    both = lax.dot_general(rhl_ref[...], yhi, nt, preferred_element_type=F32)
    logits = (both[:ROUTER_ROWS] + both[ROUTER_ROWS:]
              + lax.dot_general(rh_ref[...], ylo, nt, preferred_element_type=F32))
    row = lax.broadcasted_iota(jnp.int32, logits.shape, 0)
    logits = jnp.where(row < N_EXPERTS, logits, NEG_BIG)
    m1 = jnp.max(logits, axis=0, keepdims=True)
    i1 = jnp.min(jnp.where(logits == m1, row, ROUTER_ROWS), axis=0, keepdims=True)
    rest_l = jnp.where(row == i1, NEG_BIG, logits)
    m2 = jnp.max(rest_l, axis=0, keepdims=True)
    i2 = jnp.min(jnp.where(rest_l == m2, row, ROUTER_ROWS), axis=0, keepdims=True)
    e2 = jnp.exp(m2 - m1)
    g1 = 1.0 / (1.0 + e2)
    g2 = e2 * g1
    row8 = lax.broadcasted_iota(jnp.int32, r_ref.shape, 0)
    rec = jnp.where(row8 == 0, i1.astype(F32),
                    jnp.where(row8 == 1, i2.astype(F32),
                              jnp.where(row8 == 2, g1, jnp.where(row8 == 3, g2, 0.0))))
    o_ref[...] = y
    ob_ref[...] = yhi
    r_ref[...] = rec


def _oproj(a_mla, a_diff, x2d, wo, g, b, tm, router=None):
    t, d = x2d.shape
    row = lambda w: pl.BlockSpec((tm, w), lambda i: (i, 0))
    full = lambda shape: pl.BlockSpec(shape, lambda i: (0,) * len(shape))
    in_specs = [row(MLA_W), row(DIFF_W), row(d), full(wo.shape), full(g.shape), full(b.shape)]
    args = [a_mla, a_diff, x2d, wo, g, b]
    out_specs = [row(d)]
    out_shape = [jax.ShapeDtypeStruct((t, d), F32)]
    if router is not None:
        in_specs += [full(router[0].shape), full(router[1].shape)]
        args += list(router)
        out_specs += [row(d), pl.BlockSpec((REC_ROWS, tm), lambda i: (0, i))]
        out_shape += [jax.ShapeDtypeStruct((t, d), BF16), jax.ShapeDtypeStruct((REC_ROWS, t), F32)]
    res = pl.pallas_call(
        functools.partial(_oproj_kernel, route=router is not None),
        grid=(t // tm,),
        in_specs=in_specs, out_specs=out_specs, out_shape=out_shape,
        compiler_params=pltpu.CompilerParams(
            dimension_semantics=("arbitrary",), vmem_limit_bytes=VMEM_LIMIT),
        name="oproj_route" if router is not None else "oproj",
    )(*args)
    return res if router is not None else res[0]


def _ffn_kernel(x_ref, wg_ref, wu_ref, wd_ref, g_ref, b_ref, o_ref):
    x = x_ref[...]
    xb = x.astype(BF16)
    gate = _dot(xb, wg_ref[...])
    up = _dot(xb, wu_ref[...])
    hid = (gate * jax.nn.sigmoid(gate) * up).astype(BF16)
    f = _dot(hid, wd_ref[...])
    o_ref[...] = _layer_norm_rows(DN_ALPHA * x + f, g_ref[...], b_ref[...])


def _ffn(x2d, wg, wu, wd, g, b, tm):
    t, d = x2d.shape
    row = pl.BlockSpec((tm, d), lambda i: (i, 0))
    const = lambda shape: pl.BlockSpec(shape, lambda i: (0,) * len(shape),
                                       pipeline_mode=pl.Buffered(1))
    return pl.pallas_call(
        _ffn_kernel,
        grid=(t // tm,),
        in_specs=[row, const(wg.shape), const(wu.shape), const(wd.shape),
                  const(g.shape), const(b.shape)],
        out_specs=row,
        out_shape=jax.ShapeDtypeStruct((t, d), F32),
        compiler_params=pltpu.CompilerParams(
            dimension_semantics=("arbitrary",), vmem_limit_bytes=VMEM_LIMIT),
        name="ffn",
    )(x2d, wg, wu, wd, g, b)


def _dispatch_kernel(te_ref, uoff_ref, unc_ref, x_ref, post_ref, o_ref, acc_ref, *, tm, unit, kc):
    i = pl.program_id(0)
    e = te_ref[i]
    t = x_ref.shape[0]
    upt = tm // unit
    for r in range(upt):
        u = i * upt + r
        n = unc_ref[u]
        start0 = uoff_ref[u]
        row_pos = u * unit + lax.broadcasted_iota(jnp.int32, (unit, 1), 0)

        def chunk(c):
            nominal = start0 + c * kc
            off = pl.multiple_of(jnp.minimum(nominal, t - kc), LANES)
            dest = post_ref[pl.ds(e, 1), pl.ds(off, kc)]
            tok = off + lax.broadcasted_iota(jnp.int32, (1, kc), 1)
            dest = jnp.where(tok >= nominal, dest, -1)
            sel = jnp.where(dest == row_pos, 1.0, 0.0).astype(BF16)
            return _dot(sel, x_ref[pl.ds(off, kc), :])

        @pl.when(n > 0)
        def _():
            acc_ref[...] = chunk(0)

        @pl.when(n == 0)
        def _():
            acc_ref[...] = jnp.zeros(acc_ref.shape, F32)

        def more(c, carry):
            acc_ref[...] += chunk(c)
            return carry

        lax.fori_loop(1, n, more, 0)
        o_ref[r * unit:(r + 1) * unit, :] = acc_ref[...].astype(BF16)


def _dispatch(tile_e, uoff, unc, xb, post, nt, tm, unit, kc):
    t, d = xb.shape
    const = lambda shape: pl.BlockSpec(shape, lambda i, *_: (0,) * len(shape),
                                       pipeline_mode=pl.Buffered(1))
    grid_spec = pltpu.PrefetchScalarGridSpec(
        num_scalar_prefetch=3,
        grid=(nt,),
        in_specs=[const(xb.shape), const(post.shape)],
        out_specs=pl.BlockSpec((tm, d), lambda i, *_: (i, 0)),
        scratch_shapes=[pltpu.VMEM((unit, d), F32)],
    )
    return pl.pallas_call(
        functools.partial(_dispatch_kernel, tm=tm, unit=unit, kc=kc),
        grid_spec=grid_spec,
        out_shape=jax.ShapeDtypeStruct((nt * tm, d), BF16),
        compiler_params=pltpu.CompilerParams(
            dimension_semantics=("arbitrary",), vmem_limit_bytes=VMEM_LIMIT),
        name="dispatch",
    )(tile_e, uoff, unc, xb, post)


def _gffn_kernel(te_ref, tv_ref, x_ref, wg_ref, wu_ref, wd_ref, o_ref, acc_ref, *, nf):
    i = pl.program_id(0)
    f = pl.program_id(1)
    status = tv_ref[i]
    tm = x_ref.shape[0]
    half = tm // 2

    def swiglu_rows(rows):
        xb = x_ref[:rows, :]
        gate = _dot(xb, wg_ref[0, 0])
        up = _dot(xb, wu_ref[0, 0])
        hid = (gate * jax.nn.sigmoid(gate) * up).astype(BF16)
        part = _dot(hid, wd_ref[0, 0])

        @pl.when(f == 0)
        def _():
            acc_ref[:rows, :] = part

        @pl.when(f > 0)
        def _():
            acc_ref[:rows, :] += part

    @pl.when(status == 2)
    def _():
        swiglu_rows(tm)

    @pl.when(status == 1)
    def _():
        swiglu_rows(half)

    @pl.when(jnp.logical_and(f == nf - 1, status == 2))
    def _():
        o_ref[...] = acc_ref[...].astype(BF16)

    @pl.when(jnp.logical_and(f == nf - 1, status == 1))
    def _():
        o_ref[:half, :] = acc_ref[:half, :].astype(BF16)
        o_ref[half:, :] = jnp.zeros((tm - half, o_ref.shape[1]), BF16)

    @pl.when(jnp.logical_and(f == nf - 1, status == 0))
    def _():
        o_ref[...] = jnp.zeros(o_ref.shape, BF16)


def _gffn(tile_e, tile_v, xs, li, wg, wu, wd, tm, fc):
    p, d = xs.shape
    nt = p // tm
    nf = wg.shape[3] // fc
    fidx = lambda i, f, te, tv: jnp.where(tv[i] > 0, f, nf - 1)
    grid_spec = pltpu.PrefetchScalarGridSpec(
        num_scalar_prefetch=2,
        grid=(nt, nf),
        in_specs=[
            pl.BlockSpec((tm, d), lambda i, f, te, tv: (i, 0)),
            pl.BlockSpec((1, 1, d, fc), lambda i, f, te, tv: (li, te[i], 0, fidx(i, f, te, tv))),
            pl.BlockSpec((1, 1, d, fc), lambda i, f, te, tv: (li, te[i], 0, fidx(i, f, te, tv))),
            pl.BlockSpec((1, 1, fc, d), lambda i, f, te, tv: (li, te[i], fidx(i, f, te, tv), 0)),
        ],
        out_specs=pl.BlockSpec((tm, d), lambda i, f, te, tv: (i, 0)),
        scratch_shapes=[pltpu.VMEM((tm, d), F32)],
    )
    return pl.pallas_call(
        functools.partial(_gffn_kernel, nf=nf),
        grid_spec=grid_spec,
        out_shape=jax.ShapeDtypeStruct((p, d), BF16),
        compiler_params=pltpu.CompilerParams(
            dimension_semantics=("arbitrary", "arbitrary"), vmem_limit_bytes=VMEM_LIMIT),
        name="grouped_ffn",
    )(tile_e, tile_v, xs, wg, wu, wd)


def _combine_kernel(ral_ref, rlo_ref, rhi_ref, x_ref, rec_ref, g_ref, b_ref, ys_ref, o_ref,
                    win_ref, xwin_ref, acc_ref, sems, xsem, *, w):
    j = pl.program_id(0)
    nj = pl.num_programs(0)
    slot = lax.rem(j, 2)
    p_rows = ys_ref.shape[0]

    def win_copy(jj, e, sl):
        r = pl.multiple_of(ral_ref[jj * N_EXPERTS + e], 16)
        return pltpu.make_async_copy(ys_ref.at[pl.ds(r, w)], win_ref.at[sl, pl.ds(e * w, w)],
                                     sems.at[sl, e])

    @pl.when(j == 0)
    def _():
        for e in range(N_EXPERTS):
            win_copy(0, e, 0).start()

    @pl.when(j + 1 < nj)
    def _():
        for e in range(N_EXPERTS):
            win_copy(j + 1, e, 1 - slot).start()

    rec = rec_ref[...]
    p1, p2, g1, g2 = rec[:, 0:1], rec[:, 1:2], rec[:, 2:3], rec[:, 3:4]

    def sel(base, lo, hi):
        rows = base + lax.broadcasted_iota(jnp.int32, (1, w), 1)
        rp = jnp.where((rows >= lo) & (rows < hi), rows, -1).astype(F32)
        return (jnp.where(p1 == rp, g1, 0.0) + jnp.where(p2 == rp, g2, 0.0)).astype(BF16)

    lane = lax.broadcasted_iota(jnp.int32, (1, N_EXPERTS * w), 1)
    rp = jnp.full((1, N_EXPERTS * w), -1, jnp.int32)
    for e in range(N_EXPERTS):
        win_copy(j, e, slot).wait()
        k = j * N_EXPERTS + e
        rows = ral_ref[k] + lane - e * w
        mine = ((lane >= e * w) & (lane < (e + 1) * w)
                & (rows >= rlo_ref[k]) & (rows < rhi_ref[k]))
        rp = jnp.where(mine, rows, rp)
    rp = rp.astype(F32)
    sel_all = (jnp.where(p1 == rp, g1, 0.0) + jnp.where(p2 == rp, g2, 0.0)).astype(BF16)
    acc_ref[...] = _dot(sel_all, win_ref[slot])

    for e in range(N_EXPERTS):
        k = j * N_EXPERTS + e
        nwin = (rhi_ref[k] - ral_ref[k] + (w - 1)) // w

        def extra(wi, carry, k=k):
            nominal = ral_ref[k] + wi * w
            base = pl.multiple_of(jnp.minimum(nominal, p_rows - w), 16)
            cp = pltpu.make_async_copy(ys_ref.at[pl.ds(base, w)], xwin_ref, xsem)
            cp.start()
            cp.wait()
            acc_ref[...] += _dot(sel(base, nominal, rhi_ref[k]), xwin_ref[...])
            return carry

        lax.fori_loop(1, nwin, extra, 0)

    o_ref[...] = _layer_norm_rows(DN_ALPHA * x_ref[...] + acc_ref[...], g_ref[...], b_ref[...])


def _combine(ral, rlo, rhi, x2d, rec2, g, b, ys, tm, w):
    t, d = x2d.shape
    row = lambda c: pl.BlockSpec((tm, c), lambda i, *_: (i, 0))
    full = lambda shape: pl.BlockSpec(shape, lambda i, *_: (0,) * len(shape))
    grid_spec = pltpu.PrefetchScalarGridSpec(
        num_scalar_prefetch=3,
        grid=(t // tm,),
        in_specs=[row(d), row(rec2.shape[1]), full(g.shape), full(b.shape),
                  pl.BlockSpec(memory_space=pl.ANY)],
        out_specs=row(d),
        scratch_shapes=[pltpu.VMEM((2, N_EXPERTS * w, d), BF16), pltpu.VMEM((w, d), BF16),
                        pltpu.VMEM((tm, d), F32), pltpu.SemaphoreType.DMA((2, N_EXPERTS)),
                        pltpu.SemaphoreType.DMA(())],
    )
    return pl.pallas_call(
        functools.partial(_combine_kernel, w=w),
        grid_spec=grid_spec,
        out_shape=jax.ShapeDtypeStruct((t, d), F32),
        compiler_params=pltpu.CompilerParams(
            dimension_semantics=("arbitrary",), vmem_limit_bytes=VMEM_LIMIT),
        name="combine",
    )(ral, rlo, rhi, x2d, rec2, g, b, ys)


def _route_plan(rec, nt, tm_e, unit, kc, tm_c, w):
    t = rec.shape[1]
    i32 = jnp.int32
    ar = jnp.arange(N_EXPERTS, dtype=i32)
    m1 = (rec[0].astype(i32)[:, None] == ar[None, :]).astype(i32)
    m2 = (rec[1].astype(i32)[:, None] == ar[None, :]).astype(i32)
    m = m1 + m2
    cinc = jnp.cumsum(m, axis=0)
    counts = cinc[-1]
    padded = ((counts + tm_e - 1) // tm_e) * tm_e
    ends = jnp.cumsum(padded)
    starts = ends - padded
    posmat = starts[None, :] + (cinc - m)
    pos1 = jnp.sum(m1 * posmat, axis=1)
    pos2 = jnp.sum(m2 * posmat, axis=1)
    post = jnp.where(m.T > 0, posmat.T, -1)
    rec2 = jnp.stack([pos1.astype(F32), pos2.astype(F32), rec[2], rec[3]]
                     + [jnp.zeros((t,), F32)] * 4, axis=1)
    tile_start = jnp.arange(nt, dtype=i32) * tm_e
    tile_e = jnp.sum((tile_start[:, None] >= ends[None, :]).astype(i32), axis=1)
    tile_v = tile_e < N_EXPERTS
    last_e = jnp.max(jnp.where(tile_v, tile_e, 0))
    tile_e = jnp.where(tile_v, tile_e, last_e)
    upt = tm_e // unit
    nu = nt * upt
    oh = (jnp.repeat(tile_e, upt)[:, None] == ar[None, :]).astype(i32)
    q0 = jnp.arange(nu, dtype=i32) * unit - jnp.sum(oh * starts[None, :], axis=1)
    q1 = jnp.minimum(q0 + unit, jnp.sum(oh * counts[None, :], axis=1))
    has = jnp.repeat(tile_v, upt) & (q1 > q0)
    cb = jnp.concatenate([jnp.zeros((1, N_EXPERTS), i32), cinc[LANES - 1::LANES]], axis=0)
    cbu = jnp.sum(oh[:, None, :] * cb[None, :, :], axis=2)
    jlo = jnp.sum((cbu[:, 1:] <= q0[:, None]).astype(i32), axis=1)
    jend = jnp.sum((cbu[:, :-1] < q1[:, None]).astype(i32), axis=1)
    uoff = jnp.where(has, jlo * LANES, 0)
    unc = jnp.where(has, (jend * LANES - uoff + kc - 1) // kc, 0)
    cbt = jnp.concatenate([jnp.zeros((1, N_EXPERTS), i32), cinc[tm_c - 1::tm_c]], axis=0)
    r_lo = starts[None, :] + cbt[:-1]
    r_hi = starts[None, :] + cbt[1:]
    ral = jnp.minimum((r_lo // 16) * 16, nt * tm_e - w)
    oh_t = (tile_e[:, None] == ar[None, :]).astype(i32)
    tile_rows = (jnp.sum(oh_t * counts[None, :], axis=1)
                 - (tile_start - jnp.sum(oh_t * starts[None, :], axis=1)))
    tile_status = jnp.where(tile_v, jnp.where(tile_rows > tm_e // 2, 2, 1), 0)
    return (rec2, post, tile_e, tile_status.astype(i32), uoff, unc,
            ral.reshape(-1), r_lo.reshape(-1), r_hi.reshape(-1))


def _moe(x2d, xb, rec, li, wg, wu, wd, g, b, tm_c, tm_e, fc, unit, kc, w):
    t, d = x2d.shape
    nt = (2 * t) // tm_e + N_EXPERTS
    rec2, post, tile_e, tile_v, uoff, unc, ral, rlo, rhi = _route_plan(
        rec, nt, tm_e, unit, kc, tm_c, w)
    xs = _dispatch(tile_e, uoff, unc, xb, post, nt, tm_e, unit, kc)
    ys = _gffn(tile_e, tile_v, xs, li, wg, wu, wd, tm_e, fc)
    return _combine(ral, rlo, rhi, x2d, rec2, g, b, ys, tm_c, w)


def _rotate_half_cols(w):
    half = w.shape[-1] // 2
    return jnp.concatenate([-w[..., half:], w[..., :half]], axis=-1)


def _t5_bucket(n):
    max_exact = NUM_BUCKETS // 2
    nf = jnp.maximum(n, 1).astype(F32)
    large = max_exact + (jnp.log(nf / max_exact) / math.log(MAX_DISTANCE / max_exact)
                         * (NUM_BUCKETS - max_exact)).astype(jnp.int32)
    large = jnp.minimum(large, NUM_BUCKETS - 1)
    return jnp.where(n < max_exact, n, large)


def _pick(n, prefs):
    for p in prefs:
        if n % p == 0:
            return p
    raise ValueError(f"no tile for extent {n}")


def kernel(x, rel_bias, w_in, mla_q_norm, mla_kv_norm, mla_w_uq, mla_w_uk, mla_w_uv, diff_lambda, diff_norm, w_o, ln1_g, ln1_b, ln2_g, ln2_b, ffn_w_gate, ffn_w_up, ffn_w_down, moe_router, moe_w_gate, moe_w_up, moe_w_down):
    b, s, d = x.shape
    t = b * s
    assert d == D_MODEL
    ta = _pick(s, (512, 256))
    assert ta % ATT_BLK == 0
    assert s >= 2 * ta or s == ta
    tm_proj = _pick(s, (512, 256, 128))
    tm_tok = _pick(t, (512, 256, 128))
    tm_e = _pick(2 * t, (512, 256))
    tm_c = _pick(t, (512, 256))
    unit, wrows = 128, 192
    kc = 768
    assert tm_e % unit == 0 and t >= kc and t % LANES == 0
    fc = D_FF_EXPERT // 2

    pos = jnp.arange(s, dtype=F32)
    inv = 1.0 / (ROPE_THETA ** (jnp.arange(0, MLA_ROPE, 2, dtype=F32) / MLA_ROPE))
    ang = pos[:, None] * inv[None, :]
    cos4 = jnp.tile(jnp.cos(ang), (1, 2 * MLA_HEADS))
    sin4 = jnp.tile(jnp.sin(ang), (1, 2 * MLA_HEADS))
    rb = rel_bias.astype(F32)
    kk = jnp.arange(ta)[:, None]
    qq = jnp.arange(ta)[None, :]
    tiles = []
    for off in (0, ta):
        dist = qq + off - kk
        bucket = _t5_bucket(jnp.maximum(dist, 0))
        bt = jnp.zeros((DIFF_HEADS, ta, ta), F32)
        for bk in range(NUM_BUCKETS):
            bt = bt + jnp.where((bucket == bk)[None], rb[bk][:, None, None], 0.0)
        tiles.append(jnp.where((dist >= 0)[None], bt * LOG2E, NEG_BIG))
    bias_tiles = jnp.stack(tiles, axis=1)
    assert ta + 1 >= MAX_DISTANCE
    cfar = rb[NUM_BUCKETS - 1] * LOG2E
    moe_wg, moe_wu, moe_wd = (w.astype(BF16) for w in (moe_w_gate, moe_w_up, moe_w_down))

    xc = x
    for l in range(DEPTH):
        lambda_init = 0.8 - 0.6 * math.exp(-0.3 * l)
        wl = w_in[l]
        kr_w = wl[:, _KR0:_KR0 + MLA_ROPE]
        win = jnp.concatenate([wl[:, :_KR0 + MLA_ROPE], _rotate_half_cols(kr_w),
                               wl[:, _KR0 + MLA_ROPE:]], axis=1).astype(BF16)
        uq = mla_w_uq[l].reshape(MLA_Q_RANK, MLA_HEADS, MLA_QK)
        uq_n = uq[:, :, :MLA_NOPE].reshape(MLA_Q_RANK, -1)
        uq_r = uq[:, :, MLA_NOPE:]
        wuq = jnp.concatenate([uq_n, uq_r.reshape(MLA_Q_RANK, -1),
                               _rotate_half_cols(uq_r).reshape(MLA_Q_RANK, -1)], axis=1).astype(BF16)
        wukv = jnp.concatenate([mla_w_uk[l], mla_w_uv[l]], axis=1).astype(BF16)

        qT, k, vT, dqT, dk, dvT = _proj(
            xc, win, mla_q_norm[l][None, :], mla_kv_norm[l][None, :], wuq, wukv, cos4, sin4, tm_proj)
        a_mla = _mla_attn(qT, k, vT, ta, MLA_HP).reshape(t, MLA_W)
        a_diff = _diff_attn(cfar, dqT, dk, dvT, bias_tiles, diff_lambda[l].astype(F32),
                            diff_norm[l].astype(F32)[:, None], ta, DIFF_HP,
                            lambda_init).reshape(t, DIFF_W)
        x2d = xc.reshape(t, d)
        wo = w_o[l].astype(BF16)
        g1, b1 = ln1_g[l][None, :], ln1_b[l][None, :]
        g2, b2 = ln2_g[l][None, :], ln2_b[l][None, :]
        if l % 2 == 0:
            x1 = _oproj(a_mla, a_diff, x2d, wo, g1, b1, tm_tok)
            i = l // 2
            x2 = _ffn(x1, ffn_w_gate[i].astype(BF16), ffn_w_up[i].astype(BF16),
                      ffn_w_down[i].astype(BF16), g2, b2, tm_tok)
        else:
            i = l // 2
            rpad = jnp.pad(moe_router[i].astype(F32).T, ((0, ROUTER_ROWS - N_EXPERTS), (0, 0)))
            rhi = rpad.astype(BF16)
            rlo = (rpad - rhi.astype(F32)).astype(BF16)
            x1, x1b, rec = _oproj(a_mla, a_diff, x2d, wo, g1, b1, tm_tok,
                                  router=(jnp.concatenate([rhi, rlo], axis=0), rhi))
            x2 = _moe(x1, x1b, rec, i, moe_wg, moe_wu, moe_wd, g2, b2,
                      tm_c, tm_e, fc, unit, kc, wrows)
        xc = x2.reshape(b, s, d)
    return xc
```

```python
import functools
import math

import jax
import jax.numpy as jnp
from jax import lax
from jax.experimental import pallas as pl
from jax.experimental.pallas import tpu as pltpu

D_MODEL = 1024
DEPTH = 4
MLA_HEADS = 4
MLA_NOPE = 128
MLA_ROPE = 64
MLA_V = 128
MLA_Q_RANK = 384
MLA_KV_RANK = 256
ROPE_THETA = 10000.0
DIFF_HEADS = 4
DIFF_QK = 64
DIFF_V = 2 * DIFF_QK
NUM_BUCKETS = 32
MAX_DISTANCE = 128
D_FF_DENSE = 2816
N_EXPERTS = 8
D_FF_EXPERT = 3584
DN_ALPHA = (2 * DEPTH) ** 0.25

MLA_QK = MLA_NOPE + MLA_ROPE
MLA_W = MLA_HEADS * MLA_V
DIFF_W = DIFF_HEADS * DIFF_V
DIFF_VA = DIFF_V + 16
_CQ0, _CKV0 = 0, MLA_Q_RANK
_KR0 = _CKV0 + MLA_KV_RANK
_DQ0 = _KR0 + 2 * MLA_ROPE
_DK0 = _DQ0 + DIFF_W
_DV0 = _DK0 + DIFF_W
IN_COLS_W = _DV0 + DIFF_W

LANES = 128
VMEM_LIMIT = 56 * 1024 * 1024
NEG_BIG = -1e30
ROUTER_ROWS = 16
REC_ROWS = 8
LOG2E = math.log2(math.e)
ATT_BLK = 256
MLA_HP = 4
DIFF_HP = 2

BF16 = jnp.bfloat16
F32 = jnp.float32


def _dot(a, b):
    return jnp.dot(a, b, preferred_element_type=F32)


def _layer_norm_rows(y, g, b):
    mu = jnp.mean(y, axis=-1, keepdims=True)
    d = y - mu
    var = jnp.mean(d * d, axis=-1, keepdims=True)
    return d * lax.rsqrt(var + 1e-5) * g + b


def _rms_rows(y, g, eps):
    return y * lax.rsqrt(jnp.mean(y * y, axis=-1, keepdims=True) + eps) * g


def _proj_kernel(x_ref, win_ref, qn_ref, kvn_ref, wuq_ref, wukv_ref, cos_ref, sin_ref,
                 qT_ref, k_ref, vT_ref, dqT_ref, dk_ref, dvT_ref):
    x = x_ref[0].astype(BF16)
    h = _dot(x, win_ref[0])
    cqn = _rms_rows(h[:, _CQ0:_CQ0 + MLA_Q_RANK], qn_ref[0], 1e-6)
    ckvn = _rms_rows(h[:, _CKV0:_CKV0 + MLA_KV_RANK], kvn_ref[0], 1e-6)
    q = _dot(cqn.astype(BF16), wuq_ref[0]) * (MLA_QK ** -0.5 * LOG2E)
    kv = _dot(ckvn.astype(BF16), wukv_ref[0])
    cos4, sin4 = cos_ref[...], sin_ref[...]
    nr = MLA_HEADS * MLA_ROPE
    qr = q[:, MLA_W:MLA_W + nr] * cos4 + q[:, MLA_W + nr:] * sin4
    kr = (h[:, _KR0:_KR0 + MLA_ROPE] * cos4[:, :MLA_ROPE]
          + h[:, _KR0 + MLA_ROPE:_KR0 + 2 * MLA_ROPE] * sin4[:, :MLA_ROPE])
    qrT = qr.T
    for hh in range(MLA_HEADS):
        c0 = hh * LANES
        qnT = q[:, c0:c0 + MLA_NOPE].T
        qT_ref[0, hh] = jnp.concatenate(
            [qnT, qrT[hh * MLA_ROPE:(hh + 1) * MLA_ROPE]], axis=0).astype(BF16)
        k_ref[0, hh] = jnp.concatenate([kv[:, c0:c0 + MLA_NOPE], kr], axis=1).astype(BF16)
        vT_ref[0, hh] = kv[:, MLA_W + c0:MLA_W + c0 + MLA_V].T.astype(BF16)
        dqT = (h[:, _DQ0 + c0:_DQ0 + c0 + LANES] * (DIFF_QK ** -0.5 * LOG2E)).T.astype(BF16)
        zeros = jnp.zeros((DIFF_QK, x.shape[0]), BF16)
        dqT_ref[0, hh, 0] = jnp.concatenate([dqT[:DIFF_QK], zeros], axis=0)
        dqT_ref[0, hh, 1] = jnp.concatenate([zeros, dqT[DIFF_QK:]], axis=0)
        dk_ref[0, hh] = h[:, _DK0 + c0:_DK0 + c0 + LANES].astype(BF16)
        dvT_ref[0, hh] = jnp.concatenate(
            [h[:, _DV0 + c0:_DV0 + c0 + DIFF_V].T, jnp.ones((DIFF_VA - DIFF_V, x.shape[0]), F32)],
            axis=0).astype(BF16)


def _proj(x, l, win, qn, kvn, wuq, wukv, cos4, sin4, tm):
    b, s, d = x.shape
    hd = MLA_HEADS
    layer = lambda a: pl.BlockSpec((1,) + a.shape[1:], lambda bi, i: (l,) + (0,) * (a.ndim - 1))
    tmaj = lambda w: pl.BlockSpec((1, hd, tm, w), lambda bi, i: (bi, 0, i, 0))
    fmaj = lambda w: pl.BlockSpec((1, hd, w, tm), lambda bi, i: (bi, 0, 0, i))
    sds = jax.ShapeDtypeStruct
    return pl.pallas_call(
        _proj_kernel,
        grid=(b, s // tm),
        in_specs=[
            pl.BlockSpec((1, tm, d), lambda bi, i: (bi, i, 0)),
            layer(win), layer(qn), layer(kvn), layer(wuq), layer(wukv),
            pl.BlockSpec((tm, cos4.shape[1]), lambda bi, i: (i, 0)),
            pl.BlockSpec((tm, sin4.shape[1]), lambda bi, i: (i, 0)),
        ],
        out_specs=[fmaj(MLA_QK), tmaj(MLA_QK), fmaj(MLA_V),
                   pl.BlockSpec((1, hd, 2, LANES, tm), lambda bi, i: (bi, 0, 0, 0, i)),
                   tmaj(LANES), fmaj(DIFF_VA)],
        out_shape=[
            sds((b, hd, MLA_QK, s), BF16), sds((b, hd, s, MLA_QK), BF16), sds((b, hd, MLA_V, s), BF16),
            sds((b, hd, 2, LANES, s), BF16), sds((b, hd, s, LANES), BF16), sds((b, hd, DIFF_VA, s), BF16),
        ],
        compiler_params=pltpu.CompilerParams(
            dimension_semantics=("arbitrary", "arbitrary"), vmem_limit_bytes=VMEM_LIMIT),
        name="proj",
    )(x, win, qn, kvn, wuq, wukv, cos4, sin4)


def _softmax_step(s, vj, m_ref, l_ref, acc_ref, shift=None):
    m_prev = m_ref[...]
    smax = jnp.max(s, axis=0, keepdims=True)
    if shift is not None:
        smax = smax + shift
    m_new = jnp.maximum(m_prev, smax)
    a = jnp.exp2(m_prev - m_new)
    p = jnp.exp2(s - (m_new if shift is None else m_new - shift))
    if l_ref is not None:
        l_ref[...] = a * l_ref[...] + jnp.sum(p, axis=0, keepdims=True)
    acc_ref[...] = a * acc_ref[...] + _dot(vj, p.astype(BF16))
    m_ref[...] = m_new


def _chain_scratch(n, dv):
    return [pltpu.VMEM((1, ATT_BLK), F32)] * (2 * n) + [pltpu.VMEM((dv, ATT_BLK), F32)] * n


def _init_chains(state):
    n = len(state) // 3
    m_refs, l_refs, acc_refs = state[:n], state[n:2 * n], state[2 * n:]
    for m_ref, l_ref, acc_ref in zip(m_refs, l_refs, acc_refs):
        m_ref[...] = jnp.full(m_ref.shape, NEG_BIG, F32)
        l_ref[...] = jnp.zeros(l_ref.shape, F32)
        acc_ref[...] = jnp.zeros(acc_ref.shape, F32)
    return m_refs, l_refs, acc_refs


def _pipelined_tiles(qi, logits, next_logits, softmax_pv):
    a, b, c = 0, 1, 2

    @pl.when(qi == 0)
    def _():
        logits(0, a, True)
        next_logits(c)
        softmax_pv(0, a, "diag")

    @pl.when(qi == 1)
    def _():
        logits(1, a, True)
        softmax_pv(0, c, "near")
        next_logits(c)
        softmax_pv(1, a, "diag")

    @pl.when(qi >= 2)
    def _():
        logits(1, a)
        softmax_pv(0, c, "far")
        nfar = qi - 2

        def body(i, carry):
            j = 2 * i + 1
            logits(j + 1, b)
            softmax_pv(j, a, "far")
            logits(j + 2, a)
            softmax_pv(j + 1, b, "far")
            return carry

        lax.fori_loop(0, lax.shift_right_logical(nfar, 1), body, 0)
        odd = lax.rem(nfar, 2) == 1

        @pl.when(jnp.logical_not(odd))
        def _():
            logits(qi, b, True)
            softmax_pv(qi - 1, a, "near")
            next_logits(c)
            softmax_pv(qi, b, "diag")

        @pl.when(odd)
        def _():
            logits(qi - 1, b)
            softmax_pv(qi - 2, a, "far")
            logits(qi, a, True)
            softmax_pv(qi - 1, b, "near")
            next_logits(c)
            softmax_pv(qi, a, "diag")


def _mla_attn_kernel(qT_ref, qTn_ref, k_ref, vT_ref, o_ref, sa_ref, sb_ref, sc_ref, *state,
                     ta, hp):
    qi = pl.program_id(2)
    m_refs, l_refs, acc_refs = _init_chains(state)
    s_refs = (sa_ref, sb_ref, sc_ref)

    nb = ta // ATT_BLK
    blk = ATT_BLK

    def logits(j, buf, diagonal=False):
        off = pl.multiple_of(j * ta, ta)
        for h in range(hp):
            if not diagonal:
                s_refs[buf][h] = _dot(k_ref[0, h, pl.ds(off, ta), :], qT_ref[0, h])
                continue
            for qs in range(nb):
                nk, q0 = (qs + 1) * blk, qs * blk
                s_refs[buf][h, :nk, q0:q0 + blk] = _dot(k_ref[0, h, pl.ds(off, nk), :],
                                                        qT_ref[0, h, :, q0:q0 + blk])

    def next_logits(buf):
        for h in range(hp):
            s_refs[buf][h] = _dot(k_ref[0, h, :ta, :], qTn_ref[0, h])

    def softmax_pv(j, buf, kind):
        off = pl.multiple_of(j * ta, ta)
        diagonal = kind == "diag"
        for h in range(hp):
            for qs in range(nb):
                nk = (qs + 1) * blk if diagonal else ta
                s = s_refs[buf][h, :nk, qs * blk:(qs + 1) * blk]
                if diagonal:
                    kpos = lax.broadcasted_iota(jnp.int32, (nk, blk), 0)
                    qpos = lax.broadcasted_iota(jnp.int32, (nk, blk), 1) + qs * blk
                    s = jnp.where(kpos <= qpos, s, NEG_BIG)
                c = h * nb + qs
                _softmax_step(s, vT_ref[0, h, :, pl.ds(off, nk)], m_refs[c], l_refs[c], acc_refs[c])

    _pipelined_tiles(qi, logits, next_logits, softmax_pv)
    for h in range(hp):
        for qs in range(nb):
            c = h * nb + qs
            out = acc_refs[c][...] * (1.0 / l_refs[c][...])
            o_ref[0, qs * blk:(qs + 1) * blk, h * MLA_V:(h + 1) * MLA_V] = out.T.astype(BF16)


def _mla_attn(qT, k, vT, ta, hp):
    b, hd, _, s = qT.shape
    nq = s // ta
    return pl.pallas_call(
        functools.partial(_mla_attn_kernel, ta=ta, hp=hp),
        grid=(b, hd // hp, nq),
        in_specs=[
            pl.BlockSpec((1, hp, MLA_QK, ta), lambda bi, h, i: (bi, h, 0, i)),
            pl.BlockSpec((1, hp, MLA_QK, ta), lambda bi, h, i: (bi, h, 0, jnp.minimum(i + 1, nq - 1))),
            pl.BlockSpec((1, hp, s, MLA_QK), lambda bi, h, i: (bi, h, 0, 0)),
            pl.BlockSpec((1, hp, MLA_V, s), lambda bi, h, i: (bi, h, 0, 0)),
        ],
        out_specs=pl.BlockSpec((1, ta, hp * MLA_V), lambda bi, h, i: (bi, i, h)),
        out_shape=jax.ShapeDtypeStruct((b, s, hd * MLA_V), BF16),
        scratch_shapes=([pltpu.VMEM((hp, ta, ta), F32)] * 3
                        + _chain_scratch(hp * ta // ATT_BLK, MLA_V)),
        compiler_params=pltpu.CompilerParams(
            dimension_semantics=("arbitrary",) * 3, vmem_limit_bytes=VMEM_LIMIT),
        name="mla_attn",
    )(qT, qT, k, vT)


def _diff_attn_kernel(cfar_ref, qT_ref, qTn_ref, k_ref, vT_ref, bias_ref, lamp_ref, g_ref, o_ref,
                      sa_ref, sb_ref, sc_ref, *state, ta, hp, lambda_init):
    hg = pl.program_id(1)
    qi = pl.program_id(2)
    s_refs = (sa_ref, sb_ref, sc_ref)
    m_refs, l_refs, acc_refs = _init_chains(state)

    nb = ta // ATT_BLK
    blk = ATT_BLK

    def logits(j, buf, diagonal=False):
        off = pl.multiple_of(j * ta, ta)
        for h in range(hp):
            for c in range(2):
                if not diagonal:
                    s_refs[buf][2 * h + c] = _dot(k_ref[0, h, pl.ds(off, ta), :],
                                                  qT_ref[0, h, c])
                    continue
                for qb in range(nb):
                    nk, q0 = (qb + 1) * blk, qb * blk
                    s_refs[buf][2 * h + c, :nk, q0:q0 + blk] = _dot(
                        k_ref[0, h, pl.ds(off, nk), :], qT_ref[0, h, c, :, q0:q0 + blk])

    def next_logits(buf):
        for h in range(hp):
            for c in range(2):
                s_refs[buf][2 * h + c] = _dot(k_ref[0, h, :ta, :], qTn_ref[0, h, c])

    def softmax_pv(j, buf, kind):
        off = pl.multiple_of(j * ta, ta)
        diagonal = kind == "diag"
        for h in range(hp):
            for qb in range(nb):
                nk = (qb + 1) * blk if diagonal else ta
                vj = vT_ref[0, h, :, pl.ds(off, nk)]
                for c in range(2):
                    s = s_refs[buf][2 * h + c, :nk, qb * blk:(qb + 1) * blk]
                    i = (2 * h + c) * nb + qb
                    if kind == "far" or (kind == "near" and qb * blk + 1 >= MAX_DISTANCE):
                        _softmax_step(s, vj, m_refs[i], None, acc_refs[i],
                                      shift=cfar_ref[hg * hp + h])
                    else:
                        bias = bias_ref[h, 0 if diagonal else 1, :nk, qb * blk:(qb + 1) * blk]
                        _softmax_step(s + bias, vj, m_refs[i], None, acc_refs[i])

    _pipelined_tiles(qi, logits, next_logits, softmax_pv)

    lp = lamp_ref[...]
    lam = (jnp.exp(jnp.sum(lp[0:1] * lp[1:2], axis=-1, keepdims=True))
           - jnp.exp(jnp.sum(lp[2:3] * lp[3:4], axis=-1, keepdims=True)) + lambda_init)
    for h in range(hp):
        for qb in range(nb):
            a0 = acc_refs[(2 * h) * nb + qb][...]
            a1 = acc_refs[(2 * h + 1) * nb + qb][...]
            out = (a0[:DIFF_V] * (1.0 / a0[DIFF_V:DIFF_V + 1])
                   - lam * (a1[:DIFF_V] * (1.0 / a1[DIFF_V:DIFF_V + 1])))
            ms = jnp.mean(out * out, axis=0, keepdims=True)
            out = out * lax.rsqrt(ms + 1e-5) * g_ref[...] * (1.0 - lambda_init)
            o_ref[0, qb * blk:(qb + 1) * blk, h * DIFF_V:(h + 1) * DIFF_V] = out.T.astype(BF16)


def _diff_attn(cfar, qT, k, vT, bias_tiles, lam_params, g_col, ta, hp, lambda_init):
    b, hd, _, _, s = qT.shape
    nq = s // ta
    grid_spec = pltpu.PrefetchScalarGridSpec(
        num_scalar_prefetch=1,
        grid=(b, hd // hp, nq),
        in_specs=[
            pl.BlockSpec((1, hp, 2, LANES, ta), lambda bi, h, i, cf: (bi, h, 0, 0, i)),
            pl.BlockSpec((1, hp, 2, LANES, ta),
                         lambda bi, h, i, cf: (bi, h, 0, 0, jnp.minimum(i + 1, nq - 1))),
            pl.BlockSpec((1, hp, s, LANES), lambda bi, h, i, cf: (bi, h, 0, 0)),
            pl.BlockSpec((1, hp, DIFF_VA, s), lambda bi, h, i, cf: (bi, h, 0, 0)),
            pl.BlockSpec((hp, 2, ta, ta), lambda bi, h, i, cf: (h, 0, 0, 0)),
            pl.BlockSpec(lam_params.shape, lambda bi, h, i, cf: (0, 0)),
            pl.BlockSpec(g_col.shape, lambda bi, h, i, cf: (0, 0)),
        ],
        out_specs=pl.BlockSpec((1, ta, hp * DIFF_V), lambda bi, h, i, cf: (bi, i, h)),
        scratch_shapes=([pltpu.VMEM((2 * hp, ta, ta), F32)] * 3
                        + _chain_scratch(2 * hp * ta // ATT_BLK, DIFF_VA)),
    )
    return pl.pallas_call(
        functools.partial(_diff_attn_kernel, ta=ta, hp=hp, lambda_init=lambda_init),
        grid_spec=grid_spec,
        out_shape=jax.ShapeDtypeStruct((b, s, hd * DIFF_V), BF16),
        compiler_params=pltpu.CompilerParams(
            dimension_semantics=("arbitrary",) * 3, vmem_limit_bytes=VMEM_LIMIT),
        name="diff_attn",
    )(cfar, qT, qT, k, vT, bias_tiles, lam_params, g_col)


def _split_bf16(v):
    hi = v.astype(BF16)
    return hi, (v - hi.astype(F32)).astype(BF16)


def _oproj_route_kernel(am_ref, ad_ref, x_ref, wo_ref, g_ref, b_ref, rhl_ref, rh_ref,
                        o_ref, ob_ref, r_ref):
    mix = _dot(am_ref[...], wo_ref[0, :MLA_W, :]) + _dot(ad_ref[...], wo_ref[0, MLA_W:, :])
    y = _layer_norm_rows(DN_ALPHA * x_ref[...] + mix, g_ref[...], b_ref[...])
    yhi, ylo = _split_bf16(y)
    nt = (((1,), (1,)), ((), ()))
    both = lax.dot_general(rhl_ref[...], yhi, nt, preferred_element_type=F32)
    logits = (both[:ROUTER_ROWS] + both[ROUTER_ROWS:]
              + lax.dot_general(rh_ref[...], ylo, nt, preferred_element_type=F32))
    row = lax.broadcasted_iota(jnp.int32, logits.shape, 0)
    logits = jnp.where(row < N_EXPERTS, logits, NEG_BIG)
    m1 = jnp.max(logits, axis=0, keepdims=True)
    i1 = jnp.min(jnp.where(logits == m1, row, ROUTER_ROWS), axis=0, keepdims=True)
    rest_l = jnp.where(row == i1, NEG_BIG, logits)
    m2 = jnp.max(rest_l, axis=0, keepdims=True)
    i2 = jnp.min(jnp.where(rest_l == m2, row, ROUTER_ROWS), axis=0, keepdims=True)
    e2 = jnp.exp(m2 - m1)
    g1 = 1.0 / (1.0 + e2)
    g2 = e2 * g1
    row8 = lax.broadcasted_iota(jnp.int32, r_ref.shape, 0)
    rec = jnp.where(row8 == 0, i1.astype(F32),
                    jnp.where(row8 == 1, i2.astype(F32),
                              jnp.where(row8 == 2, g1, jnp.where(row8 == 3, g2, 0.0))))
    o_ref[...] = y
    ob_ref[...] = yhi
    r_ref[...] = rec


def _oproj_route(a_mla, a_diff, x2d, l, wo, g, b, rhl, rh, tm):
    t, d = x2d.shape
    row = lambda w: pl.BlockSpec((tm, w), lambda i: (i, 0))
    full = lambda a: pl.BlockSpec(a.shape, lambda i: (0,) * a.ndim)
    return pl.pallas_call(
        _oproj_route_kernel,
        grid=(t // tm,),
        in_specs=[row(MLA_W), row(DIFF_W), row(d),
                  pl.BlockSpec((1,) + wo.shape[1:], lambda i: (l, 0, 0)),
                  full(g), full(b), full(rhl), full(rh)],
        out_specs=[row(d), row(d), pl.BlockSpec((REC_ROWS, tm), lambda i: (0, i))],
        out_shape=[jax.ShapeDtypeStruct((t, d), F32), jax.ShapeDtypeStruct((t, d), BF16),
                   jax.ShapeDtypeStruct((REC_ROWS, t), F32)],
        compiler_params=pltpu.CompilerParams(
            dimension_semantics=("arbitrary",), vmem_limit_bytes=VMEM_LIMIT),
        name="oproj_route",
    )(a_mla, a_diff, x2d, wo, g, b, rhl, rh)


def _oproj_ffn_kernel(am_ref, ad_ref, x_ref, wo_ref, g1_ref, b1_ref, wg_ref, wu_ref, wd_ref,
                      g2_ref, b2_ref, o_ref):
    mix = _dot(am_ref[...], wo_ref[0, :MLA_W, :]) + _dot(ad_ref[...], wo_ref[0, MLA_W:, :])
    x1 = _layer_norm_rows(DN_ALPHA * x_ref[...] + mix, g1_ref[...], b1_ref[...])
    xb = x1.astype(BF16)
    gate = _dot(xb, wg_ref[0])
    up = _dot(xb, wu_ref[0])
    hid = (gate * jax.nn.sigmoid(gate) * up).astype(BF16)
    f = _dot(hid, wd_ref[0])
    o_ref[...] = _layer_norm_rows(DN_ALPHA * x1 + f, g2_ref[...], b2_ref[...])


def _oproj_ffn(a_mla, a_diff, x2d, l, wo, g1, b1, li, wg, wu, wd, g2, b2, tm):
    t, d = x2d.shape
    row = lambda w: pl.BlockSpec((tm, w), lambda i: (i, 0))
    layer = lambda a, k: pl.BlockSpec((1,) + a.shape[1:], lambda i: (k,) + (0,) * (a.ndim - 1),
                                      pipeline_mode=pl.Buffered(1))
    vec = lambda a: pl.BlockSpec(a.shape, lambda i: (0, 0))
    return pl.pallas_call(
        _oproj_ffn_kernel,
        grid=(t // tm,),
        in_specs=[row(MLA_W), row(DIFF_W), row(d), layer(wo, l), vec(g1), vec(b1),
                  layer(wg, li), layer(wu, li), layer(wd, li), vec(g2), vec(b2)],
        out_specs=row(d),
        out_shape=jax.ShapeDtypeStruct((t, d), F32),
        compiler_params=pltpu.CompilerParams(
            dimension_semantics=("arbitrary",), vmem_limit_bytes=VMEM_LIMIT),
        name="oproj_ffn",
    )(a_mla, a_diff, x2d, wo, g1, b1, wg, wu, wd, g2, b2)


def _dispatch_kernel(te_ref, uoff_ref, unc_ref, x_ref, post_ref, o_ref, acc_ref, *, tm, unit, kc):
    i = pl.program_id(0)
    e = te_ref[i]
    t = x_ref.shape[0]
    upt = tm // unit
    for r in range(upt):
        u = i * upt + r
        n = unc_ref[u]
        start0 = uoff_ref[u]
        row_pos = u * unit + lax.broadcasted_iota(jnp.int32, (unit, 1), 0)

        def chunk(c):
            nominal = start0 + c * kc
            off = pl.multiple_of(jnp.minimum(nominal, t - kc), LANES)
            dest = post_ref[pl.ds(e, 1), pl.ds(off, kc)]
            tok = off + lax.broadcasted_iota(jnp.int32, (1, kc), 1)
            dest = jnp.where(tok >= nominal, dest, -1)
            sel = jnp.where(dest == row_pos, 1.0, 0.0).astype(BF16)
            return _dot(sel, x_ref[pl.ds(off, kc), :])

        @pl.when(n > 0)
        def _():
            acc_ref[...] = chunk(0)

        @pl.when(n == 0)
        def _():
            acc_ref[...] = jnp.zeros(acc_ref.shape, F32)

        def more(c, carry):
            acc_ref[...] += chunk(c)
            return carry

        lax.fori_loop(1, n, more, 0)
        o_ref[r * unit:(r + 1) * unit, :] = acc_ref[...].astype(BF16)


def _dispatch(tile_e, uoff, unc, xb, post, nt, tm, unit, kc):
    t, d = xb.shape
    const = lambda shape: pl.BlockSpec(shape, lambda i, *_: (0,) * len(shape),
                                       pipeline_mode=pl.Buffered(1))
    grid_spec = pltpu.PrefetchScalarGridSpec(
        num_scalar_prefetch=3,
        grid=(nt,),
        in_specs=[const(xb.shape), const(post.shape)],
        out_specs=pl.BlockSpec((tm, d), lambda i, *_: (i, 0)),
        scratch_shapes=[pltpu.VMEM((unit, d), F32)],
    )
    return pl.pallas_call(
        functools.partial(_dispatch_kernel, tm=tm, unit=unit, kc=kc),
        grid_spec=grid_spec,
        out_shape=jax.ShapeDtypeStruct((nt * tm, d), BF16),
        compiler_params=pltpu.CompilerParams(
            dimension_semantics=("arbitrary",), vmem_limit_bytes=VMEM_LIMIT),
        name="dispatch",
    )(tile_e, uoff, unc, xb, post)


def _gffn_kernel(te_ref, tv_ref, x_ref, wg_ref, wu_ref, wd_ref, o_ref, acc_ref, *, nf):
    i = pl.program_id(0)
    f = pl.program_id(1)
    status = tv_ref[i]
    tm = x_ref.shape[0]
    half = tm // 2

    def swiglu_rows(rows):
        xb = x_ref[:rows, :]
        gate = _dot(xb, wg_ref[0, 0])
        up = _dot(xb, wu_ref[0, 0])
        hid = (gate * jax.nn.sigmoid(gate) * up).astype(BF16)
        part = _dot(hid, wd_ref[0, 0])

        @pl.when(f == 0)
        def _():
            acc_ref[:rows, :] = part

        @pl.when(f > 0)
        def _():
            acc_ref[:rows, :] += part

    @pl.when(status == 2)
    def _():
        swiglu_rows(tm)

    @pl.when(status == 1)
    def _():
        swiglu_rows(half)

    @pl.when(jnp.logical_and(f == nf - 1, status == 2))
    def _():
        o_ref[...] = acc_ref[...].astype(BF16)

    @pl.when(jnp.logical_and(f == nf - 1, status == 1))
    def _():
        o_ref[:half, :] = acc_ref[:half, :].astype(BF16)
        o_ref[half:, :] = jnp.zeros((tm - half, o_ref.shape[1]), BF16)

    @pl.when(jnp.logical_and(f == nf - 1, status == 0))
    def _():
        o_ref[...] = jnp.zeros(o_ref.shape, BF16)


def _gffn(tile_e, tile_v, xs, li, wg, wu, wd, tm, fc):
    p, d = xs.shape
    nt = p // tm
    nf = wg.shape[3] // fc
    fidx = lambda i, f, te, tv: jnp.where(tv[i] > 0, f, nf - 1)
    grid_spec = pltpu.PrefetchScalarGridSpec(
        num_scalar_prefetch=2,
        grid=(nt, nf),
        in_specs=[
            pl.BlockSpec((tm, d), lambda i, f, te, tv: (i, 0)),
            pl.BlockSpec((1, 1, d, fc), lambda i, f, te, tv: (li, te[i], 0, fidx(i, f, te, tv))),
            pl.BlockSpec((1, 1, d, fc), lambda i, f, te, tv: (li, te[i], 0, fidx(i, f, te, tv))),
            pl.BlockSpec((1, 1, fc, d), lambda i, f, te, tv: (li, te[i], fidx(i, f, te, tv), 0)),
        ],
        out_specs=pl.BlockSpec((tm, d), lambda i, f, te, tv: (i, 0)),
        scratch_shapes=[pltpu.VMEM((tm, d), F32)],
    )
    return pl.pallas_call(
        functools.partial(_gffn_kernel, nf=nf),
        grid_spec=grid_spec,
        out_shape=jax.ShapeDtypeStruct((p, d), BF16),
        compiler_params=pltpu.CompilerParams(
            dimension_semantics=("arbitrary", "arbitrary"), vmem_limit_bytes=VMEM_LIMIT),
        name="grouped_ffn",
    )(tile_e, tile_v, xs, wg, wu, wd)


def _combine_kernel(ral_ref, rlo_ref, rhi_ref, x_ref, rec_ref, g_ref, b_ref, ys_ref, o_ref,
                    win_ref, xwin_ref, acc_ref, sems, xsem, *, w):
    j = pl.program_id(0)
    nj = pl.num_programs(0)
    slot = lax.rem(j, 2)
    p_rows = ys_ref.shape[0]

    def win_copy(jj, e, sl):
        r = pl.multiple_of(ral_ref[jj * N_EXPERTS + e], 16)
        return pltpu.make_async_copy(ys_ref.at[pl.ds(r, w)], win_ref.at[sl, pl.ds(e * w, w)],
                                     sems.at[sl, e])

    @pl.when(j == 0)
    def _():
        for e in range(N_EXPERTS):
            win_copy(0, e, 0).start()

    @pl.when(j + 1 < nj)
    def _():
        for e in range(N_EXPERTS):
            win_copy(j + 1, e, 1 - slot).start()

    rec = rec_ref[...]
    p1, p2, g1, g2 = rec[:, 0:1], rec[:, 1:2], rec[:, 2:3], rec[:, 3:4]

    def sel(base, lo, hi):
        rows = base + lax.broadcasted_iota(jnp.int32, (1, w), 1)
        rp = jnp.where((rows >= lo) & (rows < hi), rows, -1).astype(F32)
        return (jnp.where(p1 == rp, g1, 0.0) + jnp.where(p2 == rp, g2, 0.0)).astype(BF16)

    lane = lax.broadcasted_iota(jnp.int32, (1, N_EXPERTS * w), 1)
    rp = jnp.full((1, N_EXPERTS * w), -1, jnp.int32)
    for e in range(N_EXPERTS):
        win_copy(j, e, slot).wait()
        k = j * N_EXPERTS + e
        rows = ral_ref[k] + lane - e * w
        mine = ((lane >= e * w) & (lane < (e + 1) * w)
                & (rows >= rlo_ref[k]) & (rows < rhi_ref[k]))
        rp = jnp.where(mine, rows, rp)
    rp = rp.astype(F32)
    sel_all = (jnp.where(p1 == rp, g1, 0.0) + jnp.where(p2 == rp, g2, 0.0)).astype(BF16)
    acc_ref[...] = _dot(sel_all, win_ref[slot])

    for e in range(N_EXPERTS):
        k = j * N_EXPERTS + e
        nwin = (rhi_ref[k] - ral_ref[k] + (w - 1)) // w

        def extra(wi, carry, k=k):
            nominal = ral_ref[k] + wi * w
            base = pl.multiple_of(jnp.minimum(nominal, p_rows - w), 16)
            cp = pltpu.make_async_copy(ys_ref.at[pl.ds(base, w)], xwin_ref, xsem)
            cp.start()
            cp.wait()
            acc_ref[...] += _dot(sel(base, nominal, rhi_ref[k]), xwin_ref[...])
            return carry

        lax.fori_loop(1, nwin, extra, 0)

    o_ref[...] = _layer_norm_rows(DN_ALPHA * x_ref[...] + acc_ref[...], g_ref[...], b_ref[...])


def _combine(ral, rlo, rhi, x2d, rec2, g, b, ys, tm, w):
    t, d = x2d.shape
    row = lambda c: pl.BlockSpec((tm, c), lambda i, *_: (i, 0))
    full = lambda shape: pl.BlockSpec(shape, lambda i, *_: (0,) * len(shape))
    grid_spec = pltpu.PrefetchScalarGridSpec(
        num_scalar_prefetch=3,
        grid=(t // tm,),
        in_specs=[row(d), row(rec2.shape[1]), full(g.shape), full(b.shape),
                  pl.BlockSpec(memory_space=pl.ANY)],
        out_specs=row(d),
        scratch_shapes=[pltpu.VMEM((2, N_EXPERTS * w, d), BF16), pltpu.VMEM((w, d), BF16),
                        pltpu.VMEM((tm, d), F32), pltpu.SemaphoreType.DMA((2, N_EXPERTS)),
                        pltpu.SemaphoreType.DMA(())],
    )
    return pl.pallas_call(
        functools.partial(_combine_kernel, w=w),
        grid_spec=grid_spec,
        out_shape=jax.ShapeDtypeStruct((t, d), F32),
        compiler_params=pltpu.CompilerParams(
            dimension_semantics=("arbitrary",), vmem_limit_bytes=VMEM_LIMIT),
        name="combine",
    )(ral, rlo, rhi, x2d, rec2, g, b, ys)


def _route_plan(rec, nt, tm_e, unit, kc, tm_c, w):
    t = rec.shape[1]
    i32 = jnp.int32
    ar = jnp.arange(N_EXPERTS, dtype=i32)
    m1 = (rec[0].astype(i32)[:, None] == ar[None, :]).astype(i32)
    m2 = (rec[1].astype(i32)[:, None] == ar[None, :]).astype(i32)
    m = m1 + m2
    cinc = jnp.cumsum(m, axis=0)
    counts = cinc[-1]
    padded = ((counts + tm_e - 1) // tm_e) * tm_e
    ends = jnp.cumsum(padded)
    starts = ends - padded
    posmat = starts[None, :] + (cinc - m)
    pos1 = jnp.sum(m1 * posmat, axis=1)
    pos2 = jnp.sum(m2 * posmat, axis=1)
    post = jnp.where(m.T > 0, posmat.T, -1)
    rec2 = jnp.stack([pos1.astype(F32), pos2.astype(F32), rec[2], rec[3]]
                     + [jnp.zeros((t,), F32)] * 4, axis=1)
    tile_start = jnp.arange(nt, dtype=i32) * tm_e
    tile_e = jnp.sum((tile_start[:, None] >= ends[None, :]).astype(i32), axis=1)
    tile_v = tile_e < N_EXPERTS
    last_e = jnp.max(jnp.where(tile_v, tile_e, 0))
    tile_e = jnp.where(tile_v, tile_e, last_e)
    upt = tm_e // unit
    nu = nt * upt
    oh = (jnp.repeat(tile_e, upt)[:, None] == ar[None, :]).astype(i32)
    q0 = jnp.arange(nu, dtype=i32) * unit - jnp.sum(oh * starts[None, :], axis=1)
    q1 = jnp.minimum(q0 + unit, jnp.sum(oh * counts[None, :], axis=1))
    has = jnp.repeat(tile_v, upt) & (q1 > q0)
    cb = jnp.concatenate([jnp.zeros((1, N_EXPERTS), i32), cinc[LANES - 1::LANES]], axis=0)
    cbu = jnp.sum(oh[:, None, :] * cb[None, :, :], axis=2)
    jlo = jnp.sum((cbu[:, 1:] <= q0[:, None]).astype(i32), axis=1)
    jend = jnp.sum((cbu[:, :-1] < q1[:, None]).astype(i32), axis=1)
    uoff = jnp.where(has, jlo * LANES, 0)
    unc = jnp.where(has, (jend * LANES - uoff + kc - 1) // kc, 0)
    cbt = jnp.concatenate([jnp.zeros((1, N_EXPERTS), i32), cinc[tm_c - 1::tm_c]], axis=0)
    r_lo = starts[None, :] + cbt[:-1]
    r_hi = starts[None, :] + cbt[1:]
    ral = jnp.minimum((r_lo // 16) * 16, nt * tm_e - w)
    oh_t = (tile_e[:, None] == ar[None, :]).astype(i32)
    tile_rows = (jnp.sum(oh_t * counts[None, :], axis=1)
                 - (tile_start - jnp.sum(oh_t * starts[None, :], axis=1)))
    tile_status = jnp.where(tile_v, jnp.where(tile_rows > tm_e // 2, 2, 1), 0)
    return (rec2, post, tile_e, tile_status.astype(i32), uoff, unc,
            ral.reshape(-1), r_lo.reshape(-1), r_hi.reshape(-1))


def _moe(x2d, xb, rec, li, wg, wu, wd, g, b, tm_c, tm_e, fc, unit, kc, w):
    t, d = x2d.shape
    nt = (2 * t) // tm_e + N_EXPERTS
    rec2, post, tile_e, tile_v, uoff, unc, ral, rlo, rhi = _route_plan(
        rec, nt, tm_e, unit, kc, tm_c, w)
    xs = _dispatch(tile_e, uoff, unc, xb, post, nt, tm_e, unit, kc)
    ys = _gffn(tile_e, tile_v, xs, li, wg, wu, wd, tm_e, fc)
    return _combine(ral, rlo, rhi, x2d, rec2, g, b, ys, tm_c, w)


def _rotate_half_cols(w):
    half = w.shape[-1] // 2
    return jnp.concatenate([-w[..., half:], w[..., :half]], axis=-1)


def _t5_bucket(n):
    max_exact = NUM_BUCKETS // 2
    nf = jnp.maximum(n, 1).astype(F32)
    large = max_exact + (jnp.log(nf / max_exact) / math.log(MAX_DISTANCE / max_exact)
                         * (NUM_BUCKETS - max_exact)).astype(jnp.int32)
    large = jnp.minimum(large, NUM_BUCKETS - 1)
    return jnp.where(n < max_exact, n, large)


def _pick(n, prefs):
    for p in prefs:
        if n % p == 0:
            return p
    raise ValueError(f"no tile for extent {n}")


def kernel(x, rel_bias, w_in, mla_q_norm, mla_kv_norm, mla_w_uq, mla_w_uk, mla_w_uv, diff_lambda, diff_norm, w_o, ln1_g, ln1_b, ln2_g, ln2_b, ffn_w_gate, ffn_w_up, ffn_w_down, moe_router, moe_w_gate, moe_w_up, moe_w_down):
    b, s, d = x.shape
    t = b * s
    assert d == D_MODEL
    ta = _pick(s, (512, 256))
    assert ta % ATT_BLK == 0
    assert s >= 2 * ta or s == ta
    tm_proj = _pick(s, (512, 256, 128))
    tm_tok = _pick(t, (512, 256, 128))
    tm_e = _pick(2 * t, (512, 256))
    tm_c = _pick(t, (512, 256))
    unit, wrows = 128, 192
    kc = 768
    assert tm_e % unit == 0 and t >= kc and t % LANES == 0
    fc = D_FF_EXPERT // 2

    pos = jnp.arange(s, dtype=F32)
    inv = 1.0 / (ROPE_THETA ** (jnp.arange(0, MLA_ROPE, 2, dtype=F32) / MLA_ROPE))
    ang = pos[:, None] * inv[None, :]
    cos4 = jnp.tile(jnp.cos(ang), (1, 2 * MLA_HEADS))
    sin4 = jnp.tile(jnp.sin(ang), (1, 2 * MLA_HEADS))
    rb = rel_bias.astype(F32)
    kk = jnp.arange(ta)[:, None]
    qq = jnp.arange(ta)[None, :]
    tiles = []
    for off in (0, ta):
        dist = qq + off - kk
        bucket = _t5_bucket(jnp.maximum(dist, 0))
        bt = jnp.zeros((DIFF_HEADS, ta, ta), F32)
        for bk in range(NUM_BUCKETS):
            bt = bt + jnp.where((bucket == bk)[None], rb[bk][:, None, None], 0.0)
        tiles.append(jnp.where((dist >= 0)[None], bt * LOG2E, NEG_BIG))
    bias_tiles = jnp.stack(tiles, axis=1)
    assert ta + 1 >= MAX_DISTANCE
    cfar = rb[NUM_BUCKETS - 1] * LOG2E
    moe_wg, moe_wu, moe_wd = (w.astype(BF16) for w in (moe_w_gate, moe_w_up, moe_w_down))

    kr_w = w_in[:, :, _KR0:_KR0 + MLA_ROPE]
    win = jnp.concatenate([w_in[:, :, :_KR0 + MLA_ROPE], _rotate_half_cols(kr_w),
                           w_in[:, :, _KR0 + MLA_ROPE:]], axis=2).astype(BF16)
    nl = w_in.shape[0]
    uq = mla_w_uq.reshape(nl, MLA_Q_RANK, MLA_HEADS, MLA_QK)
    uq_r = uq[..., MLA_NOPE:]
    wuq = jnp.concatenate([uq[..., :MLA_NOPE].reshape(nl, MLA_Q_RANK, -1),
                           uq_r.reshape(nl, MLA_Q_RANK, -1),
                           _rotate_half_cols(uq_r).reshape(nl, MLA_Q_RANK, -1)], axis=2).astype(BF16)
    wukv = jnp.concatenate([mla_w_uk, mla_w_uv], axis=2).astype(BF16)
    qn, kvn = mla_q_norm[:, None, :], mla_kv_norm[:, None, :]
    wo = w_o.astype(BF16)
    ffn_wg, ffn_wu, ffn_wd = (w.astype(BF16) for w in (ffn_w_gate, ffn_w_up, ffn_w_down))
    rpad = jnp.pad(jnp.swapaxes(moe_router.astype(F32), 1, 2),
                   ((0, 0), (0, ROUTER_ROWS - N_EXPERTS), (0, 0)))
    rhi = rpad.astype(BF16)
    rhl = jnp.concatenate([rhi, (rpad - rhi.astype(F32)).astype(BF16)], axis=1)

    xc = x
    for l in range(DEPTH):
        lambda_init = 0.8 - 0.6 * math.exp(-0.3 * l)
        qT, k, vT, dqT, dk, dvT = _proj(xc, l, win, qn, kvn, wuq, wukv, cos4, sin4, tm_proj)
        a_mla = _mla_attn(qT, k, vT, ta, MLA_HP).reshape(t, MLA_W)
        a_diff = _diff_attn(cfar, dqT, dk, dvT, bias_tiles, diff_lambda[l].astype(F32),
                            diff_norm[l].astype(F32)[:, None], ta, DIFF_HP,
                            lambda_init).reshape(t, DIFF_W)
        x2d = xc.reshape(t, d)
        g1, b1 = ln1_g[l][None, :], ln1_b[l][None, :]
        g2, b2 = ln2_g[l][None, :], ln2_b[l][None, :]
        i = l // 2
        if l % 2 == 0:
            x2 = _oproj_ffn(a_mla, a_diff, x2d, l, wo, g1, b1, i, ffn_wg, ffn_wu, ffn_wd,
                            g2, b2, tm_tok)
        else:
            x1, x1b, rec = _oproj_route(a_mla, a_diff, x2d, l, wo, g1, b1, rhl[i], rhi[i], tm_tok)
            x2 = _moe(x1, x1b, rec, i, moe_wg, moe_wu, moe_wd, g2, b2,
                      tm_c, tm_e, fc, unit, kc, wrows)
        xc = x2.reshape(b, s, d)
    return xc
```

```python
import functools
import math

import jax
import jax.numpy as jnp
from jax import lax
from jax.experimental import pallas as pl
from jax.experimental.pallas import tpu as pltpu

D_MODEL = 1024
DEPTH = 4
MLA_HEADS = 4
MLA_NOPE = 128
MLA_ROPE = 64
MLA_V = 128
MLA_Q_RANK = 384
MLA_KV_RANK = 256
ROPE_THETA = 10000.0
DIFF_HEADS = 4
DIFF_QK = 64
DIFF_V = 2 * DIFF_QK
NUM_BUCKETS = 32
MAX_DISTANCE = 128
D_FF_DENSE = 2816
N_EXPERTS = 8
D_FF_EXPERT = 3584
DN_ALPHA = (2 * DEPTH) ** 0.25

MLA_QK = MLA_NOPE + MLA_ROPE
MLA_W = MLA_HEADS * MLA_V
DIFF_W = DIFF_HEADS * DIFF_V
DIFF_VA = DIFF_V + 16
_CQ0, _CKV0 = 0, MLA_Q_RANK
_KR0 = _CKV0 + MLA_KV_RANK
_DQ0 = _KR0 + 2 * MLA_ROPE
_DK0 = _DQ0 + DIFF_W
_DV0 = _DK0 + DIFF_W
IN_COLS_W = _DV0 + DIFF_W

LANES = 128
VMEM_LIMIT = 56 * 1024 * 1024
NEG_BIG = -1e30
ROUTER_ROWS = 16
REC_ROWS = 8
LOG2E = math.log2(math.e)
ATT_BLK = 256
MLA_HP = 4
DIFF_HP = 2

BF16 = jnp.bfloat16
F32 = jnp.float32


def _dot(a, b):
    return jnp.dot(a, b, preferred_element_type=F32)


def _layer_norm_rows(y, g, b):
    mu = jnp.mean(y, axis=-1, keepdims=True)
    d = y - mu
    var = jnp.mean(d * d, axis=-1, keepdims=True)
    return d * lax.rsqrt(var + 1e-5) * g + b


def _rms_rows(y, g, eps):
    return y * lax.rsqrt(jnp.mean(y * y, axis=-1, keepdims=True) + eps) * g


def _proj_kernel(x_ref, win_ref, qn_ref, kvn_ref, wuq_ref, wukv_ref, cos_ref, sin_ref,
                 qT_ref, k_ref, vT_ref, dqT_ref, dk_ref, dvT_ref):
    x = x_ref[0].astype(BF16)
    h = _dot(x, win_ref[0])
    cqn = _rms_rows(h[:, _CQ0:_CQ0 + MLA_Q_RANK], qn_ref[0], 1e-6)
    ckvn = _rms_rows(h[:, _CKV0:_CKV0 + MLA_KV_RANK], kvn_ref[0], 1e-6)
    q = _dot(cqn.astype(BF16), wuq_ref[0]) * (MLA_QK ** -0.5 * LOG2E)
    kv = _dot(ckvn.astype(BF16), wukv_ref[0])
    cos4, sin4 = cos_ref[...], sin_ref[...]
    nr = MLA_HEADS * MLA_ROPE
    qr = q[:, MLA_W:MLA_W + nr] * cos4 + q[:, MLA_W + nr:] * sin4
    kr = (h[:, _KR0:_KR0 + MLA_ROPE] * cos4[:, :MLA_ROPE]
          + h[:, _KR0 + MLA_ROPE:_KR0 + 2 * MLA_ROPE] * sin4[:, :MLA_ROPE])
    qrT = qr.T
    for hh in range(MLA_HEADS):
        c0 = hh * LANES
        qnT = q[:, c0:c0 + MLA_NOPE].T
        qT_ref[0, hh] = jnp.concatenate(
            [qnT, qrT[hh * MLA_ROPE:(hh + 1) * MLA_ROPE]], axis=0).astype(BF16)
        k_ref[0, hh] = jnp.concatenate([kv[:, c0:c0 + MLA_NOPE], kr], axis=1).astype(BF16)
        vT_ref[0, hh] = kv[:, MLA_W + c0:MLA_W + c0 + MLA_V].T.astype(BF16)
        dqT = (h[:, _DQ0 + c0:_DQ0 + c0 + LANES] * (DIFF_QK ** -0.5 * LOG2E)).T.astype(BF16)
        zeros = jnp.zeros((DIFF_QK, x.shape[0]), BF16)
        dqT_ref[0, hh, 0] = jnp.concatenate([dqT[:DIFF_QK], zeros], axis=0)
        dqT_ref[0, hh, 1] = jnp.concatenate([zeros, dqT[DIFF_QK:]], axis=0)
        dk_ref[0, hh] = h[:, _DK0 + c0:_DK0 + c0 + LANES].astype(BF16)
        dvT_ref[0, hh] = jnp.concatenate(
            [h[:, _DV0 + c0:_DV0 + c0 + DIFF_V].T, jnp.ones((DIFF_VA - DIFF_V, x.shape[0]), F32)],
            axis=0).astype(BF16)


def _proj(x, l, win, qn, kvn, wuq, wukv, cos4, sin4, tm):
    b, s, d = x.shape
    hd = MLA_HEADS
    layer = lambda a: pl.BlockSpec((1,) + a.shape[1:], lambda bi, i: (l,) + (0,) * (a.ndim - 1))
    tmaj = lambda w: pl.BlockSpec((1, hd, tm, w), lambda bi, i: (bi, 0, i, 0))
    fmaj = lambda w: pl.BlockSpec((1, hd, w, tm), lambda bi, i: (bi, 0, 0, i))
    sds = jax.ShapeDtypeStruct
    return pl.pallas_call(
        _proj_kernel,
        grid=(b, s // tm),
        in_specs=[
            pl.BlockSpec((1, tm, d), lambda bi, i: (bi, i, 0)),
            layer(win), layer(qn), layer(kvn), layer(wuq), layer(wukv),
            pl.BlockSpec((tm, cos4.shape[1]), lambda bi, i: (i, 0)),
            pl.BlockSpec((tm, sin4.shape[1]), lambda bi, i: (i, 0)),
        ],
        out_specs=[fmaj(MLA_QK), tmaj(MLA_QK), fmaj(MLA_V),
                   pl.BlockSpec((1, hd, 2, LANES, tm), lambda bi, i: (bi, 0, 0, 0, i)),
                   tmaj(LANES), fmaj(DIFF_VA)],
        out_shape=[
            sds((b, hd, MLA_QK, s), BF16), sds((b, hd, s, MLA_QK), BF16), sds((b, hd, MLA_V, s), BF16),
            sds((b, hd, 2, LANES, s), BF16), sds((b, hd, s, LANES), BF16), sds((b, hd, DIFF_VA, s), BF16),
        ],
        compiler_params=pltpu.CompilerParams(
            dimension_semantics=("arbitrary", "arbitrary"), vmem_limit_bytes=VMEM_LIMIT),
        name="proj",
    )(x, win, qn, kvn, wuq, wukv, cos4, sin4)


def _softmax_step(s, vj, m_ref, l_ref, acc_ref, shift=None):
    m_prev = m_ref[...]
    smax = jnp.max(s, axis=0, keepdims=True)
    if shift is not None:
        smax = smax + shift
    m_new = jnp.maximum(m_prev, smax)
    a = jnp.exp2(m_prev - m_new)
    p = jnp.exp2(s - (m_new if shift is None else m_new - shift))
    if l_ref is not None:
        l_ref[...] = a * l_ref[...] + jnp.sum(p, axis=0, keepdims=True)
    acc_ref[...] = a * acc_ref[...] + _dot(vj, p.astype(BF16))
    m_ref[...] = m_new


def _chain_scratch(n, dv):
    return [pltpu.VMEM((1, ATT_BLK), F32)] * (2 * n) + [pltpu.VMEM((dv, ATT_BLK), F32)] * n


def _init_chains(state):
    n = len(state) // 3
    m_refs, l_refs, acc_refs = state[:n], state[n:2 * n], state[2 * n:]
    for m_ref, l_ref, acc_ref in zip(m_refs, l_refs, acc_refs):
        m_ref[...] = jnp.full(m_ref.shape, NEG_BIG, F32)
        l_ref[...] = jnp.zeros(l_ref.shape, F32)
        acc_ref[...] = jnp.zeros(acc_ref.shape, F32)
    return m_refs, l_refs, acc_refs


def _pipelined_tiles(qi, logits, next_logits, softmax_pv):
    a, b, c = 0, 1, 2

    @pl.when(qi == 0)
    def _():
        logits(0, a, True)
        next_logits(c)
        softmax_pv(0, a, "diag")

    @pl.when(qi == 1)
    def _():
        logits(1, a, True)
        softmax_pv(0, c, "near")
        next_logits(c)
        softmax_pv(1, a, "diag")

    @pl.when(qi >= 2)
    def _():
        logits(1, a)
        softmax_pv(0, c, "far")
        nfar = qi - 2

        def body(i, carry):
            j = 2 * i + 1
            logits(j + 1, b)
            softmax_pv(j, a, "far")
            logits(j + 2, a)
            softmax_pv(j + 1, b, "far")
            return carry

        lax.fori_loop(0, lax.shift_right_logical(nfar, 1), body, 0)
        odd = lax.rem(nfar, 2) == 1

        @pl.when(jnp.logical_not(odd))
        def _():
            logits(qi, b, True)
            softmax_pv(qi - 1, a, "near")
            next_logits(c)
            softmax_pv(qi, b, "diag")

        @pl.when(odd)
        def _():
            logits(qi - 1, b)
            softmax_pv(qi - 2, a, "far")
            logits(qi, a, True)
            softmax_pv(qi - 1, b, "near")
            next_logits(c)
            softmax_pv(qi, a, "diag")


def _mla_attn_kernel(qT_ref, qTn_ref, k_ref, vT_ref, o_ref, sa_ref, sb_ref, sc_ref, *state,
                     ta, hp):
    qi = pl.program_id(2)
    m_refs, l_refs, acc_refs = _init_chains(state)
    s_refs = (sa_ref, sb_ref, sc_ref)

    nb = ta // ATT_BLK
    blk = ATT_BLK

    def logits(j, buf, diagonal=False):
        off = pl.multiple_of(j * ta, ta)
        for h in range(hp):
            if not diagonal:
                s_refs[buf][h] = _dot(k_ref[0, h, pl.ds(off, ta), :], qT_ref[0, h])
                continue
            for qs in range(nb):
                nk, q0 = (qs + 1) * blk, qs * blk
                s_refs[buf][h, :nk, q0:q0 + blk] = _dot(k_ref[0, h, pl.ds(off, nk), :],
                                                        qT_ref[0, h, :, q0:q0 + blk])

    def next_logits(buf):
        for h in range(hp):
            s_refs[buf][h] = _dot(k_ref[0, h, :ta, :], qTn_ref[0, h])

    def softmax_pv(j, buf, kind):
        off = pl.multiple_of(j * ta, ta)
        diagonal = kind == "diag"
        for h in range(hp):
            for qs in range(nb):
                nk = (qs + 1) * blk if diagonal else ta
                s = s_refs[buf][h, :nk, qs * blk:(qs + 1) * blk]
                if diagonal:
                    kpos = lax.broadcasted_iota(jnp.int32, (nk, blk), 0)
                    qpos = lax.broadcasted_iota(jnp.int32, (nk, blk), 1) + qs * blk
                    s = jnp.where(kpos <= qpos, s, NEG_BIG)
                c = h * nb + qs
                _softmax_step(s, vT_ref[0, h, :, pl.ds(off, nk)], m_refs[c], l_refs[c], acc_refs[c])

    _pipelined_tiles(qi, logits, next_logits, softmax_pv)
    for h in range(hp):
        for qs in range(nb):
            c = h * nb + qs
            out = acc_refs[c][...] * (1.0 / l_refs[c][...])
            o_ref[0, qs * blk:(qs + 1) * blk, h * MLA_V:(h + 1) * MLA_V] = out.T.astype(BF16)


def _mla_attn(qT, k, vT, ta, hp):
    b, hd, _, s = qT.shape
    nq = s // ta
    return pl.pallas_call(
        functools.partial(_mla_attn_kernel, ta=ta, hp=hp),
        grid=(b, hd // hp, nq),
        in_specs=[
            pl.BlockSpec((1, hp, MLA_QK, ta), lambda bi, h, i: (bi, h, 0, i)),
            pl.BlockSpec((1, hp, MLA_QK, ta), lambda bi, h, i: (bi, h, 0, jnp.minimum(i + 1, nq - 1))),
            pl.BlockSpec((1, hp, s, MLA_QK), lambda bi, h, i: (bi, h, 0, 0)),
            pl.BlockSpec((1, hp, MLA_V, s), lambda bi, h, i: (bi, h, 0, 0)),
        ],
        out_specs=pl.BlockSpec((1, ta, hp * MLA_V), lambda bi, h, i: (bi, i, h)),
        out_shape=jax.ShapeDtypeStruct((b, s, hd * MLA_V), BF16),
        scratch_shapes=([pltpu.VMEM((hp, ta, ta), F32)] * 3
                        + _chain_scratch(hp * ta // ATT_BLK, MLA_V)),
        compiler_params=pltpu.CompilerParams(
            dimension_semantics=("arbitrary",) * 3, vmem_limit_bytes=VMEM_LIMIT),
        name="mla_attn",
    )(qT, qT, k, vT)


def _diff_attn_kernel(cfar_ref, qT_ref, qTn_ref, k_ref, vT_ref, bias_ref, lamp_ref, g_ref, *rest,
                      ta, hp, lambda_init, ncast):
    cast_in, o_ref, cast_out = rest[:ncast], rest[ncast], rest[ncast + 1:2 * ncast + 1]
    s_refs = rest[2 * ncast + 1:2 * ncast + 4]
    state = rest[2 * ncast + 4:]
    hg = pl.program_id(1)
    qi = pl.program_id(2)
    for src_ref, dst_ref in zip(cast_in, cast_out):
        dst_ref[...] = src_ref[...].astype(BF16)
    m_refs, l_refs, acc_refs = _init_chains(state)

    nb = ta // ATT_BLK
    blk = ATT_BLK

    def logits(j, buf, diagonal=False):
        off = pl.multiple_of(j * ta, ta)
        for h in range(hp):
            for c in range(2):
                if not diagonal:
                    s_refs[buf][2 * h + c] = _dot(k_ref[0, h, pl.ds(off, ta), :],
                                                  qT_ref[0, h, c])
                    continue
                for qb in range(nb):
                    nk, q0 = (qb + 1) * blk, qb * blk
                    s_refs[buf][2 * h + c, :nk, q0:q0 + blk] = _dot(
                        k_ref[0, h, pl.ds(off, nk), :], qT_ref[0, h, c, :, q0:q0 + blk])

    def next_logits(buf):
        for h in range(hp):
            for c in range(2):
                s_refs[buf][2 * h + c] = _dot(k_ref[0, h, :ta, :], qTn_ref[0, h, c])

    def softmax_pv(j, buf, kind):
        off = pl.multiple_of(j * ta, ta)
        diagonal = kind == "diag"
        for h in range(hp):
            for qb in range(nb):
                nk = (qb + 1) * blk if diagonal else ta
                vj = vT_ref[0, h, :, pl.ds(off, nk)]
                for c in range(2):
                    s = s_refs[buf][2 * h + c, :nk, qb * blk:(qb + 1) * blk]
                    i = (2 * h + c) * nb + qb
                    if kind == "far" or (kind == "near" and qb * blk + 1 >= MAX_DISTANCE):
                        _softmax_step(s, vj, m_refs[i], None, acc_refs[i],
                                      shift=cfar_ref[hg * hp + h])
                    else:
                        bias = bias_ref[h, 0 if diagonal else 1, :nk, qb * blk:(qb + 1) * blk]
                        _softmax_step(s + bias, vj, m_refs[i], None, acc_refs[i])

    _pipelined_tiles(qi, logits, next_logits, softmax_pv)

    lp = lamp_ref[...]
    lam = (jnp.exp(jnp.sum(lp[0:1] * lp[1:2], axis=-1, keepdims=True))
           - jnp.exp(jnp.sum(lp[2:3] * lp[3:4], axis=-1, keepdims=True)) + lambda_init)
    for h in range(hp):
        for qb in range(nb):
            a0 = acc_refs[(2 * h) * nb + qb][...]
            a1 = acc_refs[(2 * h + 1) * nb + qb][...]
            out = (a0[:DIFF_V] * (1.0 / a0[DIFF_V:DIFF_V + 1])
                   - lam * (a1[:DIFF_V] * (1.0 / a1[DIFF_V:DIFF_V + 1])))
            ms = jnp.mean(out * out, axis=0, keepdims=True)
            out = out * lax.rsqrt(ms + 1e-5) * g_ref[...] * (1.0 - lambda_init)
            o_ref[0, qb * blk:(qb + 1) * blk, h * DIFF_V:(h + 1) * DIFF_V] = out.T.astype(BF16)


def _cast_rows(rows_total, nsteps):
    rows = rows_total // nsteps
    return rows if rows * nsteps == rows_total and rows % 16 == 0 else 0


def _diff_attn(cfar, qT, k, vT, bias_tiles, lam_params, g_col, ta, hp, lambda_init,
               cast=(), cast_part=(0, 1)):
    b, hd, _, _, s = qT.shape
    nq, nhg = s // ta, hd // hp
    nsteps = b * nhg * nq
    part, nparts = cast_part
    crows = [_cast_rows(a.shape[0] // nparts, nsteps) for a in cast]
    assert all(crows), "cast tables must split into whole tiles per grid step"
    step = lambda bi, h, i: (bi * nhg + h) * nq + i
    grid_spec = pltpu.PrefetchScalarGridSpec(
        num_scalar_prefetch=1,
        grid=(b, nhg, nq),
        in_specs=[
            pl.BlockSpec((1, hp, 2, LANES, ta), lambda bi, h, i, cf: (bi, h, 0, 0, i)),
            pl.BlockSpec((1, hp, 2, LANES, ta),
                         lambda bi, h, i, cf: (bi, h, 0, 0, jnp.minimum(i + 1, nq - 1))),
            pl.BlockSpec((1, hp, s, LANES), lambda bi, h, i, cf: (bi, h, 0, 0)),
            pl.BlockSpec((1, hp, DIFF_VA, s), lambda bi, h, i, cf: (bi, h, 0, 0)),
            pl.BlockSpec((hp, 2, ta, ta), lambda bi, h, i, cf: (h, 0, 0, 0)),
            pl.BlockSpec(lam_params.shape, lambda bi, h, i, cf: (0, 0)),
            pl.BlockSpec(g_col.shape, lambda bi, h, i, cf: (0, 0)),
        ] + [pl.BlockSpec((r, a.shape[1]), lambda bi, h, i, cf: (part * nsteps + step(bi, h, i), 0))
             for a, r in zip(cast, crows)],
        out_specs=[pl.BlockSpec((1, ta, hp * DIFF_V), lambda bi, h, i, cf: (bi, i, h))]
        + [pl.BlockSpec((r, a.shape[1]), lambda bi, h, i, cf: (step(bi, h, i), 0))
           for a, r in zip(cast, crows)],
        scratch_shapes=([pltpu.VMEM((2 * hp, ta, ta), F32)] * 3
                        + _chain_scratch(2 * hp * ta // ATT_BLK, DIFF_VA)),
    )
    res = pl.pallas_call(
        functools.partial(_diff_attn_kernel, ta=ta, hp=hp, lambda_init=lambda_init,
                          ncast=len(cast)),
        grid_spec=grid_spec,
        out_shape=[jax.ShapeDtypeStruct((b, s, hd * DIFF_V), BF16)]
        + [jax.ShapeDtypeStruct((a.shape[0] // nparts, a.shape[1]), BF16) for a in cast],
        compiler_params=pltpu.CompilerParams(
            dimension_semantics=("arbitrary",) * 3, vmem_limit_bytes=VMEM_LIMIT),
        name="diff_attn",
    )(cfar, qT, qT, k, vT, bias_tiles, lam_params, g_col, *cast)
    return res[0], res[1:]


def _split_bf16(v):
    hi = v.astype(BF16)
    return hi, (v - hi.astype(F32)).astype(BF16)


def _oproj_route_kernel(am_ref, ad_ref, x_ref, wo_ref, g_ref, b_ref, rhl_ref, rh_ref,
                        o_ref, ob_ref, r_ref):
    mix = _dot(am_ref[...], wo_ref[0, :MLA_W, :]) + _dot(ad_ref[...], wo_ref[0, MLA_W:, :])
    y = _layer_norm_rows(DN_ALPHA * x_ref[...] + mix, g_ref[...], b_ref[...])
    yhi, ylo = _split_bf16(y)
    nt = (((1,), (1,)), ((), ()))
    both = lax.dot_general(rhl_ref[...], yhi, nt, preferred_element_type=F32)
    logits = (both[:ROUTER_ROWS] + both[ROUTER_ROWS:]
              + lax.dot_general(rh_ref[...], ylo, nt, preferred_element_type=F32))
    row = lax.broadcasted_iota(jnp.int32, logits.shape, 0)
    logits = jnp.where(row < N_EXPERTS, logits, NEG_BIG)
    m1 = jnp.max(logits, axis=0, keepdims=True)
    i1 = jnp.min(jnp.where(logits == m1, row, ROUTER_ROWS), axis=0, keepdims=True)
    rest_l = jnp.where(row == i1, NEG_BIG, logits)
    m2 = jnp.max(rest_l, axis=0, keepdims=True)
    i2 = jnp.min(jnp.where(rest_l == m2, row, ROUTER_ROWS), axis=0, keepdims=True)
    e2 = jnp.exp(m2 - m1)
    g1 = 1.0 / (1.0 + e2)
    g2 = e2 * g1
    row8 = lax.broadcasted_iota(jnp.int32, r_ref.shape, 0)
    rec = jnp.where(row8 == 0, i1.astype(F32),
                    jnp.where(row8 == 1, i2.astype(F32),
                              jnp.where(row8 == 2, g1, jnp.where(row8 == 3, g2, 0.0))))
    o_ref[...] = y
    ob_ref[...] = yhi
    r_ref[...] = rec


def _oproj_route(a_mla, a_diff, x2d, l, wo, g, b, rhl, rh, tm):
    t, d = x2d.shape
    row = lambda w: pl.BlockSpec((tm, w), lambda i: (i, 0))
    full = lambda a: pl.BlockSpec(a.shape, lambda i: (0,) * a.ndim)
    return pl.pallas_call(
        _oproj_route_kernel,
        grid=(t // tm,),
        in_specs=[row(MLA_W), row(DIFF_W), row(d),
                  pl.BlockSpec((1,) + wo.shape[1:], lambda i: (l, 0, 0)),
                  full(g), full(b), full(rhl), full(rh)],
        out_specs=[row(d), row(d), pl.BlockSpec((REC_ROWS, tm), lambda i: (0, i))],
        out_shape=[jax.ShapeDtypeStruct((t, d), F32), jax.ShapeDtypeStruct((t, d), BF16),
                   jax.ShapeDtypeStruct((REC_ROWS, t), F32)],
        compiler_params=pltpu.CompilerParams(
            dimension_semantics=("arbitrary",), vmem_limit_bytes=VMEM_LIMIT),
        name="oproj_route",
    )(a_mla, a_diff, x2d, wo, g, b, rhl, rh)


def _oproj_ffn_kernel(am_ref, ad_ref, x_ref, wo_ref, g1_ref, b1_ref, wg_ref, wu_ref, wd_ref,
                      g2_ref, b2_ref, o_ref):
    mix = _dot(am_ref[...], wo_ref[0, :MLA_W, :]) + _dot(ad_ref[...], wo_ref[0, MLA_W:, :])
    x1 = _layer_norm_rows(DN_ALPHA * x_ref[...] + mix, g1_ref[...], b1_ref[...])
    xb = x1.astype(BF16)
    gate = _dot(xb, wg_ref[0])
    up = _dot(xb, wu_ref[0])
    hid = (gate * jax.nn.sigmoid(gate) * up).astype(BF16)
    f = _dot(hid, wd_ref[0])
    o_ref[...] = _layer_norm_rows(DN_ALPHA * x1 + f, g2_ref[...], b2_ref[...])


def _oproj_ffn(a_mla, a_diff, x2d, l, wo, g1, b1, li, wg, wu, wd, g2, b2, tm):
    t, d = x2d.shape
    row = lambda w: pl.BlockSpec((tm, w), lambda i: (i, 0))
    layer = lambda a, k: pl.BlockSpec((1,) + a.shape[1:], lambda i: (k,) + (0,) * (a.ndim - 1),
                                      pipeline_mode=pl.Buffered(1))
    vec = lambda a: pl.BlockSpec(a.shape, lambda i: (0, 0))
    return pl.pallas_call(
        _oproj_ffn_kernel,
        grid=(t // tm,),
        in_specs=[row(MLA_W), row(DIFF_W), row(d), layer(wo, l), vec(g1), vec(b1),
                  layer(wg, li), layer(wu, li), layer(wd, li), vec(g2), vec(b2)],
        out_specs=row(d),
        out_shape=jax.ShapeDtypeStruct((t, d), F32),
        compiler_params=pltpu.CompilerParams(
            dimension_semantics=("arbitrary",), vmem_limit_bytes=VMEM_LIMIT),
        name="oproj_ffn",
    )(a_mla, a_diff, x2d, wo, g1, b1, wg, wu, wd, g2, b2)


def _dispatch_kernel(te_ref, uoff_ref, unc_ref, x_ref, post_ref, o_ref, acc_ref, *, tm, unit, kc):
    i = pl.program_id(0)
    e = te_ref[i]
    t = x_ref.shape[0]
    upt = tm // unit
    for r in range(upt):
        u = i * upt + r
        n = unc_ref[u]
        start0 = uoff_ref[u]
        row_pos = u * unit + lax.broadcasted_iota(jnp.int32, (unit, 1), 0)

        def chunk(c):
            nominal = start0 + c * kc
            off = pl.multiple_of(jnp.minimum(nominal, t - kc), LANES)
            dest = post_ref[pl.ds(e, 1), pl.ds(off, kc)]
            tok = off + lax.broadcasted_iota(jnp.int32, (1, kc), 1)
            dest = jnp.where(tok >= nominal, dest, -1)
            sel = jnp.where(dest == row_pos, 1.0, 0.0).astype(BF16)
            return _dot(sel, x_ref[pl.ds(off, kc), :])

        @pl.when(n > 0)
        def _():
            acc_ref[...] = chunk(0)

        @pl.when(n == 0)
        def _():
            acc_ref[...] = jnp.zeros(acc_ref.shape, F32)

        def more(c, carry):
            acc_ref[...] += chunk(c)
            return carry

        lax.fori_loop(1, n, more, 0)
        o_ref[r * unit:(r + 1) * unit, :] = acc_ref[...].astype(BF16)


def _dispatch(tile_e, uoff, unc, xb, post, nt, tm, unit, kc):
    t, d = xb.shape
    const = lambda shape: pl.BlockSpec(shape, lambda i, *_: (0,) * len(shape),
                                       pipeline_mode=pl.Buffered(1))
    grid_spec = pltpu.PrefetchScalarGridSpec(
        num_scalar_prefetch=3,
        grid=(nt,),
        in_specs=[const(xb.shape), const(post.shape)],
        out_specs=pl.BlockSpec((tm, d), lambda i, *_: (i, 0)),
        scratch_shapes=[pltpu.VMEM((unit, d), F32)],
    )
    return pl.pallas_call(
        functools.partial(_dispatch_kernel, tm=tm, unit=unit, kc=kc),
        grid_spec=grid_spec,
        out_shape=jax.ShapeDtypeStruct((nt * tm, d), BF16),
        compiler_params=pltpu.CompilerParams(
            dimension_semantics=("arbitrary",), vmem_limit_bytes=VMEM_LIMIT),
        name="dispatch",
    )(tile_e, uoff, unc, xb, post)


def _gffn_kernel(te_ref, tv_ref, x_ref, wg_ref, wu_ref, wd_ref, o_ref, acc_ref, *, nf):
    i = pl.program_id(0)
    f = pl.program_id(1)
    status = tv_ref[i]
    tm = x_ref.shape[0]
    half = tm // 2

    def swiglu_rows(rows):
        xb = x_ref[:rows, :]
        gate = _dot(xb, wg_ref[0, 0])
        up = _dot(xb, wu_ref[0, 0])
        hid = (gate * jax.nn.sigmoid(gate) * up).astype(BF16)
        part = _dot(hid, wd_ref[0, 0])

        @pl.when(f == 0)
        def _():
            acc_ref[:rows, :] = part

        @pl.when(f > 0)
        def _():
            acc_ref[:rows, :] += part

    @pl.when(status == 2)
    def _():
        swiglu_rows(tm)

    @pl.when(status == 1)
    def _():
        swiglu_rows(half)

    @pl.when(jnp.logical_and(f == nf - 1, status == 2))
    def _():
        o_ref[...] = acc_ref[...].astype(BF16)

    @pl.when(jnp.logical_and(f == nf - 1, status == 1))
    def _():
        o_ref[:half, :] = acc_ref[:half, :].astype(BF16)
        o_ref[half:, :] = jnp.zeros((tm - half, o_ref.shape[1]), BF16)

    @pl.when(jnp.logical_and(f == nf - 1, status == 0))
    def _():
        o_ref[...] = jnp.zeros(o_ref.shape, BF16)


def _gffn(tile_e, tile_v, xs, li, wg, wu, wd, tm, fc):
    p, d = xs.shape
    nt = p // tm
    nf = wg.shape[3] // fc
    fidx = lambda i, f, te, tv: jnp.where(tv[i] > 0, f, nf - 1)
    grid_spec = pltpu.PrefetchScalarGridSpec(
        num_scalar_prefetch=2,
        grid=(nt, nf),
        in_specs=[
            pl.BlockSpec((tm, d), lambda i, f, te, tv: (i, 0)),
            pl.BlockSpec((1, 1, d, fc), lambda i, f, te, tv: (li, te[i], 0, fidx(i, f, te, tv))),
            pl.BlockSpec((1, 1, d, fc), lambda i, f, te, tv: (li, te[i], 0, fidx(i, f, te, tv))),
            pl.BlockSpec((1, 1, fc, d), lambda i, f, te, tv: (li, te[i], fidx(i, f, te, tv), 0)),
        ],
        out_specs=pl.BlockSpec((tm, d), lambda i, f, te, tv: (i, 0)),
        scratch_shapes=[pltpu.VMEM((tm, d), F32)],
    )
    return pl.pallas_call(
        functools.partial(_gffn_kernel, nf=nf),
        grid_spec=grid_spec,
        out_shape=jax.ShapeDtypeStruct((p, d), BF16),
        compiler_params=pltpu.CompilerParams(
            dimension_semantics=("arbitrary", "arbitrary"), vmem_limit_bytes=VMEM_LIMIT),
        name="grouped_ffn",
    )(tile_e, tile_v, xs, wg, wu, wd)


def _combine_kernel(ral_ref, rlo_ref, rhi_ref, x_ref, rec_ref, g_ref, b_ref, ys_ref, o_ref,
                    win_ref, xwin_ref, acc_ref, sems, xsem, *, w):
    j = pl.program_id(0)
    nj = pl.num_programs(0)
    slot = lax.rem(j, 2)
    p_rows = ys_ref.shape[0]

    def win_copy(jj, e, sl):
        r = pl.multiple_of(ral_ref[jj * N_EXPERTS + e], 16)
        return pltpu.make_async_copy(ys_ref.at[pl.ds(r, w)], win_ref.at[sl, pl.ds(e * w, w)],
                                     sems.at[sl, e])

    @pl.when(j == 0)
    def _():
        for e in range(N_EXPERTS):
            win_copy(0, e, 0).start()

    @pl.when(j + 1 < nj)
    def _():
        for e in range(N_EXPERTS):
            win_copy(j + 1, e, 1 - slot).start()

    rec = rec_ref[...]
    p1, p2, g1, g2 = rec[:, 0:1], rec[:, 1:2], rec[:, 2:3], rec[:, 3:4]

    def sel(base, lo, hi):
        rows = base + lax.broadcasted_iota(jnp.int32, (1, w), 1)
        rp = jnp.where((rows >= lo) & (rows < hi), rows, -1).astype(F32)
        return (jnp.where(p1 == rp, g1, 0.0) + jnp.where(p2 == rp, g2, 0.0)).astype(BF16)

    lane = lax.broadcasted_iota(jnp.int32, (1, N_EXPERTS * w), 1)
    rp = jnp.full((1, N_EXPERTS * w), -1, jnp.int32)
    for e in range(N_EXPERTS):
        win_copy(j, e, slot).wait()
        k = j * N_EXPERTS + e
        rows = ral_ref[k] + lane - e * w
        mine = ((lane >= e * w) & (lane < (e + 1) * w)
                & (rows >= rlo_ref[k]) & (rows < rhi_ref[k]))
        rp = jnp.where(mine, rows, rp)
    rp = rp.astype(F32)
    sel_all = (jnp.where(p1 == rp, g1, 0.0) + jnp.where(p2 == rp, g2, 0.0)).astype(BF16)
    acc_ref[...] = _dot(sel_all, win_ref[slot])

    for e in range(N_EXPERTS):
        k = j * N_EXPERTS + e
        nwin = (rhi_ref[k] - ral_ref[k] + (w - 1)) // w

        def extra(wi, carry, k=k):
            nominal = ral_ref[k] + wi * w
            base = pl.multiple_of(jnp.minimum(nominal, p_rows - w), 16)
            cp = pltpu.make_async_copy(ys_ref.at[pl.ds(base, w)], xwin_ref, xsem)
            cp.start()
            cp.wait()
            acc_ref[...] += _dot(sel(base, nominal, rhi_ref[k]), xwin_ref[...])
            return carry

        lax.fori_loop(1, nwin, extra, 0)

    o_ref[...] = _layer_norm_rows(DN_ALPHA * x_ref[...] + acc_ref[...], g_ref[...], b_ref[...])


def _combine(ral, rlo, rhi, x2d, rec2, g, b, ys, tm, w):
    t, d = x2d.shape
    row = lambda c: pl.BlockSpec((tm, c), lambda i, *_: (i, 0))
    full = lambda shape: pl.BlockSpec(shape, lambda i, *_: (0,) * len(shape))
    grid_spec = pltpu.PrefetchScalarGridSpec(
        num_scalar_prefetch=3,
        grid=(t // tm,),
        in_specs=[row(d), row(rec2.shape[1]), full(g.shape), full(b.shape),
                  pl.BlockSpec(memory_space=pl.ANY)],
        out_specs=row(d),
        scratch_shapes=[pltpu.VMEM((2, N_EXPERTS * w, d), BF16), pltpu.VMEM((w, d), BF16),
                        pltpu.VMEM((tm, d), F32), pltpu.SemaphoreType.DMA((2, N_EXPERTS)),
                        pltpu.SemaphoreType.DMA(())],
    )
    return pl.pallas_call(
        functools.partial(_combine_kernel, w=w),
        grid_spec=grid_spec,
        out_shape=jax.ShapeDtypeStruct((t, d), F32),
        compiler_params=pltpu.CompilerParams(
            dimension_semantics=("arbitrary",), vmem_limit_bytes=VMEM_LIMIT),
        name="combine",
    )(ral, rlo, rhi, x2d, rec2, g, b, ys)


def _route_plan(rec, nt, tm_e, unit, kc, tm_c, w):
    t = rec.shape[1]
    i32 = jnp.int32
    ar = jnp.arange(N_EXPERTS, dtype=i32)
    m1 = (rec[0].astype(i32)[:, None] == ar[None, :]).astype(i32)
    m2 = (rec[1].astype(i32)[:, None] == ar[None, :]).astype(i32)
    m = m1 + m2
    cinc = jnp.cumsum(m, axis=0)
    counts = cinc[-1]
    padded = ((counts + tm_e - 1) // tm_e) * tm_e
    ends = jnp.cumsum(padded)
    starts = ends - padded
    posmat = starts[None, :] + (cinc - m)
    pos1 = jnp.sum(m1 * posmat, axis=1)
    pos2 = jnp.sum(m2 * posmat, axis=1)
    post = jnp.where(m.T > 0, posmat.T, -1)
    rec2 = jnp.stack([pos1.astype(F32), pos2.astype(F32), rec[2], rec[3]]
                     + [jnp.zeros((t,), F32)] * 4, axis=1)
    tile_start = jnp.arange(nt, dtype=i32) * tm_e
    tile_e = jnp.sum((tile_start[:, None] >= ends[None, :]).astype(i32), axis=1)
    tile_v = tile_e < N_EXPERTS
    last_e = jnp.max(jnp.where(tile_v, tile_e, 0))
    tile_e = jnp.where(tile_v, tile_e, last_e)
    upt = tm_e // unit
    nu = nt * upt
    oh = (jnp.repeat(tile_e, upt)[:, None] == ar[None, :]).astype(i32)
    q0 = jnp.arange(nu, dtype=i32) * unit - jnp.sum(oh * starts[None, :], axis=1)
    q1 = jnp.minimum(q0 + unit, jnp.sum(oh * counts[None, :], axis=1))
    has = jnp.repeat(tile_v, upt) & (q1 > q0)
    cb = jnp.concatenate([jnp.zeros((1, N_EXPERTS), i32), cinc[LANES - 1::LANES]], axis=0)
    cbu = jnp.sum(oh[:, None, :] * cb[None, :, :], axis=2)
    jlo = jnp.sum((cbu[:, 1:] <= q0[:, None]).astype(i32), axis=1)
    jend = jnp.sum((cbu[:, :-1] < q1[:, None]).astype(i32), axis=1)
    uoff = jnp.where(has, jlo * LANES, 0)
    unc = jnp.where(has, (jend * LANES - uoff + kc - 1) // kc, 0)
    cbt = jnp.concatenate([jnp.zeros((1, N_EXPERTS), i32), cinc[tm_c - 1::tm_c]], axis=0)
    r_lo = starts[None, :] + cbt[:-1]
    r_hi = starts[None, :] + cbt[1:]
    ral = jnp.minimum((r_lo // 16) * 16, nt * tm_e - w)
    oh_t = (tile_e[:, None] == ar[None, :]).astype(i32)
    tile_rows = (jnp.sum(oh_t * counts[None, :], axis=1)
                 - (tile_start - jnp.sum(oh_t * starts[None, :], axis=1)))
    tile_status = jnp.where(tile_v, jnp.where(tile_rows > tm_e // 2, 2, 1), 0)
    return (rec2, post, tile_e, tile_status.astype(i32), uoff, unc,
            ral.reshape(-1), r_lo.reshape(-1), r_hi.reshape(-1))


def _moe(x2d, xb, rec, li, wg, wu, wd, g, b, tm_c, tm_e, fc, unit, kc, w):
    t, d = x2d.shape
    nt = (2 * t) // tm_e + N_EXPERTS
    rec2, post, tile_e, tile_v, uoff, unc, ral, rlo, rhi = _route_plan(
        rec, nt, tm_e, unit, kc, tm_c, w)
    xs = _dispatch(tile_e, uoff, unc, xb, post, nt, tm_e, unit, kc)
    ys = _gffn(tile_e, tile_v, xs, li, wg, wu, wd, tm_e, fc)
    return _combine(ral, rlo, rhi, x2d, rec2, g, b, ys, tm_c, w)


def _rotate_half_cols(w):
    half = w.shape[-1] // 2
    return jnp.concatenate([-w[..., half:], w[..., :half]], axis=-1)


def _t5_bucket(n):
    max_exact = NUM_BUCKETS // 2
    nf = jnp.maximum(n, 1).astype(F32)
    large = max_exact + (jnp.log(nf / max_exact) / math.log(MAX_DISTANCE / max_exact)
                         * (NUM_BUCKETS - max_exact)).astype(jnp.int32)
    large = jnp.minimum(large, NUM_BUCKETS - 1)
    return jnp.where(n < max_exact, n, large)


def _pick(n, prefs):
    for p in prefs:
        if n % p == 0:
            return p
    raise ValueError(f"no tile for extent {n}")


def kernel(x, rel_bias, w_in, mla_q_norm, mla_kv_norm, mla_w_uq, mla_w_uk, mla_w_uv, diff_lambda, diff_norm, w_o, ln1_g, ln1_b, ln2_g, ln2_b, ffn_w_gate, ffn_w_up, ffn_w_down, moe_router, moe_w_gate, moe_w_up, moe_w_down):
    b, s, d = x.shape
    t = b * s
    assert d == D_MODEL
    ta = _pick(s, (512, 256))
    assert ta % ATT_BLK == 0
    assert s >= 2 * ta or s == ta
    tm_proj = _pick(s, (512, 256, 128))
    tm_tok = _pick(t, (512, 256, 128))
    tm_e = _pick(2 * t, (512, 256))
    tm_c = _pick(t, (512, 256))
    unit, wrows = 128, 192
    kc = 768
    assert tm_e % unit == 0 and t >= kc and t % LANES == 0
    fc = D_FF_EXPERT // 2

    pos = jnp.arange(s, dtype=F32)
    inv = 1.0 / (ROPE_THETA ** (jnp.arange(0, MLA_ROPE, 2, dtype=F32) / MLA_ROPE))
    ang = pos[:, None] * inv[None, :]
    cos4 = jnp.tile(jnp.cos(ang), (1, 2 * MLA_HEADS))
    sin4 = jnp.tile(jnp.sin(ang), (1, 2 * MLA_HEADS))
    rb = rel_bias.astype(F32)
    kk = jnp.arange(ta)[:, None]
    qq = jnp.arange(ta)[None, :]
    tiles = []
    for off in (0, ta):
        dist = qq + off - kk
        bucket = _t5_bucket(jnp.maximum(dist, 0))
        bt = jnp.zeros((DIFF_HEADS, ta, ta), F32)
        for bk in range(NUM_BUCKETS):
            bt = bt + jnp.where((bucket == bk)[None], rb[bk][:, None, None], 0.0)
        tiles.append(jnp.where((dist >= 0)[None], bt * LOG2E, NEG_BIG))
    bias_tiles = jnp.stack(tiles, axis=1)
    assert ta + 1 >= MAX_DISTANCE
    cfar = rb[NUM_BUCKETS - 1] * LOG2E
    n_moe = moe_w_gate.shape[0]
    moe_tables = (moe_w_gate.reshape(-1, D_FF_EXPERT), moe_w_up.reshape(-1, D_FF_EXPERT),
                  moe_w_down.reshape(-1, d))

    kr_w = w_in[:, :, _KR0:_KR0 + MLA_ROPE]
    win = jnp.concatenate([w_in[:, :, :_KR0 + MLA_ROPE], _rotate_half_cols(kr_w),
                           w_in[:, :, _KR0 + MLA_ROPE:]], axis=2).astype(BF16)
    nl = w_in.shape[0]
    uq = mla_w_uq.reshape(nl, MLA_Q_RANK, MLA_HEADS, MLA_QK)
    uq_r = uq[..., MLA_NOPE:]
    wuq = jnp.concatenate([uq[..., :MLA_NOPE].reshape(nl, MLA_Q_RANK, -1),
                           uq_r.reshape(nl, MLA_Q_RANK, -1),
                           _rotate_half_cols(uq_r).reshape(nl, MLA_Q_RANK, -1)], axis=2).astype(BF16)
    wukv = jnp.concatenate([mla_w_uk, mla_w_uv], axis=2).astype(BF16)
    qn, kvn = mla_q_norm[:, None, :], mla_kv_norm[:, None, :]
    wo = w_o.astype(BF16)
    ffn_wg, ffn_wu, ffn_wd = (w.astype(BF16) for w in (ffn_w_gate, ffn_w_up, ffn_w_down))
    rpad = jnp.pad(jnp.swapaxes(moe_router.astype(F32), 1, 2),
                   ((0, 0), (0, ROUTER_ROWS - N_EXPERTS), (0, 0)))
    rhi = rpad.astype(BF16)
    rhl = jnp.concatenate([rhi, (rpad - rhi.astype(F32)).astype(BF16)], axis=1)

    xc = x
    for l in range(DEPTH):
        lambda_init = 0.8 - 0.6 * math.exp(-0.3 * l)
        qT, k, vT, dqT, dk, dvT = _proj(xc, l, win, qn, kvn, wuq, wukv, cos4, sin4, tm_proj)
        a_mla = _mla_attn(qT, k, vT, ta, MLA_HP).reshape(t, MLA_W)
        i = l // 2
        dense = l % 2 == 0
        a_diff, casts = _diff_attn(cfar, dqT, dk, dvT, bias_tiles, diff_lambda[l].astype(F32),
                                   diff_norm[l].astype(F32)[:, None], ta, DIFF_HP, lambda_init,
                                   cast=moe_tables if dense and i < n_moe else (),
                                   cast_part=(i, n_moe))
        a_diff = a_diff.reshape(t, DIFF_W)
        if casts:
            moe_wg = casts[0].reshape(1, N_EXPERTS, d, D_FF_EXPERT)
            moe_wu = casts[1].reshape(1, N_EXPERTS, d, D_FF_EXPERT)
            moe_wd = casts[2].reshape(1, N_EXPERTS, D_FF_EXPERT, d)
        x2d = xc.reshape(t, d)
        g1, b1 = ln1_g[l][None, :], ln1_b[l][None, :]
        g2, b2 = ln2_g[l][None, :], ln2_b[l][None, :]
        if dense:
            x2 = _oproj_ffn(a_mla, a_diff, x2d, l, wo, g1, b1, i, ffn_wg, ffn_wu, ffn_wd,
                            g2, b2, tm_tok)
        else:
            x1, x1b, rec = _oproj_route(a_mla, a_diff, x2d, l, wo, g1, b1, rhl[i], rhi[i], tm_tok)
            x2 = _moe(x1, x1b, rec, 0, moe_wg, moe_wu, moe_wd, g2, b2,
                      tm_c, tm_e, fc, unit, kc, wrows)
        xc = x2.reshape(b, s, d)
    return xc
```

```python
import functools
import math

import jax
import jax.numpy as jnp
from jax import lax
from jax.experimental import pallas as pl
from jax.experimental.pallas import tpu as pltpu

D_MODEL = 1024
DEPTH = 4
MLA_HEADS = 4
MLA_NOPE = 128
MLA_ROPE = 64
MLA_V = 128
MLA_Q_RANK = 384
MLA_KV_RANK = 256
ROPE_THETA = 10000.0
DIFF_HEADS = 4
DIFF_QK = 64
DIFF_V = 2 * DIFF_QK
NUM_BUCKETS = 32
MAX_DISTANCE = 128
D_FF_DENSE = 2816
N_EXPERTS = 8
D_FF_EXPERT = 3584
DN_ALPHA = (2 * DEPTH) ** 0.25

MLA_QK = MLA_NOPE + MLA_ROPE
MLA_W = MLA_HEADS * MLA_V
DIFF_W = DIFF_HEADS * DIFF_V
DIFF_VA = DIFF_V + 16
_CQ0, _CKV0 = 0, MLA_Q_RANK
_KR0 = _CKV0 + MLA_KV_RANK
_DQ0 = _KR0 + 2 * MLA_ROPE
_DK0 = _DQ0 + DIFF_W
_DV0 = _DK0 + DIFF_W
IN_COLS_W = _DV0 + DIFF_W

LANES = 128
BF16_ROWS = 16
VMEM_LIMIT = 56 * 1024 * 1024
NEG_BIG = -1e30
ROUTER_ROWS = 16
REC_ROWS = 8
LOG2E = math.log2(math.e)
ATT_BLK = 256
MLA_HP = 4
DIFF_HP = 2

BF16 = jnp.bfloat16
F32 = jnp.float32


def _dot(a, b):
    return jnp.dot(a, b, preferred_element_type=F32)


def _layer_norm_rows(y, g, b):
    mu = jnp.mean(y, axis=-1, keepdims=True)
    d = y - mu
    var = jnp.mean(d * d, axis=-1, keepdims=True)
    return d * lax.rsqrt(var + 1e-5) * g + b


def _rms_rows(y, g, eps):
    return y * lax.rsqrt(jnp.mean(y * y, axis=-1, keepdims=True) + eps) * g


def _proj_kernel(x_ref, win_ref, qn_ref, kvn_ref, wuq_ref, wukv_ref, cos_ref, sin_ref,
                 qT_ref, k_ref, vT_ref, dqT_ref, dk_ref, dvT_ref):
    x = x_ref[0].astype(BF16)
    h = _dot(x, win_ref[0])
    cqn = _rms_rows(h[:, _CQ0:_CQ0 + MLA_Q_RANK], qn_ref[0], 1e-6)
    ckvn = _rms_rows(h[:, _CKV0:_CKV0 + MLA_KV_RANK], kvn_ref[0], 1e-6)
    q = _dot(cqn.astype(BF16), wuq_ref[0]) * (MLA_QK ** -0.5 * LOG2E)
    kv = _dot(ckvn.astype(BF16), wukv_ref[0])
    cos4, sin4 = cos_ref[...], sin_ref[...]
    nr = MLA_HEADS * MLA_ROPE
    qr = q[:, MLA_W:MLA_W + nr] * cos4 + q[:, MLA_W + nr:] * sin4
    kr = (h[:, _KR0:_KR0 + MLA_ROPE] * cos4[:, :MLA_ROPE]
          + h[:, _KR0 + MLA_ROPE:_KR0 + 2 * MLA_ROPE] * sin4[:, :MLA_ROPE])
    qrT = qr.T
    for hh in range(MLA_HEADS):
        c0 = hh * LANES
        qnT = q[:, c0:c0 + MLA_NOPE].T
        qT_ref[0, hh] = jnp.concatenate(
            [qnT, qrT[hh * MLA_ROPE:(hh + 1) * MLA_ROPE]], axis=0).astype(BF16)
        k_ref[0, hh] = jnp.concatenate([kv[:, c0:c0 + MLA_NOPE], kr], axis=1).astype(BF16)
        vT_ref[0, hh] = kv[:, MLA_W + c0:MLA_W + c0 + MLA_V].T.astype(BF16)
        dqT = (h[:, _DQ0 + c0:_DQ0 + c0 + LANES] * (DIFF_QK ** -0.5 * LOG2E)).T.astype(BF16)
        zeros = jnp.zeros((DIFF_QK, x.shape[0]), BF16)
        dqT_ref[0, hh, 0] = jnp.concatenate([dqT[:DIFF_QK], zeros], axis=0)
        dqT_ref[0, hh, 1] = jnp.concatenate([zeros, dqT[DIFF_QK:]], axis=0)
        dk_ref[0, hh] = h[:, _DK0 + c0:_DK0 + c0 + LANES].astype(BF16)
        dvT_ref[0, hh] = jnp.concatenate(
            [h[:, _DV0 + c0:_DV0 + c0 + DIFF_V].T, jnp.ones((DIFF_VA - DIFF_V, x.shape[0]), F32)],
            axis=0).astype(BF16)


def _proj(x, l, win, qn, kvn, wuq, wukv, cos4, sin4, tm):
    b, s, d = x.shape
    hd = MLA_HEADS
    layer = lambda a: pl.BlockSpec((1,) + a.shape[1:], lambda bi, i: (l,) + (0,) * (a.ndim - 1))
    tmaj = lambda w: pl.BlockSpec((1, hd, tm, w), lambda bi, i: (bi, 0, i, 0))
    fmaj = lambda w: pl.BlockSpec((1, hd, w, tm), lambda bi, i: (bi, 0, 0, i))
    sds = jax.ShapeDtypeStruct
    return pl.pallas_call(
        _proj_kernel,
        grid=(b, s // tm),
        in_specs=[
            pl.BlockSpec((1, tm, d), lambda bi, i: (bi, i, 0)),
            layer(win), layer(qn), layer(kvn), layer(wuq), layer(wukv),
            pl.BlockSpec((tm, cos4.shape[1]), lambda bi, i: (i, 0)),
            pl.BlockSpec((tm, sin4.shape[1]), lambda bi, i: (i, 0)),
        ],
        out_specs=[fmaj(MLA_QK), tmaj(MLA_QK), fmaj(MLA_V),
                   pl.BlockSpec((1, hd, 2, LANES, tm), lambda bi, i: (bi, 0, 0, 0, i)),
                   tmaj(LANES), fmaj(DIFF_VA)],
        out_shape=[
            sds((b, hd, MLA_QK, s), BF16), sds((b, hd, s, MLA_QK), BF16), sds((b, hd, MLA_V, s), BF16),
            sds((b, hd, 2, LANES, s), BF16), sds((b, hd, s, LANES), BF16), sds((b, hd, DIFF_VA, s), BF16),
        ],
        compiler_params=pltpu.CompilerParams(
            dimension_semantics=("arbitrary", "arbitrary"), vmem_limit_bytes=VMEM_LIMIT),
        name="proj",
    )(x, win, qn, kvn, wuq, wukv, cos4, sin4)


def _softmax_step(s, vj, m_ref, l_ref, acc_ref, shift=None):
    m_prev = m_ref[...]
    smax = jnp.max(s, axis=0, keepdims=True)
    if shift is not None:
        smax = smax + shift
    m_new = jnp.maximum(m_prev, smax)
    a = jnp.exp2(m_prev - m_new)
    p = jnp.exp2(s - (m_new if shift is None else m_new - shift))
    if l_ref is not None:
        l_ref[...] = a * l_ref[...] + jnp.sum(p, axis=0, keepdims=True)
    acc_ref[...] = a * acc_ref[...] + _dot(vj, p.astype(BF16))
    m_ref[...] = m_new


def _chain_scratch(n, dv):
    return [pltpu.VMEM((1, ATT_BLK), F32)] * (2 * n) + [pltpu.VMEM((dv, ATT_BLK), F32)] * n


def _init_chains(state):
    n = len(state) // 3
    m_refs, l_refs, acc_refs = state[:n], state[n:2 * n], state[2 * n:]
    for m_ref, l_ref, acc_ref in zip(m_refs, l_refs, acc_refs):
        m_ref[...] = jnp.full(m_ref.shape, NEG_BIG, F32)
        l_ref[...] = jnp.zeros(l_ref.shape, F32)
        acc_ref[...] = jnp.zeros(acc_ref.shape, F32)
    return m_refs, l_refs, acc_refs


def _pipelined_tiles(qi, logits, next_logits, softmax_pv):
    a, b, c = 0, 1, 2

    @pl.when(qi == 0)
    def _():
        logits(0, a, True)
        next_logits(c)
        softmax_pv(0, a, "diag")

    @pl.when(qi == 1)
    def _():
        logits(1, a, True)
        softmax_pv(0, c, "near")
        next_logits(c)
        softmax_pv(1, a, "diag")

    @pl.when(qi >= 2)
    def _():
        logits(1, a)
        softmax_pv(0, c, "far")
        nfar = qi - 2

        def body(i, carry):
            j = 2 * i + 1
            logits(j + 1, b)
            softmax_pv(j, a, "far")
            logits(j + 2, a)
            softmax_pv(j + 1, b, "far")
            return carry

        lax.fori_loop(0, lax.shift_right_logical(nfar, 1), body, 0)
        odd = lax.rem(nfar, 2) == 1

        @pl.when(jnp.logical_not(odd))
        def _():
            logits(qi, b, True)
            softmax_pv(qi - 1, a, "near")
            next_logits(c)
            softmax_pv(qi, b, "diag")

        @pl.when(odd)
        def _():
            logits(qi - 1, b)
            softmax_pv(qi - 2, a, "far")
            logits(qi, a, True)
            softmax_pv(qi - 1, b, "near")
            next_logits(c)
            softmax_pv(qi, a, "diag")


def _mla_attn_kernel(qT_ref, qTn_ref, k_ref, vT_ref, o_ref, sa_ref, sb_ref, sc_ref, *state,
                     ta, hp):
    qi = pl.program_id(2)
    m_refs, l_refs, acc_refs = _init_chains(state)
    s_refs = (sa_ref, sb_ref, sc_ref)

    nb = ta // ATT_BLK
    blk = ATT_BLK

    def logits(j, buf, diagonal=False):
        off = pl.multiple_of(j * ta, ta)
        for h in range(hp):
            if not diagonal:
                s_refs[buf][h] = _dot(k_ref[0, h, pl.ds(off, ta), :], qT_ref[0, h])
                continue
            for qs in range(nb):
                nk, q0 = (qs + 1) * blk, qs * blk
                s_refs[buf][h, :nk, q0:q0 + blk] = _dot(k_ref[0, h, pl.ds(off, nk), :],
                                                        qT_ref[0, h, :, q0:q0 + blk])

    def next_logits(buf):
        for h in range(hp):
            s_refs[buf][h] = _dot(k_ref[0, h, :ta, :], qTn_ref[0, h])

    def softmax_pv(j, buf, kind):
        off = pl.multiple_of(j * ta, ta)
        diagonal = kind == "diag"
        for h in range(hp):
            for qs in range(nb):
                nk = (qs + 1) * blk if diagonal else ta
                s = s_refs[buf][h, :nk, qs * blk:(qs + 1) * blk]
                if diagonal:
                    kpos = lax.broadcasted_iota(jnp.int32, (nk, blk), 0)
                    qpos = lax.broadcasted_iota(jnp.int32, (nk, blk), 1) + qs * blk
                    s = jnp.where(kpos <= qpos, s, NEG_BIG)
                c = h * nb + qs
                _softmax_step(s, vT_ref[0, h, :, pl.ds(off, nk)], m_refs[c], l_refs[c], acc_refs[c])

    _pipelined_tiles(qi, logits, next_logits, softmax_pv)
    for h in range(hp):
        for qs in range(nb):
            c = h * nb + qs
            out = acc_refs[c][...] * (1.0 / l_refs[c][...])
            o_ref[0, qs * blk:(qs + 1) * blk, h * MLA_V:(h + 1) * MLA_V] = out.T.astype(BF16)


def _mla_attn(qT, k, vT, ta, hp):
    b, hd, _, s = qT.shape
    nq = s // ta
    return pl.pallas_call(
        functools.partial(_mla_attn_kernel, ta=ta, hp=hp),
        grid=(b, hd // hp, nq),
        in_specs=[
            pl.BlockSpec((1, hp, MLA_QK, ta), lambda bi, h, i: (bi, h, 0, i)),
            pl.BlockSpec((1, hp, MLA_QK, ta), lambda bi, h, i: (bi, h, 0, jnp.minimum(i + 1, nq - 1))),
            pl.BlockSpec((1, hp, s, MLA_QK), lambda bi, h, i: (bi, h, 0, 0)),
            pl.BlockSpec((1, hp, MLA_V, s), lambda bi, h, i: (bi, h, 0, 0)),
        ],
        out_specs=pl.BlockSpec((1, ta, hp * MLA_V), lambda bi, h, i: (bi, i, h)),
        out_shape=jax.ShapeDtypeStruct((b, s, hd * MLA_V), BF16),
        scratch_shapes=([pltpu.VMEM((hp, ta, ta), F32)] * 3
                        + _chain_scratch(hp * ta // ATT_BLK, MLA_V)),
        compiler_params=pltpu.CompilerParams(
            dimension_semantics=("arbitrary",) * 3, vmem_limit_bytes=VMEM_LIMIT),
        name="mla_attn",
    )(qT, qT, k, vT)


def _diff_attn_kernel(cfar_ref, qT_ref, qTn_ref, k_ref, vT_ref, bias_ref, lamp_ref, g_ref, *rest,
                      ta, hp, lambda_init, ncast):
    cast_in, o_ref, cast_out = rest[:ncast], rest[ncast], rest[ncast + 1:2 * ncast + 1]
    s_refs = rest[2 * ncast + 1:2 * ncast + 4]
    state = rest[2 * ncast + 4:]
    hg = pl.program_id(1)
    qi = pl.program_id(2)
    for src_ref, dst_ref in zip(cast_in, cast_out):
        dst_ref[...] = src_ref[...].astype(BF16)
    m_refs, l_refs, acc_refs = _init_chains(state)

    nb = ta // ATT_BLK
    blk = ATT_BLK

    def logits(j, buf, diagonal=False):
        off = pl.multiple_of(j * ta, ta)
        for h in range(hp):
            for c in range(2):
                if not diagonal:
                    s_refs[buf][2 * h + c] = _dot(k_ref[0, h, pl.ds(off, ta), :],
                                                  qT_ref[0, h, c])
                    continue
                for qb in range(nb):
                    nk, q0 = (qb + 1) * blk, qb * blk
                    s_refs[buf][2 * h + c, :nk, q0:q0 + blk] = _dot(
                        k_ref[0, h, pl.ds(off, nk), :], qT_ref[0, h, c, :, q0:q0 + blk])

    def next_logits(buf):
        for h in range(hp):
            for c in range(2):
                s_refs[buf][2 * h + c] = _dot(k_ref[0, h, :ta, :], qTn_ref[0, h, c])

    def softmax_pv(j, buf, kind):
        off = pl.multiple_of(j * ta, ta)
        diagonal = kind == "diag"
        for h in range(hp):
            for qb in range(nb):
                nk = (qb + 1) * blk if diagonal else ta
                vj = vT_ref[0, h, :, pl.ds(off, nk)]
                for c in range(2):
                    s = s_refs[buf][2 * h + c, :nk, qb * blk:(qb + 1) * blk]
                    i = (2 * h + c) * nb + qb
                    if kind == "far" or (kind == "near" and qb * blk + 1 >= MAX_DISTANCE):
                        _softmax_step(s, vj, m_refs[i], None, acc_refs[i],
                                      shift=cfar_ref[hg * hp + h])
                    else:
                        bias = bias_ref[h, 0 if diagonal else 1, :nk, qb * blk:(qb + 1) * blk]
                        _softmax_step(s + bias, vj, m_refs[i], None, acc_refs[i])

    _pipelined_tiles(qi, logits, next_logits, softmax_pv)

    lp = lamp_ref[...]
    lam = (jnp.exp(jnp.sum(lp[0:1] * lp[1:2], axis=-1, keepdims=True))
           - jnp.exp(jnp.sum(lp[2:3] * lp[3:4], axis=-1, keepdims=True)) + lambda_init)
    for h in range(hp):
        for qb in range(nb):
            a0 = acc_refs[(2 * h) * nb + qb][...]
            a1 = acc_refs[(2 * h + 1) * nb + qb][...]
            out = (a0[:DIFF_V] * (1.0 / a0[DIFF_V:DIFF_V + 1])
                   - lam * (a1[:DIFF_V] * (1.0 / a1[DIFF_V:DIFF_V + 1])))
            ms = jnp.mean(out * out, axis=0, keepdims=True)
            out = out * lax.rsqrt(ms + 1e-5) * g_ref[...] * (1.0 - lambda_init)
            o_ref[0, qb * blk:(qb + 1) * blk, h * DIFF_V:(h + 1) * DIFF_V] = out.T.astype(BF16)


def _cast_rows(rows_total, nsteps):
    rows = rows_total // nsteps
    return rows if rows * nsteps == rows_total and rows % BF16_ROWS == 0 else 0


def _diff_attn(cfar, qT, k, vT, bias_tiles, lam_params, g_col, ta, hp, lambda_init,
               cast=(), cast_part=(0, 1)):
    b, hd, _, _, s = qT.shape
    nq, nhg = s // ta, hd // hp
    nsteps = b * nhg * nq
    part, nparts = cast_part
    crows = [_cast_rows(a.shape[0] // nparts, nsteps) for a in cast]
    assert all(crows), "cast tables must split into whole tiles per grid step"
    step = lambda bi, h, i: (bi * nhg + h) * nq + i
    grid_spec = pltpu.PrefetchScalarGridSpec(
        num_scalar_prefetch=1,
        grid=(b, nhg, nq),
        in_specs=[
            pl.BlockSpec((1, hp, 2, LANES, ta), lambda bi, h, i, cf: (bi, h, 0, 0, i)),
            pl.BlockSpec((1, hp, 2, LANES, ta),
                         lambda bi, h, i, cf: (bi, h, 0, 0, jnp.minimum(i + 1, nq - 1))),
            pl.BlockSpec((1, hp, s, LANES), lambda bi, h, i, cf: (bi, h, 0, 0)),
            pl.BlockSpec((1, hp, DIFF_VA, s), lambda bi, h, i, cf: (bi, h, 0, 0)),
            pl.BlockSpec((hp, 2, ta, ta), lambda bi, h, i, cf: (h, 0, 0, 0)),
            pl.BlockSpec(lam_params.shape, lambda bi, h, i, cf: (0, 0)),
            pl.BlockSpec(g_col.shape, lambda bi, h, i, cf: (0, 0)),
        ] + [pl.BlockSpec((r, a.shape[1]), lambda bi, h, i, cf: (part * nsteps + step(bi, h, i), 0))
             for a, r in zip(cast, crows)],
        out_specs=[pl.BlockSpec((1, ta, hp * DIFF_V), lambda bi, h, i, cf: (bi, i, h))]
        + [pl.BlockSpec((r, a.shape[1]), lambda bi, h, i, cf: (step(bi, h, i), 0))
           for a, r in zip(cast, crows)],
        scratch_shapes=([pltpu.VMEM((2 * hp, ta, ta), F32)] * 3
                        + _chain_scratch(2 * hp * ta // ATT_BLK, DIFF_VA)),
    )
    res = pl.pallas_call(
        functools.partial(_diff_attn_kernel, ta=ta, hp=hp, lambda_init=lambda_init,
                          ncast=len(cast)),
        grid_spec=grid_spec,
        out_shape=[jax.ShapeDtypeStruct((b, s, hd * DIFF_V), BF16)]
        + [jax.ShapeDtypeStruct((a.shape[0] // nparts, a.shape[1]), BF16) for a in cast],
        compiler_params=pltpu.CompilerParams(
            dimension_semantics=("arbitrary",) * 3, vmem_limit_bytes=VMEM_LIMIT),
        name="diff_attn",
    )(cfar, qT, qT, k, vT, bias_tiles, lam_params, g_col, *cast)
    return res[0], res[1:]


def _split_bf16(v):
    hi = v.astype(BF16)
    return hi, (v - hi.astype(F32)).astype(BF16)


def _oproj_route_kernel(am_ref, ad_ref, x_ref, wo_ref, g_ref, b_ref, rhl_ref, rh_ref,
                        o_ref, ob_ref, r_ref):
    mix = _dot(am_ref[...], wo_ref[0, :MLA_W, :]) + _dot(ad_ref[...], wo_ref[0, MLA_W:, :])
    y = _layer_norm_rows(DN_ALPHA * x_ref[...] + mix, g_ref[...], b_ref[...])
    yhi, ylo = _split_bf16(y)
    nt = (((1,), (1,)), ((), ()))
    both = lax.dot_general(rhl_ref[...], yhi, nt, preferred_element_type=F32)
    logits = (both[:ROUTER_ROWS] + both[ROUTER_ROWS:]
              + lax.dot_general(rh_ref[...], ylo, nt, preferred_element_type=F32))
    row = lax.broadcasted_iota(jnp.int32, logits.shape, 0)
    logits = jnp.where(row < N_EXPERTS, logits, NEG_BIG)
    m1 = jnp.max(logits, axis=0, keepdims=True)
    i1 = jnp.min(jnp.where(logits == m1, row, ROUTER_ROWS), axis=0, keepdims=True)
    rest_l = jnp.where(row == i1, NEG_BIG, logits)
    m2 = jnp.max(rest_l, axis=0, keepdims=True)
    i2 = jnp.min(jnp.where(rest_l == m2, row, ROUTER_ROWS), axis=0, keepdims=True)
    e2 = jnp.exp(m2 - m1)
    g1 = 1.0 / (1.0 + e2)
    g2 = e2 * g1
    row8 = lax.broadcasted_iota(jnp.int32, r_ref.shape, 0)
    rec = jnp.where(row8 == 0, i1.astype(F32),
                    jnp.where(row8 == 1, i2.astype(F32),
                              jnp.where(row8 == 2, g1, jnp.where(row8 == 3, g2, 0.0))))
    o_ref[...] = y
    ob_ref[...] = yhi
    r_ref[...] = rec


def _oproj_route(a_mla, a_diff, x2d, l, wo, g, b, rhl, rh, tm):
    t, d = x2d.shape
    row = lambda w: pl.BlockSpec((tm, w), lambda i: (i, 0))
    full = lambda a: pl.BlockSpec(a.shape, lambda i: (0,) * a.ndim)
    return pl.pallas_call(
        _oproj_route_kernel,
        grid=(t // tm,),
        in_specs=[row(MLA_W), row(DIFF_W), row(d),
                  pl.BlockSpec((1,) + wo.shape[1:], lambda i: (l, 0, 0)),
                  full(g), full(b), full(rhl), full(rh)],
        out_specs=[row(d), row(d), pl.BlockSpec((REC_ROWS, tm), lambda i: (0, i))],
        out_shape=[jax.ShapeDtypeStruct((t, d), F32), jax.ShapeDtypeStruct((t, d), BF16),
                   jax.ShapeDtypeStruct((REC_ROWS, t), F32)],
        compiler_params=pltpu.CompilerParams(
            dimension_semantics=("arbitrary",), vmem_limit_bytes=VMEM_LIMIT),
        name="oproj_route",
    )(a_mla, a_diff, x2d, wo, g, b, rhl, rh)


def _oproj_ffn_kernel(am_ref, ad_ref, x_ref, wo_ref, g1_ref, b1_ref, wg_ref, wu_ref, wd_ref,
                      g2_ref, b2_ref, o_ref):
    mix = _dot(am_ref[...], wo_ref[0, :MLA_W, :]) + _dot(ad_ref[...], wo_ref[0, MLA_W:, :])
    x1 = _layer_norm_rows(DN_ALPHA * x_ref[...] + mix, g1_ref[...], b1_ref[...])
    xb = x1.astype(BF16)
    gate = _dot(xb, wg_ref[0])
    up = _dot(xb, wu_ref[0])
    hid = (gate * jax.nn.sigmoid(gate) * up).astype(BF16)
    f = _dot(hid, wd_ref[0])
    o_ref[...] = _layer_norm_rows(DN_ALPHA * x1 + f, g2_ref[...], b2_ref[...])


def _oproj_ffn(a_mla, a_diff, x2d, l, wo, g1, b1, li, wg, wu, wd, g2, b2, tm):
    t, d = x2d.shape
    row = lambda w: pl.BlockSpec((tm, w), lambda i: (i, 0))
    layer = lambda a, k: pl.BlockSpec((1,) + a.shape[1:], lambda i: (k,) + (0,) * (a.ndim - 1),
                                      pipeline_mode=pl.Buffered(1))
    vec = lambda a: pl.BlockSpec(a.shape, lambda i: (0, 0))
    return pl.pallas_call(
        _oproj_ffn_kernel,
        grid=(t // tm,),
        in_specs=[row(MLA_W), row(DIFF_W), row(d), layer(wo, l), vec(g1), vec(b1),
                  layer(wg, li), layer(wu, li), layer(wd, li), vec(g2), vec(b2)],
        out_specs=row(d),
        out_shape=jax.ShapeDtypeStruct((t, d), F32),
        compiler_params=pltpu.CompilerParams(
            dimension_semantics=("arbitrary",), vmem_limit_bytes=VMEM_LIMIT),
        name="oproj_ffn",
    )(a_mla, a_diff, x2d, wo, g1, b1, wg, wu, wd, g2, b2)


def _dispatch_kernel(te_ref, uoff_ref, unc_ref, x_ref, post_ref, o_ref, acc_ref, *, tm, unit, kc):
    i = pl.program_id(0)
    e = te_ref[i]
    t = x_ref.shape[0]
    upt = tm // unit
    for r in range(upt):
        u = i * upt + r
        n = unc_ref[u]
        start0 = uoff_ref[u]
        row_pos = u * unit + lax.broadcasted_iota(jnp.int32, (unit, 1), 0)

        def chunk(c):
            nominal = start0 + c * kc
            off = pl.multiple_of(jnp.minimum(nominal, t - kc), LANES)
            dest = post_ref[pl.ds(e, 1), pl.ds(off, kc)]
            tok = off + lax.broadcasted_iota(jnp.int32, (1, kc), 1)
            dest = jnp.where(tok >= nominal, dest, -1)
            sel = jnp.where(dest == row_pos, 1.0, 0.0).astype(BF16)
            return _dot(sel, x_ref[pl.ds(off, kc), :])

        @pl.when(n > 0)
        def _():
            acc_ref[...] = chunk(0)

        @pl.when(n == 0)
        def _():
            acc_ref[...] = jnp.zeros(acc_ref.shape, F32)

        def more(c, carry):
            acc_ref[...] += chunk(c)
            return carry

        lax.fori_loop(1, n, more, 0)
        o_ref[r * unit:(r + 1) * unit, :] = acc_ref[...].astype(BF16)


def _dispatch(tile_e, uoff, unc, xb, post, nt, tm, unit, kc):
    t, d = xb.shape
    const = lambda shape: pl.BlockSpec(shape, lambda i, *_: (0,) * len(shape),
                                       pipeline_mode=pl.Buffered(1))
    grid_spec = pltpu.PrefetchScalarGridSpec(
        num_scalar_prefetch=3,
        grid=(nt,),
        in_specs=[const(xb.shape), const(post.shape)],
        out_specs=pl.BlockSpec((tm, d), lambda i, *_: (i, 0)),
        scratch_shapes=[pltpu.VMEM((unit, d), F32)],
    )
    return pl.pallas_call(
        functools.partial(_dispatch_kernel, tm=tm, unit=unit, kc=kc),
        grid_spec=grid_spec,
        out_shape=jax.ShapeDtypeStruct((nt * tm, d), BF16),
        compiler_params=pltpu.CompilerParams(
            dimension_semantics=("arbitrary",), vmem_limit_bytes=VMEM_LIMIT),
        name="dispatch",
    )(tile_e, uoff, unc, xb, post)


def _gffn_kernel(te_ref, tv_ref, x_ref, wg_ref, wu_ref, wd_ref, o_ref, acc_ref, *, nf):
    i = pl.program_id(0)
    f = pl.program_id(1)
    status = tv_ref[i]
    tm = x_ref.shape[0]
    half = tm // 2

    def swiglu_rows(rows):
        xb = x_ref[:rows, :]
        gate = _dot(xb, wg_ref[0, 0])
        up = _dot(xb, wu_ref[0, 0])
        hid = (gate * jax.nn.sigmoid(gate) * up).astype(BF16)
        part = _dot(hid, wd_ref[0, 0])

        @pl.when(f == 0)
        def _():
            acc_ref[:rows, :] = part

        @pl.when(f > 0)
        def _():
            acc_ref[:rows, :] += part

    @pl.when(status == 2)
    def _():
        swiglu_rows(tm)

    @pl.when(status == 1)
    def _():
        swiglu_rows(half)

    @pl.when(jnp.logical_and(f == nf - 1, status == 2))
    def _():
        o_ref[...] = acc_ref[...].astype(BF16)

    @pl.when(jnp.logical_and(f == nf - 1, status == 1))
    def _():
        o_ref[:half, :] = acc_ref[:half, :].astype(BF16)
        o_ref[half:, :] = jnp.zeros((tm - half, o_ref.shape[1]), BF16)

    @pl.when(jnp.logical_and(f == nf - 1, status == 0))
    def _():
        o_ref[...] = jnp.zeros(o_ref.shape, BF16)


def _gffn(tile_e, tile_v, xs, li, wg, wu, wd, tm, fc):
    p, d = xs.shape
    nt = p // tm
    nf = wg.shape[3] // fc
    fidx = lambda i, f, te, tv: jnp.where(tv[i] > 0, f, nf - 1)
    grid_spec = pltpu.PrefetchScalarGridSpec(
        num_scalar_prefetch=2,
        grid=(nt, nf),
        in_specs=[
            pl.BlockSpec((tm, d), lambda i, f, te, tv: (i, 0)),
            pl.BlockSpec((1, 1, d, fc), lambda i, f, te, tv: (li, te[i], 0, fidx(i, f, te, tv))),
            pl.BlockSpec((1, 1, d, fc), lambda i, f, te, tv: (li, te[i], 0, fidx(i, f, te, tv))),
            pl.BlockSpec((1, 1, fc, d), lambda i, f, te, tv: (li, te[i], fidx(i, f, te, tv), 0)),
        ],
        out_specs=pl.BlockSpec((tm, d), lambda i, f, te, tv: (i, 0)),
        scratch_shapes=[pltpu.VMEM((tm, d), F32)],
    )
    return pl.pallas_call(
        functools.partial(_gffn_kernel, nf=nf),
        grid_spec=grid_spec,
        out_shape=jax.ShapeDtypeStruct((p, d), BF16),
        compiler_params=pltpu.CompilerParams(
            dimension_semantics=("arbitrary", "arbitrary"), vmem_limit_bytes=VMEM_LIMIT),
        name="grouped_ffn",
    )(tile_e, tile_v, xs, wg, wu, wd)


def _combine_kernel(ral_ref, rlo_ref, rhi_ref, x_ref, rec_ref, g_ref, b_ref, ys_ref, o_ref,
                    win_ref, xwin_ref, acc_ref, sems, xsem, *, w):
    j = pl.program_id(0)
    nj = pl.num_programs(0)
    slot = lax.rem(j, 2)
    p_rows = ys_ref.shape[0]

    def win_copy(jj, e, sl):
        r = pl.multiple_of(ral_ref[jj * N_EXPERTS + e], BF16_ROWS)
        return pltpu.make_async_copy(ys_ref.at[pl.ds(r, w)], win_ref.at[sl, pl.ds(e * w, w)],
                                     sems.at[sl, e])

    @pl.when(j == 0)
    def _():
        for e in range(N_EXPERTS):
            win_copy(0, e, 0).start()

    @pl.when(j + 1 < nj)
    def _():
        for e in range(N_EXPERTS):
            win_copy(j + 1, e, 1 - slot).start()

    rec = rec_ref[...]
    p1, p2, g1, g2 = rec[:, 0:1], rec[:, 1:2], rec[:, 2:3], rec[:, 3:4]

    def sel(base, lo, hi):
        rows = base + lax.broadcasted_iota(jnp.int32, (1, w), 1)
        rp = jnp.where((rows >= lo) & (rows < hi), rows, -1).astype(F32)
        return (jnp.where(p1 == rp, g1, 0.0) + jnp.where(p2 == rp, g2, 0.0)).astype(BF16)

    lane = lax.broadcasted_iota(jnp.int32, (1, N_EXPERTS * w), 1)
    rp = jnp.full((1, N_EXPERTS * w), -1, jnp.int32)
    for e in range(N_EXPERTS):
        win_copy(j, e, slot).wait()
        k = j * N_EXPERTS + e
        rows = ral_ref[k] + lane - e * w
        mine = ((lane >= e * w) & (lane < (e + 1) * w)
                & (rows >= rlo_ref[k]) & (rows < rhi_ref[k]))
        rp = jnp.where(mine, rows, rp)
    rp = rp.astype(F32)
    sel_all = (jnp.where(p1 == rp, g1, 0.0) + jnp.where(p2 == rp, g2, 0.0)).astype(BF16)
    acc_ref[...] = _dot(sel_all, win_ref[slot])

    for e in range(N_EXPERTS):
        k = j * N_EXPERTS + e
        nwin = (rhi_ref[k] - ral_ref[k] + (w - 1)) // w

        def extra(wi, carry, k=k):
            nominal = ral_ref[k] + wi * w
            base = pl.multiple_of(jnp.minimum(nominal, p_rows - w), BF16_ROWS)
            cp = pltpu.make_async_copy(ys_ref.at[pl.ds(base, w)], xwin_ref, xsem)
            cp.start()
            cp.wait()
            acc_ref[...] += _dot(sel(base, nominal, rhi_ref[k]), xwin_ref[...])
            return carry

        lax.fori_loop(1, nwin, extra, 0)

    o_ref[...] = _layer_norm_rows(DN_ALPHA * x_ref[...] + acc_ref[...], g_ref[...], b_ref[...])


def _combine(ral, rlo, rhi, x2d, rec2, g, b, ys, tm, w):
    t, d = x2d.shape
    row = lambda c: pl.BlockSpec((tm, c), lambda i, *_: (i, 0))
    full = lambda shape: pl.BlockSpec(shape, lambda i, *_: (0,) * len(shape))
    grid_spec = pltpu.PrefetchScalarGridSpec(
        num_scalar_prefetch=3,
        grid=(t // tm,),
        in_specs=[row(d), row(rec2.shape[1]), full(g.shape), full(b.shape),
                  pl.BlockSpec(memory_space=pl.ANY)],
        out_specs=row(d),
        scratch_shapes=[pltpu.VMEM((2, N_EXPERTS * w, d), BF16), pltpu.VMEM((w, d), BF16),
                        pltpu.VMEM((tm, d), F32), pltpu.SemaphoreType.DMA((2, N_EXPERTS)),
                        pltpu.SemaphoreType.DMA(())],
    )
    return pl.pallas_call(
        functools.partial(_combine_kernel, w=w),
        grid_spec=grid_spec,
        out_shape=jax.ShapeDtypeStruct((t, d), F32),
        compiler_params=pltpu.CompilerParams(
            dimension_semantics=("arbitrary",), vmem_limit_bytes=VMEM_LIMIT),
        name="combine",
    )(ral, rlo, rhi, x2d, rec2, g, b, ys)


def _route_plan(rec, nt, tm_e, unit, kc, tm_c, w):
    t = rec.shape[1]
    i32 = jnp.int32
    ar = jnp.arange(N_EXPERTS, dtype=i32)
    m1 = (rec[0].astype(i32)[:, None] == ar[None, :]).astype(i32)
    m2 = (rec[1].astype(i32)[:, None] == ar[None, :]).astype(i32)
    m = m1 + m2
    cinc = jnp.cumsum(m, axis=0)
    counts = cinc[-1]
    padded = ((counts + tm_e - 1) // tm_e) * tm_e
    ends = jnp.cumsum(padded)
    starts = ends - padded
    posmat = starts[None, :] + (cinc - m)
    pos1 = jnp.sum(m1 * posmat, axis=1)
    pos2 = jnp.sum(m2 * posmat, axis=1)
    post = jnp.where(m.T > 0, posmat.T, -1)
    rec2 = jnp.stack([pos1.astype(F32), pos2.astype(F32), rec[2], rec[3]]
                     + [jnp.zeros((t,), F32)] * 4, axis=1)
    tile_start = jnp.arange(nt, dtype=i32) * tm_e
    tile_e = jnp.sum((tile_start[:, None] >= ends[None, :]).astype(i32), axis=1)
    tile_v = tile_e < N_EXPERTS
    last_e = jnp.max(jnp.where(tile_v, tile_e, 0))
    tile_e = jnp.where(tile_v, tile_e, last_e)
    upt = tm_e // unit
    nu = nt * upt
    oh = (jnp.repeat(tile_e, upt)[:, None] == ar[None, :]).astype(i32)
    q0 = jnp.arange(nu, dtype=i32) * unit - jnp.sum(oh * starts[None, :], axis=1)
    q1 = jnp.minimum(q0 + unit, jnp.sum(oh * counts[None, :], axis=1))
    has = jnp.repeat(tile_v, upt) & (q1 > q0)
    cb = jnp.concatenate([jnp.zeros((1, N_EXPERTS), i32), cinc[LANES - 1::LANES]], axis=0)
    cbu = jnp.sum(oh[:, None, :] * cb[None, :, :], axis=2)
    jlo = jnp.sum((cbu[:, 1:] <= q0[:, None]).astype(i32), axis=1)
    jend = jnp.sum((cbu[:, :-1] < q1[:, None]).astype(i32), axis=1)
    uoff = jnp.where(has, jlo * LANES, 0)
    unc = jnp.where(has, (jend * LANES - uoff + kc - 1) // kc, 0)
    cbt = jnp.concatenate([jnp.zeros((1, N_EXPERTS), i32), cinc[tm_c - 1::tm_c]], axis=0)
    r_lo = starts[None, :] + cbt[:-1]
    r_hi = starts[None, :] + cbt[1:]
    ral = jnp.minimum((r_lo // BF16_ROWS) * BF16_ROWS, nt * tm_e - w)
    oh_t = (tile_e[:, None] == ar[None, :]).astype(i32)
    tile_rows = (jnp.sum(oh_t * counts[None, :], axis=1)
                 - (tile_start - jnp.sum(oh_t * starts[None, :], axis=1)))
    tile_status = jnp.where(tile_v, jnp.where(tile_rows > tm_e // 2, 2, 1), 0)
    return (rec2, post, tile_e, tile_status.astype(i32), uoff, unc,
            ral.reshape(-1), r_lo.reshape(-1), r_hi.reshape(-1))


def _moe(x2d, xb, rec, li, wg, wu, wd, g, b, tm_c, tm_e, fc, unit, kc, w):
    t, d = x2d.shape
    nt = (2 * t) // tm_e + N_EXPERTS
    rec2, post, tile_e, tile_v, uoff, unc, ral, rlo, rhi = _route_plan(
        rec, nt, tm_e, unit, kc, tm_c, w)
    xs = _dispatch(tile_e, uoff, unc, xb, post, nt, tm_e, unit, kc)
    ys = _gffn(tile_e, tile_v, xs, li, wg, wu, wd, tm_e, fc)
    return _combine(ral, rlo, rhi, x2d, rec2, g, b, ys, tm_c, w)


def _rotate_half_cols(w):
    half = w.shape[-1] // 2
    return jnp.concatenate([-w[..., half:], w[..., :half]], axis=-1)


def _t5_bucket(n):
    max_exact = NUM_BUCKETS // 2
    nf = jnp.maximum(n, 1).astype(F32)
    large = max_exact + (jnp.log(nf / max_exact) / math.log(MAX_DISTANCE / max_exact)
                         * (NUM_BUCKETS - max_exact)).astype(jnp.int32)
    large = jnp.minimum(large, NUM_BUCKETS - 1)
    return jnp.where(n < max_exact, n, large)


def _pick(n, prefs):
    for p in prefs:
        if n % p == 0:
            return p
    raise ValueError(f"no tile for extent {n}")


def kernel(x, rel_bias, w_in, mla_q_norm, mla_kv_norm, mla_w_uq, mla_w_uk, mla_w_uv, diff_lambda, diff_norm, w_o, ln1_g, ln1_b, ln2_g, ln2_b, ffn_w_gate, ffn_w_up, ffn_w_down, moe_router, moe_w_gate, moe_w_up, moe_w_down):
    b, s, d = x.shape
    t = b * s
    assert d == D_MODEL
    ta = _pick(s, (512, 256))
    assert ta % ATT_BLK == 0
    assert s >= 2 * ta or s == ta
    tm_proj = _pick(s, (512, 256, 128))
    tm_tok = _pick(t, (512, 256, 128))
    tm_e = _pick(2 * t, (512, 256))
    tm_c = _pick(t, (512, 256))
    unit, kc, wrows = 128, 768, 224
    assert tm_e % unit == 0 and t >= kc and t % LANES == 0
    fc = D_FF_EXPERT // 2

    pos = jnp.arange(s, dtype=F32)
    inv = 1.0 / (ROPE_THETA ** (jnp.arange(0, MLA_ROPE, 2, dtype=F32) / MLA_ROPE))
    ang = pos[:, None] * inv[None, :]
    cos4 = jnp.tile(jnp.cos(ang), (1, 2 * MLA_HEADS))
    sin4 = jnp.tile(jnp.sin(ang), (1, 2 * MLA_HEADS))
    rb = rel_bias.astype(F32)
    period = 2 * ta
    dd = jnp.arange(period)
    bucket = _t5_bucket(dd)
    bd = jnp.zeros((DIFF_HEADS, period), F32)
    for bk in range(NUM_BUCKETS):
        bd = bd + jnp.where((bucket == bk)[None], rb[bk][:, None], 0.0)
    bd = bd * LOG2E

    def toeplitz(w):
        rows = jnp.tile(w, (1, ta))[:, :ta * (period - 1)].reshape(DIFF_HEADS, ta, period - 1)
        return rows[:, :, :ta]

    bias_tiles = jnp.stack(
        [toeplitz(jnp.where(dd[None] < ta, bd, NEG_BIG)),
         toeplitz(jnp.roll(bd, -ta, axis=1))], axis=1)
    assert ta + 1 >= MAX_DISTANCE
    cfar = rb[NUM_BUCKETS - 1] * LOG2E
    n_moe = moe_w_gate.shape[0]
    moe_tables = (moe_w_gate.reshape(-1, D_FF_EXPERT), moe_w_up.reshape(-1, D_FF_EXPERT),
                  moe_w_down.reshape(-1, d))

    kr_w = w_in[:, :, _KR0:_KR0 + MLA_ROPE]
    win = jnp.concatenate([w_in[:, :, :_KR0 + MLA_ROPE], _rotate_half_cols(kr_w),
                           w_in[:, :, _KR0 + MLA_ROPE:]], axis=2).astype(BF16)
    nl = w_in.shape[0]
    uq = mla_w_uq.reshape(nl, MLA_Q_RANK, MLA_HEADS, MLA_QK)
    uq_r = uq[..., MLA_NOPE:]
    wuq = jnp.concatenate([uq[..., :MLA_NOPE].reshape(nl, MLA_Q_RANK, -1),
                           uq_r.reshape(nl, MLA_Q_RANK, -1),
                           _rotate_half_cols(uq_r).reshape(nl, MLA_Q_RANK, -1)], axis=2).astype(BF16)
    wukv = jnp.concatenate([mla_w_uk, mla_w_uv], axis=2).astype(BF16)
    qn, kvn = mla_q_norm[:, None, :], mla_kv_norm[:, None, :]
    wo = w_o.astype(BF16)
    ffn_wg, ffn_wu, ffn_wd = (w.astype(BF16) for w in (ffn_w_gate, ffn_w_up, ffn_w_down))
    rpad = jnp.pad(jnp.swapaxes(moe_router.astype(F32), 1, 2),
                   ((0, 0), (0, ROUTER_ROWS - N_EXPERTS), (0, 0)))
    rhi = rpad.astype(BF16)
    rhl = jnp.concatenate([rhi, (rpad - rhi.astype(F32)).astype(BF16)], axis=1)

    xc = x
    for l in range(DEPTH):
        lambda_init = 0.8 - 0.6 * math.exp(-0.3 * l)
        qT, k, vT, dqT, dk, dvT = _proj(xc, l, win, qn, kvn, wuq, wukv, cos4, sin4, tm_proj)
        a_mla = _mla_attn(qT, k, vT, ta, MLA_HP).reshape(t, MLA_W)
        i = l // 2
        dense = l % 2 == 0
        a_diff, casts = _diff_attn(cfar, dqT, dk, dvT, bias_tiles, diff_lambda[l].astype(F32),
                                   diff_norm[l].astype(F32)[:, None], ta, DIFF_HP, lambda_init,
                                   cast=moe_tables if dense and i < n_moe else (),
                                   cast_part=(i, n_moe))
        a_diff = a_diff.reshape(t, DIFF_W)
        if casts:
            moe_wg = casts[0].reshape(1, N_EXPERTS, d, D_FF_EXPERT)
            moe_wu = casts[1].reshape(1, N_EXPERTS, d, D_FF_EXPERT)
            moe_wd = casts[2].reshape(1, N_EXPERTS, D_FF_EXPERT, d)
        x2d = xc.reshape(t, d)
        g1, b1 = ln1_g[l][None, :], ln1_b[l][None, :]
        g2, b2 = ln2_g[l][None, :], ln2_b[l][None, :]
        if dense:
            x2 = _oproj_ffn(a_mla, a_diff, x2d, l, wo, g1, b1, i, ffn_wg, ffn_wu, ffn_wd,
                            g2, b2, tm_tok)
        else:
            x1, x1b, rec = _oproj_route(a_mla, a_diff, x2d, l, wo, g1, b1, rhl[i], rhi[i], tm_tok)
            x2 = _moe(x1, x1b, rec, 0, moe_wg, moe_wu, moe_wd, g2, b2,
                      tm_c, tm_e, fc, unit, kc, wrows)
        xc = x2.reshape(b, s, d)
    return xc
```

```python
import functools
import math

import jax
import jax.numpy as jnp
from jax import lax
from jax.experimental import pallas as pl
from jax.experimental.pallas import tpu as pltpu

D_MODEL = 1024
DEPTH = 4
MLA_HEADS = 4
MLA_NOPE = 128
MLA_ROPE = 64
MLA_V = 128
MLA_Q_RANK = 384
MLA_KV_RANK = 256
ROPE_THETA = 10000.0
DIFF_HEADS = 4
DIFF_QK = 64
DIFF_V = 2 * DIFF_QK
NUM_BUCKETS = 32
MAX_DISTANCE = 128
D_FF_DENSE = 2816
N_EXPERTS = 8
D_FF_EXPERT = 3584
DN_ALPHA = (2 * DEPTH) ** 0.25

MLA_QK = MLA_NOPE + MLA_ROPE
MLA_W = MLA_HEADS * MLA_V
DIFF_W = DIFF_HEADS * DIFF_V
DIFF_VA = DIFF_V + 16
_CQ0, _CKV0 = 0, MLA_Q_RANK
_KR0 = _CKV0 + MLA_KV_RANK
_DQ0 = _KR0 + 2 * MLA_ROPE
_DK0 = _DQ0 + DIFF_W
_DV0 = _DK0 + DIFF_W
IN_COLS_W = _DV0 + DIFF_W

LANES = 128
BF16_ROWS = 16
VMEM_LIMIT = 56 * 1024 * 1024
NEG_BIG = -1e30
ROUTER_ROWS = 16
REC_ROWS = 8
LOG2E = math.log2(math.e)
ATT_BLK = 256
MLA_HP = 4
DIFF_HP = 2

BF16 = jnp.bfloat16
F32 = jnp.float32


def _dot(a, b):
    return jnp.dot(a, b, preferred_element_type=F32)


def _layer_norm_rows(y, g, b):
    mu = jnp.mean(y, axis=-1, keepdims=True)
    d = y - mu
    var = jnp.mean(d * d, axis=-1, keepdims=True)
    return d * lax.rsqrt(var + 1e-5) * g + b


def _rms_rows(y, g, eps):
    return y * lax.rsqrt(jnp.mean(y * y, axis=-1, keepdims=True) + eps) * g


def _proj_kernel(x_ref, win_ref, qn_ref, kvn_ref, wuq_ref, wukv_ref, cos_ref, sin_ref,
                 qT_ref, k_ref, vT_ref, dqT_ref, dk_ref, dvT_ref):
    x = x_ref[0].astype(BF16)
    h = _dot(x, win_ref[0])
    cqn = _rms_rows(h[:, _CQ0:_CQ0 + MLA_Q_RANK], qn_ref[0], 1e-6)
    ckvn = _rms_rows(h[:, _CKV0:_CKV0 + MLA_KV_RANK], kvn_ref[0], 1e-6)
    q = _dot(cqn.astype(BF16), wuq_ref[0]) * (MLA_QK ** -0.5 * LOG2E)
    kv = _dot(ckvn.astype(BF16), wukv_ref[0])
    cos4, sin4 = cos_ref[...], sin_ref[...]
    nr = MLA_HEADS * MLA_ROPE
    qr = q[:, MLA_W:MLA_W + nr] * cos4 + q[:, MLA_W + nr:] * sin4
    kr = (h[:, _KR0:_KR0 + MLA_ROPE] * cos4[:, :MLA_ROPE]
          + h[:, _KR0 + MLA_ROPE:_KR0 + 2 * MLA_ROPE] * sin4[:, :MLA_ROPE])
    qrT = qr.T
    for hh in range(MLA_HEADS):
        c0 = hh * LANES
        qnT = q[:, c0:c0 + MLA_NOPE].T
        qT_ref[0, hh] = jnp.concatenate(
            [qnT, qrT[hh * MLA_ROPE:(hh + 1) * MLA_ROPE]], axis=0).astype(BF16)
        k_ref[0, hh] = jnp.concatenate([kv[:, c0:c0 + MLA_NOPE], kr], axis=1).astype(BF16)
        vT_ref[0, hh] = kv[:, MLA_W + c0:MLA_W + c0 + MLA_V].T.astype(BF16)
        dqT = (h[:, _DQ0 + c0:_DQ0 + c0 + LANES] * (DIFF_QK ** -0.5 * LOG2E)).T.astype(BF16)
        zeros = jnp.zeros((DIFF_QK, x.shape[0]), BF16)
        dqT_ref[0, hh, 0] = jnp.concatenate([dqT[:DIFF_QK], zeros], axis=0)
        dqT_ref[0, hh, 1] = jnp.concatenate([zeros, dqT[DIFF_QK:]], axis=0)
        dk_ref[0, hh] = h[:, _DK0 + c0:_DK0 + c0 + LANES].astype(BF16)
        dvT_ref[0, hh] = jnp.concatenate(
            [h[:, _DV0 + c0:_DV0 + c0 + DIFF_V].T, jnp.ones((DIFF_VA - DIFF_V, x.shape[0]), F32)],
            axis=0).astype(BF16)


def _proj(x, l, win, qn, kvn, wuq, wukv, cos4, sin4, tm):
    b, s, d = x.shape
    hd = MLA_HEADS
    layer = lambda a: pl.BlockSpec((1,) + a.shape[1:], lambda bi, i: (l,) + (0,) * (a.ndim - 1))
    tmaj = lambda w: pl.BlockSpec((1, hd, tm, w), lambda bi, i: (bi, 0, i, 0))
    fmaj = lambda w: pl.BlockSpec((1, hd, w, tm), lambda bi, i: (bi, 0, 0, i))
    sds = jax.ShapeDtypeStruct
    return pl.pallas_call(
        _proj_kernel,
        grid=(b, s // tm),
        in_specs=[
            pl.BlockSpec((1, tm, d), lambda bi, i: (bi, i, 0)),
            layer(win), layer(qn), layer(kvn), layer(wuq), layer(wukv),
            pl.BlockSpec((tm, cos4.shape[1]), lambda bi, i: (i, 0)),
            pl.BlockSpec((tm, sin4.shape[1]), lambda bi, i: (i, 0)),
        ],
        out_specs=[fmaj(MLA_QK), tmaj(MLA_QK), fmaj(MLA_V),
                   pl.BlockSpec((1, hd, 2, LANES, tm), lambda bi, i: (bi, 0, 0, 0, i)),
                   tmaj(LANES), fmaj(DIFF_VA)],
        out_shape=[
            sds((b, hd, MLA_QK, s), BF16), sds((b, hd, s, MLA_QK), BF16), sds((b, hd, MLA_V, s), BF16),
            sds((b, hd, 2, LANES, s), BF16), sds((b, hd, s, LANES), BF16), sds((b, hd, DIFF_VA, s), BF16),
        ],
        compiler_params=pltpu.CompilerParams(
            dimension_semantics=("arbitrary", "arbitrary"), vmem_limit_bytes=VMEM_LIMIT),
        name="proj",
    )(x, win, qn, kvn, wuq, wukv, cos4, sin4)


def _softmax_step(s, vj, m_ref, l_ref, acc_ref, shift=None):
    m_prev = m_ref[...]
    smax = jnp.max(s, axis=0, keepdims=True)
    if shift is not None:
        smax = smax + shift
    m_new = jnp.maximum(m_prev, smax)
    a = jnp.exp2(m_prev - m_new)
    p = jnp.exp2(s - (m_new if shift is None else m_new - shift))
    if l_ref is not None:
        l_ref[...] = a * l_ref[...] + jnp.sum(p, axis=0, keepdims=True)
    acc_ref[...] = a * acc_ref[...] + _dot(vj, p.astype(BF16))
    m_ref[...] = m_new


def _chain_scratch(n, dv):
    return [pltpu.VMEM((1, ATT_BLK), F32)] * (2 * n) + [pltpu.VMEM((dv, ATT_BLK), F32)] * n


def _init_chains(state):
    n = len(state) // 3
    m_refs, l_refs, acc_refs = state[:n], state[n:2 * n], state[2 * n:]
    for m_ref, l_ref, acc_ref in zip(m_refs, l_refs, acc_refs):
        m_ref[...] = jnp.full(m_ref.shape, NEG_BIG, F32)
        l_ref[...] = jnp.zeros(l_ref.shape, F32)
        acc_ref[...] = jnp.zeros(acc_ref.shape, F32)
    return m_refs, l_refs, acc_refs


def _pipelined_tiles(qi, logits, next_logits, softmax_pv):
    a, b, c = 0, 1, 2

    @pl.when(qi == 0)
    def _():
        logits(0, a, True)
        next_logits(c)
        softmax_pv(0, a, "diag")

    @pl.when(qi == 1)
    def _():
        logits(1, a, True)
        softmax_pv(0, c, "near")
        next_logits(c)
        softmax_pv(1, a, "diag")

    @pl.when(qi >= 2)
    def _():
        logits(1, a)
        softmax_pv(0, c, "far")
        nfar = qi - 2

        def body(i, carry):
            j = 2 * i + 1
            logits(j + 1, b)
            softmax_pv(j, a, "far")
            logits(j + 2, a)
            softmax_pv(j + 1, b, "far")
            return carry

        lax.fori_loop(0, lax.shift_right_logical(nfar, 1), body, 0)
        odd = lax.rem(nfar, 2) == 1

        @pl.when(jnp.logical_not(odd))
        def _():
            logits(qi, b, True)
            softmax_pv(qi - 1, a, "near")
            next_logits(c)
            softmax_pv(qi, b, "diag")

        @pl.when(odd)
        def _():
            logits(qi - 1, b)
            softmax_pv(qi - 2, a, "far")
            logits(qi, a, True)
            softmax_pv(qi - 1, b, "near")
            next_logits(c)
            softmax_pv(qi, a, "diag")


def _mla_attn_kernel(qT_ref, qTn_ref, k_ref, vT_ref, o_ref, sa_ref, sb_ref, sc_ref, *state,
                     ta, hp):
    qi = pl.program_id(2)
    m_refs, l_refs, acc_refs = _init_chains(state)
    s_refs = (sa_ref, sb_ref, sc_ref)

    nb = ta // ATT_BLK
    blk = ATT_BLK

    def logits(j, buf, diagonal=False):
        off = pl.multiple_of(j * ta, ta)
        for h in range(hp):
            if not diagonal:
                s_refs[buf][h] = _dot(k_ref[0, h, pl.ds(off, ta), :], qT_ref[0, h])
                continue
            for qs in range(nb):
                nk, q0 = (qs + 1) * blk, qs * blk
                s_refs[buf][h, :nk, q0:q0 + blk] = _dot(k_ref[0, h, pl.ds(off, nk), :],
                                                        qT_ref[0, h, :, q0:q0 + blk])

    def next_logits(buf):
        for h in range(hp):
            s_refs[buf][h] = _dot(k_ref[0, h, :ta, :], qTn_ref[0, h])

    def softmax_pv(j, buf, kind):
        off = pl.multiple_of(j * ta, ta)
        diagonal = kind == "diag"
        for h in range(hp):
            for qs in range(nb):
                nk = (qs + 1) * blk if diagonal else ta
                s = s_refs[buf][h, :nk, qs * blk:(qs + 1) * blk]
                if diagonal:
                    kpos = lax.broadcasted_iota(jnp.int32, (nk, blk), 0)
                    qpos = lax.broadcasted_iota(jnp.int32, (nk, blk), 1) + qs * blk
                    s = jnp.where(kpos <= qpos, s, NEG_BIG)
                c = h * nb + qs
                _softmax_step(s, vT_ref[0, h, :, pl.ds(off, nk)], m_refs[c], l_refs[c], acc_refs[c])

    _pipelined_tiles(qi, logits, next_logits, softmax_pv)
    for h in range(hp):
        for qs in range(nb):
            c = h * nb + qs
            out = acc_refs[c][...] * (1.0 / l_refs[c][...])
            o_ref[0, qs * blk:(qs + 1) * blk, h * MLA_V:(h + 1) * MLA_V] = out.T.astype(BF16)


def _mla_attn(qT, k, vT, ta, hp):
    b, hd, _, s = qT.shape
    nq = s // ta
    return pl.pallas_call(
        functools.partial(_mla_attn_kernel, ta=ta, hp=hp),
        grid=(b, hd // hp, nq),
        in_specs=[
            pl.BlockSpec((1, hp, MLA_QK, ta), lambda bi, h, i: (bi, h, 0, i)),
            pl.BlockSpec((1, hp, MLA_QK, ta), lambda bi, h, i: (bi, h, 0, jnp.minimum(i + 1, nq - 1))),
            pl.BlockSpec((1, hp, s, MLA_QK), lambda bi, h, i: (bi, h, 0, 0)),
            pl.BlockSpec((1, hp, MLA_V, s), lambda bi, h, i: (bi, h, 0, 0)),
        ],
        out_specs=pl.BlockSpec((1, ta, hp * MLA_V), lambda bi, h, i: (bi, i, h)),
        out_shape=jax.ShapeDtypeStruct((b, s, hd * MLA_V), BF16),
        scratch_shapes=([pltpu.VMEM((hp, ta, ta), F32)] * 3
                        + _chain_scratch(hp * ta // ATT_BLK, MLA_V)),
        compiler_params=pltpu.CompilerParams(
            dimension_semantics=("arbitrary",) * 3, vmem_limit_bytes=VMEM_LIMIT),
        name="mla_attn",
    )(qT, qT, k, vT)


def _diff_attn_kernel(cfar_ref, qT_ref, qTn_ref, k_ref, vT_ref, bias_ref, lamp_ref, g_ref, *rest,
                      ta, hp, lambda_init, ncast):
    cast_in, o_ref, cast_out = rest[:ncast], rest[ncast], rest[ncast + 1:2 * ncast + 1]
    s_refs = rest[2 * ncast + 1:2 * ncast + 4]
    state = rest[2 * ncast + 4:]
    hg = pl.program_id(1)
    qi = pl.program_id(2)
    for src_ref, dst_ref in zip(cast_in, cast_out):
        dst_ref[...] = src_ref[...].astype(BF16)
    m_refs, l_refs, acc_refs = _init_chains(state)

    nb = ta // ATT_BLK
    blk = ATT_BLK

    def logits(j, buf, diagonal=False):
        off = pl.multiple_of(j * ta, ta)
        for h in range(hp):
            for c in range(2):
                if not diagonal:
                    s_refs[buf][2 * h + c] = _dot(k_ref[0, h, pl.ds(off, ta), :],
                                                  qT_ref[0, h, c])
                    continue
                for qb in range(nb):
                    nk, q0 = (qb + 1) * blk, qb * blk
                    s_refs[buf][2 * h + c, :nk, q0:q0 + blk] = _dot(
                        k_ref[0, h, pl.ds(off, nk), :], qT_ref[0, h, c, :, q0:q0 + blk])

    def next_logits(buf):
        for h in range(hp):
            for c in range(2):
                s_refs[buf][2 * h + c] = _dot(k_ref[0, h, :ta, :], qTn_ref[0, h, c])

    def softmax_pv(j, buf, kind):
        off = pl.multiple_of(j * ta, ta)
        diagonal = kind == "diag"
        for h in range(hp):
            for qb in range(nb):
                nk = (qb + 1) * blk if diagonal else ta
                vj = vT_ref[0, h, :, pl.ds(off, nk)]
                for c in range(2):
                    s = s_refs[buf][2 * h + c, :nk, qb * blk:(qb + 1) * blk]
                    i = (2 * h + c) * nb + qb
                    if kind == "far" or (kind == "near" and qb * blk + 1 >= MAX_DISTANCE):
                        _softmax_step(s, vj, m_refs[i], None, acc_refs[i],
                                      shift=cfar_ref[hg * hp + h])
                    else:
                        bias = bias_ref[h, 0 if diagonal else 1, :nk, qb * blk:(qb + 1) * blk]
                        _softmax_step(s + bias, vj, m_refs[i], None, acc_refs[i])

    _pipelined_tiles(qi, logits, next_logits, softmax_pv)

    lp = lamp_ref[...]
    lam = (jnp.exp(jnp.sum(lp[0:1] * lp[1:2], axis=-1, keepdims=True))
           - jnp.exp(jnp.sum(lp[2:3] * lp[3:4], axis=-1, keepdims=True)) + lambda_init)
    for h in range(hp):
        for qb in range(nb):
            a0 = acc_refs[(2 * h) * nb + qb][...]
            a1 = acc_refs[(2 * h + 1) * nb + qb][...]
            out = (a0[:DIFF_V] * (1.0 / a0[DIFF_V:DIFF_V + 1])
                   - lam * (a1[:DIFF_V] * (1.0 / a1[DIFF_V:DIFF_V + 1])))
            ms = jnp.mean(out * out, axis=0, keepdims=True)
            out = out * lax.rsqrt(ms + 1e-5) * g_ref[...] * (1.0 - lambda_init)
            o_ref[0, qb * blk:(qb + 1) * blk, h * DIFF_V:(h + 1) * DIFF_V] = out.T.astype(BF16)


def _cast_rows(rows_total, nsteps):
    rows = rows_total // nsteps
    return rows if rows * nsteps == rows_total and rows % BF16_ROWS == 0 else 0


def _diff_attn(cfar, qT, k, vT, bias_tiles, lam_params, g_col, ta, hp, lambda_init,
               cast=(), cast_part=(0, 1)):
    b, hd, _, _, s = qT.shape
    nq, nhg = s // ta, hd // hp
    nsteps = b * nhg * nq
    part, nparts = cast_part
    crows = [_cast_rows(a.shape[0] // nparts, nsteps) for a in cast]
    assert all(crows), "cast tables must split into whole tiles per grid step"
    step = lambda bi, h, i: (bi * nhg + h) * nq + i
    grid_spec = pltpu.PrefetchScalarGridSpec(
        num_scalar_prefetch=1,
        grid=(b, nhg, nq),
        in_specs=[
            pl.BlockSpec((1, hp, 2, LANES, ta), lambda bi, h, i, cf: (bi, h, 0, 0, i)),
            pl.BlockSpec((1, hp, 2, LANES, ta),
                         lambda bi, h, i, cf: (bi, h, 0, 0, jnp.minimum(i + 1, nq - 1))),
            pl.BlockSpec((1, hp, s, LANES), lambda bi, h, i, cf: (bi, h, 0, 0)),
            pl.BlockSpec((1, hp, DIFF_VA, s), lambda bi, h, i, cf: (bi, h, 0, 0)),
            pl.BlockSpec((hp, 2, ta, ta), lambda bi, h, i, cf: (h, 0, 0, 0)),
            pl.BlockSpec(lam_params.shape, lambda bi, h, i, cf: (0, 0)),
            pl.BlockSpec(g_col.shape, lambda bi, h, i, cf: (0, 0)),
        ] + [pl.BlockSpec((r, a.shape[1]), lambda bi, h, i, cf: (part * nsteps + step(bi, h, i), 0))
             for a, r in zip(cast, crows)],
        out_specs=[pl.BlockSpec((1, ta, hp * DIFF_V), lambda bi, h, i, cf: (bi, i, h))]
        + [pl.BlockSpec((r, a.shape[1]), lambda bi, h, i, cf: (step(bi, h, i), 0))
           for a, r in zip(cast, crows)],
        scratch_shapes=([pltpu.VMEM((2 * hp, ta, ta), F32)] * 3
                        + _chain_scratch(2 * hp * ta // ATT_BLK, DIFF_VA)),
    )
    res = pl.pallas_call(
        functools.partial(_diff_attn_kernel, ta=ta, hp=hp, lambda_init=lambda_init,
                          ncast=len(cast)),
        grid_spec=grid_spec,
        out_shape=[jax.ShapeDtypeStruct((b, s, hd * DIFF_V), BF16)]
        + [jax.ShapeDtypeStruct((a.shape[0] // nparts, a.shape[1]), BF16) for a in cast],
        compiler_params=pltpu.CompilerParams(
            dimension_semantics=("arbitrary",) * 3, vmem_limit_bytes=VMEM_LIMIT),
        name="diff_attn",
    )(cfar, qT, qT, k, vT, bias_tiles, lam_params, g_col, *cast)
    return res[0], res[1:]


def _split_bf16(v):
    hi = v.astype(BF16)
    return hi, (v - hi.astype(F32)).astype(BF16)


def _oproj_route_kernel(am_ref, ad_ref, x_ref, wo_ref, g_ref, b_ref, rhl_ref, rh_ref,
                        o_ref, ob_ref, r_ref):
    mix = _dot(am_ref[...], wo_ref[0, :MLA_W, :]) + _dot(ad_ref[...], wo_ref[0, MLA_W:, :])
    y = _layer_norm_rows(DN_ALPHA * x_ref[...] + mix, g_ref[...], b_ref[...])
    yhi, ylo = _split_bf16(y)
    nt = (((1,), (1,)), ((), ()))
    both = lax.dot_general(rhl_ref[...], yhi, nt, preferred_element_type=F32)
    logits = (both[:ROUTER_ROWS] + both[ROUTER_ROWS:]
              + lax.dot_general(rh_ref[...], ylo, nt, preferred_element_type=F32))
    row = lax.broadcasted_iota(jnp.int32, logits.shape, 0)
    logits = jnp.where(row < N_EXPERTS, logits, NEG_BIG)
    m1 = jnp.max(logits, axis=0, keepdims=True)
    i1 = jnp.min(jnp.where(logits == m1, row, ROUTER_ROWS), axis=0, keepdims=True)
    rest_l = jnp.where(row == i1, NEG_BIG, logits)
    m2 = jnp.max(rest_l, axis=0, keepdims=True)
    i2 = jnp.min(jnp.where(rest_l == m2, row, ROUTER_ROWS), axis=0, keepdims=True)
    e2 = jnp.exp(m2 - m1)
    g1 = 1.0 / (1.0 + e2)
    g2 = e2 * g1
    row8 = lax.broadcasted_iota(jnp.int32, r_ref.shape, 0)
    rec = jnp.where(row8 == 0, i1.astype(F32),
                    jnp.where(row8 == 1, i2.astype(F32),
                              jnp.where(row8 == 2, g1, jnp.where(row8 == 3, g2, 0.0))))
    o_ref[...] = y
    ob_ref[...] = yhi
    r_ref[...] = rec


def _oproj_route(a_mla, a_diff, x2d, l, wo, g, b, rhl, rh, tm):
    t, d = x2d.shape
    row = lambda w: pl.BlockSpec((tm, w), lambda i: (i, 0))
    full = lambda a: pl.BlockSpec(a.shape, lambda i: (0,) * a.ndim)
    return pl.pallas_call(
        _oproj_route_kernel,
        grid=(t // tm,),
        in_specs=[row(MLA_W), row(DIFF_W), row(d),
                  pl.BlockSpec((1,) + wo.shape[1:], lambda i: (l, 0, 0)),
                  full(g), full(b), full(rhl), full(rh)],
        out_specs=[row(d), row(d), pl.BlockSpec((REC_ROWS, tm), lambda i: (0, i))],
        out_shape=[jax.ShapeDtypeStruct((t, d), F32), jax.ShapeDtypeStruct((t, d), BF16),
                   jax.ShapeDtypeStruct((REC_ROWS, t), F32)],
        compiler_params=pltpu.CompilerParams(
            dimension_semantics=("arbitrary",), vmem_limit_bytes=VMEM_LIMIT),
        name="oproj_route",
    )(a_mla, a_diff, x2d, wo, g, b, rhl, rh)


def _oproj_ffn_kernel(am_ref, ad_ref, x_ref, wo_ref, g1_ref, b1_ref, wg_ref, wu_ref, wd_ref,
                      g2_ref, b2_ref, o_ref):
    mix = _dot(am_ref[...], wo_ref[0, :MLA_W, :]) + _dot(ad_ref[...], wo_ref[0, MLA_W:, :])
    x1 = _layer_norm_rows(DN_ALPHA * x_ref[...] + mix, g1_ref[...], b1_ref[...])
    xb = x1.astype(BF16)
    gate = _dot(xb, wg_ref[0])
    up = _dot(xb, wu_ref[0])
    hid = (gate * jax.nn.sigmoid(gate) * up).astype(BF16)
    f = _dot(hid, wd_ref[0])
    o_ref[...] = _layer_norm_rows(DN_ALPHA * x1 + f, g2_ref[...], b2_ref[...])


def _oproj_ffn(a_mla, a_diff, x2d, l, wo, g1, b1, li, wg, wu, wd, g2, b2, tm):
    t, d = x2d.shape
    row = lambda w: pl.BlockSpec((tm, w), lambda i: (i, 0))
    layer = lambda a, k: pl.BlockSpec((1,) + a.shape[1:], lambda i: (k,) + (0,) * (a.ndim - 1),
                                      pipeline_mode=pl.Buffered(1))
    vec = lambda a: pl.BlockSpec(a.shape, lambda i: (0, 0))
    return pl.pallas_call(
        _oproj_ffn_kernel,
        grid=(t // tm,),
        in_specs=[row(MLA_W), row(DIFF_W), row(d), layer(wo, l), vec(g1), vec(b1),
                  layer(wg, li), layer(wu, li), layer(wd, li), vec(g2), vec(b2)],
        out_specs=row(d),
        out_shape=jax.ShapeDtypeStruct((t, d), F32),
        compiler_params=pltpu.CompilerParams(
            dimension_semantics=("arbitrary",), vmem_limit_bytes=VMEM_LIMIT),
        name="oproj_ffn",
    )(a_mla, a_diff, x2d, wo, g1, b1, wg, wu, wd, g2, b2)


def _dispatch_kernel(ue_ref, uoff_ref, unc_ref, x_ref, post_ref, o_ref, *, rows, unit, kc):
    i = pl.program_id(0)
    t = x_ref.shape[0]
    ups = rows // unit

    def chunk(u, c):
        nominal = uoff_ref[u] + c * kc
        off = pl.multiple_of(jnp.minimum(nominal, t - kc), LANES)
        row_pos = u * unit + lax.broadcasted_iota(jnp.int32, (unit, 1), 0)
        dest = post_ref[pl.ds(ue_ref[u], 1), pl.ds(off, kc)]
        tok = off + lax.broadcasted_iota(jnp.int32, (1, kc), 1)
        dest = jnp.where(tok >= nominal, dest, -1)
        sel = jnp.where(dest == row_pos, 1.0, 0.0).astype(BF16)
        return _dot(sel, x_ref[pl.ds(off, kc), :]).astype(BF16)

    for r in range(ups):
        o_ref[r * unit:(r + 1) * unit, :] = chunk(i * ups + r, 0)

    for r in range(ups):
        def more(c, carry, r=r):
            o_ref[r * unit:(r + 1) * unit, :] += chunk(i * ups + r, c)
            return carry

        lax.fori_loop(1, unc_ref[i * ups + r], more, 0)


def _dispatch(unit_e, uoff, unc, xb, post, n_rows, rows, unit, kc):
    t, d = xb.shape
    const = lambda shape: pl.BlockSpec(shape, lambda i, *_: (0,) * len(shape),
                                       pipeline_mode=pl.Buffered(1))
    grid_spec = pltpu.PrefetchScalarGridSpec(
        num_scalar_prefetch=3,
        grid=(n_rows // rows,),
        in_specs=[const(xb.shape), const(post.shape)],
        out_specs=pl.BlockSpec((rows, d), lambda i, *_: (i, 0)),
    )
    return pl.pallas_call(
        functools.partial(_dispatch_kernel, rows=rows, unit=unit, kc=kc),
        grid_spec=grid_spec,
        out_shape=jax.ShapeDtypeStruct((n_rows, d), BF16),
        compiler_params=pltpu.CompilerParams(
            dimension_semantics=("arbitrary",), vmem_limit_bytes=VMEM_LIMIT),
        name="dispatch",
    )(unit_e, uoff, unc, xb, post)


def _gffn_kernel(te_ref, tv_ref, x_ref, wg_ref, wu_ref, wd_ref, o_ref, acc_ref, *, nf):
    i = pl.program_id(0)
    f = pl.program_id(1)
    status = tv_ref[i]
    tm = x_ref.shape[0]
    half = tm // 2

    def swiglu_rows(rows):
        xb = x_ref[:rows, :]
        gate = _dot(xb, wg_ref[0, 0])
        up = _dot(xb, wu_ref[0, 0])
        hid = (gate * jax.nn.sigmoid(gate) * up).astype(BF16)
        part = _dot(hid, wd_ref[0, 0])

        @pl.when(f == 0)
        def _():
            acc_ref[:rows, :] = part

        @pl.when(f > 0)
        def _():
            acc_ref[:rows, :] += part

    @pl.when(status == 2)
    def _():
        swiglu_rows(tm)

    @pl.when(status == 1)
    def _():
        swiglu_rows(half)

    @pl.when(jnp.logical_and(f == nf - 1, status == 2))
    def _():
        o_ref[...] = acc_ref[...].astype(BF16)

    @pl.when(jnp.logical_and(f == nf - 1, status == 1))
    def _():
        o_ref[:half, :] = acc_ref[:half, :].astype(BF16)
        o_ref[half:, :] = jnp.zeros((tm - half, o_ref.shape[1]), BF16)

    @pl.when(jnp.logical_and(f == nf - 1, status == 0))
    def _():
        o_ref[...] = jnp.zeros(o_ref.shape, BF16)


def _gffn(tile_e, tile_v, xs, li, wg, wu, wd, tm, fc):
    p, d = xs.shape
    nt = p // tm
    nf = wg.shape[3] // fc
    fidx = lambda i, f, te, tv: jnp.where(tv[i] > 0, f, nf - 1)
    grid_spec = pltpu.PrefetchScalarGridSpec(
        num_scalar_prefetch=2,
        grid=(nt, nf),
        in_specs=[
            pl.BlockSpec((tm, d), lambda i, f, te, tv: (i, 0)),
            pl.BlockSpec((1, 1, d, fc), lambda i, f, te, tv: (li, te[i], 0, fidx(i, f, te, tv))),
            pl.BlockSpec((1, 1, d, fc), lambda i, f, te, tv: (li, te[i], 0, fidx(i, f, te, tv))),
            pl.BlockSpec((1, 1, fc, d), lambda i, f, te, tv: (li, te[i], fidx(i, f, te, tv), 0)),
        ],
        out_specs=pl.BlockSpec((tm, d), lambda i, f, te, tv: (i, 0)),
        scratch_shapes=[pltpu.VMEM((tm, d), F32)],
    )
    return pl.pallas_call(
        functools.partial(_gffn_kernel, nf=nf),
        grid_spec=grid_spec,
        out_shape=jax.ShapeDtypeStruct((p, d), BF16),
        compiler_params=pltpu.CompilerParams(
            dimension_semantics=("arbitrary", "arbitrary"), vmem_limit_bytes=VMEM_LIMIT),
        name="grouped_ffn",
    )(tile_e, tile_v, xs, wg, wu, wd)


def _combine_kernel(ral_ref, rlo_ref, rhi_ref, x_ref, rec_ref, g_ref, b_ref, ys_ref, o_ref,
                    win_ref, xwin_ref, acc_ref, sems, xsem, *, w):
    j = pl.program_id(0)
    nj = pl.num_programs(0)
    slot = lax.rem(j, 2)
    p_rows = ys_ref.shape[0]

    def win_copy(jj, e, sl):
        r = pl.multiple_of(ral_ref[jj * N_EXPERTS + e], BF16_ROWS)
        return pltpu.make_async_copy(ys_ref.at[pl.ds(r, w)], win_ref.at[sl, pl.ds(e * w, w)],
                                     sems.at[sl, e])

    @pl.when(j == 0)
    def _():
        for e in range(N_EXPERTS):
            win_copy(0, e, 0).start()

    @pl.when(j + 1 < nj)
    def _():
        for e in range(N_EXPERTS):
            win_copy(j + 1, e, 1 - slot).start()

    rec = rec_ref[...]
    p1, p2, g1, g2 = rec[:, 0:1], rec[:, 1:2], rec[:, 2:3], rec[:, 3:4]

    def sel(base, lo, hi):
        rows = base + lax.broadcasted_iota(jnp.int32, (1, w), 1)
        rp = jnp.where((rows >= lo) & (rows < hi), rows, -1).astype(F32)
        return (jnp.where(p1 == rp, g1, 0.0) + jnp.where(p2 == rp, g2, 0.0)).astype(BF16)

    lane = lax.broadcasted_iota(jnp.int32, (1, N_EXPERTS * w), 1)
    rp = jnp.full((1, N_EXPERTS * w), -1, jnp.int32)
    for e in range(N_EXPERTS):
        win_copy(j, e, slot).wait()
        k = j * N_EXPERTS + e
        rows = ral_ref[k] + lane - e * w
        mine = ((lane >= e * w) & (lane < (e + 1) * w)
                & (rows >= rlo_ref[k]) & (rows < rhi_ref[k]))
        rp = jnp.where(mine, rows, rp)
    rp = rp.astype(F32)
    tm = rec.shape[0]
    step = min(tm, 256)
    for r0 in range(0, tm, step):
        rows = slice(r0, r0 + step)
        sel_rows = (jnp.where(p1[rows] == rp, g1[rows], 0.0)
                    + jnp.where(p2[rows] == rp, g2[rows], 0.0)).astype(BF16)
        acc_ref[rows, :] = _dot(sel_rows, win_ref[slot])

    for e in range(N_EXPERTS):
        k = j * N_EXPERTS + e
        nwin = (rhi_ref[k] - ral_ref[k] + (w - 1)) // w

        def extra(wi, carry, k=k):
            nominal = ral_ref[k] + wi * w
            base = pl.multiple_of(jnp.minimum(nominal, p_rows - w), BF16_ROWS)
            cp = pltpu.make_async_copy(ys_ref.at[pl.ds(base, w)], xwin_ref, xsem)
            cp.start()
            cp.wait()
            acc_ref[...] += _dot(sel(base, nominal, rhi_ref[k]), xwin_ref[...])
            return carry

        lax.fori_loop(1, nwin, extra, 0)

    o_ref[...] = _layer_norm_rows(DN_ALPHA * x_ref[...] + acc_ref[...], g_ref[...], b_ref[...])


def _combine(ral, rlo, rhi, x2d, rec2, g, b, ys, tm, w):
    t, d = x2d.shape
    row = lambda c: pl.BlockSpec((tm, c), lambda i, *_: (i, 0))
    full = lambda shape: pl.BlockSpec(shape, lambda i, *_: (0,) * len(shape))
    grid_spec = pltpu.PrefetchScalarGridSpec(
        num_scalar_prefetch=3,
        grid=(t // tm,),
        in_specs=[row(d), row(rec2.shape[1]), full(g.shape), full(b.shape),
                  pl.BlockSpec(memory_space=pl.ANY)],
        out_specs=row(d),
        scratch_shapes=[pltpu.VMEM((2, N_EXPERTS * w, d), BF16), pltpu.VMEM((w, d), BF16),
                        pltpu.VMEM((tm, d), F32), pltpu.SemaphoreType.DMA((2, N_EXPERTS)),
                        pltpu.SemaphoreType.DMA(())],
    )
    return pl.pallas_call(
        functools.partial(_combine_kernel, w=w),
        grid_spec=grid_spec,
        out_shape=jax.ShapeDtypeStruct((t, d), F32),
        compiler_params=pltpu.CompilerParams(
            dimension_semantics=("arbitrary",), vmem_limit_bytes=VMEM_LIMIT),
        name="combine",
    )(ral, rlo, rhi, x2d, rec2, g, b, ys)


def _route_plan(rec, nt, tm_e, unit, kc, tm_c, w):
    t = rec.shape[1]
    i32 = jnp.int32
    ar = jnp.arange(N_EXPERTS, dtype=i32)
    m1 = (rec[0].astype(i32)[:, None] == ar[None, :]).astype(i32)
    m2 = (rec[1].astype(i32)[:, None] == ar[None, :]).astype(i32)
    m = m1 + m2
    cinc = jnp.cumsum(m, axis=0)
    counts = cinc[-1]
    padded = ((counts + tm_e - 1) // tm_e) * tm_e
    ends = jnp.cumsum(padded)
    starts = ends - padded
    posmat = starts[None, :] + (cinc - m)
    pos1 = jnp.sum(m1 * posmat, axis=1)
    pos2 = jnp.sum(m2 * posmat, axis=1)
    post = jnp.where(m.T > 0, posmat.T, -1)
    rec2 = jnp.stack([pos1.astype(F32), pos2.astype(F32), rec[2], rec[3]]
                     + [jnp.zeros((t,), F32)] * 4, axis=1)
    tile_start = jnp.arange(nt, dtype=i32) * tm_e
    tile_e = jnp.sum((tile_start[:, None] >= ends[None, :]).astype(i32), axis=1)
    tile_v = tile_e < N_EXPERTS
    last_e = jnp.max(jnp.where(tile_v, tile_e, 0))
    tile_e = jnp.where(tile_v, tile_e, last_e)
    upt = tm_e // unit
    nu = nt * upt
    oh = (jnp.repeat(tile_e, upt)[:, None] == ar[None, :]).astype(i32)
    q0 = jnp.arange(nu, dtype=i32) * unit - jnp.sum(oh * starts[None, :], axis=1)
    q1 = jnp.minimum(q0 + unit, jnp.sum(oh * counts[None, :], axis=1))
    has = jnp.repeat(tile_v, upt) & (q1 > q0)
    cb = jnp.concatenate([jnp.zeros((1, N_EXPERTS), i32), cinc[LANES - 1::LANES]], axis=0)
    cbu = jnp.sum(oh[:, None, :] * cb[None, :, :], axis=2)
    jlo = jnp.sum((cbu[:, 1:] <= q0[:, None]).astype(i32), axis=1)
    jend = jnp.sum((cbu[:, :-1] < q1[:, None]).astype(i32), axis=1)
    uoff = jnp.where(has, jlo * LANES, 0)
    unc = jnp.where(has, (jend * LANES - uoff + kc - 1) // kc, 0)
    cbt = jnp.concatenate([jnp.zeros((1, N_EXPERTS), i32), cinc[tm_c - 1::tm_c]], axis=0)
    r_lo = starts[None, :] + cbt[:-1]
    r_hi = starts[None, :] + cbt[1:]
    ral = jnp.minimum((r_lo // BF16_ROWS) * BF16_ROWS, nt * tm_e - w)
    oh_t = (tile_e[:, None] == ar[None, :]).astype(i32)
    tile_rows = (jnp.sum(oh_t * counts[None, :], axis=1)
                 - (tile_start - jnp.sum(oh_t * starts[None, :], axis=1)))
    tile_status = jnp.where(tile_v, jnp.where(tile_rows > tm_e // 2, 2, 1), 0)
    return (rec2, post, tile_e, tile_status.astype(i32), uoff, unc,
            ral.reshape(-1), r_lo.reshape(-1), r_hi.reshape(-1))


def _moe(x2d, xb, rec, li, wg, wu, wd, g, b, tm_c, tm_e, fc, unit, kc, w):
    t, d = x2d.shape
    nt = (2 * t) // tm_e + N_EXPERTS
    rec2, post, tile_e, tile_v, uoff, unc, ral, rlo, rhi = _route_plan(
        rec, nt, tm_e, unit, kc, tm_c, w)
    n_rows = nt * tm_e
    rows_d = 2 * tm_e if n_rows % (2 * tm_e) == 0 else tm_e
    xs = _dispatch(jnp.repeat(tile_e, tm_e // unit), uoff, unc, xb, post, n_rows, rows_d, unit, kc)
    ys = _gffn(tile_e, tile_v, xs, li, wg, wu, wd, tm_e, fc)
    return _combine(ral, rlo, rhi, x2d, rec2, g, b, ys, tm_c, w)


def _rotate_half_cols(w):
    half = w.shape[-1] // 2
    return jnp.concatenate([-w[..., half:], w[..., :half]], axis=-1)


def _t5_bucket(n):
    max_exact = NUM_BUCKETS // 2
    nf = jnp.maximum(n, 1).astype(F32)
    large = max_exact + (jnp.log(nf / max_exact) / math.log(MAX_DISTANCE / max_exact)
                         * (NUM_BUCKETS - max_exact)).astype(jnp.int32)
    large = jnp.minimum(large, NUM_BUCKETS - 1)
    return jnp.where(n < max_exact, n, large)


def _pick(n, prefs):
    for p in prefs:
        if n % p == 0:
            return p
    raise ValueError(f"no tile for extent {n}")


def kernel(x, rel_bias, w_in, mla_q_norm, mla_kv_norm, mla_w_uq, mla_w_uk, mla_w_uv, diff_lambda, diff_norm, w_o, ln1_g, ln1_b, ln2_g, ln2_b, ffn_w_gate, ffn_w_up, ffn_w_down, moe_router, moe_w_gate, moe_w_up, moe_w_down):
    b, s, d = x.shape
    t = b * s
    assert d == D_MODEL
    ta = _pick(s, (512, 256))
    assert ta % ATT_BLK == 0
    assert s >= 2 * ta or s == ta
    tm_proj = _pick(s, (512, 256, 128))
    tm_tok = _pick(t, (512, 256, 128))
    tm_e = _pick(2 * t, (512, 256))
    tm_c = _pick(t, (512, 256))
    unit, kc, wrows = 128, 768, 224
    assert tm_e % unit == 0 and t >= kc and t % LANES == 0
    fc = D_FF_EXPERT // 2

    pos = jnp.arange(s, dtype=F32)
    inv = 1.0 / (ROPE_THETA ** (jnp.arange(0, MLA_ROPE, 2, dtype=F32) / MLA_ROPE))
    ang = pos[:, None] * inv[None, :]
    cos4 = jnp.tile(jnp.cos(ang), (1, 2 * MLA_HEADS))
    sin4 = jnp.tile(jnp.sin(ang), (1, 2 * MLA_HEADS))
    rb = rel_bias.astype(F32)
    period = 2 * ta
    dd = jnp.arange(period)
    bucket = _t5_bucket(dd)
    bd = jnp.zeros((DIFF_HEADS, period), F32)
    for bk in range(NUM_BUCKETS):
        bd = bd + jnp.where((bucket == bk)[None], rb[bk][:, None], 0.0)
    bd = bd * LOG2E

    def toeplitz(w):
        rows = jnp.tile(w, (1, ta))[:, :ta * (period - 1)].reshape(DIFF_HEADS, ta, period - 1)
        return rows[:, :, :ta]

    bias_tiles = jnp.stack(
        [toeplitz(jnp.where(dd[None] < ta, bd, NEG_BIG)),
         toeplitz(jnp.roll(bd, -ta, axis=1))], axis=1)
    assert ta + 1 >= MAX_DISTANCE
    cfar = rb[NUM_BUCKETS - 1] * LOG2E
    n_moe = moe_w_gate.shape[0]
    moe_tables = (moe_w_gate.reshape(-1, D_FF_EXPERT), moe_w_up.reshape(-1, D_FF_EXPERT),
                  moe_w_down.reshape(-1, d))

    kr_w = w_in[:, :, _KR0:_KR0 + MLA_ROPE]
    win = jnp.concatenate([w_in[:, :, :_KR0 + MLA_ROPE], _rotate_half_cols(kr_w),
                           w_in[:, :, _KR0 + MLA_ROPE:]], axis=2).astype(BF16)
    nl = w_in.shape[0]
    uq = mla_w_uq.reshape(nl, MLA_Q_RANK, MLA_HEADS, MLA_QK)
    uq_r = uq[..., MLA_NOPE:]
    wuq = jnp.concatenate([uq[..., :MLA_NOPE].reshape(nl, MLA_Q_RANK, -1),
                           uq_r.reshape(nl, MLA_Q_RANK, -1),
                           _rotate_half_cols(uq_r).reshape(nl, MLA_Q_RANK, -1)], axis=2).astype(BF16)
    wukv = jnp.concatenate([mla_w_uk, mla_w_uv], axis=2).astype(BF16)
    qn, kvn = mla_q_norm[:, None, :], mla_kv_norm[:, None, :]
    wo = w_o.astype(BF16)
    ffn_wg, ffn_wu, ffn_wd = (w.astype(BF16) for w in (ffn_w_gate, ffn_w_up, ffn_w_down))
    rpad = jnp.pad(jnp.swapaxes(moe_router.astype(F32), 1, 2),
                   ((0, 0), (0, ROUTER_ROWS - N_EXPERTS), (0, 0)))
    rhi = rpad.astype(BF16)
    rhl = jnp.concatenate([rhi, (rpad - rhi.astype(F32)).astype(BF16)], axis=1)

    xc = x
    for l in range(DEPTH):
        lambda_init = 0.8 - 0.6 * math.exp(-0.3 * l)
        qT, k, vT, dqT, dk, dvT = _proj(xc, l, win, qn, kvn, wuq, wukv, cos4, sin4, tm_proj)
        a_mla = _mla_attn(qT, k, vT, ta, MLA_HP).reshape(t, MLA_W)
        i = l // 2
        dense = l % 2 == 0
        a_diff, casts = _diff_attn(cfar, dqT, dk, dvT, bias_tiles, diff_lambda[l].astype(F32),
                                   diff_norm[l].astype(F32)[:, None], ta, DIFF_HP, lambda_init,
                                   cast=moe_tables if dense and i < n_moe else (),
                                   cast_part=(i, n_moe))
        a_diff = a_diff.reshape(t, DIFF_W)
        if casts:
            moe_wg = casts[0].reshape(1, N_EXPERTS, d, D_FF_EXPERT)
            moe_wu = casts[1].reshape(1, N_EXPERTS, d, D_FF_EXPERT)
            moe_wd = casts[2].reshape(1, N_EXPERTS, D_FF_EXPERT, d)
        x2d = xc.reshape(t, d)
        g1, b1 = ln1_g[l][None, :], ln1_b[l][None, :]
        g2, b2 = ln2_g[l][None, :], ln2_b[l][None, :]
        if dense:
            x2 = _oproj_ffn(a_mla, a_diff, x2d, l, wo, g1, b1, i, ffn_wg, ffn_wu, ffn_wd,
                            g2, b2, tm_tok)
        else:
            x1, x1b, rec = _oproj_route(a_mla, a_diff, x2d, l, wo, g1, b1, rhl[i], rhi[i], tm_tok)
            x2 = _moe(x1, x1b, rec, 0, moe_wg, moe_wu, moe_wd, g2, b2,
                      tm_c, tm_e, fc, unit, kc, wrows)
        xc = x2.reshape(b, s, d)
    return xc
```

```python
import functools
import math

import jax
import jax.numpy as jnp
from jax import lax
from jax.experimental import pallas as pl
from jax.experimental.pallas import tpu as pltpu

D_MODEL = 1024
DEPTH = 4
MLA_HEADS = 4
MLA_NOPE = 128
MLA_ROPE = 64
MLA_V = 128
MLA_Q_RANK = 384
MLA_KV_RANK = 256
ROPE_THETA = 10000.0
DIFF_HEADS = 4
DIFF_QK = 64
DIFF_V = 2 * DIFF_QK
NUM_BUCKETS = 32
MAX_DISTANCE = 128
D_FF_DENSE = 2816
N_EXPERTS = 8
D_FF_EXPERT = 3584
DN_ALPHA = (2 * DEPTH) ** 0.25

MLA_QK = MLA_NOPE + MLA_ROPE
MLA_W = MLA_HEADS * MLA_V
DIFF_W = DIFF_HEADS * DIFF_V
DIFF_VA = DIFF_V + 16
_CQ0, _CKV0 = 0, MLA_Q_RANK
_KR0 = _CKV0 + MLA_KV_RANK
_DQ0 = _KR0 + 2 * MLA_ROPE
_DK0 = _DQ0 + DIFF_W
_DV0 = _DK0 + DIFF_W
IN_COLS_W = _DV0 + DIFF_W

LANES = 128
BF16_ROWS = 16
VMEM_LIMIT = 56 * 1024 * 1024
NEG_BIG = -1e30
ROUTER_ROWS = 16
REC_ROWS = 8
LOG2E = math.log2(math.e)
ATT_BLK = 256
MLA_HP = 4
DIFF_HP = 2

BF16 = jnp.bfloat16
F32 = jnp.float32


def _dot(a, b):
    return jnp.dot(a, b, preferred_element_type=F32)


def _layer_norm_rows(y, g, b):
    mu = jnp.mean(y, axis=-1, keepdims=True)
    d = y - mu
    var = jnp.mean(d * d, axis=-1, keepdims=True)
    return d * lax.rsqrt(var + 1e-5) * g + b


def _rms_rows(y, g, eps):
    return y * lax.rsqrt(jnp.mean(y * y, axis=-1, keepdims=True) + eps) * g


def _proj_kernel(x_ref, win_ref, qn_ref, kvn_ref, wuq_ref, wukv_ref, cos_ref, sin_ref,
                 qT_ref, k_ref, vT_ref, dqT_ref, dk_ref, dvT_ref):
    x = x_ref[0].astype(BF16)
    h = lax.dot_general(x, win_ref[0], (((1,), (1,)), ((), ())),
                        preferred_element_type=F32)
    cqn = _rms_rows(h[:, _CQ0:_CQ0 + MLA_Q_RANK], qn_ref[0], 1e-6)
    ckvn = _rms_rows(h[:, _CKV0:_CKV0 + MLA_KV_RANK], kvn_ref[0], 1e-6)
    q = _dot(cqn.astype(BF16), wuq_ref[0]) * (MLA_QK ** -0.5 * LOG2E)
    kv = _dot(ckvn.astype(BF16), wukv_ref[0])
    cos4, sin4 = cos_ref[...], sin_ref[...]
    nr = MLA_HEADS * MLA_ROPE
    qr = q[:, MLA_W:MLA_W + nr] * cos4 + q[:, MLA_W + nr:] * sin4
    kr = (h[:, _KR0:_KR0 + MLA_ROPE] * cos4[:, :MLA_ROPE]
          + h[:, _KR0 + MLA_ROPE:_KR0 + 2 * MLA_ROPE] * sin4[:, :MLA_ROPE])
    qrT = qr.T
    for hh in range(MLA_HEADS):
        c0 = hh * LANES
        qnT = q[:, c0:c0 + MLA_NOPE].T
        qT_ref[0, hh] = jnp.concatenate(
            [qnT, qrT[hh * MLA_ROPE:(hh + 1) * MLA_ROPE]], axis=0).astype(BF16)
        k_ref[0, hh] = jnp.concatenate([kv[:, c0:c0 + MLA_NOPE], kr], axis=1).astype(BF16)
        vT_ref[0, hh] = kv[:, MLA_W + c0:MLA_W + c0 + MLA_V].T.astype(BF16)
        dqT = (h[:, _DQ0 + c0:_DQ0 + c0 + LANES] * (DIFF_QK ** -0.5 * LOG2E)).T.astype(BF16)
        zeros = jnp.zeros((DIFF_QK, x.shape[0]), BF16)
        dqT_ref[0, hh, 0] = jnp.concatenate([dqT[:DIFF_QK], zeros], axis=0)
        dqT_ref[0, hh, 1] = jnp.concatenate([zeros, dqT[DIFF_QK:]], axis=0)
        dk_ref[0, hh] = h[:, _DK0 + c0:_DK0 + c0 + LANES].astype(BF16)
        dvT_ref[0, hh] = jnp.concatenate(
            [h[:, _DV0 + c0:_DV0 + c0 + DIFF_V].T, jnp.ones((DIFF_VA - DIFF_V, x.shape[0]), F32)],
            axis=0).astype(BF16)


def _proj(x, l, win, qn, kvn, wuq, wukv, cos4, sin4, tm):
    b, s, d = x.shape
    hd = MLA_HEADS
    layer = lambda a: pl.BlockSpec((1,) + a.shape[1:], lambda bi, i: (l,) + (0,) * (a.ndim - 1))
    tmaj = lambda w: pl.BlockSpec((1, hd, tm, w), lambda bi, i: (bi, 0, i, 0))
    fmaj = lambda w: pl.BlockSpec((1, hd, w, tm), lambda bi, i: (bi, 0, 0, i))
    sds = jax.ShapeDtypeStruct
    return pl.pallas_call(
        _proj_kernel,
        grid=(b, s // tm),
        in_specs=[
            pl.BlockSpec((1, tm, d), lambda bi, i: (bi, i, 0)),
            layer(win), layer(qn), layer(kvn), layer(wuq), layer(wukv),
            pl.BlockSpec((tm, cos4.shape[1]), lambda bi, i: (i, 0)),
            pl.BlockSpec((tm, sin4.shape[1]), lambda bi, i: (i, 0)),
        ],
        out_specs=[fmaj(MLA_QK), tmaj(MLA_QK), fmaj(MLA_V),
                   pl.BlockSpec((1, hd, 2, LANES, tm), lambda bi, i: (bi, 0, 0, 0, i)),
                   tmaj(LANES), fmaj(DIFF_VA)],
        out_shape=[
            sds((b, hd, MLA_QK, s), BF16), sds((b, hd, s, MLA_QK), BF16), sds((b, hd, MLA_V, s), BF16),
            sds((b, hd, 2, LANES, s), BF16), sds((b, hd, s, LANES), BF16), sds((b, hd, DIFF_VA, s), BF16),
        ],
        compiler_params=pltpu.CompilerParams(
            dimension_semantics=("arbitrary", "arbitrary"), vmem_limit_bytes=VMEM_LIMIT),
        name="proj",
    )(x, win, qn, kvn, wuq, wukv, cos4, sin4)


def _softmax_step(s, vj, m_ref, l_ref, acc_ref, shift=None):
    m_prev = m_ref[...]
    smax = jnp.max(s, axis=0, keepdims=True)
    if shift is not None:
        smax = smax + shift
    m_new = jnp.maximum(m_prev, smax)
    a = jnp.exp2(m_prev - m_new)
    p = jnp.exp2(s - (m_new if shift is None else m_new - shift))
    if l_ref is not None:
        l_ref[...] = a * l_ref[...] + jnp.sum(p, axis=0, keepdims=True)
    acc_ref[...] = a * acc_ref[...] + _dot(vj, p.astype(BF16))
    m_ref[...] = m_new


def _chain_scratch(n, dv):
    return [pltpu.VMEM((1, ATT_BLK), F32)] * (2 * n) + [pltpu.VMEM((dv, ATT_BLK), F32)] * n


def _init_chains(state):
    n = len(state) // 3
    m_refs, l_refs, acc_refs = state[:n], state[n:2 * n], state[2 * n:]
    for m_ref, l_ref, acc_ref in zip(m_refs, l_refs, acc_refs):
        m_ref[...] = jnp.full(m_ref.shape, NEG_BIG, F32)
        l_ref[...] = jnp.zeros(l_ref.shape, F32)
        acc_ref[...] = jnp.zeros(acc_ref.shape, F32)
    return m_refs, l_refs, acc_refs


def _pipelined_tiles(qi, logits, next_logits, softmax_pv):
    a, b, c = 0, 1, 2

    @pl.when(qi == 0)
    def _():
        logits(0, a, True)
        next_logits(c)
        softmax_pv(0, a, "diag")

    @pl.when(qi == 1)
    def _():
        logits(1, a, True)
        softmax_pv(0, c, "near")
        next_logits(c)
        softmax_pv(1, a, "diag")

    @pl.when(qi >= 2)
    def _():
        logits(1, a)
        softmax_pv(0, c, "far")
        nfar = qi - 2

        def body(i, carry):
            j = 2 * i + 1
            logits(j + 1, b)
            softmax_pv(j, a, "far")
            logits(j + 2, a)
            softmax_pv(j + 1, b, "far")
            return carry

        lax.fori_loop(0, lax.shift_right_logical(nfar, 1), body, 0)
        odd = lax.rem(nfar, 2) == 1

        @pl.when(jnp.logical_not(odd))
        def _():
            logits(qi, b, True)
            softmax_pv(qi - 1, a, "near")
            next_logits(c)
            softmax_pv(qi, b, "diag")

        @pl.when(odd)
        def _():
            logits(qi - 1, b)
            softmax_pv(qi - 2, a, "far")
            logits(qi, a, True)
            softmax_pv(qi - 1, b, "near")
            next_logits(c)
            softmax_pv(qi, a, "diag")


def _mla_attn_kernel(qT_ref, qTn_ref, k_ref, vT_ref, o_ref, sa_ref, sb_ref, sc_ref, *state,
                     ta, hp):
    qi = pl.program_id(2)
    m_refs, l_refs, acc_refs = _init_chains(state)
    s_refs = (sa_ref, sb_ref, sc_ref)

    nb = ta // ATT_BLK
    blk = ATT_BLK

    def logits(j, buf, diagonal=False):
        off = pl.multiple_of(j * ta, ta)
        for h in range(hp):
            if not diagonal:
                s_refs[buf][h] = _dot(k_ref[0, h, pl.ds(off, ta), :], qT_ref[0, h])
                continue
            for qs in range(nb):
                nk, q0 = (qs + 1) * blk, qs * blk
                s_refs[buf][h, :nk, q0:q0 + blk] = _dot(k_ref[0, h, pl.ds(off, nk), :],
                                                        qT_ref[0, h, :, q0:q0 + blk])

    def next_logits(buf):
        for h in range(hp):
            s_refs[buf][h] = _dot(k_ref[0, h, :ta, :], qTn_ref[0, h])

    def softmax_pv(j, buf, kind):
        off = pl.multiple_of(j * ta, ta)
        diagonal = kind == "diag"
        for h in range(hp):
            for qs in range(nb):
                nk = (qs + 1) * blk if diagonal else ta
                s = s_refs[buf][h, :nk, qs * blk:(qs + 1) * blk]
                if diagonal:
                    kpos = lax.broadcasted_iota(jnp.int32, (nk, blk), 0)
                    qpos = lax.broadcasted_iota(jnp.int32, (nk, blk), 1) + qs * blk
                    s = jnp.where(kpos <= qpos, s, NEG_BIG)
                c = h * nb + qs
                _softmax_step(s, vT_ref[0, h, :, pl.ds(off, nk)], m_refs[c], l_refs[c], acc_refs[c])

    _pipelined_tiles(qi, logits, next_logits, softmax_pv)
    for h in range(hp):
        for qs in range(nb):
            c = h * nb + qs
            out = acc_refs[c][...] * (1.0 / l_refs[c][...])
            o_ref[0, qs * blk:(qs + 1) * blk, h * MLA_V:(h + 1) * MLA_V] = out.T.astype(BF16)


def _mla_attn(qT, k, vT, ta, hp):
    b, hd, _, s = qT.shape
    nq = s // ta
    return pl.pallas_call(
        functools.partial(_mla_attn_kernel, ta=ta, hp=hp),
        grid=(b, hd // hp, nq),
        in_specs=[
            pl.BlockSpec((1, hp, MLA_QK, ta), lambda bi, h, i: (bi, h, 0, i)),
            pl.BlockSpec((1, hp, MLA_QK, ta), lambda bi, h, i: (bi, h, 0, jnp.minimum(i + 1, nq - 1))),
            pl.BlockSpec((1, hp, s, MLA_QK), lambda bi, h, i: (bi, h, 0, 0)),
            pl.BlockSpec((1, hp, MLA_V, s), lambda bi, h, i: (bi, h, 0, 0)),
        ],
        out_specs=pl.BlockSpec((1, ta, hp * MLA_V), lambda bi, h, i: (bi, i, h)),
        out_shape=jax.ShapeDtypeStruct((b, s, hd * MLA_V), BF16),
        scratch_shapes=([pltpu.VMEM((hp, ta, ta), F32)] * 3
                        + _chain_scratch(hp * ta // ATT_BLK, MLA_V)),
        compiler_params=pltpu.CompilerParams(
            dimension_semantics=("arbitrary",) * 3, vmem_limit_bytes=VMEM_LIMIT),
        name="mla_attn",
    )(qT, qT, k, vT)


def _diff_attn_kernel(cfar_ref, qT_ref, qTn_ref, k_ref, vT_ref, bias_ref, lamp_ref, g_ref, *rest,
                      ta, hp, lambda_init, ncast):
    cast_in, o_ref, cast_out = rest[:ncast], rest[ncast], rest[ncast + 1:2 * ncast + 1]
    s_refs = rest[2 * ncast + 1:2 * ncast + 4]
    state = rest[2 * ncast + 4:]
    hg = pl.program_id(1)
    qi = pl.program_id(2)
    for src_ref, dst_ref in zip(cast_in, cast_out):
        dst_ref[...] = src_ref[...].astype(BF16)
    m_refs, l_refs, acc_refs = _init_chains(state)

    nb = ta // ATT_BLK
    blk = ATT_BLK

    def logits(j, buf, diagonal=False):
        off = pl.multiple_of(j * ta, ta)
        for h in range(hp):
            for c in range(2):
                if not diagonal:
                    s_refs[buf][2 * h + c] = _dot(k_ref[0, h, pl.ds(off, ta), :],
                                                  qT_ref[0, h, c])
                    continue
                for qb in range(nb):
                    nk, q0 = (qb + 1) * blk, qb * blk
                    s_refs[buf][2 * h + c, :nk, q0:q0 + blk] = _dot(
                        k_ref[0, h, pl.ds(off, nk), :], qT_ref[0, h, c, :, q0:q0 + blk])

    def next_logits(buf):
        for h in range(hp):
            for c in range(2):
                s_refs[buf][2 * h + c] = _dot(k_ref[0, h, :ta, :], qTn_ref[0, h, c])

    def softmax_pv(j, buf, kind):
        off = pl.multiple_of(j * ta, ta)
        diagonal = kind == "diag"
        for h in range(hp):
            for qb in range(nb):
                nk = (qb + 1) * blk if diagonal else ta
                vj = vT_ref[0, h, :, pl.ds(off, nk)]
                for c in range(2):
                    s = s_refs[buf][2 * h + c, :nk, qb * blk:(qb + 1) * blk]
                    i = (2 * h + c) * nb + qb
                    if kind == "far" or (kind == "near" and qb * blk + 1 >= MAX_DISTANCE):
                        _softmax_step(s, vj, m_refs[i], None, acc_refs[i],
                                      shift=cfar_ref[hg * hp + h])
                    else:
                        bias = bias_ref[h, 0 if diagonal else 1, :nk, qb * blk:(qb + 1) * blk]
                        _softmax_step(s + bias, vj, m_refs[i], None, acc_refs[i])

    _pipelined_tiles(qi, logits, next_logits, softmax_pv)

    lp = lamp_ref[...]
    lam = (jnp.exp(jnp.sum(lp[0:1] * lp[1:2], axis=-1, keepdims=True))
           - jnp.exp(jnp.sum(lp[2:3] * lp[3:4], axis=-1, keepdims=True)) + lambda_init)
    for h in range(hp):
        for qb in range(nb):
            a0 = acc_refs[(2 * h) * nb + qb][...]
            a1 = acc_refs[(2 * h + 1) * nb + qb][...]
            out = (a0[:DIFF_V] * (1.0 / a0[DIFF_V:DIFF_V + 1])
                   - lam * (a1[:DIFF_V] * (1.0 / a1[DIFF_V:DIFF_V + 1])))
            ms = jnp.mean(out * out, axis=0, keepdims=True)
            out = out * lax.rsqrt(ms + 1e-5) * g_ref[...] * (1.0 - lambda_init)
            o_ref[0, qb * blk:(qb + 1) * blk, h * DIFF_V:(h + 1) * DIFF_V] = out.T.astype(BF16)


def _cast_rows(rows_total, nsteps):
    rows = rows_total // nsteps
    return rows if rows * nsteps == rows_total and rows % BF16_ROWS == 0 else 0


def _diff_attn(cfar, qT, k, vT, bias_tiles, lam_params, g_col, ta, hp, lambda_init,
               cast=(), cast_part=(0, 1)):
    b, hd, _, _, s = qT.shape
    nq, nhg = s // ta, hd // hp
    nsteps = b * nhg * nq
    part, nparts = cast_part
    crows = [_cast_rows(a.shape[0] // nparts, nsteps) for a in cast]
    assert all(crows), "cast tables must split into whole tiles per grid step"
    step = lambda bi, h, i: (bi * nhg + h) * nq + i
    grid_spec = pltpu.PrefetchScalarGridSpec(
        num_scalar_prefetch=1,
        grid=(b, nhg, nq),
        in_specs=[
            pl.BlockSpec((1, hp, 2, LANES, ta), lambda bi, h, i, cf: (bi, h, 0, 0, i)),
            pl.BlockSpec((1, hp, 2, LANES, ta),
                         lambda bi, h, i, cf: (bi, h, 0, 0, jnp.minimum(i + 1, nq - 1))),
            pl.BlockSpec((1, hp, s, LANES), lambda bi, h, i, cf: (bi, h, 0, 0)),
            pl.BlockSpec((1, hp, DIFF_VA, s), lambda bi, h, i, cf: (bi, h, 0, 0)),
            pl.BlockSpec((hp, 2, ta, ta), lambda bi, h, i, cf: (h, 0, 0, 0)),
            pl.BlockSpec(lam_params.shape, lambda bi, h, i, cf: (0, 0)),
            pl.BlockSpec(g_col.shape, lambda bi, h, i, cf: (0, 0)),
        ] + [pl.BlockSpec((r, a.shape[1]), lambda bi, h, i, cf: (part * nsteps + step(bi, h, i), 0))
             for a, r in zip(cast, crows)],
        out_specs=[pl.BlockSpec((1, ta, hp * DIFF_V), lambda bi, h, i, cf: (bi, i, h))]
        + [pl.BlockSpec((r, a.shape[1]), lambda bi, h, i, cf: (step(bi, h, i), 0))
           for a, r in zip(cast, crows)],
        scratch_shapes=([pltpu.VMEM((2 * hp, ta, ta), F32)] * 3
                        + _chain_scratch(2 * hp * ta // ATT_BLK, DIFF_VA)),
    )
    res = pl.pallas_call(
        functools.partial(_diff_attn_kernel, ta=ta, hp=hp, lambda_init=lambda_init,
                          ncast=len(cast)),
        grid_spec=grid_spec,
        out_shape=[jax.ShapeDtypeStruct((b, s, hd * DIFF_V), BF16)]
        + [jax.ShapeDtypeStruct((a.shape[0] // nparts, a.shape[1]), BF16) for a in cast],
        compiler_params=pltpu.CompilerParams(
            dimension_semantics=("arbitrary",) * 3, vmem_limit_bytes=VMEM_LIMIT),
        name="diff_attn",
    )(cfar, qT, qT, k, vT, bias_tiles, lam_params, g_col, *cast)
    return res[0], res[1:]


def _split_bf16(v):
    hi = v.astype(BF16)
    return hi, (v - hi.astype(F32)).astype(BF16)


def _oproj_route_kernel(am_ref, ad_ref, x_ref, wo_ref, g_ref, b_ref, rhl_ref, rh_ref,
                        o_ref, ob_ref, r_ref):
    mix = _dot(am_ref[...], wo_ref[0, :MLA_W, :]) + _dot(ad_ref[...], wo_ref[0, MLA_W:, :])
    y = _layer_norm_rows(DN_ALPHA * x_ref[...] + mix, g_ref[...], b_ref[...])
    yhi, ylo = _split_bf16(y)
    nt = (((1,), (1,)), ((), ()))
    both = lax.dot_general(rhl_ref[...], yhi, nt, preferred_element_type=F32)
    logits = (both[:ROUTER_ROWS] + both[ROUTER_ROWS:]
              + lax.dot_general(rh_ref[...], ylo, nt, preferred_element_type=F32))
    row = lax.broadcasted_iota(jnp.int32, logits.shape, 0)
    logits = jnp.where(row < N_EXPERTS, logits, NEG_BIG)
    m1 = jnp.max(logits, axis=0, keepdims=True)
    i1 = jnp.min(jnp.where(logits == m1, row, ROUTER_ROWS), axis=0, keepdims=True)
    rest_l = jnp.where(row == i1, NEG_BIG, logits)
    m2 = jnp.max(rest_l, axis=0, keepdims=True)
    i2 = jnp.min(jnp.where(rest_l == m2, row, ROUTER_ROWS), axis=0, keepdims=True)
    e2 = jnp.exp(m2 - m1)
    g1 = 1.0 / (1.0 + e2)
    g2 = e2 * g1
    row8 = lax.broadcasted_iota(jnp.int32, r_ref.shape, 0)
    rec = jnp.where(row8 == 0, i1.astype(F32),
                    jnp.where(row8 == 1, i2.astype(F32),
                              jnp.where(row8 == 2, g1, jnp.where(row8 == 3, g2, 0.0))))
    o_ref[...] = y
    ob_ref[...] = yhi
    r_ref[...] = rec


def _oproj_route(a_mla, a_diff, x2d, l, wo, g, b, rhl, rh, tm):
    t, d = x2d.shape
    row = lambda w: pl.BlockSpec((tm, w), lambda i: (i, 0))
    full = lambda a: pl.BlockSpec(a.shape, lambda i: (0,) * a.ndim)
    return pl.pallas_call(
        _oproj_route_kernel,
        grid=(t // tm,),
        in_specs=[row(MLA_W), row(DIFF_W), row(d),
                  pl.BlockSpec((1,) + wo.shape[1:], lambda i: (l, 0, 0)),
                  full(g), full(b), full(rhl), full(rh)],
        out_specs=[row(d), row(d), pl.BlockSpec((REC_ROWS, tm), lambda i: (0, i))],
        out_shape=[jax.ShapeDtypeStruct((t, d), F32), jax.ShapeDtypeStruct((t, d), BF16),
                   jax.ShapeDtypeStruct((REC_ROWS, t), F32)],
        compiler_params=pltpu.CompilerParams(
            dimension_semantics=("arbitrary",), vmem_limit_bytes=VMEM_LIMIT),
        name="oproj_route",
    )(a_mla, a_diff, x2d, wo, g, b, rhl, rh)


def _oproj_ffn_kernel(am_ref, ad_ref, x_ref, wo_ref, g1_ref, b1_ref, wg_ref, wu_ref, wd_ref,
                      g2_ref, b2_ref, o_ref):
    mix = _dot(am_ref[...], wo_ref[0, :MLA_W, :]) + _dot(ad_ref[...], wo_ref[0, MLA_W:, :])
    x1 = _layer_norm_rows(DN_ALPHA * x_ref[...] + mix, g1_ref[...], b1_ref[...])
    xb = x1.astype(BF16)
    gate = _dot(xb, wg_ref[0])
    up = _dot(xb, wu_ref[0])
    hid = (gate * jax.nn.sigmoid(gate) * up).astype(BF16)
    f = _dot(hid, wd_ref[0])
    o_ref[...] = _layer_norm_rows(DN_ALPHA * x1 + f, g2_ref[...], b2_ref[...])


def _oproj_ffn(a_mla, a_diff, x2d, l, wo, g1, b1, li, wg, wu, wd, g2, b2, tm):
    t, d = x2d.shape
    row = lambda w: pl.BlockSpec((tm, w), lambda i: (i, 0))
    layer = lambda a, k: pl.BlockSpec((1,) + a.shape[1:], lambda i: (k,) + (0,) * (a.ndim - 1),
                                      pipeline_mode=pl.Buffered(1))
    vec = lambda a: pl.BlockSpec(a.shape, lambda i: (0, 0))
    return pl.pallas_call(
        _oproj_ffn_kernel,
        grid=(t // tm,),
        in_specs=[row(MLA_W), row(DIFF_W), row(d), layer(wo, l), vec(g1), vec(b1),
                  layer(wg, li), layer(wu, li), layer(wd, li), vec(g2), vec(b2)],
        out_specs=row(d),
        out_shape=jax.ShapeDtypeStruct((t, d), F32),
        compiler_params=pltpu.CompilerParams(
            dimension_semantics=("arbitrary",), vmem_limit_bytes=VMEM_LIMIT),
        name="oproj_ffn",
    )(a_mla, a_diff, x2d, wo, g1, b1, wg, wu, wd, g2, b2)


def _dispatch_kernel(ue_ref, uoff_ref, unc_ref, x_ref, post_ref, o_ref, *, rows, unit, kc):
    i = pl.program_id(0)
    t = x_ref.shape[0]
    ups = rows // unit

    def chunk(u, c):
        nominal = uoff_ref[u] + c * kc
        off = pl.multiple_of(jnp.minimum(nominal, t - kc), LANES)
        row_pos = u * unit + lax.broadcasted_iota(jnp.int32, (unit, 1), 0)
        dest = post_ref[pl.ds(ue_ref[u], 1), pl.ds(off, kc)]
        tok = off + lax.broadcasted_iota(jnp.int32, (1, kc), 1)
        dest = jnp.where(tok >= nominal, dest, -1)
        sel = jnp.where(dest == row_pos, 1.0, 0.0).astype(BF16)
        return _dot(sel, x_ref[pl.ds(off, kc), :]).astype(BF16)

    for r in range(ups):
        o_ref[r * unit:(r + 1) * unit, :] = chunk(i * ups + r, 0)

    for r in range(ups):
        def more(c, carry, r=r):
            o_ref[r * unit:(r + 1) * unit, :] += chunk(i * ups + r, c)
            return carry

        lax.fori_loop(1, unc_ref[i * ups + r], more, 0)


def _dispatch(unit_e, uoff, unc, xb, post, n_rows, rows, unit, kc):
    t, d = xb.shape
    const = lambda shape: pl.BlockSpec(shape, lambda i, *_: (0,) * len(shape),
                                       pipeline_mode=pl.Buffered(1))
    grid_spec = pltpu.PrefetchScalarGridSpec(
        num_scalar_prefetch=3,
        grid=(n_rows // rows,),
        in_specs=[const(xb.shape), const(post.shape)],
        out_specs=pl.BlockSpec((rows, d), lambda i, *_: (i, 0)),
    )
    return pl.pallas_call(
        functools.partial(_dispatch_kernel, rows=rows, unit=unit, kc=kc),
        grid_spec=grid_spec,
        out_shape=jax.ShapeDtypeStruct((n_rows, d), BF16),
        compiler_params=pltpu.CompilerParams(
            dimension_semantics=("arbitrary",), vmem_limit_bytes=VMEM_LIMIT),
        name="dispatch",
    )(unit_e, uoff, unc, xb, post)


def _gffn_kernel(te_ref, tv_ref, x_ref, wg_ref, wu_ref, wd_ref, o_ref, acc_ref, *, nf):
    i = pl.program_id(0)
    f = pl.program_id(1)
    status = tv_ref[i]
    tm = x_ref.shape[0]
    half = tm // 2

    def swiglu_rows(rows):
        xb = x_ref[:rows, :]
        gate = _dot(xb, wg_ref[0, 0])
        up = _dot(xb, wu_ref[0, 0])
        hid = (gate * jax.nn.sigmoid(gate) * up).astype(BF16)
        part = _dot(hid, wd_ref[0, 0])

        @pl.when(f == 0)
        def _():
            acc_ref[:rows, :] = part

        @pl.when(f > 0)
        def _():
            acc_ref[:rows, :] += part

    @pl.when(status == 2)
    def _():
        swiglu_rows(tm)

    @pl.when(status == 1)
    def _():
        swiglu_rows(half)

    @pl.when(jnp.logical_and(f == nf - 1, status == 2))
    def _():
        o_ref[...] = acc_ref[...].astype(BF16)

    @pl.when(jnp.logical_and(f == nf - 1, status == 1))
    def _():
        o_ref[:half, :] = acc_ref[:half, :].astype(BF16)
        o_ref[half:, :] = jnp.zeros((tm - half, o_ref.shape[1]), BF16)

    @pl.when(jnp.logical_and(f == nf - 1, status == 0))
    def _():
        o_ref[...] = jnp.zeros(o_ref.shape, BF16)


def _gffn(tile_e, tile_v, xs, li, wg, wu, wd, tm, fc):
    p, d = xs.shape
    nt = p // tm
    nf = wg.shape[3] // fc
    fidx = lambda i, f, te, tv: jnp.where(tv[i] > 0, f, nf - 1)
    grid_spec = pltpu.PrefetchScalarGridSpec(
        num_scalar_prefetch=2,
        grid=(nt, nf),
        in_specs=[
            pl.BlockSpec((tm, d), lambda i, f, te, tv: (i, 0)),
            pl.BlockSpec((1, 1, d, fc), lambda i, f, te, tv: (li, te[i], 0, fidx(i, f, te, tv))),
            pl.BlockSpec((1, 1, d, fc), lambda i, f, te, tv: (li, te[i], 0, fidx(i, f, te, tv))),
            pl.BlockSpec((1, 1, fc, d), lambda i, f, te, tv: (li, te[i], fidx(i, f, te, tv), 0)),
        ],
        out_specs=pl.BlockSpec((tm, d), lambda i, f, te, tv: (i, 0)),
        scratch_shapes=[pltpu.VMEM((tm, d), F32)],
    )
    return pl.pallas_call(
        functools.partial(_gffn_kernel, nf=nf),
        grid_spec=grid_spec,
        out_shape=jax.ShapeDtypeStruct((p, d), BF16),
        compiler_params=pltpu.CompilerParams(
            dimension_semantics=("arbitrary", "arbitrary"), vmem_limit_bytes=VMEM_LIMIT),
        name="grouped_ffn",
    )(tile_e, tile_v, xs, wg, wu, wd)


def _combine_kernel(ral_ref, rlo_ref, rhi_ref, x_ref, rec_ref, g_ref, b_ref, ys_ref, o_ref,
                    win_ref, xwin_ref, acc_ref, sems, xsem, *, w):
    j = pl.program_id(0)
    nj = pl.num_programs(0)
    slot = lax.rem(j, 2)
    p_rows = ys_ref.shape[0]

    def win_copy(jj, e, sl):
        r = pl.multiple_of(ral_ref[jj * N_EXPERTS + e], BF16_ROWS)
        return pltpu.make_async_copy(ys_ref.at[pl.ds(r, w)], win_ref.at[sl, pl.ds(e * w, w)],
                                     sems.at[sl, e])

    @pl.when(j == 0)
    def _():
        for e in range(N_EXPERTS):
            win_copy(0, e, 0).start()

    @pl.when(j + 1 < nj)
    def _():
        for e in range(N_EXPERTS):
            win_copy(j + 1, e, 1 - slot).start()

    rec = rec_ref[...]
    p1, p2, g1, g2 = rec[:, 0:1], rec[:, 1:2], rec[:, 2:3], rec[:, 3:4]

    def sel(base, lo, hi):
        rows = base + lax.broadcasted_iota(jnp.int32, (1, w), 1)
        rp = jnp.where((rows >= lo) & (rows < hi), rows, -1).astype(F32)
        return (jnp.where(p1 == rp, g1, 0.0) + jnp.where(p2 == rp, g2, 0.0)).astype(BF16)

    lane = lax.broadcasted_iota(jnp.int32, (1, N_EXPERTS * w), 1)
    rp = jnp.full((1, N_EXPERTS * w), -1, jnp.int32)
    for e in range(N_EXPERTS):
        win_copy(j, e, slot).wait()
        k = j * N_EXPERTS + e
        rows = ral_ref[k] + lane - e * w
        mine = ((lane >= e * w) & (lane < (e + 1) * w)
                & (rows >= rlo_ref[k]) & (rows < rhi_ref[k]))
        rp = jnp.where(mine, rows, rp)
    rp = rp.astype(F32)
    tm = rec.shape[0]
    step = min(tm, 256)
    for r0 in range(0, tm, step):
        rows = slice(r0, r0 + step)
        sel_rows = (jnp.where(p1[rows] == rp, g1[rows], 0.0)
                    + jnp.where(p2[rows] == rp, g2[rows], 0.0)).astype(BF16)
        acc_ref[rows, :] = _dot(sel_rows, win_ref[slot])

    for e in range(N_EXPERTS):
        k = j * N_EXPERTS + e
        nwin = (rhi_ref[k] - ral_ref[k] + (w - 1)) // w

        def extra(wi, carry, k=k):
            nominal = ral_ref[k] + wi * w
            base = pl.multiple_of(jnp.minimum(nominal, p_rows - w), BF16_ROWS)
            cp = pltpu.make_async_copy(ys_ref.at[pl.ds(base, w)], xwin_ref, xsem)
            cp.start()
            cp.wait()
            acc_ref[...] += _dot(sel(base, nominal, rhi_ref[k]), xwin_ref[...])
            return carry

        lax.fori_loop(1, nwin, extra, 0)

    o_ref[...] = _layer_norm_rows(DN_ALPHA * x_ref[...] + acc_ref[...], g_ref[...], b_ref[...])


def _combine(ral, rlo, rhi, x2d, rec2, g, b, ys, tm, w):
    t, d = x2d.shape
    row = lambda c: pl.BlockSpec((tm, c), lambda i, *_: (i, 0))
    full = lambda shape: pl.BlockSpec(shape, lambda i, *_: (0,) * len(shape))
    grid_spec = pltpu.PrefetchScalarGridSpec(
        num_scalar_prefetch=3,
        grid=(t // tm,),
        in_specs=[row(d), row(rec2.shape[1]), full(g.shape), full(b.shape),
                  pl.BlockSpec(memory_space=pl.ANY)],
        out_specs=row(d),
        scratch_shapes=[pltpu.VMEM((2, N_EXPERTS * w, d), BF16), pltpu.VMEM((w, d), BF16),
                        pltpu.VMEM((tm, d), F32), pltpu.SemaphoreType.DMA((2, N_EXPERTS)),
                        pltpu.SemaphoreType.DMA(())],
    )
    return pl.pallas_call(
        functools.partial(_combine_kernel, w=w),
        grid_spec=grid_spec,
        out_shape=jax.ShapeDtypeStruct((t, d), F32),
        compiler_params=pltpu.CompilerParams(
            dimension_semantics=("arbitrary",), vmem_limit_bytes=VMEM_LIMIT),
        name="combine",
    )(ral, rlo, rhi, x2d, rec2, g, b, ys)


def _route_plan(rec, nt, tm_e, unit, kc, tm_c, w):
    t = rec.shape[1]
    i32 = jnp.int32
    ar = jnp.arange(N_EXPERTS, dtype=i32)
    m1 = (rec[0].astype(i32)[:, None] == ar[None, :]).astype(i32)
    m2 = (rec[1].astype(i32)[:, None] == ar[None, :]).astype(i32)
    m = m1 + m2
    cinc = jnp.cumsum(m, axis=0)
    counts = cinc[-1]
    padded = ((counts + tm_e - 1) // tm_e) * tm_e
    ends = jnp.cumsum(padded)
    starts = ends - padded
    posmat = starts[None, :] + (cinc - m)
    pos1 = jnp.sum(m1 * posmat, axis=1)
    pos2 = jnp.sum(m2 * posmat, axis=1)
    post = jnp.where(m.T > 0, posmat.T, -1)
    rec2 = jnp.stack([pos1.astype(F32), pos2.astype(F32), rec[2], rec[3]]
                     + [jnp.zeros((t,), F32)] * 4, axis=1)
    tile_start = jnp.arange(nt, dtype=i32) * tm_e
    tile_e = jnp.sum((tile_start[:, None] >= ends[None, :]).astype(i32), axis=1)
    tile_v = tile_e < N_EXPERTS
    last_e = jnp.max(jnp.where(tile_v, tile_e, 0))
    tile_e = jnp.where(tile_v, tile_e, last_e)
    upt = tm_e // unit
    nu = nt * upt
    oh = (jnp.repeat(tile_e, upt)[:, None] == ar[None, :]).astype(i32)
    q0 = jnp.arange(nu, dtype=i32) * unit - jnp.sum(oh * starts[None, :], axis=1)
    q1 = jnp.minimum(q0 + unit, jnp.sum(oh * counts[None, :], axis=1))
    has = jnp.repeat(tile_v, upt) & (q1 > q0)
    cb = jnp.concatenate([jnp.zeros((1, N_EXPERTS), i32), cinc[LANES - 1::LANES]], axis=0)
    cbu = jnp.sum(oh[:, None, :] * cb[None, :, :], axis=2)
    jlo = jnp.sum((cbu[:, 1:] <= q0[:, None]).astype(i32), axis=1)
    jend = jnp.sum((cbu[:, :-1] < q1[:, None]).astype(i32), axis=1)
    uoff = jnp.where(has, jlo * LANES, 0)
    unc = jnp.where(has, (jend * LANES - uoff + kc - 1) // kc, 0)
    cbt = jnp.concatenate([jnp.zeros((1, N_EXPERTS), i32), cinc[tm_c - 1::tm_c]], axis=0)
    r_lo = starts[None, :] + cbt[:-1]
    r_hi = starts[None, :] + cbt[1:]
    ral = jnp.minimum((r_lo // BF16_ROWS) * BF16_ROWS, nt * tm_e - w)
    oh_t = (tile_e[:, None] == ar[None, :]).astype(i32)
    tile_rows = (jnp.sum(oh_t * counts[None, :], axis=1)
                 - (tile_start - jnp.sum(oh_t * starts[None, :], axis=1)))
    tile_status = jnp.where(tile_v, jnp.where(tile_rows > tm_e // 2, 2, 1), 0)
    return (rec2, post, tile_e, tile_status.astype(i32), uoff, unc,
            ral.reshape(-1), r_lo.reshape(-1), r_hi.reshape(-1))


def _moe(x2d, xb, rec, li, wg, wu, wd, g, b, tm_c, tm_e, fc, unit, kc, w):
    t, d = x2d.shape
    nt = (2 * t) // tm_e + N_EXPERTS
    rec2, post, tile_e, tile_v, uoff, unc, ral, rlo, rhi = _route_plan(
        rec, nt, tm_e, unit, kc, tm_c, w)
    n_rows = nt * tm_e
    rows_d = 2 * tm_e if n_rows % (2 * tm_e) == 0 else tm_e
    xs = _dispatch(jnp.repeat(tile_e, tm_e // unit), uoff, unc, xb, post, n_rows, rows_d, unit, kc)
    ys = _gffn(tile_e, tile_v, xs, li, wg, wu, wd, tm_e, fc)
    return _combine(ral, rlo, rhi, x2d, rec2, g, b, ys, tm_c, w)


def _rotate_half_cols(w):
    half = w.shape[-1] // 2
    return jnp.concatenate([-w[..., half:], w[..., :half]], axis=-1)


def _t5_bucket(n):
    max_exact = NUM_BUCKETS // 2
    nf = jnp.maximum(n, 1).astype(F32)
    large = max_exact + (jnp.log(nf / max_exact) / math.log(MAX_DISTANCE / max_exact)
                         * (NUM_BUCKETS - max_exact)).astype(jnp.int32)
    large = jnp.minimum(large, NUM_BUCKETS - 1)
    return jnp.where(n < max_exact, n, large)


def _pick(n, prefs):
    for p in prefs:
        if n % p == 0:
            return p
    raise ValueError(f"no tile for extent {n}")


def kernel(x, rel_bias, w_in, mla_q_norm, mla_kv_norm, mla_w_uq, mla_w_uk, mla_w_uv, diff_lambda, diff_norm, w_o, ln1_g, ln1_b, ln2_g, ln2_b, ffn_w_gate, ffn_w_up, ffn_w_down, moe_router, moe_w_gate, moe_w_up, moe_w_down):
    b, s, d = x.shape
    t = b * s
    assert d == D_MODEL
    ta = _pick(s, (512, 256))
    assert ta % ATT_BLK == 0
    assert s >= 2 * ta or s == ta
    tm_proj = _pick(s, (512, 256, 128))
    tm_tok = _pick(t, (512, 256, 128))
    tm_e = _pick(2 * t, (512, 256))
    tm_c = _pick(t, (512, 256))
    unit, kc, wrows = 128, 768, 224
    assert tm_e % unit == 0 and t >= kc and t % LANES == 0
    fc = D_FF_EXPERT // 2

    pos = jnp.arange(s, dtype=F32)
    inv = 1.0 / (ROPE_THETA ** (jnp.arange(0, MLA_ROPE, 2, dtype=F32) / MLA_ROPE))
    ang = pos[:, None] * inv[None, :]
    cos4 = jnp.tile(jnp.cos(ang), (1, 2 * MLA_HEADS))
    sin4 = jnp.tile(jnp.sin(ang), (1, 2 * MLA_HEADS))
    rb = rel_bias.astype(F32)
    kk = jnp.arange(ta)[:, None]
    qq = jnp.arange(ta)[None, :]
    tiles = []
    for off in (0, ta):
        dist = qq + off - kk
        bucket = _t5_bucket(jnp.maximum(dist, 0))
        bt = jnp.zeros((DIFF_HEADS, ta, ta), F32)
        for bk in range(NUM_BUCKETS):
            bt = bt + jnp.where((bucket == bk)[None], rb[bk][:, None, None], 0.0)
        tiles.append(jnp.where((dist >= 0)[None], bt * LOG2E, NEG_BIG))
    bias_tiles = jnp.stack(tiles, axis=1)
    assert ta + 1 >= MAX_DISTANCE
    cfar = rb[NUM_BUCKETS - 1] * LOG2E
    n_moe = moe_w_gate.shape[0]
    moe_tables = (moe_w_gate.reshape(-1, D_FF_EXPERT), moe_w_up.reshape(-1, D_FF_EXPERT),
                  moe_w_down.reshape(-1, d))

    w_t = jnp.swapaxes(w_in, 1, 2)
    kr_t = w_t[:, _KR0:_KR0 + MLA_ROPE]
    half = MLA_ROPE // 2
    win = jnp.concatenate([w_t[:, :_KR0 + MLA_ROPE], -kr_t[:, half:], kr_t[:, :half],
                           w_t[:, _KR0 + MLA_ROPE:]], axis=1).astype(BF16)
    nl = w_in.shape[0]
    uq = mla_w_uq.reshape(nl, MLA_Q_RANK, MLA_HEADS, MLA_QK)
    uq_r = uq[..., MLA_NOPE:]
    wuq = jnp.concatenate([uq[..., :MLA_NOPE].reshape(nl, MLA_Q_RANK, -1),
                           uq_r.reshape(nl, MLA_Q_RANK, -1),
                           _rotate_half_cols(uq_r).reshape(nl, MLA_Q_RANK, -1)], axis=2).astype(BF16)
    wukv = jnp.concatenate([mla_w_uk, mla_w_uv], axis=2).astype(BF16)
    qn, kvn = mla_q_norm[:, None, :], mla_kv_norm[:, None, :]
    wo = w_o.astype(BF16)
    ffn_wg, ffn_wu, ffn_wd = (w.astype(BF16) for w in (ffn_w_gate, ffn_w_up, ffn_w_down))
    rpad = jnp.pad(jnp.swapaxes(moe_router.astype(F32), 1, 2),
                   ((0, 0), (0, ROUTER_ROWS - N_EXPERTS), (0, 0)))
    rhi = rpad.astype(BF16)
    rhl = jnp.concatenate([rhi, (rpad - rhi.astype(F32)).astype(BF16)], axis=1)

    xc = x
    for l in range(DEPTH):
        lambda_init = 0.8 - 0.6 * math.exp(-0.3 * l)
        qT, k, vT, dqT, dk, dvT = _proj(xc, l, win, qn, kvn, wuq, wukv, cos4, sin4, tm_proj)
        a_mla = _mla_attn(qT, k, vT, ta, MLA_HP).reshape(t, MLA_W)
        i = l // 2
        dense = l % 2 == 0
        a_diff, casts = _diff_attn(cfar, dqT, dk, dvT, bias_tiles, diff_lambda[l].astype(F32),
                                   diff_norm[l].astype(F32)[:, None], ta, DIFF_HP, lambda_init,
                                   cast=moe_tables if dense and i < n_moe else (),
                                   cast_part=(i, n_moe))
        a_diff = a_diff.reshape(t, DIFF_W)
        if casts:
            moe_wg = casts[0].reshape(1, N_EXPERTS, d, D_FF_EXPERT)
            moe_wu = casts[1].reshape(1, N_EXPERTS, d, D_FF_EXPERT)
            moe_wd = casts[2].reshape(1, N_EXPERTS, D_FF_EXPERT, d)
        x2d = xc.reshape(t, d)
        g1, b1 = ln1_g[l][None, :], ln1_b[l][None, :]
        g2, b2 = ln2_g[l][None, :], ln2_b[l][None, :]
        if dense:
            x2 = _oproj_ffn(a_mla, a_diff, x2d, l, wo, g1, b1, i, ffn_wg, ffn_wu, ffn_wd,
                            g2, b2, tm_tok)
        else:
            x1, x1b, rec = _oproj_route(a_mla, a_diff, x2d, l, wo, g1, b1, rhl[i], rhi[i], tm_tok)
            x2 = _moe(x1, x1b, rec, 0, moe_wg, moe_wu, moe_wd, g2, b2,
                      tm_c, tm_e, fc, unit, kc, wrows)
        xc = x2.reshape(b, s, d)
    return xc
```

```python
import functools
import math

import jax
import jax.numpy as jnp
from jax import lax
from jax.experimental import pallas as pl
from jax.experimental.pallas import tpu as pltpu

D_MODEL = 1024
DEPTH = 4
MLA_HEADS = 4
MLA_NOPE = 128
MLA_ROPE = 64
MLA_V = 128
MLA_Q_RANK = 384
MLA_KV_RANK = 256
ROPE_THETA = 10000.0
DIFF_HEADS = 4
DIFF_QK = 64
DIFF_V = 2 * DIFF_QK
NUM_BUCKETS = 32
MAX_DISTANCE = 128
D_FF_DENSE = 2816
N_EXPERTS = 8
D_FF_EXPERT = 3584
DN_ALPHA = (2 * DEPTH) ** 0.25

MLA_QK = MLA_NOPE + MLA_ROPE
MLA_W = MLA_HEADS * MLA_V
DIFF_W = DIFF_HEADS * DIFF_V
DIFF_VA = DIFF_V + 16
_CQ0, _CKV0 = 0, MLA_Q_RANK
_KR0 = _CKV0 + MLA_KV_RANK
_DQ0 = _KR0 + 2 * MLA_ROPE
_DK0 = _DQ0 + DIFF_W
_DV0 = _DK0 + DIFF_W
IN_COLS_W = _DV0 + DIFF_W

LANES = 128
BF16_ROWS = 16
VMEM_LIMIT = 56 * 1024 * 1024
NEG_BIG = -1e30
ROUTER_ROWS = 16
REC_ROWS = 8
LOG2E = math.log2(math.e)
ATT_BLK = 256
MLA_HP = 4
DIFF_HP = 2

BF16 = jnp.bfloat16
F32 = jnp.float32


def _dot(a, b):
    return jnp.dot(a, b, preferred_element_type=F32)


def _layer_norm_rows(y, g, b):
    mu = jnp.mean(y, axis=-1, keepdims=True)
    d = y - mu
    var = jnp.mean(d * d, axis=-1, keepdims=True)
    return d * lax.rsqrt(var + 1e-5) * g + b


def _rms_rows(y, g, eps):
    return y * lax.rsqrt(jnp.mean(y * y, axis=-1, keepdims=True) + eps) * g


def _proj_kernel(x_ref, win_ref, qn_ref, kvn_ref, wuq_ref, wukv_ref, cos_ref, sin_ref,
                 qT_ref, k_ref, vT_ref, dqT_ref, dk_ref, dvT_ref):
    x = x_ref[0].astype(BF16)
    h = lax.dot_general(x, win_ref[0], (((1,), (1,)), ((), ())),
                        preferred_element_type=F32)
    cqn = _rms_rows(h[:, _CQ0:_CQ0 + MLA_Q_RANK], qn_ref[0], 1e-6)
    ckvn = _rms_rows(h[:, _CKV0:_CKV0 + MLA_KV_RANK], kvn_ref[0], 1e-6)
    q = _dot(cqn.astype(BF16), wuq_ref[0]) * (MLA_QK ** -0.5 * LOG2E)
    kv = _dot(ckvn.astype(BF16), wukv_ref[0])
    cos4, sin4 = cos_ref[...], sin_ref[...]
    nr = MLA_HEADS * MLA_ROPE
    qr = q[:, MLA_W:MLA_W + nr] * cos4 + q[:, MLA_W + nr:] * sin4
    kr = (h[:, _KR0:_KR0 + MLA_ROPE] * cos4[:, :MLA_ROPE]
          + h[:, _KR0 + MLA_ROPE:_KR0 + 2 * MLA_ROPE] * sin4[:, :MLA_ROPE])
    qrT = qr.T
    for hh in range(MLA_HEADS):
        c0 = hh * LANES
        qnT = q[:, c0:c0 + MLA_NOPE].T
        qT_ref[0, hh] = jnp.concatenate(
            [qnT, qrT[hh * MLA_ROPE:(hh + 1) * MLA_ROPE]], axis=0).astype(BF16)
        k_ref[0, hh] = jnp.concatenate([kv[:, c0:c0 + MLA_NOPE], kr], axis=1).astype(BF16)
        vT_ref[0, hh] = kv[:, MLA_W + c0:MLA_W + c0 + MLA_V].T.astype(BF16)
        dqT = (h[:, _DQ0 + c0:_DQ0 + c0 + LANES] * (DIFF_QK ** -0.5 * LOG2E)).T.astype(BF16)
        zeros = jnp.zeros((DIFF_QK, x.shape[0]), BF16)
        dqT_ref[0, hh, 0] = jnp.concatenate([dqT[:DIFF_QK], zeros], axis=0)
        dqT_ref[0, hh, 1] = jnp.concatenate([zeros, dqT[DIFF_QK:]], axis=0)
        dk_ref[0, hh] = h[:, _DK0 + c0:_DK0 + c0 + LANES].astype(BF16)
        dvT_ref[0, hh] = jnp.concatenate(
            [h[:, _DV0 + c0:_DV0 + c0 + DIFF_V].T, jnp.ones((DIFF_VA - DIFF_V, x.shape[0]), F32)],
            axis=0).astype(BF16)


def _proj(x, l, win, qn, kvn, wuq, wukv, cos4, sin4, tm):
    b, s, d = x.shape
    hd = MLA_HEADS
    layer = lambda a: pl.BlockSpec((1,) + a.shape[1:], lambda bi, i: (l,) + (0,) * (a.ndim - 1))
    tmaj = lambda w: pl.BlockSpec((1, hd, tm, w), lambda bi, i: (bi, 0, i, 0))
    fmaj = lambda w: pl.BlockSpec((1, hd, w, tm), lambda bi, i: (bi, 0, 0, i))
    sds = jax.ShapeDtypeStruct
    return pl.pallas_call(
        _proj_kernel,
        grid=(b, s // tm),
        in_specs=[
            pl.BlockSpec((1, tm, d), lambda bi, i: (bi, i, 0)),
            layer(win), layer(qn), layer(kvn), layer(wuq), layer(wukv),
            pl.BlockSpec((tm, cos4.shape[1]), lambda bi, i: (i, 0)),
            pl.BlockSpec((tm, sin4.shape[1]), lambda bi, i: (i, 0)),
        ],
        out_specs=[fmaj(MLA_QK), tmaj(MLA_QK), fmaj(MLA_V),
                   pl.BlockSpec((1, hd, 2, LANES, tm), lambda bi, i: (bi, 0, 0, 0, i)),
                   tmaj(LANES), fmaj(DIFF_VA)],
        out_shape=[
            sds((b, hd, MLA_QK, s), BF16), sds((b, hd, s, MLA_QK), BF16), sds((b, hd, MLA_V, s), BF16),
            sds((b, hd, 2, LANES, s), BF16), sds((b, hd, s, LANES), BF16), sds((b, hd, DIFF_VA, s), BF16),
        ],
        compiler_params=pltpu.CompilerParams(
            dimension_semantics=("arbitrary", "arbitrary"), vmem_limit_bytes=VMEM_LIMIT),
        name="proj",
    )(x, win, qn, kvn, wuq, wukv, cos4, sin4)


def _softmax_step(s, vj, m_ref, l_ref, acc_ref, shift=None):
    m_prev = m_ref[...]
    smax = jnp.max(s, axis=0, keepdims=True)
    if shift is not None:
        smax = smax + shift
    m_new = jnp.maximum(m_prev, smax)
    a = jnp.exp2(m_prev - m_new)
    p = jnp.exp2(s - (m_new if shift is None else m_new - shift))
    if l_ref is not None:
        l_ref[...] = a * l_ref[...] + jnp.sum(p, axis=0, keepdims=True)
    acc_ref[...] = a * acc_ref[...] + _dot(vj, p.astype(BF16))
    m_ref[...] = m_new


def _chain_scratch(n, dv):
    return [pltpu.VMEM((1, ATT_BLK), F32)] * (2 * n) + [pltpu.VMEM((dv, ATT_BLK), F32)] * n


def _init_chains(state):
    n = len(state) // 3
    m_refs, l_refs, acc_refs = state[:n], state[n:2 * n], state[2 * n:]
    for m_ref, l_ref, acc_ref in zip(m_refs, l_refs, acc_refs):
        m_ref[...] = jnp.full(m_ref.shape, NEG_BIG, F32)
        l_ref[...] = jnp.zeros(l_ref.shape, F32)
        acc_ref[...] = jnp.zeros(acc_ref.shape, F32)
    return m_refs, l_refs, acc_refs


def _pipelined_tiles(qi, logits, next_logits, softmax_pv):
    a, b, c = 0, 1, 2

    @pl.when(qi == 0)
    def _():
        logits(0, a, True)
        next_logits(c)
        softmax_pv(0, a, "diag")

    @pl.when(qi == 1)
    def _():
        logits(1, a, True)
        softmax_pv(0, c, "near")
        next_logits(c)
        softmax_pv(1, a, "diag")

    @pl.when(qi >= 2)
    def _():
        logits(1, a)
        softmax_pv(0, c, "far")
        nfar = qi - 2

        def body(i, carry):
            j = 2 * i + 1
            logits(j + 1, b)
            softmax_pv(j, a, "far")
            logits(j + 2, a)
            softmax_pv(j + 1, b, "far")
            return carry

        lax.fori_loop(0, lax.shift_right_logical(nfar, 1), body, 0)
        odd = lax.rem(nfar, 2) == 1

        @pl.when(jnp.logical_not(odd))
        def _():
            logits(qi, b, True)
            softmax_pv(qi - 1, a, "near")
            next_logits(c)
            softmax_pv(qi, b, "diag")

        @pl.when(odd)
        def _():
            logits(qi - 1, b)
            softmax_pv(qi - 2, a, "far")
            logits(qi, a, True)
            softmax_pv(qi - 1, b, "near")
            next_logits(c)
            softmax_pv(qi, a, "diag")


def _mla_attn_kernel(qT_ref, qTn_ref, k_ref, vT_ref, *rest, ta, hp, ncast):
    cast_in, o_ref, cast_out = rest[:ncast], rest[ncast], rest[ncast + 1:2 * ncast + 1]
    s_refs = rest[2 * ncast + 1:2 * ncast + 4]
    state = rest[2 * ncast + 4:]
    qi = pl.program_id(2)
    for src_ref, dst_ref in zip(cast_in, cast_out):
        dst_ref[...] = src_ref[...].astype(BF16)
    m_refs, l_refs, acc_refs = _init_chains(state)

    nb = ta // ATT_BLK
    blk = ATT_BLK

    def logits(j, buf, diagonal=False):
        off = pl.multiple_of(j * ta, ta)
        for h in range(hp):
            if not diagonal:
                s_refs[buf][h] = _dot(k_ref[0, h, pl.ds(off, ta), :], qT_ref[0, h])
                continue
            for qs in range(nb):
                nk, q0 = (qs + 1) * blk, qs * blk
                s_refs[buf][h, :nk, q0:q0 + blk] = _dot(k_ref[0, h, pl.ds(off, nk), :],
                                                        qT_ref[0, h, :, q0:q0 + blk])

    def next_logits(buf):
        for h in range(hp):
            s_refs[buf][h] = _dot(k_ref[0, h, :ta, :], qTn_ref[0, h])

    def softmax_pv(j, buf, kind):
        off = pl.multiple_of(j * ta, ta)
        diagonal = kind == "diag"
        for h in range(hp):
            for qs in range(nb):
                nk = (qs + 1) * blk if diagonal else ta
                s = s_refs[buf][h, :nk, qs * blk:(qs + 1) * blk]
                if diagonal:
                    kpos = lax.broadcasted_iota(jnp.int32, (nk, blk), 0)
                    qpos = lax.broadcasted_iota(jnp.int32, (nk, blk), 1) + qs * blk
                    s = jnp.where(kpos <= qpos, s, NEG_BIG)
                c = h * nb + qs
                _softmax_step(s, vT_ref[0, h, :, pl.ds(off, nk)], m_refs[c], l_refs[c], acc_refs[c])

    _pipelined_tiles(qi, logits, next_logits, softmax_pv)
    for h in range(hp):
        for qs in range(nb):
            c = h * nb + qs
            out = acc_refs[c][...] * (1.0 / l_refs[c][...])
            o_ref[0, qs * blk:(qs + 1) * blk, h * MLA_V:(h + 1) * MLA_V] = out.T.astype(BF16)


def _mla_attn(qT, k, vT, ta, hp, cast=()):
    b, hd, _, s = qT.shape
    nq, nhg = s // ta, hd // hp
    nsteps = b * nhg * nq
    crows = [_cast_rows(a.shape[0], nsteps) for a in cast]
    assert all(crows), "cast tables must split into whole tiles per grid step"
    step = lambda bi, h, i: ((bi * nhg + h) * nq + i, 0)
    res = pl.pallas_call(
        functools.partial(_mla_attn_kernel, ta=ta, hp=hp, ncast=len(cast)),
        grid=(b, nhg, nq),
        in_specs=[
            pl.BlockSpec((1, hp, MLA_QK, ta), lambda bi, h, i: (bi, h, 0, i)),
            pl.BlockSpec((1, hp, MLA_QK, ta), lambda bi, h, i: (bi, h, 0, jnp.minimum(i + 1, nq - 1))),
            pl.BlockSpec((1, hp, s, MLA_QK), lambda bi, h, i: (bi, h, 0, 0)),
            pl.BlockSpec((1, hp, MLA_V, s), lambda bi, h, i: (bi, h, 0, 0)),
        ] + [pl.BlockSpec((r, a.shape[1]), step) for a, r in zip(cast, crows)],
        out_specs=[pl.BlockSpec((1, ta, hp * MLA_V), lambda bi, h, i: (bi, i, h))]
        + [pl.BlockSpec((r, a.shape[1]), step) for a, r in zip(cast, crows)],
        out_shape=[jax.ShapeDtypeStruct((b, s, hd * MLA_V), BF16)]
        + [jax.ShapeDtypeStruct(a.shape, BF16) for a in cast],
        scratch_shapes=([pltpu.VMEM((hp, ta, ta), F32)] * 3
                        + _chain_scratch(hp * ta // ATT_BLK, MLA_V)),
        compiler_params=pltpu.CompilerParams(
            dimension_semantics=("arbitrary",) * 3, vmem_limit_bytes=VMEM_LIMIT),
        name="mla_attn",
    )(qT, qT, k, vT, *cast)
    return res[0], res[1:]


def _diff_attn_kernel(cfar_ref, qT_ref, qTn_ref, k_ref, vT_ref, bias_ref, lamp_ref, g_ref, *rest,
                      ta, hp, lambda_init, ncast):
    cast_in, o_ref, cast_out = rest[:ncast], rest[ncast], rest[ncast + 1:2 * ncast + 1]
    s_refs = rest[2 * ncast + 1:2 * ncast + 4]
    state = rest[2 * ncast + 4:]
    hg = pl.program_id(1)
    qi = pl.program_id(2)
    for src_ref, dst_ref in zip(cast_in, cast_out):
        dst_ref[...] = src_ref[...].astype(BF16)
    m_refs, l_refs, acc_refs = _init_chains(state)

    nb = ta // ATT_BLK
    blk = ATT_BLK

    def logits(j, buf, diagonal=False):
        off = pl.multiple_of(j * ta, ta)
        for h in range(hp):
            for c in range(2):
                if not diagonal:
                    s_refs[buf][2 * h + c] = _dot(k_ref[0, h, pl.ds(off, ta), :],
                                                  qT_ref[0, h, c])
                    continue
                for qb in range(nb):
                    nk, q0 = (qb + 1) * blk, qb * blk
                    s_refs[buf][2 * h + c, :nk, q0:q0 + blk] = _dot(
                        k_ref[0, h, pl.ds(off, nk), :], qT_ref[0, h, c, :, q0:q0 + blk])

    def next_logits(buf):
        for h in range(hp):
            for c in range(2):
                s_refs[buf][2 * h + c] = _dot(k_ref[0, h, :ta, :], qTn_ref[0, h, c])

    def softmax_pv(j, buf, kind):
        off = pl.multiple_of(j * ta, ta)
        diagonal = kind == "diag"
        for h in range(hp):
            for qb in range(nb):
                nk = (qb + 1) * blk if diagonal else ta
                vj = vT_ref[0, h, :, pl.ds(off, nk)]
                for c in range(2):
                    s = s_refs[buf][2 * h + c, :nk, qb * blk:(qb + 1) * blk]
                    i = (2 * h + c) * nb + qb
                    if kind == "far" or (kind == "near" and qb * blk + 1 >= MAX_DISTANCE):
                        _softmax_step(s, vj, m_refs[i], None, acc_refs[i],
                                      shift=cfar_ref[hg * hp + h])
                    else:
                        bias = bias_ref[h, 0 if diagonal else 1, :nk, qb * blk:(qb + 1) * blk]
                        _softmax_step(s + bias, vj, m_refs[i], None, acc_refs[i])

    _pipelined_tiles(qi, logits, next_logits, softmax_pv)

    lp = lamp_ref[...]
    lam = (jnp.exp(jnp.sum(lp[0:1] * lp[1:2], axis=-1, keepdims=True))
           - jnp.exp(jnp.sum(lp[2:3] * lp[3:4], axis=-1, keepdims=True)) + lambda_init)
    for h in range(hp):
        for qb in range(nb):
            a0 = acc_refs[(2 * h) * nb + qb][...]
            a1 = acc_refs[(2 * h + 1) * nb + qb][...]
            out = (a0[:DIFF_V] * (1.0 / a0[DIFF_V:DIFF_V + 1])
                   - lam * (a1[:DIFF_V] * (1.0 / a1[DIFF_V:DIFF_V + 1])))
            ms = jnp.mean(out * out, axis=0, keepdims=True)
            out = out * lax.rsqrt(ms + 1e-5) * g_ref[...] * (1.0 - lambda_init)
            o_ref[0, qb * blk:(qb + 1) * blk, h * DIFF_V:(h + 1) * DIFF_V] = out.T.astype(BF16)


def _cast_rows(rows_total, nsteps):
    rows = rows_total // nsteps
    return rows if rows * nsteps == rows_total and rows % BF16_ROWS == 0 else 0


def _diff_attn(cfar, qT, k, vT, bias_tiles, lam_params, g_col, ta, hp, lambda_init,
               cast=(), cast_part=(0, 1)):
    b, hd, _, _, s = qT.shape
    nq, nhg = s // ta, hd // hp
    nsteps = b * nhg * nq
    part, nparts = cast_part
    crows = [_cast_rows(a.shape[0] // nparts, nsteps) for a in cast]
    assert all(crows), "cast tables must split into whole tiles per grid step"
    step = lambda bi, h, i: (bi * nhg + h) * nq + i
    grid_spec = pltpu.PrefetchScalarGridSpec(
        num_scalar_prefetch=1,
        grid=(b, nhg, nq),
        in_specs=[
            pl.BlockSpec((1, hp, 2, LANES, ta), lambda bi, h, i, cf: (bi, h, 0, 0, i)),
            pl.BlockSpec((1, hp, 2, LANES, ta),
                         lambda bi, h, i, cf: (bi, h, 0, 0, jnp.minimum(i + 1, nq - 1))),
            pl.BlockSpec((1, hp, s, LANES), lambda bi, h, i, cf: (bi, h, 0, 0)),
            pl.BlockSpec((1, hp, DIFF_VA, s), lambda bi, h, i, cf: (bi, h, 0, 0)),
            pl.BlockSpec((hp, 2, ta, ta), lambda bi, h, i, cf: (h, 0, 0, 0)),
            pl.BlockSpec(lam_params.shape, lambda bi, h, i, cf: (0, 0)),
            pl.BlockSpec(g_col.shape, lambda bi, h, i, cf: (0, 0)),
        ] + [pl.BlockSpec((r, a.shape[1]), lambda bi, h, i, cf: (part * nsteps + step(bi, h, i), 0))
             for a, r in zip(cast, crows)],
        out_specs=[pl.BlockSpec((1, ta, hp * DIFF_V), lambda bi, h, i, cf: (bi, i, h))]
        + [pl.BlockSpec((r, a.shape[1]), lambda bi, h, i, cf: (step(bi, h, i), 0))
           for a, r in zip(cast, crows)],
        scratch_shapes=([pltpu.VMEM((2 * hp, ta, ta), F32)] * 3
                        + _chain_scratch(2 * hp * ta // ATT_BLK, DIFF_VA)),
    )
    res = pl.pallas_call(
        functools.partial(_diff_attn_kernel, ta=ta, hp=hp, lambda_init=lambda_init,
                          ncast=len(cast)),
        grid_spec=grid_spec,
        out_shape=[jax.ShapeDtypeStruct((b, s, hd * DIFF_V), BF16)]
        + [jax.ShapeDtypeStruct((a.shape[0] // nparts, a.shape[1]), BF16) for a in cast],
        compiler_params=pltpu.CompilerParams(
            dimension_semantics=("arbitrary",) * 3, vmem_limit_bytes=VMEM_LIMIT),
        name="diff_attn",
    )(cfar, qT, qT, k, vT, bias_tiles, lam_params, g_col, *cast)
    return res[0], res[1:]


def _split_bf16(v):
    hi = v.astype(BF16)
    return hi, (v - hi.astype(F32)).astype(BF16)


def _oproj_route_kernel(am_ref, ad_ref, x_ref, wo_ref, g_ref, b_ref, rhl_ref, rh_ref,
                        o_ref, ob_ref, r_ref):
    tm = x_ref.shape[0]
    hh = tm // 2
    halves = (slice(0, hh), slice(hh, tm))
    nt = (((1,), (1,)), ((), ()))
    mix = [_dot(am_ref[r, :], wo_ref[0, :MLA_W, :]) + _dot(ad_ref[r, :], wo_ref[0, MLA_W:, :])
           for r in halves]
    ys = [_layer_norm_rows(DN_ALPHA * x_ref[r, :] + m, g_ref[...], b_ref[...])
          for r, m in zip(halves, mix)]
    splits = [_split_bf16(y) for y in ys]
    boths = [lax.dot_general(rhl_ref[...], yhi, nt, preferred_element_type=F32)
             for yhi, _ in splits]
    lows = [lax.dot_general(rh_ref[...], ylo, nt, preferred_element_type=F32) for _, ylo in splits]
    for r, y, (yhi, _), both, low in zip(halves, ys, splits, boths, lows):
        logits = both[:ROUTER_ROWS] + both[ROUTER_ROWS:] + low
        row = lax.broadcasted_iota(jnp.int32, logits.shape, 0)
        logits = jnp.where(row < N_EXPERTS, logits, NEG_BIG)
        m1 = jnp.max(logits, axis=0, keepdims=True)
        i1 = jnp.min(jnp.where(logits == m1, row, ROUTER_ROWS), axis=0, keepdims=True)
        rest_l = jnp.where(row == i1, NEG_BIG, logits)
        m2 = jnp.max(rest_l, axis=0, keepdims=True)
        i2 = jnp.min(jnp.where(rest_l == m2, row, ROUTER_ROWS), axis=0, keepdims=True)
        e2 = jnp.exp(m2 - m1)
        g1 = 1.0 / (1.0 + e2)
        g2 = e2 * g1
        row8 = lax.broadcasted_iota(jnp.int32, (REC_ROWS, hh), 0)
        rec = jnp.where(row8 == 0, i1.astype(F32),
                        jnp.where(row8 == 1, i2.astype(F32),
                                  jnp.where(row8 == 2, g1, jnp.where(row8 == 3, g2, 0.0))))
        o_ref[r, :] = y
        ob_ref[r, :] = yhi
        r_ref[:, r] = rec


def _oproj_route(a_mla, a_diff, x2d, l, wo, g, b, rhl, rh, tm):
    t, d = x2d.shape
    row = lambda w: pl.BlockSpec((tm, w), lambda i: (i, 0))
    full = lambda a: pl.BlockSpec(a.shape, lambda i: (0,) * a.ndim)
    return pl.pallas_call(
        _oproj_route_kernel,
        grid=(t // tm,),
        in_specs=[row(MLA_W), row(DIFF_W), row(d),
                  pl.BlockSpec((1,) + wo.shape[1:], lambda i: (l, 0, 0)),
                  full(g), full(b), full(rhl), full(rh)],
        out_specs=[row(d), row(d), pl.BlockSpec((REC_ROWS, tm), lambda i: (0, i))],
        out_shape=[jax.ShapeDtypeStruct((t, d), F32), jax.ShapeDtypeStruct((t, d), BF16),
                   jax.ShapeDtypeStruct((REC_ROWS, t), F32)],
        compiler_params=pltpu.CompilerParams(
            dimension_semantics=("arbitrary",), vmem_limit_bytes=VMEM_LIMIT),
        name="oproj_route",
    )(a_mla, a_diff, x2d, wo, g, b, rhl, rh)


def _oproj_ffn_kernel(am_ref, ad_ref, x_ref, wo_ref, g1_ref, b1_ref, wg_ref, wu_ref, wd_ref,
                      g2_ref, b2_ref, o_ref):
    mix = _dot(am_ref[...], wo_ref[0, :MLA_W, :]) + _dot(ad_ref[...], wo_ref[0, MLA_W:, :])
    x1 = _layer_norm_rows(DN_ALPHA * x_ref[...] + mix, g1_ref[...], b1_ref[...])
    xb = x1.astype(BF16)
    gate = _dot(xb, wg_ref[0])
    up = _dot(xb, wu_ref[0])
    hid = (gate * jax.nn.sigmoid(gate) * up).astype(BF16)
    f = _dot(hid, wd_ref[0])
    o_ref[...] = _layer_norm_rows(DN_ALPHA * x1 + f, g2_ref[...], b2_ref[...])


def _oproj_ffn(a_mla, a_diff, x2d, l, wo, g1, b1, li, wg, wu, wd, g2, b2, tm):
    t, d = x2d.shape
    row = lambda w: pl.BlockSpec((tm, w), lambda i: (i, 0))
    layer = lambda a, k: pl.BlockSpec((1,) + a.shape[1:], lambda i: (k,) + (0,) * (a.ndim - 1),
                                      pipeline_mode=pl.Buffered(1))
    vec = lambda a: pl.BlockSpec(a.shape, lambda i: (0, 0))
    return pl.pallas_call(
        _oproj_ffn_kernel,
        grid=(t // tm,),
        in_specs=[row(MLA_W), row(DIFF_W), row(d), layer(wo, l), vec(g1), vec(b1),
                  layer(wg, li), layer(wu, li), layer(wd, li), vec(g2), vec(b2)],
        out_specs=row(d),
        out_shape=jax.ShapeDtypeStruct((t, d), F32),
        compiler_params=pltpu.CompilerParams(
            dimension_semantics=("arbitrary",), vmem_limit_bytes=VMEM_LIMIT),
        name="oproj_ffn",
    )(a_mla, a_diff, x2d, wo, g1, b1, wg, wu, wd, g2, b2)


def _dispatch_kernel(ue_ref, uoff_ref, unc_ref, x_ref, post_ref, o_ref, *, rows, unit, kc):
    i = pl.program_id(0)
    t = x_ref.shape[0]
    ups = rows // unit

    def chunk(u, c):
        nominal = uoff_ref[u] + c * kc
        off = pl.multiple_of(jnp.minimum(nominal, t - kc), LANES)
        row_pos = u * unit + lax.broadcasted_iota(jnp.int32, (unit, 1), 0)
        dest = post_ref[pl.ds(ue_ref[u], 1), pl.ds(off, kc)]
        tok = off + lax.broadcasted_iota(jnp.int32, (1, kc), 1)
        dest = jnp.where(tok >= nominal, dest, -1)
        sel = jnp.where(dest == row_pos, 1.0, 0.0).astype(BF16)
        return _dot(sel, x_ref[pl.ds(off, kc), :]).astype(BF16)

    for r in range(ups):
        o_ref[r * unit:(r + 1) * unit, :] = chunk(i * ups + r, 0)

    for r in range(ups):
        def more(c, carry, r=r):
            o_ref[r * unit:(r + 1) * unit, :] += chunk(i * ups + r, c)
            return carry

        lax.fori_loop(1, unc_ref[i * ups + r], more, 0)


def _dispatch(unit_e, uoff, unc, xb, post, n_rows, rows, unit, kc):
    t, d = xb.shape
    const = lambda shape: pl.BlockSpec(shape, lambda i, *_: (0,) * len(shape),
                                       pipeline_mode=pl.Buffered(1))
    grid_spec = pltpu.PrefetchScalarGridSpec(
        num_scalar_prefetch=3,
        grid=(n_rows // rows,),
        in_specs=[const(xb.shape), const(post.shape)],
        out_specs=pl.BlockSpec((rows, d), lambda i, *_: (i, 0)),
    )
    return pl.pallas_call(
        functools.partial(_dispatch_kernel, rows=rows, unit=unit, kc=kc),
        grid_spec=grid_spec,
        out_shape=jax.ShapeDtypeStruct((n_rows, d), BF16),
        compiler_params=pltpu.CompilerParams(
            dimension_semantics=("arbitrary",), vmem_limit_bytes=VMEM_LIMIT),
        name="dispatch",
    )(unit_e, uoff, unc, xb, post)


def _gffn_kernel(te_ref, tv_ref, x_ref, wg_ref, wu_ref, wd_ref, o_ref, acc_ref, *, nf):
    i = pl.program_id(0)
    f = pl.program_id(1)
    status = tv_ref[i]
    tm = x_ref.shape[0]
    half = tm // 2

    def swiglu_rows(rows):
        xb = x_ref[:rows, :]
        gate = _dot(xb, wg_ref[0, 0])
        up = _dot(xb, wu_ref[0, 0])
        hid = (gate * jax.nn.sigmoid(gate) * up).astype(BF16)
        part = _dot(hid, wd_ref[0, 0])

        @pl.when(f == 0)
        def _():
            acc_ref[:rows, :] = part

        @pl.when(f > 0)
        def _():
            acc_ref[:rows, :] += part

    @pl.when(status == 2)
    def _():
        swiglu_rows(tm)

    @pl.when(status == 1)
    def _():
        swiglu_rows(half)

    @pl.when(jnp.logical_and(f == nf - 1, status == 2))
    def _():
        o_ref[...] = acc_ref[...].astype(BF16)

    @pl.when(jnp.logical_and(f == nf - 1, status == 1))
    def _():
        o_ref[:half, :] = acc_ref[:half, :].astype(BF16)
        o_ref[half:, :] = jnp.zeros((tm - half, o_ref.shape[1]), BF16)

    @pl.when(jnp.logical_and(f == nf - 1, status == 0))
    def _():
        o_ref[...] = jnp.zeros(o_ref.shape, BF16)


def _gffn(tile_e, tile_v, xs, li, wg, wu, wd, tm, fc):
    p, d = xs.shape
    nt = p // tm
    nf = wg.shape[3] // fc
    fidx = lambda i, f, te, tv: jnp.where(tv[i] > 0, f, nf - 1)
    grid_spec = pltpu.PrefetchScalarGridSpec(
        num_scalar_prefetch=2,
        grid=(nt, nf),
        in_specs=[
            pl.BlockSpec((tm, d), lambda i, f, te, tv: (i, 0)),
            pl.BlockSpec((1, 1, d, fc), lambda i, f, te, tv: (li, te[i], 0, fidx(i, f, te, tv))),
            pl.BlockSpec((1, 1, d, fc), lambda i, f, te, tv: (li, te[i], 0, fidx(i, f, te, tv))),
            pl.BlockSpec((1, 1, fc, d), lambda i, f, te, tv: (li, te[i], fidx(i, f, te, tv), 0)),
        ],
        out_specs=pl.BlockSpec((tm, d), lambda i, f, te, tv: (i, 0)),
        scratch_shapes=[pltpu.VMEM((tm, d), F32)],
    )
    return pl.pallas_call(
        functools.partial(_gffn_kernel, nf=nf),
        grid_spec=grid_spec,
        out_shape=jax.ShapeDtypeStruct((p, d), BF16),
        compiler_params=pltpu.CompilerParams(
            dimension_semantics=("arbitrary", "arbitrary"), vmem_limit_bytes=VMEM_LIMIT),
        name="grouped_ffn",
    )(tile_e, tile_v, xs, wg, wu, wd)


def _combine_kernel(ral_ref, rlo_ref, rhi_ref, x_ref, rec_ref, g_ref, b_ref, ys_ref, o_ref,
                    win_ref, xwin_ref, acc_ref, sems, xsem, *, w):
    j = pl.program_id(0)
    nj = pl.num_programs(0)
    slot = lax.rem(j, 2)
    p_rows = ys_ref.shape[0]

    def win_copy(jj, e, sl):
        r = pl.multiple_of(ral_ref[jj * N_EXPERTS + e], BF16_ROWS)
        return pltpu.make_async_copy(ys_ref.at[pl.ds(r, w)], win_ref.at[sl, pl.ds(e * w, w)],
                                     sems.at[sl, e])

    @pl.when(j == 0)
    def _():
        for e in range(N_EXPERTS):
            win_copy(0, e, 0).start()

    @pl.when(j + 1 < nj)
    def _():
        for e in range(N_EXPERTS):
            win_copy(j + 1, e, 1 - slot).start()

    rec = rec_ref[...]
    p1, p2, g1, g2 = rec[:, 0:1], rec[:, 1:2], rec[:, 2:3], rec[:, 3:4]

    def sel(base, lo, hi):
        rows = base + lax.broadcasted_iota(jnp.int32, (1, w), 1)
        rp = jnp.where((rows >= lo) & (rows < hi), rows, -1).astype(F32)
        return (jnp.where(p1 == rp, g1, 0.0) + jnp.where(p2 == rp, g2, 0.0)).astype(BF16)

    lane = lax.broadcasted_iota(jnp.int32, (1, N_EXPERTS * w), 1)
    rp = jnp.full((1, N_EXPERTS * w), -1, jnp.int32)
    for e in range(N_EXPERTS):
        win_copy(j, e, slot).wait()
        k = j * N_EXPERTS + e
        rows = ral_ref[k] + lane - e * w
        mine = ((lane >= e * w) & (lane < (e + 1) * w)
                & (rows >= rlo_ref[k]) & (rows < rhi_ref[k]))
        rp = jnp.where(mine, rows, rp)
    rp = rp.astype(F32)
    tm = rec.shape[0]
    step = min(tm, 256)
    for r0 in range(0, tm, step):
        rows = slice(r0, r0 + step)
        sel_rows = (jnp.where(p1[rows] == rp, g1[rows], 0.0)
                    + jnp.where(p2[rows] == rp, g2[rows], 0.0)).astype(BF16)
        acc_ref[rows, :] = _dot(sel_rows, win_ref[slot])

    for e in range(N_EXPERTS):
        k = j * N_EXPERTS + e
        nwin = (rhi_ref[k] - ral_ref[k] + (w - 1)) // w

        def extra(wi, carry, k=k):
            nominal = ral_ref[k] + wi * w
            base = pl.multiple_of(jnp.minimum(nominal, p_rows - w), BF16_ROWS)
            cp = pltpu.make_async_copy(ys_ref.at[pl.ds(base, w)], xwin_ref, xsem)
            cp.start()
            cp.wait()
            acc_ref[...] += _dot(sel(base, nominal, rhi_ref[k]), xwin_ref[...])
            return carry

        lax.fori_loop(1, nwin, extra, 0)

    o_ref[...] = _layer_norm_rows(DN_ALPHA * x_ref[...] + acc_ref[...], g_ref[...], b_ref[...])


def _combine(ral, rlo, rhi, x2d, rec2, g, b, ys, tm, w):
    t, d = x2d.shape
    row = lambda c: pl.BlockSpec((tm, c), lambda i, *_: (i, 0))
    full = lambda shape: pl.BlockSpec(shape, lambda i, *_: (0,) * len(shape))
    grid_spec = pltpu.PrefetchScalarGridSpec(
        num_scalar_prefetch=3,
        grid=(t // tm,),
        in_specs=[row(d), row(rec2.shape[1]), full(g.shape), full(b.shape),
                  pl.BlockSpec(memory_space=pl.ANY)],
        out_specs=row(d),
        scratch_shapes=[pltpu.VMEM((2, N_EXPERTS * w, d), BF16), pltpu.VMEM((w, d), BF16),
                        pltpu.VMEM((tm, d), F32), pltpu.SemaphoreType.DMA((2, N_EXPERTS)),
                        pltpu.SemaphoreType.DMA(())],
    )
    return pl.pallas_call(
        functools.partial(_combine_kernel, w=w),
        grid_spec=grid_spec,
        out_shape=jax.ShapeDtypeStruct((t, d), F32),
        compiler_params=pltpu.CompilerParams(
            dimension_semantics=("arbitrary",), vmem_limit_bytes=VMEM_LIMIT),
        name="combine",
    )(ral, rlo, rhi, x2d, rec2, g, b, ys)


def _route_plan(rec, nt, tm_e, unit, kc, tm_c, w):
    t = rec.shape[1]
    i32 = jnp.int32
    ar = jnp.arange(N_EXPERTS, dtype=i32)
    m1 = (rec[0].astype(i32)[:, None] == ar[None, :]).astype(i32)
    m2 = (rec[1].astype(i32)[:, None] == ar[None, :]).astype(i32)
    m = m1 + m2
    cinc = jnp.cumsum(m, axis=0)
    counts = cinc[-1]
    padded = ((counts + tm_e - 1) // tm_e) * tm_e
    ends = jnp.cumsum(padded)
    starts = ends - padded
    posmat = starts[None, :] + (cinc - m)
    pos1 = jnp.sum(m1 * posmat, axis=1)
    pos2 = jnp.sum(m2 * posmat, axis=1)
    post = jnp.where(m.T > 0, posmat.T, -1)
    rec2 = jnp.stack([pos1.astype(F32), pos2.astype(F32), rec[2], rec[3]]
                     + [jnp.zeros((t,), F32)] * 4, axis=1)
    tile_start = jnp.arange(nt, dtype=i32) * tm_e
    tile_e = jnp.sum((tile_start[:, None] >= ends[None, :]).astype(i32), axis=1)
    tile_v = tile_e < N_EXPERTS
    last_e = jnp.max(jnp.where(tile_v, tile_e, 0))
    tile_e = jnp.where(tile_v, tile_e, last_e)
    upt = tm_e // unit
    nu = nt * upt
    oh = (jnp.repeat(tile_e, upt)[:, None] == ar[None, :]).astype(i32)
    q0 = jnp.arange(nu, dtype=i32) * unit - jnp.sum(oh * starts[None, :], axis=1)
    q1 = jnp.minimum(q0 + unit, jnp.sum(oh * counts[None, :], axis=1))
    has = jnp.repeat(tile_v, upt) & (q1 > q0)
    cb = jnp.concatenate([jnp.zeros((1, N_EXPERTS), i32), cinc[LANES - 1::LANES]], axis=0)
    cbu = jnp.sum(oh[:, None, :] * cb[None, :, :], axis=2)
    jlo = jnp.sum((cbu[:, 1:] <= q0[:, None]).astype(i32), axis=1)
    jend = jnp.sum((cbu[:, :-1] < q1[:, None]).astype(i32), axis=1)
    uoff = jnp.where(has, jlo * LANES, 0)
    unc = jnp.where(has, (jend * LANES - uoff + kc - 1) // kc, 0)
    cbt = jnp.concatenate([jnp.zeros((1, N_EXPERTS), i32), cinc[tm_c - 1::tm_c]], axis=0)
    r_lo = starts[None, :] + cbt[:-1]
    r_hi = starts[None, :] + cbt[1:]
    ral = jnp.minimum((r_lo // BF16_ROWS) * BF16_ROWS, nt * tm_e - w)
    oh_t = (tile_e[:, None] == ar[None, :]).astype(i32)
    tile_rows = (jnp.sum(oh_t * counts[None, :], axis=1)
                 - (tile_start - jnp.sum(oh_t * starts[None, :], axis=1)))
    tile_status = jnp.where(tile_v, jnp.where(tile_rows > tm_e // 2, 2, 1), 0)
    return (rec2, post, tile_e, tile_status.astype(i32), uoff, unc,
            ral.reshape(-1), r_lo.reshape(-1), r_hi.reshape(-1))


def _moe(x2d, xb, rec, li, wg, wu, wd, g, b, tm_c, tm_e, fc, unit, kc, w):
    t, d = x2d.shape
    nt = (2 * t) // tm_e + N_EXPERTS
    rec2, post, tile_e, tile_v, uoff, unc, ral, rlo, rhi = _route_plan(
        rec, nt, tm_e, unit, kc, tm_c, w)
    n_rows = nt * tm_e
    rows_d = 2 * tm_e if n_rows % (2 * tm_e) == 0 else tm_e
    xs = _dispatch(jnp.repeat(tile_e, tm_e // unit), uoff, unc, xb, post, n_rows, rows_d, unit, kc)
    ys = _gffn(tile_e, tile_v, xs, li, wg, wu, wd, tm_e, fc)
    return _combine(ral, rlo, rhi, x2d, rec2, g, b, ys, tm_c, w)


def _rotate_half_cols(w):
    half = w.shape[-1] // 2
    return jnp.concatenate([-w[..., half:], w[..., :half]], axis=-1)


def _t5_bucket(n):
    max_exact = NUM_BUCKETS // 2
    nf = jnp.maximum(n, 1).astype(F32)
    large = max_exact + (jnp.log(nf / max_exact) / math.log(MAX_DISTANCE / max_exact)
                         * (NUM_BUCKETS - max_exact)).astype(jnp.int32)
    large = jnp.minimum(large, NUM_BUCKETS - 1)
    return jnp.where(n < max_exact, n, large)


def _pick(n, prefs):
    for p in prefs:
        if n % p == 0:
            return p
    raise ValueError(f"no tile for extent {n}")


def kernel(x, rel_bias, w_in, mla_q_norm, mla_kv_norm, mla_w_uq, mla_w_uk, mla_w_uv, diff_lambda, diff_norm, w_o, ln1_g, ln1_b, ln2_g, ln2_b, ffn_w_gate, ffn_w_up, ffn_w_down, moe_router, moe_w_gate, moe_w_up, moe_w_down):
    b, s, d = x.shape
    t = b * s
    assert d == D_MODEL
    ta = _pick(s, (512, 256))
    assert ta % ATT_BLK == 0
    assert s >= 2 * ta or s == ta
    tm_proj = _pick(s, (512, 256, 128))
    tm_tok = _pick(t, (512, 256, 128))
    tm_e = _pick(2 * t, (512, 256))
    tm_c = _pick(t, (512, 256))
    unit, kc, wrows = 128, 768, 224
    assert tm_e % unit == 0 and t >= kc and t % LANES == 0
    fc = D_FF_EXPERT // 2

    pos = jnp.arange(s, dtype=F32)
    inv = 1.0 / (ROPE_THETA ** (jnp.arange(0, MLA_ROPE, 2, dtype=F32) / MLA_ROPE))
    ang = pos[:, None] * inv[None, :]
    cos4 = jnp.tile(jnp.cos(ang), (1, 2 * MLA_HEADS))
    sin4 = jnp.tile(jnp.sin(ang), (1, 2 * MLA_HEADS))
    rb = rel_bias.astype(F32)
    kk = jnp.arange(ta)[:, None]
    qq = jnp.arange(ta)[None, :]
    tiles = []
    for off in (0, ta):
        dist = qq + off - kk
        bucket = _t5_bucket(jnp.maximum(dist, 0))
        bt = jnp.zeros((DIFF_HEADS, ta, ta), F32)
        for bk in range(NUM_BUCKETS):
            bt = bt + jnp.where((bucket == bk)[None], rb[bk][:, None, None], 0.0)
        tiles.append(jnp.where((dist >= 0)[None], bt * LOG2E, NEG_BIG))
    bias_tiles = jnp.stack(tiles, axis=1)
    assert ta + 1 >= MAX_DISTANCE
    cfar = rb[NUM_BUCKETS - 1] * LOG2E
    n_moe = moe_w_gate.shape[0]
    moe_tables = (moe_w_gate.reshape(-1, D_FF_EXPERT), moe_w_up.reshape(-1, D_FF_EXPERT),
                  moe_w_down.reshape(-1, d))

    w_t = jnp.swapaxes(w_in, 1, 2)
    kr_t = w_t[:, _KR0:_KR0 + MLA_ROPE]
    half = MLA_ROPE // 2
    win = jnp.concatenate([w_t[:, :_KR0 + MLA_ROPE], -kr_t[:, half:], kr_t[:, :half],
                           w_t[:, _KR0 + MLA_ROPE:]], axis=1).astype(BF16)
    nl = w_in.shape[0]
    uq = mla_w_uq.reshape(nl, MLA_Q_RANK, MLA_HEADS, MLA_QK)
    uq_r = uq[..., MLA_NOPE:]
    wuq = jnp.concatenate([uq[..., :MLA_NOPE].reshape(nl, MLA_Q_RANK, -1),
                           uq_r.reshape(nl, MLA_Q_RANK, -1),
                           _rotate_half_cols(uq_r).reshape(nl, MLA_Q_RANK, -1)], axis=2).astype(BF16)
    wukv = jnp.concatenate([mla_w_uk, mla_w_uv], axis=2).astype(BF16)
    qn, kvn = mla_q_norm[:, None, :], mla_kv_norm[:, None, :]
    late = (w_o, ffn_w_gate, ffn_w_up, ffn_w_down)
    late_tables = tuple(w.reshape(-1, w.shape[-1]) for w in late)
    mla_steps = b * (MLA_HEADS // MLA_HP) * (s // ta)
    ride = all(_cast_rows(a.shape[0], mla_steps) for a in late_tables)
    if not ride:
        wo, ffn_wg, ffn_wu, ffn_wd = (w.astype(BF16) for w in late)
    rpad =jnp.pad(jnp.swapaxes(moe_router.astype(F32), 1, 2),
                   ((0, 0), (0, ROUTER_ROWS - N_EXPERTS), (0, 0)))
    rhi = rpad.astype(BF16)
    rhl = jnp.concatenate([rhi, (rpad - rhi.astype(F32)).astype(BF16)], axis=1)

    xc = x
    for l in range(DEPTH):
        lambda_init = 0.8 - 0.6 * math.exp(-0.3 * l)
        qT, k, vT, dqT, dk, dvT = _proj(xc, l, win, qn, kvn, wuq, wukv, cos4, sin4, tm_proj)
        a_mla, casts = _mla_attn(qT, k, vT, ta, MLA_HP,
                                 cast=late_tables if ride and l == 0 else ())
        a_mla = a_mla.reshape(t, MLA_W)
        if casts:
            wo, ffn_wg, ffn_wu, ffn_wd = (c.reshape(w.shape) for c, w in zip(casts, late))
        i = l // 2
        dense = l % 2 == 0
        a_diff, casts = _diff_attn(cfar, dqT, dk, dvT, bias_tiles, diff_lambda[l].astype(F32),
                                   diff_norm[l].astype(F32)[:, None], ta, DIFF_HP, lambda_init,
                                   cast=moe_tables if dense and i < n_moe else (),
                                   cast_part=(i, n_moe))
        a_diff = a_diff.reshape(t, DIFF_W)
        if casts:
            moe_wg = casts[0].reshape(1, N_EXPERTS, d, D_FF_EXPERT)
            moe_wu = casts[1].reshape(1, N_EXPERTS, d, D_FF_EXPERT)
            moe_wd = casts[2].reshape(1, N_EXPERTS, D_FF_EXPERT, d)
        x2d = xc.reshape(t, d)
        g1, b1 = ln1_g[l][None, :], ln1_b[l][None, :]
        g2, b2 = ln2_g[l][None, :], ln2_b[l][None, :]
        if dense:
            x2 = _oproj_ffn(a_mla, a_diff, x2d, l, wo, g1, b1, i, ffn_wg, ffn_wu, ffn_wd,
                            g2, b2, tm_tok)
        else:
            x1, x1b, rec = _oproj_route(a_mla, a_diff, x2d, l, wo, g1, b1, rhl[i], rhi[i], tm_tok)
            x2 = _moe(x1, x1b, rec, 0, moe_wg, moe_wu, moe_wd, g2, b2,
                      tm_c, tm_e, fc, unit, kc, wrows)
        xc = x2.reshape(b, s, d)
    return xc
```

```python
import functools
import math

import jax
import jax.numpy as jnp
from jax import lax
from jax.experimental import pallas as pl
from jax.experimental.pallas import tpu as pltpu

D_MODEL = 1024
DEPTH = 4
MLA_HEADS = 4
MLA_NOPE = 128
MLA_ROPE = 64
MLA_V = 128
MLA_Q_RANK = 384
MLA_KV_RANK = 256
ROPE_THETA = 10000.0
DIFF_HEADS = 4
DIFF_QK = 64
DIFF_V = 2 * DIFF_QK
NUM_BUCKETS = 32
MAX_DISTANCE = 128
D_FF_DENSE = 2816
N_EXPERTS = 8
D_FF_EXPERT = 3584
DN_ALPHA = (2 * DEPTH) ** 0.25

MLA_QK = MLA_NOPE + MLA_ROPE
MLA_W = MLA_HEADS * MLA_V
DIFF_W = DIFF_HEADS * DIFF_V
BF16_ROWS = 16
DIFF_VA = DIFF_V + BF16_ROWS
_CQ0, _CKV0 = 0, MLA_Q_RANK
_KR0 = _CKV0 + MLA_KV_RANK
_DQ0 = _KR0 + 2 * MLA_ROPE
_DK0 = _DQ0 + DIFF_W
_DV0 = _DK0 + DIFF_W
IN_COLS_W = _DV0 + DIFF_W

LANES = 128
VMEM_LIMIT = 56 * 1024 * 1024
NEG_BIG = -1e30
ROUTER_ROWS = 16
REC_ROWS = 8
LOG2E = math.log2(math.e)
ATT_BLK = 256
MLA_HP = 4
DIFF_HP = 2

BF16 = jnp.bfloat16
F32 = jnp.float32


def _dot(a, b):
    return jnp.dot(a, b, preferred_element_type=F32)


def _layer_norm_rows(y, g, b):
    mu = jnp.mean(y, axis=-1, keepdims=True)
    d = y - mu
    var = jnp.mean(d * d, axis=-1, keepdims=True)
    return d * lax.rsqrt(var + 1e-5) * g + b


def _rms_rows(y, g, eps):
    return y * lax.rsqrt(jnp.mean(y * y, axis=-1, keepdims=True) + eps) * g


def _proj_kernel(x_ref, win_ref, qn_ref, kvn_ref, wuq_ref, wukv_ref, cos_ref, sin_ref,
                 qT_ref, k_ref, vT_ref, dqT_ref, dk_ref, dvT_ref):
    x = x_ref[0].astype(BF16)
    h = lax.dot_general(x, win_ref[0], (((1,), (1,)), ((), ())),
                        preferred_element_type=F32)
    cqn = _rms_rows(h[:, _CQ0:_CQ0 + MLA_Q_RANK], qn_ref[0], 1e-6)
    ckvn = _rms_rows(h[:, _CKV0:_CKV0 + MLA_KV_RANK], kvn_ref[0], 1e-6)
    q = _dot(cqn.astype(BF16), wuq_ref[0]) * (MLA_QK ** -0.5 * LOG2E)
    kv = _dot(ckvn.astype(BF16), wukv_ref[0])
    cos4, sin4 = cos_ref[...], sin_ref[...]
    nr = MLA_HEADS * MLA_ROPE
    qr = q[:, MLA_W:MLA_W + nr] * cos4 + q[:, MLA_W + nr:] * sin4
    kr = (h[:, _KR0:_KR0 + MLA_ROPE] * cos4[:, :MLA_ROPE]
          + h[:, _KR0 + MLA_ROPE:_KR0 + 2 * MLA_ROPE] * sin4[:, :MLA_ROPE])
    qrT = qr.T
    for hh in range(MLA_HEADS):
        c0 = hh * LANES
        qnT = q[:, c0:c0 + MLA_NOPE].T
        qT_ref[0, hh] = jnp.concatenate(
            [qnT, qrT[hh * MLA_ROPE:(hh + 1) * MLA_ROPE]], axis=0).astype(BF16)
        k_ref[0, hh] = jnp.concatenate([kv[:, c0:c0 + MLA_NOPE], kr], axis=1).astype(BF16)
        vT_ref[0, hh] = kv[:, MLA_W + c0:MLA_W + c0 + MLA_V].T.astype(BF16)
        dqT = (h[:, _DQ0 + c0:_DQ0 + c0 + LANES] * (DIFF_QK ** -0.5 * LOG2E)).T.astype(BF16)
        zeros = jnp.zeros((DIFF_QK, x.shape[0]), BF16)
        dqT_ref[0, hh, 0] = jnp.concatenate([dqT[:DIFF_QK], zeros], axis=0)
        dqT_ref[0, hh, 1] = jnp.concatenate([zeros, dqT[DIFF_QK:]], axis=0)
        dk_ref[0, hh] = h[:, _DK0 + c0:_DK0 + c0 + LANES].astype(BF16)
        dvT_ref[0, hh] = jnp.concatenate(
            [h[:, _DV0 + c0:_DV0 + c0 + DIFF_V].T, jnp.ones((DIFF_VA - DIFF_V, x.shape[0]), F32)],
            axis=0).astype(BF16)


def _proj(x, l, win, qn, kvn, wuq, wukv, cos4, sin4, tm):
    b, s, d = x.shape
    hd = MLA_HEADS
    layer = lambda a: pl.BlockSpec((1,) + a.shape[1:], lambda bi, i: (l,) + (0,) * (a.ndim - 1))
    tmaj = lambda w: pl.BlockSpec((1, hd, tm, w), lambda bi, i: (bi, 0, i, 0))
    fmaj = lambda w: pl.BlockSpec((1, hd, w, tm), lambda bi, i: (bi, 0, 0, i))
    sds = jax.ShapeDtypeStruct
    return pl.pallas_call(
        _proj_kernel,
        grid=(b, s // tm),
        in_specs=[
            pl.BlockSpec((1, tm, d), lambda bi, i: (bi, i, 0)),
            layer(win), layer(qn), layer(kvn), layer(wuq), layer(wukv),
            pl.BlockSpec((tm, cos4.shape[1]), lambda bi, i: (i, 0)),
            pl.BlockSpec((tm, sin4.shape[1]), lambda bi, i: (i, 0)),
        ],
        out_specs=[fmaj(MLA_QK), tmaj(MLA_QK), fmaj(MLA_V),
                   pl.BlockSpec((1, hd, 2, LANES, tm), lambda bi, i: (bi, 0, 0, 0, i)),
                   tmaj(LANES), fmaj(DIFF_VA)],
        out_shape=[
            sds((b, hd, MLA_QK, s), BF16), sds((b, hd, s, MLA_QK), BF16), sds((b, hd, MLA_V, s), BF16),
            sds((b, hd, 2, LANES, s), BF16), sds((b, hd, s, LANES), BF16), sds((b, hd, DIFF_VA, s), BF16),
        ],
        compiler_params=pltpu.CompilerParams(
            dimension_semantics=("arbitrary", "arbitrary"), vmem_limit_bytes=VMEM_LIMIT),
        name="proj",
    )(x, win, qn, kvn, wuq, wukv, cos4, sin4)


def _softmax_step(s, vj, m_ref, l_ref, acc_ref, shift=None):
    m_prev = m_ref[...]
    smax = jnp.max(s, axis=0, keepdims=True)
    if shift is not None:
        smax = smax + shift
    m_new = jnp.maximum(m_prev, smax)
    a = jnp.exp2(m_prev - m_new)
    p = jnp.exp2(s - (m_new if shift is None else m_new - shift))
    if l_ref is not None:
        l_ref[...] = a * l_ref[...] + jnp.sum(p, axis=0, keepdims=True)
    acc_ref[...] = a * acc_ref[...] + _dot(vj, p.astype(BF16))
    m_ref[...] = m_new


def _chain_scratch(n, dv):
    return [pltpu.VMEM((1, ATT_BLK), F32)] * (2 * n) + [pltpu.VMEM((dv, ATT_BLK), F32)] * n


def _init_chains(state):
    n = len(state) // 3
    m_refs, l_refs, acc_refs = state[:n], state[n:2 * n], state[2 * n:]
    for m_ref, l_ref, acc_ref in zip(m_refs, l_refs, acc_refs):
        m_ref[...] = jnp.full(m_ref.shape, NEG_BIG, F32)
        l_ref[...] = jnp.zeros(l_ref.shape, F32)
        acc_ref[...] = jnp.zeros(acc_ref.shape, F32)
    return m_refs, l_refs, acc_refs


def _pipelined_tiles(qi, logits, next_logits, softmax_pv):
    a, b, c = 0, 1, 2

    @pl.when(qi == 0)
    def _():
        logits(0, a, True)
        next_logits(c)
        softmax_pv(0, a, "diag")

    @pl.when(qi == 1)
    def _():
        logits(1, a, True)
        softmax_pv(0, c, "near")
        next_logits(c)
        softmax_pv(1, a, "diag")

    @pl.when(qi >= 2)
    def _():
        logits(1, a)
        softmax_pv(0, c, "far")
        nfar = qi - 2

        def body(i, carry):
            j = 2 * i + 1
            logits(j + 1, b)
            softmax_pv(j, a, "far")
            logits(j + 2, a)
            softmax_pv(j + 1, b, "far")
            return carry

        lax.fori_loop(0, lax.shift_right_logical(nfar, 1), body, 0)
        odd = lax.rem(nfar, 2) == 1

        @pl.when(jnp.logical_not(odd))
        def _():
            logits(qi, b, True)
            softmax_pv(qi - 1, a, "near")
            next_logits(c)
            softmax_pv(qi, b, "diag")

        @pl.when(odd)
        def _():
            logits(qi - 1, b)
            softmax_pv(qi - 2, a, "far")
            logits(qi, a, True)
            softmax_pv(qi - 1, b, "near")
            next_logits(c)
            softmax_pv(qi, a, "diag")


def _mla_attn_kernel(qT_ref, qTn_ref, k_ref, vT_ref, *rest, ta, hp, ncast):
    cast_in, o_ref, cast_out = rest[:ncast], rest[ncast], rest[ncast + 1:2 * ncast + 1]
    s_refs = rest[2 * ncast + 1:2 * ncast + 4]
    state = rest[2 * ncast + 4:]
    qi = pl.program_id(2)
    for src_ref, dst_ref in zip(cast_in, cast_out):
        dst_ref[...] = src_ref[...].astype(BF16)
    m_refs, l_refs, acc_refs = _init_chains(state)

    nb = ta // ATT_BLK
    blk = ATT_BLK

    def logits(j, buf, diagonal=False):
        off = pl.multiple_of(j * ta, ta)
        for h in range(hp):
            if not diagonal:
                s_refs[buf][h] = _dot(k_ref[0, h, pl.ds(off, ta), :], qT_ref[0, h])
                continue
            for qs in range(nb):
                nk, q0 = (qs + 1) * blk, qs * blk
                s_refs[buf][h, :nk, q0:q0 + blk] = _dot(k_ref[0, h, pl.ds(off, nk), :],
                                                        qT_ref[0, h, :, q0:q0 + blk])

    def next_logits(buf):
        for h in range(hp):
            s_refs[buf][h] = _dot(k_ref[0, h, :ta, :], qTn_ref[0, h])

    def softmax_pv(j, buf, kind):
        off = pl.multiple_of(j * ta, ta)
        diagonal = kind == "diag"
        for h in range(hp):
            for qs in range(nb):
                nk = (qs + 1) * blk if diagonal else ta
                s = s_refs[buf][h, :nk, qs * blk:(qs + 1) * blk]
                if diagonal:
                    kpos = lax.broadcasted_iota(jnp.int32, (nk, blk), 0)
                    qpos = lax.broadcasted_iota(jnp.int32, (nk, blk), 1) + qs * blk
                    s = jnp.where(kpos <= qpos, s, NEG_BIG)
                c = h * nb + qs
                _softmax_step(s, vT_ref[0, h, :, pl.ds(off, nk)], m_refs[c], l_refs[c], acc_refs[c])

    _pipelined_tiles(qi, logits, next_logits, softmax_pv)
    for h in range(hp):
        for qs in range(nb):
            c = h * nb + qs
            out = acc_refs[c][...] * (1.0 / l_refs[c][...])
            o_ref[0, qs * blk:(qs + 1) * blk, h * MLA_V:(h + 1) * MLA_V] = out.T.astype(BF16)


def _mla_attn(qT, k, vT, ta, hp, cast=()):
    b, hd, _, s = qT.shape
    nq, nhg = s // ta, hd // hp
    nsteps = b * nhg * nq
    crows = [_cast_rows(a.shape[0], nsteps) for a in cast]
    assert all(crows), "cast tables must split into whole tiles per grid step"
    step = lambda bi, h, i: ((bi * nhg + h) * nq + i, 0)
    res = pl.pallas_call(
        functools.partial(_mla_attn_kernel, ta=ta, hp=hp, ncast=len(cast)),
        grid=(b, nhg, nq),
        in_specs=[
            pl.BlockSpec((1, hp, MLA_QK, ta), lambda bi, h, i: (bi, h, 0, i)),
            pl.BlockSpec((1, hp, MLA_QK, ta), lambda bi, h, i: (bi, h, 0, jnp.minimum(i + 1, nq - 1))),
            pl.BlockSpec((1, hp, s, MLA_QK), lambda bi, h, i: (bi, h, 0, 0)),
            pl.BlockSpec((1, hp, MLA_V, s), lambda bi, h, i: (bi, h, 0, 0)),
        ] + [pl.BlockSpec((r, a.shape[1]), step) for a, r in zip(cast, crows)],
        out_specs=[pl.BlockSpec((1, ta, hp * MLA_V), lambda bi, h, i: (bi, i, h))]
        + [pl.BlockSpec((r, a.shape[1]), step) for a, r in zip(cast, crows)],
        out_shape=[jax.ShapeDtypeStruct((b, s, hd * MLA_V), BF16)]
        + [jax.ShapeDtypeStruct(a.shape, BF16) for a in cast],
        scratch_shapes=([pltpu.VMEM((hp, ta, ta), F32)] * 3
                        + _chain_scratch(hp * ta // ATT_BLK, MLA_V)),
        compiler_params=pltpu.CompilerParams(
            dimension_semantics=("arbitrary",) * 3, vmem_limit_bytes=VMEM_LIMIT),
        name="mla_attn",
    )(qT, qT, k, vT, *cast)
    return res[0], res[1:]


def _diff_attn_kernel(cfar_ref, qT_ref, qTn_ref, k_ref, vT_ref, bias_ref, lamp_ref, g_ref, *rest,
                      ta, hp, lambda_init, ncast):
    cast_in, o_ref, cast_out = rest[:ncast], rest[ncast], rest[ncast + 1:2 * ncast + 1]
    s_refs = rest[2 * ncast + 1:2 * ncast + 4]
    state = rest[2 * ncast + 4:]
    hg = pl.program_id(1)
    qi = pl.program_id(2)
    for src_ref, dst_ref in zip(cast_in, cast_out):
        dst_ref[...] = src_ref[...].astype(BF16)
    m_refs, l_refs, acc_refs = _init_chains(state)

    nb = ta // ATT_BLK
    blk = ATT_BLK

    def logits(j, buf, diagonal=False):
        off = pl.multiple_of(j * ta, ta)
        for h in range(hp):
            for c in range(2):
                if not diagonal:
                    s_refs[buf][2 * h + c] = _dot(k_ref[0, h, pl.ds(off, ta), :],
                                                  qT_ref[0, h, c])
                    continue
                for qb in range(nb):
                    nk, q0 = (qb + 1) * blk, qb * blk
                    s_refs[buf][2 * h + c, :nk, q0:q0 + blk] = _dot(
                        k_ref[0, h, pl.ds(off, nk), :], qT_ref[0, h, c, :, q0:q0 + blk])

    def next_logits(buf):
        for h in range(hp):
            for c in range(2):
                s_refs[buf][2 * h + c] = _dot(k_ref[0, h, :ta, :], qTn_ref[0, h, c])

    def softmax_pv(j, buf, kind):
        off = pl.multiple_of(j * ta, ta)
        diagonal = kind == "diag"
        for h in range(hp):
            for qb in range(nb):
                nk = (qb + 1) * blk if diagonal else ta
                vj = vT_ref[0, h, :, pl.ds(off, nk)]
                for c in range(2):
                    s = s_refs[buf][2 * h + c, :nk, qb * blk:(qb + 1) * blk]
                    i = (2 * h + c) * nb + qb
                    if kind == "far" or (kind == "near" and qb * blk + 1 >= MAX_DISTANCE):
                        _softmax_step(s, vj, m_refs[i], None, acc_refs[i],
                                      shift=cfar_ref[hg * hp + h])
                    else:
                        bias = bias_ref[h, 0 if diagonal else 1, :nk, qb * blk:(qb + 1) * blk]
                        _softmax_step(s + bias, vj, m_refs[i], None, acc_refs[i])

    _pipelined_tiles(qi, logits, next_logits, softmax_pv)

    lp = lamp_ref[...]
    lam = (jnp.exp(jnp.sum(lp[0:1] * lp[1:2], axis=-1, keepdims=True))
           - jnp.exp(jnp.sum(lp[2:3] * lp[3:4], axis=-1, keepdims=True)) + lambda_init)
    for h in range(hp):
        for qb in range(nb):
            a0 = acc_refs[(2 * h) * nb + qb][...]
            a1 = acc_refs[(2 * h + 1) * nb + qb][...]
            out = (a0[:DIFF_V] * (1.0 / a0[DIFF_V:DIFF_V + 1])
                   - lam * (a1[:DIFF_V] * (1.0 / a1[DIFF_V:DIFF_V + 1])))
            ms = jnp.mean(out * out, axis=0, keepdims=True)
            out = out * lax.rsqrt(ms + 1e-5) * g_ref[...] * (1.0 - lambda_init)
            o_ref[0, qb * blk:(qb + 1) * blk, h * DIFF_V:(h + 1) * DIFF_V] = out.T.astype(BF16)


def _cast_rows(rows_total, nsteps):
    rows = rows_total // nsteps
    return rows if rows * nsteps == rows_total and rows % BF16_ROWS == 0 else 0


def _diff_attn(cfar, qT, k, vT, bias_tiles, lam_params, g_col, ta, hp, lambda_init,
               cast=(), cast_part=(0, 1)):
    b, hd, _, _, s = qT.shape
    nq, nhg = s // ta, hd // hp
    nsteps = b * nhg * nq
    part, nparts = cast_part
    crows = [_cast_rows(a.shape[0] // nparts, nsteps) for a in cast]
    assert all(crows), "cast tables must split into whole tiles per grid step"
    step = lambda bi, h, i: (bi * nhg + h) * nq + i
    grid_spec = pltpu.PrefetchScalarGridSpec(
        num_scalar_prefetch=1,
        grid=(b, nhg, nq),
        in_specs=[
            pl.BlockSpec((1, hp, 2, LANES, ta), lambda bi, h, i, cf: (bi, h, 0, 0, i)),
            pl.BlockSpec((1, hp, 2, LANES, ta),
                         lambda bi, h, i, cf: (bi, h, 0, 0, jnp.minimum(i + 1, nq - 1))),
            pl.BlockSpec((1, hp, s, LANES), lambda bi, h, i, cf: (bi, h, 0, 0)),
            pl.BlockSpec((1, hp, DIFF_VA, s), lambda bi, h, i, cf: (bi, h, 0, 0)),
            pl.BlockSpec((hp, 2, ta, ta), lambda bi, h, i, cf: (h, 0, 0, 0)),
            pl.BlockSpec(lam_params.shape, lambda bi, h, i, cf: (0, 0)),
            pl.BlockSpec(g_col.shape, lambda bi, h, i, cf: (0, 0)),
        ] + [pl.BlockSpec((r, a.shape[1]), lambda bi, h, i, cf: (part * nsteps + step(bi, h, i), 0))
             for a, r in zip(cast, crows)],
        out_specs=[pl.BlockSpec((1, ta, hp * DIFF_V), lambda bi, h, i, cf: (bi, i, h))]
        + [pl.BlockSpec((r, a.shape[1]), lambda bi, h, i, cf: (step(bi, h, i), 0))
           for a, r in zip(cast, crows)],
        scratch_shapes=([pltpu.VMEM((2 * hp, ta, ta), F32)] * 3
                        + _chain_scratch(2 * hp * ta // ATT_BLK, DIFF_VA)),
    )
    res = pl.pallas_call(
        functools.partial(_diff_attn_kernel, ta=ta, hp=hp, lambda_init=lambda_init,
                          ncast=len(cast)),
        grid_spec=grid_spec,
        out_shape=[jax.ShapeDtypeStruct((b, s, hd * DIFF_V), BF16)]
        + [jax.ShapeDtypeStruct((a.shape[0] // nparts, a.shape[1]), BF16) for a in cast],
        compiler_params=pltpu.CompilerParams(
            dimension_semantics=("arbitrary",) * 3, vmem_limit_bytes=VMEM_LIMIT),
        name="diff_attn",
    )(cfar, qT, qT, k, vT, bias_tiles, lam_params, g_col, *cast)
    return res[0], res[1:]


def _split_bf16(v):
    hi = v.astype(BF16)
    return hi, (v - hi.astype(F32)).astype(BF16)


def _oproj_route_kernel(am_ref, ad_ref, x_ref, wo_ref, g_ref, b_ref, rhl_ref, rh_ref,
                        o_ref, ob_ref, r_ref):
    tm = x_ref.shape[0]
    hh = tm // 2
    halves = (slice(0, hh), slice(hh, tm))
    nt = (((1,), (1,)), ((), ()))
    mix = [_dot(am_ref[r, :], wo_ref[0, :MLA_W, :]) + _dot(ad_ref[r, :], wo_ref[0, MLA_W:, :])
           for r in halves]
    ys = [_layer_norm_rows(DN_ALPHA * x_ref[r, :] + m, g_ref[...], b_ref[...])
          for r, m in zip(halves, mix)]
    splits = [_split_bf16(y) for y in ys]
    boths = [lax.dot_general(rhl_ref[...], yhi, nt, preferred_element_type=F32)
             for yhi, _ in splits]
    lows = [lax.dot_general(rh_ref[...], ylo, nt, preferred_element_type=F32) for _, ylo in splits]
    for r, y, (yhi, _), both, low in zip(halves, ys, splits, boths, lows):
        logits = both[:ROUTER_ROWS] + both[ROUTER_ROWS:] + low
        row = lax.broadcasted_iota(jnp.int32, logits.shape, 0)
        logits = jnp.where(row < N_EXPERTS, logits, NEG_BIG)
        m1 = jnp.max(logits, axis=0, keepdims=True)
        i1 = jnp.min(jnp.where(logits == m1, row, ROUTER_ROWS), axis=0, keepdims=True)
        rest_l = jnp.where(row == i1, NEG_BIG, logits)
        m2 = jnp.max(rest_l, axis=0, keepdims=True)
        i2 = jnp.min(jnp.where(rest_l == m2, row, ROUTER_ROWS), axis=0, keepdims=True)
        e2 = jnp.exp(m2 - m1)
        g1 = 1.0 / (1.0 + e2)
        g2 = e2 * g1
        row8 = lax.broadcasted_iota(jnp.int32, (REC_ROWS, hh), 0)
        rec = jnp.where(row8 == 0, i1.astype(F32),
                        jnp.where(row8 == 1, i2.astype(F32),
                                  jnp.where(row8 == 2, g1, jnp.where(row8 == 3, g2, 0.0))))
        o_ref[r, :] = y
        ob_ref[r, :] = yhi
        r_ref[:, r] = rec


def _oproj_route(a_mla, a_diff, x2d, l, wo, g, b, rhl, rh, tm):
    t, d = x2d.shape
    row = lambda w: pl.BlockSpec((tm, w), lambda i: (i, 0))
    full = lambda a: pl.BlockSpec(a.shape, lambda i: (0,) * a.ndim)
    return pl.pallas_call(
        _oproj_route_kernel,
        grid=(t // tm,),
        in_specs=[row(MLA_W), row(DIFF_W), row(d),
                  pl.BlockSpec((1,) + wo.shape[1:], lambda i: (l, 0, 0)),
                  full(g), full(b), full(rhl), full(rh)],
        out_specs=[row(d), row(d), pl.BlockSpec((REC_ROWS, tm), lambda i: (0, i))],
        out_shape=[jax.ShapeDtypeStruct((t, d), F32), jax.ShapeDtypeStruct((t, d), BF16),
                   jax.ShapeDtypeStruct((REC_ROWS, t), F32)],
        compiler_params=pltpu.CompilerParams(
            dimension_semantics=("arbitrary",), vmem_limit_bytes=VMEM_LIMIT),
        name="oproj_route",
    )(a_mla, a_diff, x2d, wo, g, b, rhl, rh)


def _oproj_ffn_kernel(am_ref, ad_ref, x_ref, wo_ref, g1_ref, b1_ref, wg_ref, wu_ref, wd_ref,
                      g2_ref, b2_ref, o_ref):
    mix = _dot(am_ref[...], wo_ref[0, :MLA_W, :]) + _dot(ad_ref[...], wo_ref[0, MLA_W:, :])
    x1 = _layer_norm_rows(DN_ALPHA * x_ref[...] + mix, g1_ref[...], b1_ref[...])
    xb = x1.astype(BF16)
    gate = _dot(xb, wg_ref[0])
    up = _dot(xb, wu_ref[0])
    hid = (gate * jax.nn.sigmoid(gate) * up).astype(BF16)
    f = _dot(hid, wd_ref[0])
    o_ref[...] = _layer_norm_rows(DN_ALPHA * x1 + f, g2_ref[...], b2_ref[...])


def _oproj_ffn(a_mla, a_diff, x2d, l, wo, g1, b1, li, wg, wu, wd, g2, b2, tm):
    t, d = x2d.shape
    row = lambda w: pl.BlockSpec((tm, w), lambda i: (i, 0))
    layer = lambda a, k: pl.BlockSpec((1,) + a.shape[1:], lambda i: (k,) + (0,) * (a.ndim - 1),
                                      pipeline_mode=pl.Buffered(1))
    vec = lambda a: pl.BlockSpec(a.shape, lambda i: (0, 0))
    return pl.pallas_call(
        _oproj_ffn_kernel,
        grid=(t // tm,),
        in_specs=[row(MLA_W), row(DIFF_W), row(d), layer(wo, l), vec(g1), vec(b1),
                  layer(wg, li), layer(wu, li), layer(wd, li), vec(g2), vec(b2)],
        out_specs=row(d),
        out_shape=jax.ShapeDtypeStruct((t, d), F32),
        compiler_params=pltpu.CompilerParams(
            dimension_semantics=("arbitrary",), vmem_limit_bytes=VMEM_LIMIT),
        name="oproj_ffn",
    )(a_mla, a_diff, x2d, wo, g1, b1, wg, wu, wd, g2, b2)


def _dispatch_kernel(ue_ref, uoff_ref, unc_ref, x_ref, post_ref, o_ref, *, rows, unit, kc):
    i = pl.program_id(0)
    t = x_ref.shape[0]
    ups = rows // unit

    def chunk(u, c):
        nominal = uoff_ref[u] + c * kc
        off = pl.multiple_of(jnp.minimum(nominal, t - kc), LANES)
        row_pos = u * unit + lax.broadcasted_iota(jnp.int32, (unit, 1), 0)
        dest = post_ref[pl.ds(ue_ref[u], 1), pl.ds(off, kc)]
        tok = off + lax.broadcasted_iota(jnp.int32, (1, kc), 1)
        dest = jnp.where(tok >= nominal, dest, -1)
        sel = jnp.where(dest == row_pos, 1.0, 0.0).astype(BF16)
        return _dot(sel, x_ref[pl.ds(off, kc), :]).astype(BF16)

    for r in range(ups):
        o_ref[r * unit:(r + 1) * unit, :] = chunk(i * ups + r, 0)

    for r in range(ups):
        def more(c, carry, r=r):
            o_ref[r * unit:(r + 1) * unit, :] += chunk(i * ups + r, c)
            return carry

        lax.fori_loop(1, unc_ref[i * ups + r], more, 0)


def _dispatch(unit_e, uoff, unc, xb, post, n_rows, rows, unit, kc):
    t, d = xb.shape
    const = lambda shape: pl.BlockSpec(shape, lambda i, *_: (0,) * len(shape),
                                       pipeline_mode=pl.Buffered(1))
    grid_spec = pltpu.PrefetchScalarGridSpec(
        num_scalar_prefetch=3,
        grid=(n_rows // rows,),
        in_specs=[const(xb.shape), const(post.shape)],
        out_specs=pl.BlockSpec((rows, d), lambda i, *_: (i, 0)),
    )
    return pl.pallas_call(
        functools.partial(_dispatch_kernel, rows=rows, unit=unit, kc=kc),
        grid_spec=grid_spec,
        out_shape=jax.ShapeDtypeStruct((n_rows, d), BF16),
        compiler_params=pltpu.CompilerParams(
            dimension_semantics=("arbitrary",), vmem_limit_bytes=VMEM_LIMIT),
        name="dispatch",
    )(unit_e, uoff, unc, xb, post)


def _gffn_kernel(te_ref, tv_ref, x_ref, wg_ref, wu_ref, wd_ref, o_ref, acc_ref, *, nf):
    i = pl.program_id(0)
    f = pl.program_id(1)
    status = tv_ref[i]
    tm = x_ref.shape[0]
    half = tm // 2

    def swiglu_rows(rows):
        xb = x_ref[:rows, :]
        gate = _dot(xb, wg_ref[0, 0])
        up = _dot(xb, wu_ref[0, 0])
        hid = (gate * jax.nn.sigmoid(gate) * up).astype(BF16)
        part = _dot(hid, wd_ref[0, 0])

        @pl.when(f == 0)
        def _():
            acc_ref[:rows, :] = part

        @pl.when(f > 0)
        def _():
            acc_ref[:rows, :] += part

    @pl.when(status == 2)
    def _():
        swiglu_rows(tm)

    @pl.when(status == 1)
    def _():
        swiglu_rows(half)

    @pl.when(jnp.logical_and(f == nf - 1, status == 2))
    def _():
        o_ref[...] = acc_ref[...].astype(BF16)

    @pl.when(jnp.logical_and(f == nf - 1, status == 1))
    def _():
        o_ref[:half, :] = acc_ref[:half, :].astype(BF16)
        o_ref[half:, :] = jnp.zeros((tm - half, o_ref.shape[1]), BF16)

    @pl.when(jnp.logical_and(f == nf - 1, status == 0))
    def _():
        o_ref[...] = jnp.zeros(o_ref.shape, BF16)


def _gffn(tile_e, tile_v, xs, li, wg, wu, wd, tm, fc):
    p, d = xs.shape
    nt = p // tm
    nf = wg.shape[3] // fc
    fidx = lambda i, f, te, tv: jnp.where(tv[i] > 0, f, nf - 1)
    grid_spec = pltpu.PrefetchScalarGridSpec(
        num_scalar_prefetch=2,
        grid=(nt, nf),
        in_specs=[
            pl.BlockSpec((tm, d), lambda i, f, te, tv: (i, 0)),
            pl.BlockSpec((1, 1, d, fc), lambda i, f, te, tv: (li, te[i], 0, fidx(i, f, te, tv))),
            pl.BlockSpec((1, 1, d, fc), lambda i, f, te, tv: (li, te[i], 0, fidx(i, f, te, tv))),
            pl.BlockSpec((1, 1, fc, d), lambda i, f, te, tv: (li, te[i], fidx(i, f, te, tv), 0)),
        ],
        out_specs=pl.BlockSpec((tm, d), lambda i, f, te, tv: (i, 0)),
        scratch_shapes=[pltpu.VMEM((tm, d), F32)],
    )
    return pl.pallas_call(
        functools.partial(_gffn_kernel, nf=nf),
        grid_spec=grid_spec,
        out_shape=jax.ShapeDtypeStruct((p, d), BF16),
        compiler_params=pltpu.CompilerParams(
            dimension_semantics=("arbitrary", "arbitrary"), vmem_limit_bytes=VMEM_LIMIT),
        name="grouped_ffn",
    )(tile_e, tile_v, xs, wg, wu, wd)


def _combine_kernel(ral_ref, rlo_ref, rhi_ref, x_ref, rec_ref, g_ref, b_ref, ys_ref, o_ref,
                    win_ref, xwin_ref, acc_ref, sems, xsem, *, w):
    j = pl.program_id(0)
    nj = pl.num_programs(0)
    slot = lax.rem(j, 2)
    p_rows = ys_ref.shape[0]

    def win_copy(jj, e, sl):
        r = pl.multiple_of(ral_ref[jj * N_EXPERTS + e], BF16_ROWS)
        return pltpu.make_async_copy(ys_ref.at[pl.ds(r, w)], win_ref.at[sl, pl.ds(e * w, w)],
                                     sems.at[sl, e])

    @pl.when(j == 0)
    def _():
        for e in range(N_EXPERTS):
            win_copy(0, e, 0).start()

    @pl.when(j + 1 < nj)
    def _():
        for e in range(N_EXPERTS):
            win_copy(j + 1, e, 1 - slot).start()

    rec = rec_ref[...]
    p1, p2, g1, g2 = rec[:, 0:1], rec[:, 1:2], rec[:, 2:3], rec[:, 3:4]

    def sel(base, lo, hi):
        rows = base + lax.broadcasted_iota(jnp.int32, (1, w), 1)
        rp = jnp.where((rows >= lo) & (rows < hi), rows, -1).astype(F32)
        return (jnp.where(p1 == rp, g1, 0.0) + jnp.where(p2 == rp, g2, 0.0)).astype(BF16)

    lane = lax.broadcasted_iota(jnp.int32, (1, N_EXPERTS * w), 1)
    rp = jnp.full((1, N_EXPERTS * w), -1, jnp.int32)
    for e in range(N_EXPERTS):
        win_copy(j, e, slot).wait()
        k = j * N_EXPERTS + e
        rows = ral_ref[k] + lane - e * w
        mine = ((lane >= e * w) & (lane < (e + 1) * w)
                & (rows >= rlo_ref[k]) & (rows < rhi_ref[k]))
        rp = jnp.where(mine, rows, rp)
    rp = rp.astype(F32)
    tm = rec.shape[0]
    step = min(tm, 256)
    for r0 in range(0, tm, step):
        rows = slice(r0, r0 + step)
        sel_rows = (jnp.where(p1[rows] == rp, g1[rows], 0.0)
                    + jnp.where(p2[rows] == rp, g2[rows], 0.0)).astype(BF16)
        acc_ref[rows, :] = _dot(sel_rows, win_ref[slot])

    for e in range(N_EXPERTS):
        k = j * N_EXPERTS + e
        nwin = (rhi_ref[k] - ral_ref[k] + (w - 1)) // w

        def extra(wi, carry, k=k):
            nominal = ral_ref[k] + wi * w
            base = pl.multiple_of(jnp.minimum(nominal, p_rows - w), BF16_ROWS)
            cp = pltpu.make_async_copy(ys_ref.at[pl.ds(base, w)], xwin_ref, xsem)
            cp.start()
            cp.wait()
            acc_ref[...] += _dot(sel(base, nominal, rhi_ref[k]), xwin_ref[...])
            return carry

        lax.fori_loop(1, nwin, extra, 0)

    o_ref[...] = _layer_norm_rows(DN_ALPHA * x_ref[...] + acc_ref[...], g_ref[...], b_ref[...])


def _combine(ral, rlo, rhi, x2d, rec2, g, b, ys, tm, w):
    t, d = x2d.shape
    row = lambda c: pl.BlockSpec((tm, c), lambda i, *_: (i, 0))
    full = lambda shape: pl.BlockSpec(shape, lambda i, *_: (0,) * len(shape))
    grid_spec = pltpu.PrefetchScalarGridSpec(
        num_scalar_prefetch=3,
        grid=(t // tm,),
        in_specs=[row(d), row(rec2.shape[1]), full(g.shape), full(b.shape),
                  pl.BlockSpec(memory_space=pl.ANY)],
        out_specs=row(d),
        scratch_shapes=[pltpu.VMEM((2, N_EXPERTS * w, d), BF16), pltpu.VMEM((w, d), BF16),
                        pltpu.VMEM((tm, d), F32), pltpu.SemaphoreType.DMA((2, N_EXPERTS)),
                        pltpu.SemaphoreType.DMA(())],
    )
    return pl.pallas_call(
        functools.partial(_combine_kernel, w=w),
        grid_spec=grid_spec,
        out_shape=jax.ShapeDtypeStruct((t, d), F32),
        compiler_params=pltpu.CompilerParams(
            dimension_semantics=("arbitrary",), vmem_limit_bytes=VMEM_LIMIT),
        name="combine",
    )(ral, rlo, rhi, x2d, rec2, g, b, ys)


def _route_plan(rec, nt, tm_e, unit, kc, tm_c, w):
    t = rec.shape[1]
    i32 = jnp.int32
    ar = jnp.arange(N_EXPERTS, dtype=i32)
    m1 = (rec[0].astype(i32)[:, None] == ar[None, :]).astype(i32)
    m2 = (rec[1].astype(i32)[:, None] == ar[None, :]).astype(i32)
    m = m1 + m2
    cinc = jnp.cumsum(m, axis=0)
    counts = cinc[-1]
    padded = ((counts + tm_e - 1) // tm_e) * tm_e
    ends = jnp.cumsum(padded)
    starts = ends - padded
    posmat = starts[None, :] + (cinc - m)
    pos1 = jnp.sum(m1 * posmat, axis=1)
    pos2 = jnp.sum(m2 * posmat, axis=1)
    post = jnp.where(m.T > 0, posmat.T, -1)
    rec2 = jnp.stack([pos1.astype(F32), pos2.astype(F32), rec[2], rec[3]]
                     + [jnp.zeros((t,), F32)] * 4, axis=1)
    tile_start = jnp.arange(nt, dtype=i32) * tm_e
    tile_e = jnp.sum((tile_start[:, None] >= ends[None, :]).astype(i32), axis=1)
    tile_v = tile_e < N_EXPERTS
    last_e = jnp.max(jnp.where(tile_v, tile_e, 0))
    tile_e = jnp.where(tile_v, tile_e, last_e)
    upt = tm_e // unit
    nu = nt * upt
    oh = (jnp.repeat(tile_e, upt)[:, None] == ar[None, :]).astype(i32)
    q0 = jnp.arange(nu, dtype=i32) * unit - jnp.sum(oh * starts[None, :], axis=1)
    q1 = jnp.minimum(q0 + unit, jnp.sum(oh * counts[None, :], axis=1))
    has = jnp.repeat(tile_v, upt) & (q1 > q0)
    cb = jnp.concatenate([jnp.zeros((1, N_EXPERTS), i32), cinc[LANES - 1::LANES]], axis=0)
    cbu = jnp.sum(oh[:, None, :] * cb[None, :, :], axis=2)
    jlo = jnp.sum((cbu[:, 1:] <= q0[:, None]).astype(i32), axis=1)
    jend = jnp.sum((cbu[:, :-1] < q1[:, None]).astype(i32), axis=1)
    uoff = jnp.where(has, jlo * LANES, 0)
    unc = jnp.where(has, (jend * LANES - uoff + kc - 1) // kc, 0)
    cbt = jnp.concatenate([jnp.zeros((1, N_EXPERTS), i32), cinc[tm_c - 1::tm_c]], axis=0)
    r_lo = starts[None, :] + cbt[:-1]
    r_hi = starts[None, :] + cbt[1:]
    ral = jnp.minimum((r_lo // BF16_ROWS) * BF16_ROWS, nt * tm_e - w)
    oh_t = (tile_e[:, None] == ar[None, :]).astype(i32)
    tile_rows = (jnp.sum(oh_t * counts[None, :], axis=1)
                 - (tile_start - jnp.sum(oh_t * starts[None, :], axis=1)))
    tile_status = jnp.where(tile_v, jnp.where(tile_rows > tm_e // 2, 2, 1), 0)
    return (rec2, post, tile_e, tile_status.astype(i32), uoff, unc,
            ral.reshape(-1), r_lo.reshape(-1), r_hi.reshape(-1))


def _moe(x2d, xb, rec, li, wg, wu, wd, g, b, tm_c, tm_e, fc, unit, kc, w):
    t, d = x2d.shape
    nt = (2 * t) // tm_e + N_EXPERTS
    rec2, post, tile_e, tile_v, uoff, unc, ral, rlo, rhi = _route_plan(
        rec, nt, tm_e, unit, kc, tm_c, w)
    n_rows = nt * tm_e
    rows_d = 2 * tm_e if n_rows % (2 * tm_e) == 0 else tm_e
    xs = _dispatch(jnp.repeat(tile_e, tm_e // unit), uoff, unc, xb, post, n_rows, rows_d, unit, kc)
    ys = _gffn(tile_e, tile_v, xs, li, wg, wu, wd, tm_e, fc)
    return _combine(ral, rlo, rhi, x2d, rec2, g, b, ys, tm_c, w)


def _rotate_half_cols(w):
    half = w.shape[-1] // 2
    return jnp.concatenate([-w[..., half:], w[..., :half]], axis=-1)


def _t5_bucket(n):
    max_exact = NUM_BUCKETS // 2
    nf = jnp.maximum(n, 1).astype(F32)
    large = max_exact + (jnp.log(nf / max_exact) / math.log(MAX_DISTANCE / max_exact)
                         * (NUM_BUCKETS - max_exact)).astype(jnp.int32)
    large = jnp.minimum(large, NUM_BUCKETS - 1)
    return jnp.where(n < max_exact, n, large)


def _pick(n, prefs):
    for p in prefs:
        if n % p == 0:
            return p
    raise ValueError(f"no tile for extent {n}")


def kernel(x, rel_bias, w_in, mla_q_norm, mla_kv_norm, mla_w_uq, mla_w_uk, mla_w_uv, diff_lambda, diff_norm, w_o, ln1_g, ln1_b, ln2_g, ln2_b, ffn_w_gate, ffn_w_up, ffn_w_down, moe_router, moe_w_gate, moe_w_up, moe_w_down):
    b, s, d = x.shape
    t = b * s
    assert d == D_MODEL
    ta = _pick(s, (512, 256))
    assert ta % ATT_BLK == 0
    assert s >= 2 * ta or s == ta
    tm_proj = _pick(s, (512, 256, 128))
    tm_tok = _pick(t, (512, 256, 128))
    tm_e = _pick(2 * t, (512, 256))
    tm_c = _pick(t, (512, 256))
    unit, kc, wrows = 128, 768, 224
    assert tm_e % unit == 0 and t >= kc and t % LANES == 0
    fc = D_FF_EXPERT // 2

    pos = jnp.arange(s, dtype=F32)
    inv = 1.0 / (ROPE_THETA ** (jnp.arange(0, MLA_ROPE, 2, dtype=F32) / MLA_ROPE))
    ang = pos[:, None] * inv[None, :]
    cos4 = jnp.tile(jnp.cos(ang), (1, 2 * MLA_HEADS))
    sin4 = jnp.tile(jnp.sin(ang), (1, 2 * MLA_HEADS))
    rb = rel_bias.astype(F32)
    assert ta + 1 >= MAX_DISTANCE and LANES >= MAX_DISTANCE
    cfar = rb[NUM_BUCKETS - 1] * LOG2E
    kk = jnp.arange(LANES)[:, None]
    qq = jnp.arange(LANES)[None, :]

    def bias_block(delta):
        shape = (DIFF_HEADS, LANES, LANES)
        if delta < 0:
            return jnp.full(shape, NEG_BIG, F32)
        if delta >= 2:
            return jnp.broadcast_to(cfar[:, None, None], shape)
        dist = delta * LANES + qq - kk
        bucket = _t5_bucket(jnp.maximum(dist, 0))
        bt = jnp.zeros(shape, F32)
        for bk in range(NUM_BUCKETS):
            bt = bt + jnp.where((bucket == bk)[None], rb[bk][:, None, None], 0.0)
        return jnp.where((dist >= 0)[None], bt * LOG2E, NEG_BIG)

    nbk = ta // LANES
    tiles = [jnp.concatenate([jnp.concatenate([bias_block(cb - kb + off) for cb in range(nbk)], axis=2)
                              for kb in range(nbk)], axis=1)
             for off in (0, nbk)]
    bias_tiles = jnp.stack(tiles, axis=1)
    n_moe = moe_w_gate.shape[0]
    moe_tables = (moe_w_gate.reshape(-1, D_FF_EXPERT), moe_w_up.reshape(-1, D_FF_EXPERT),
                  moe_w_down.reshape(-1, d))

    w_t = jnp.swapaxes(w_in, 1, 2)
    kr_t = w_t[:, _KR0:_KR0 + MLA_ROPE]
    half = MLA_ROPE // 2
    win = jnp.concatenate([w_t[:, :_KR0 + MLA_ROPE], -kr_t[:, half:], kr_t[:, :half],
                           w_t[:, _KR0 + MLA_ROPE:]], axis=1).astype(BF16)
    nl = w_in.shape[0]
    uq = mla_w_uq.reshape(nl, MLA_Q_RANK, MLA_HEADS, MLA_QK)
    uq_r = uq[..., MLA_NOPE:]
    wuq = jnp.concatenate([uq[..., :MLA_NOPE].reshape(nl, MLA_Q_RANK, -1),
                           uq_r.reshape(nl, MLA_Q_RANK, -1),
                           _rotate_half_cols(uq_r).reshape(nl, MLA_Q_RANK, -1)], axis=2).astype(BF16)
    wukv = jnp.concatenate([mla_w_uk, mla_w_uv], axis=2).astype(BF16)
    qn, kvn = mla_q_norm[:, None, :], mla_kv_norm[:, None, :]
    late = (w_o, ffn_w_gate, ffn_w_up, ffn_w_down)
    late_tables = tuple(w.reshape(-1, w.shape[-1]) for w in late)
    mla_steps = b * (MLA_HEADS // MLA_HP) * (s // ta)
    ride = all(_cast_rows(a.shape[0], mla_steps) for a in late_tables)
    if not ride:
        wo, ffn_wg, ffn_wu, ffn_wd = (w.astype(BF16) for w in late)
    rpad =jnp.pad(jnp.swapaxes(moe_router.astype(F32), 1, 2),
                   ((0, 0), (0, ROUTER_ROWS - N_EXPERTS), (0, 0)))
    rhi = rpad.astype(BF16)
    rhl = jnp.concatenate([rhi, (rpad - rhi.astype(F32)).astype(BF16)], axis=1)

    xc = x
    for l in range(DEPTH):
        lambda_init = 0.8 - 0.6 * math.exp(-0.3 * l)
        qT, k, vT, dqT, dk, dvT = _proj(xc, l, win, qn, kvn, wuq, wukv, cos4, sin4, tm_proj)
        a_mla, casts = _mla_attn(qT, k, vT, ta, MLA_HP,
                                 cast=late_tables if ride and l == 0 else ())
        a_mla = a_mla.reshape(t, MLA_W)
        if casts:
            wo, ffn_wg, ffn_wu, ffn_wd = (c.reshape(w.shape) for c, w in zip(casts, late))
        i = l // 2
        dense = l % 2 == 0
        a_diff, casts = _diff_attn(cfar, dqT, dk, dvT, bias_tiles, diff_lambda[l].astype(F32),
                                   diff_norm[l].astype(F32)[:, None], ta, DIFF_HP, lambda_init,
                                   cast=moe_tables if dense and i < n_moe else (),
                                   cast_part=(i, n_moe))
        a_diff = a_diff.reshape(t, DIFF_W)
        if casts:
            moe_wg = casts[0].reshape(1, N_EXPERTS, d, D_FF_EXPERT)
            moe_wu = casts[1].reshape(1, N_EXPERTS, d, D_FF_EXPERT)
            moe_wd = casts[2].reshape(1, N_EXPERTS, D_FF_EXPERT, d)
        x2d = xc.reshape(t, d)
        g1, b1 = ln1_g[l][None, :], ln1_b[l][None, :]
        g2, b2 = ln2_g[l][None, :], ln2_b[l][None, :]
        if dense:
            x2 = _oproj_ffn(a_mla, a_diff, x2d, l, wo, g1, b1, i, ffn_wg, ffn_wu, ffn_wd,
                            g2, b2, tm_tok)
        else:
            x1, x1b, rec = _oproj_route(a_mla, a_diff, x2d, l, wo, g1, b1, rhl[i], rhi[i], tm_tok)
            x2 = _moe(x1, x1b, rec, 0, moe_wg, moe_wu, moe_wd, g2, b2,
                      tm_c, tm_e, fc, unit, kc, wrows)
        xc = x2.reshape(b, s, d)
    return xc
```
